```python
import math
import jax, jax.numpy as jnp
from jax import lax
import numpy as np

D_MODEL = 2048
BATCH = 1
SEQ = 8192
DEPTH = 4

GRID_W = 64
CTX_LEN = 256
MIX_W = D_MODEL
S5_W = D_MODEL // 4
S5_CH = 16
S5_G = S5_W // S5_CH
S5_P = 64
CONV_W = D_MODEL // 4
CONV_K = 31
ML_W = D_MODEL // 2
ML_H = 4
ML_DH = ML_W // ML_H
ML_CHUNK = 128
D_FF = ((8 * D_MODEL + 3 * 256 - 1) // (3 * 256)) * 256
EPS = 1e-6
LN_EPS = 1e-5

OFF_S5 = 0
OFF_CONV = OFF_S5 + S5_W
OFF_Q = OFF_CONV + 2 * CONV_W
OFF_K = OFF_Q + ML_W
OFF_V = OFF_K + ML_W
OFF_O = OFF_V + ML_W
OFF_G = OFF_O + ML_W
IN_W = OFF_G + 4 * ML_H

kernel_name = 'hymba_s5_conformer_mlstm_dit'


def _rmsnorm(x, g):
    xf = x.astype(jnp.float32)
    xf = xf * lax.rsqrt(jnp.mean(xf * xf, axis=-1, keepdims=True) + EPS)
    return (xf * g.astype(jnp.float32)).astype(x.dtype)


def _layernorm(x, g, b):
    xf = x.astype(jnp.float32)
    xc = xf - jnp.mean(xf, axis=-1, keepdims=True)
    var = jnp.mean(xc * xc, axis=-1, keepdims=True)
    return (xc * lax.rsqrt(var + LN_EPS) * g.astype(jnp.float32) + b.astype(jnp.float32)).astype(x.dtype)


def _adaln(c_vec, w_mod, b_mod):
    m = jax.nn.silu(c_vec) @ w_mod + b_mod
    return jnp.split(m[..., None, :], 6, axis=-1)


def _swiglu(h, w_in, w_out):
    g, u = jnp.split(h @ w_in, 2, axis=-1)
    return (jax.nn.silu(g) * u) @ w_out


def _lin_rec(e1, e2):
    a1, b1 = e1
    a2, b2 = e2
    return a1 * a2, a2 * b1 + b2


def _s5_discretize(lam_re, lam_im, log_step, b_re, b_im):
    f32 = jnp.float32
    lam = lax.complex(lam_re.astype(f32), lam_im.astype(f32))
    lam_bar = jnp.exp(lam * jnp.exp(log_step.astype(f32)))
    b = lax.complex(b_re.astype(f32), b_im.astype(f32))
    b_bar = ((lam_bar - 1.0) / lam)[..., None] * b
    return lam_bar, b_bar


def _s5_scan(bu, lam_bar, h0, reverse):
    first, last = (-1, 0) if reverse else (0, -1)
    bu = bu.at[:, first].add(lam_bar * h0)
    a = jnp.broadcast_to(lam_bar, bu.shape)
    _, h = lax.associative_scan(_lin_rec, (a, bu), reverse=reverse, axis=1)
    return h, h[:, last]


def s5_mixer(u_c, u_x, lam_re, lam_im, log_step, b_re, b_im, c_re, c_im, d_skip, w_glu, b_glu, need_ctx):
    f32 = jnp.float32
    uc = u_c.astype(f32).reshape(u_c.shape[:2] + (S5_G, S5_CH))
    ux = u_x.astype(f32).reshape(u_x.shape[:2] + (S5_G, S5_CH))
    d = d_skip.astype(f32).reshape(S5_G, S5_CH)
    y_c = d * uc
    y_x = d * ux
    h0 = jnp.zeros((ux.shape[0], S5_G, S5_P), jnp.complex64)
    for di, reverse in enumerate((False, True)):
        lam_bar, b_bar = _s5_discretize(lam_re[di], lam_im[di], log_step[di], b_re[di], b_im[di])
        c_mat = lax.complex(c_re[di].astype(f32), c_im[di].astype(f32))
        h_c, h_fin = _s5_scan(jnp.einsum('blgc,gpc->blgp', uc, b_bar), lam_bar, h0, reverse)
        h_x, _ = _s5_scan(jnp.einsum('blgc,gpc->blgp', ux, b_bar), lam_bar, h_fin, reverse)
        y_x = y_x + jnp.einsum('blgp,gcp->blgc', h_x, c_mat).real
        if need_ctx:
            y_c = y_c + jnp.einsum('blgp,gcp->blgc', h_c, c_mat).real

    def glu(y, like):
        g = jax.nn.gelu(y.reshape(y.shape[:2] + (S5_W,))).astype(like.dtype)
        return g * jax.nn.sigmoid(g @ w_glu + b_glu)

    out_x = glu(y_x, u_x)
    out_c = glu(y_c, u_c) if need_ctx else None
    return out_c, out_x


def conv_module(pw, dw_w, dw_b, ln_g, ln_b, rows):
    a, b = jnp.split(pw, 2, axis=-1)
    g = a * jax.nn.sigmoid(b)
    bsz, length, ch = g.shape
    seqs = g.reshape(bsz * rows, length // rows, ch)
    y = lax.conv_general_dilated(seqs, dw_w[:, None, :], window_strides=(1,),
                                 padding=[(CONV_K // 2, CONV_K // 2)],
                                 dimension_numbers=('NWC', 'WIO', 'NWC'),
                                 feature_group_count=ch)
    y = y.reshape(bsz, length, ch) + dw_b
    return jax.nn.silu(_layernorm(y, ln_g, ln_b))


def _chunks(t):
    bsz, length = t.shape[:2]
    t = t.reshape((bsz, length // ML_CHUNK, ML_CHUNK) + t.shape[2:])
    return jnp.swapaxes(jnp.moveaxis(t, 1, 0), 2, 3)


def _unchunk(t):
    t = jnp.moveaxis(jnp.swapaxes(t, 2, 3), 0, 1)
    return t.reshape((t.shape[0], -1) + t.shape[3:])


def _mlstm_chunkwise(q, k, v, log_i, log_f, state):
    tril = jnp.tril(jnp.ones((ML_CHUNK, ML_CHUNK), bool))

    def step(carry, xs):
        cm, n, m = carry
        qc, kc, vc, li, lf = xs
        b = jnp.cumsum(lf, axis=-1)
        d_log = jnp.where(tril, b[..., :, None] - b[..., None, :] + li[..., None, :], -jnp.inf)
        inter = b + m[..., None]
        m_row = jnp.maximum(inter, jnp.max(d_log, axis=-1))
        s = jnp.einsum('bhjd,bhsd->bhjs', qc, kc) * jnp.exp(d_log - m_row[..., None])
        w_inter = jnp.exp(inter - m_row)
        num = jnp.einsum('bhjs,bhsv->bhjv', s, vc) + w_inter[..., None] * jnp.einsum('bhjd,bhdv->bhjv', qc, cm)
        den = jnp.sum(s, axis=-1) + w_inter * jnp.einsum('bhjd,bhd->bhj', qc, n)
        h = num / jnp.maximum(jnp.abs(den), jnp.exp(-m_row))[..., None]
        g = b[..., -1:] - b + li
        m_new = jnp.maximum(b[..., -1] + m, jnp.max(g, axis=-1))
        kw = kc * jnp.exp(g - m_new[..., None])[..., None]
        decay = jnp.exp(b[..., -1] + m - m_new)
        cm = decay[..., None, None] * cm + jnp.einsum('bhsd,bhsv->bhdv', kw, vc)
        n = decay[..., None] * n + jnp.sum(kw, axis=-2)
        return (cm, n, m_new), h

    xs = (_chunks(q), _chunks(k), _chunks(v), _chunks(log_i), _chunks(log_f))
    state, h = lax.scan(step, state, xs)
    return _unchunk(h), state


def _mlstm_dir(q, k, v, li, lf, state, reverse):
    if reverse:
        q, k, v, li, lf = (jnp.flip(t, axis=1) for t in (q, k, v, li, lf))
    h, state = _mlstm_chunkwise(q, k, v, li, lf, state)
    return (jnp.flip(h, axis=1) if reverse else h), state


def _mlstm_inputs(z):
    bsz, length = z.shape[:2]
    f32 = jnp.float32
    q = z[..., OFF_Q:OFF_K].astype(f32).reshape(bsz, length, ML_H, ML_DH)
    k = z[..., OFF_K:OFF_V].astype(f32).reshape(bsz, length, ML_H, ML_DH) * (ML_DH ** -0.5)
    v = z[..., OFF_V:OFF_O].astype(f32).reshape(bsz, length, ML_H, ML_DH)
    gates = z[..., OFF_G:IN_W].astype(f32).reshape(bsz, length, 4, ML_H)
    log_i = (gates[:, :, 0], gates[:, :, 2])
    log_f = (jax.nn.log_sigmoid(gates[:, :, 1]), jax.nn.log_sigmoid(gates[:, :, 3]))
    return q, k, v, log_i, log_f


def _mlstm_readout(h, z, norm_g):
    bsz, length = h.shape[:2]
    hc = h - jnp.mean(h, axis=-1, keepdims=True)
    var = jnp.mean(hc * hc, axis=-1, keepdims=True)
    h = (hc * lax.rsqrt(var + LN_EPS)).reshape(bsz, length, ML_W) * norm_g.astype(jnp.float32)
    o = jax.nn.sigmoid(z[..., OFF_O:OFF_G].astype(jnp.float32))
    return (o * h).astype(z.dtype)


def mlstm_mixer(zc, zx, norm_g, need_ctx):
    qc, kc, vc, lic, lfc = _mlstm_inputs(zc)
    qx, kx, vx, lix, lfx = _mlstm_inputs(zx)
    bsz = zx.shape[0]
    f32 = jnp.float32
    zero = (jnp.zeros((bsz, ML_H, ML_DH, ML_DH), f32), jnp.zeros((bsz, ML_H, ML_DH), f32),
            jnp.zeros((bsz, ML_H), f32))
    h_c_sum = 0.0
    h_x_sum = 0.0
    for di, reverse in enumerate((False, True)):
        h_c, st = _mlstm_dir(qc, kc, vc, lic[di], lfc[di], zero, reverse)
        h_x, _ = _mlstm_dir(qx, kx, vx, lix[di], lfx[di], st, reverse)
        h_c_sum = h_c_sum + h_c
        h_x_sum = h_x_sum + h_x
    out_x = _mlstm_readout(h_x_sum, zx, norm_g)
    out_c = _mlstm_readout(h_c_sum, zc, norm_g) if need_ctx else None
    return out_c, out_x


def setup_inputs(seed: int = 0) -> dict:
    key = jax.random.key(seed)
    ks = jax.random.split(key, 32)
    f32 = jnp.float32
    L = DEPTH

    def nrm(k, shape, s):
        return s * jax.random.normal(k, shape, f32)

    f_bias = jnp.linspace(3.0, 6.0, ML_H, dtype=f32)
    b_in = nrm(ks[8], (L, IN_W), 0.02)
    b_in = b_in.at[:, OFF_G + ML_H:OFF_G + 2 * ML_H].add(f_bias).at[:, OFF_G + 3 * ML_H:].add(f_bias)
    return {
        'x': nrm(ks[0], (BATCH, SEQ, D_MODEL), 1.0),
        'c': nrm(ks[1], (BATCH, D_MODEL), 1.0),
        'ctx': nrm(ks[2], (BATCH, CTX_LEN, D_MODEL), 1.0),
        'c_ctx': nrm(ks[3], (D_MODEL,), 1.0),
        'w_mod': nrm(ks[4], (L, D_MODEL, 6 * D_MODEL), 0.5 * D_MODEL ** -0.5),
        'b_mod': nrm(ks[5], (L, 6 * D_MODEL), 0.02),
        'norm1_g': 1.0 + nrm(ks[6], (L, D_MODEL), 0.02),
        'w_in': nrm(ks[7], (L, D_MODEL, IN_W), D_MODEL ** -0.5),
        'b_in': b_in,
        's5_lam_re': -0.5 + nrm(ks[9], (L, 2, S5_G, S5_P), 0.01),
        's5_lam_im': math.pi * jnp.arange(S5_P, dtype=f32) + nrm(ks[10], (L, 2, S5_G, S5_P), 0.01),
        's5_log_step': jax.random.uniform(ks[11], (L, 2, S5_G, S5_P), f32, math.log(1e-3), math.log(1e-1)),
        's5_b_re': nrm(ks[12], (L, 2, S5_G, S5_P, S5_CH), (2 * S5_CH) ** -0.5),
        's5_b_im': nrm(ks[13], (L, 2, S5_G, S5_P, S5_CH), (2 * S5_CH) ** -0.5),
        's5_c_re': nrm(ks[14], (L, 2, S5_G, S5_CH, S5_P), S5_P ** -0.5),
        's5_c_im': nrm(ks[15], (L, 2, S5_G, S5_CH, S5_P), S5_P ** -0.5),
        's5_d': nrm(ks[16], (L, S5_W), 1.0),
        's5_w_glu': nrm(ks[17], (L, S5_W, S5_W), S5_W ** -0.5),
        's5_b_glu': nrm(ks[18], (L, S5_W), 0.02),
        'conv_dw_w': nrm(ks[19], (L, CONV_K, CONV_W), CONV_K ** -0.5),
        'conv_dw_b': nrm(ks[20], (L, CONV_W), 0.02),
        'conv_ln_g': 1.0 + nrm(ks[21], (L, CONV_W), 0.02),
        'conv_ln_b': nrm(ks[22], (L, CONV_W), 0.02),
        'ml_norm_g': 1.0 + nrm(ks[23], (L, ML_W), 0.02),
        'w_out': nrm(ks[24], (L, MIX_W, D_MODEL), MIX_W ** -0.5),
        'norm2_g': 1.0 + nrm(ks[25], (L, D_MODEL), 0.02),
        'w_ffn_in': nrm(ks[26], (L, D_MODEL, 2 * D_FF), D_MODEL ** -0.5),
        'w_ffn_out': nrm(ks[27], (L, D_FF, D_MODEL), D_FF ** -0.5),
        'norm_f_g': 1.0 + nrm(ks[28], (D_MODEL,), 0.02),
    }


def reference(x, c, ctx, c_ctx, w_mod, b_mod, norm1_g, w_in, b_in, s5_lam_re, s5_lam_im,
              s5_log_step, s5_b_re, s5_b_im, s5_c_re, s5_c_im, s5_d, s5_w_glu, s5_b_glu,
              conv_dw_w, conv_dw_b, conv_ln_g, conv_ln_b, ml_norm_g, w_out, norm2_g,
              w_ffn_in, w_ffn_out, norm_f_g):
    rows = x.shape[1] // GRID_W
    for l in range(DEPTH):
        need_ctx = l < DEPTH - 1
        sh1_x, sc1_x, g1_x, sh2_x, sc2_x, g2_x = _adaln(c, w_mod[l], b_mod[l])
        sh1_c, sc1_c, g1_c, sh2_c, sc2_c, g2_c = _adaln(c_ctx, w_mod[l], b_mod[l])

        hx = _rmsnorm(x, norm1_g[l]) * (1.0 + sc1_x) + sh1_x
        hc = _rmsnorm(ctx, norm1_g[l]) * (1.0 + sc1_c) + sh1_c
        zx = hx @ w_in[l] + b_in[l]
        zc = hc @ w_in[l] + b_in[l]
        s5_c, s5_x = s5_mixer(zc[..., OFF_S5:OFF_CONV], zx[..., OFF_S5:OFF_CONV],
                              s5_lam_re[l], s5_lam_im[l], s5_log_step[l], s5_b_re[l], s5_b_im[l],
                              s5_c_re[l], s5_c_im[l], s5_d[l], s5_w_glu[l], s5_b_glu[l], need_ctx)
        ml_c, ml_x = mlstm_mixer(zc, zx, ml_norm_g[l], need_ctx)
        cv_x = conv_module(zx[..., OFF_CONV:OFF_Q], conv_dw_w[l], conv_dw_b[l],
                           conv_ln_g[l], conv_ln_b[l], rows)
        x = x + g1_x * (jnp.concatenate([s5_x, cv_x, ml_x], axis=-1) @ w_out[l])

        hx2 = _rmsnorm(x, norm2_g[l]) * (1.0 + sc2_x) + sh2_x
        x = x + g2_x * _swiglu(hx2, w_ffn_in[l], w_ffn_out[l])

        if need_ctx:
            cv_c = conv_module(zc[..., OFF_CONV:OFF_Q], conv_dw_w[l], conv_dw_b[l],
                               conv_ln_g[l], conv_ln_b[l], 1)
            ctx = ctx + g1_c * (jnp.concatenate([s5_c, cv_c, ml_c], axis=-1) @ w_out[l])
            hc2 = _rmsnorm(ctx, norm2_g[l]) * (1.0 + sc2_c) + sh2_c
            ctx = ctx + g2_c * _swiglu(hc2, w_ffn_in[l], w_ffn_out[l])
    return _rmsnorm(x, norm_f_g)
```

```python
import functools

import jax
import jax.numpy as jnp
from jax import lax
from jax.experimental import pallas as pl
from jax.experimental.pallas import tpu as pltpu

F32 = jnp.float32
BF16 = jnp.bfloat16

D_MODEL = 2048
SEQ = 8192
CTX = 256
NTOK = SEQ + CTX
DEPTH = 4
GRID_W = 64

S5_W = 512
S5_CH = 16
S5_G = 32
S5_P = 64
CONV_W = 512
CONV_K = 31
ML_W = 1024
ML_H = 4
ML_DH = 256
D_FF = 5632
EPS = 1e-6
LN_EPS = 1e-5

OFF_CONV = 512
OFF_Q = 1536
OFF_K = 2560
OFF_V = 3584
OFF_O = 4608
OFF_G = 5632

LANES = 128
SUBLANES = 8
VMEM_LIMIT = 56 * 1024 * 1024

ROW_TILE = 256
MM_TILE_M = 768
MM_TILE_N = 512

S5_T = 8
S5_ROWS = NTOK // S5_T
S5_CTX_ROWS = CTX // S5_T
S5_SETS = 4
S5_SET_CH = S5_W // S5_SETS
S5_SET_ST = (S5_G // S5_SETS) * S5_P
S5_FLAT = S5_T * S5_SET_CH

ML_CHUNK = 256
ML_NCHUNK = NTOK // ML_CHUNK


def _cparams(sem, vmem=VMEM_LIMIT):
    return pltpu.CompilerParams(dimension_semantics=sem, vmem_limit_bytes=vmem)


def _dot(a, b):
    return jnp.dot(a.astype(BF16), b.astype(BF16), preferred_element_type=F32)


def _split3(x):
    a = x.astype(BF16)
    r = x - a.astype(F32)
    b = r.astype(BF16)
    c = (r - b.astype(F32)).astype(BF16)
    return a, b, c


def _mod_body(cc_ref, w_ref, b_ref, o_ref):
    s = jax.nn.silu(cc_ref[...])
    o_ref[0] = _dot(s, w_ref[0]) + b_ref[0]


def _modulation(cc, w_mod, b_mod):
    depth, _, n = w_mod.shape
    tn = 1024
    return pl.pallas_call(
        _mod_body,
        grid=(depth, n // tn),
        in_specs=[
            pl.BlockSpec((SUBLANES, D_MODEL), lambda l, j: (0, 0)),
            pl.BlockSpec((1, D_MODEL, tn), lambda l, j: (l, 0, j)),
            pl.BlockSpec((1, 1, tn), lambda l, j: (l, 0, j)),
        ],
        out_specs=pl.BlockSpec((1, SUBLANES, tn), lambda l, j: (l, 0, j)),
        out_shape=jax.ShapeDtypeStruct((depth, SUBLANES, n), F32),
        compiler_params=_cparams(("arbitrary", "arbitrary")),
        name="adaln_modulation",
    )(cc, w_mod, b_mod.reshape(depth, 1, n))


def _mod_row(m_ref, is_ctx):
    return jnp.where(is_ctx, m_ref[0, 1:2, :], m_ref[0, 0:1, :])


def _norm_body(*refs, modulate, with_gates):
    it = iter(refs)
    x_ref, g_ref = next(it), next(it)
    sh_ref = sc_ref = wg_ref = bg_ref = gate_ref = None
    if modulate:
        sh_ref, sc_ref = next(it), next(it)
    if with_gates:
        wg_ref, bg_ref = next(it), next(it)
    h_ref = next(it)
    if with_gates:
        gate_ref = next(it)

    xf = x_ref[...]
    ms = jnp.mean(xf * xf, axis=-1, keepdims=True)
    h = xf * lax.rsqrt(ms + EPS) * g_ref[...]
    if modulate:
        is_ctx = pl.program_id(0) == 0
        h = h * (1.0 + _mod_row(sc_ref, is_ctx)) + _mod_row(sh_ref, is_ctx)
    h_ref[...] = h.astype(h_ref.dtype)
    if with_gates:
        hi = h.astype(BF16)
        lo = (h - hi.astype(F32)).astype(BF16)
        w = wg_ref[...]
        whi = w.astype(BF16)
        wlo = (w - whi.astype(F32)).astype(BF16)
        acc = jnp.dot(hi, whi, preferred_element_type=F32)
        acc += jnp.dot(hi, wlo, preferred_element_type=F32)
        acc += jnp.dot(lo, whi, preferred_element_type=F32)
        gate_ref[...] = acc + bg_ref[...]


def _norm_mod(xs, g, mod_all, layer, phase, w_gate=None, b_gate=None):
    with_gates = w_gate is not None
    n_tiles = NTOK // ROW_TILE
    in_specs = [
        pl.BlockSpec((ROW_TILE, D_MODEL), lambda i: (i, 0)),
        pl.BlockSpec((1, D_MODEL), lambda i: (0, 0)),
        pl.BlockSpec((1, SUBLANES, D_MODEL), lambda i: (layer, 0, 3 * phase)),
        pl.BlockSpec((1, SUBLANES, D_MODEL), lambda i: (layer, 0, 3 * phase + 1)),
    ]
    args = [xs, g.reshape(1, D_MODEL), mod_all, mod_all]
    out_specs = [pl.BlockSpec((ROW_TILE, D_MODEL), lambda i: (i, 0))]
    out_shape = [jax.ShapeDtypeStruct((NTOK, D_MODEL), BF16)]
    if with_gates:
        in_specs += [pl.BlockSpec((D_MODEL, LANES), lambda i: (0, 0)),
                     pl.BlockSpec((1, LANES), lambda i: (0, 0))]
        args += [w_gate, b_gate]
        out_specs.append(pl.BlockSpec((ROW_TILE, LANES), lambda i: (i, 0)))
        out_shape.append(jax.ShapeDtypeStruct((NTOK, LANES), F32))
    return pl.pallas_call(
        functools.partial(_norm_body, modulate=True, with_gates=with_gates),
        grid=(n_tiles,),
        in_specs=in_specs,
        out_specs=out_specs,
        out_shape=out_shape,
        compiler_params=_cparams(("arbitrary",)),
        name="rmsnorm_modulate",
    )(*args)


def _final_norm(xs, g):
    skip = CTX // ROW_TILE
    return pl.pallas_call(
        functools.partial(_norm_body, modulate=False, with_gates=False),
        grid=(SEQ // ROW_TILE,),
        in_specs=[pl.BlockSpec((ROW_TILE, D_MODEL), lambda i: (i + skip, 0)),
                  pl.BlockSpec((1, D_MODEL), lambda i: (0, 0))],
        out_specs=[pl.BlockSpec((ROW_TILE, D_MODEL), lambda i: (i, 0))],
        out_shape=[jax.ShapeDtypeStruct((SEQ, D_MODEL), F32)],
        compiler_params=_cparams(("arbitrary",)),
        name="final_rmsnorm",
    )(xs, g.reshape(1, D_MODEL))[0]


def _in_proj_body(a_ref, w_ref, b_ref, o_ref, wbf_ref):
    @pl.when(pl.program_id(1) == 0)
    def _():
        wbf_ref[...] = w_ref[...].astype(BF16)

    o_ref[...] = jnp.dot(a_ref[...], wbf_ref[...], preferred_element_type=F32) + b_ref[...]


def _in_proj(h, w_in, b_in, layer):
    tm, tn = MM_TILE_M, MM_TILE_N
    in_w = w_in.shape[2]
    return pl.pallas_call(
        _in_proj_body,
        grid=(OFF_G // tn, NTOK // tm),
        in_specs=[
            pl.BlockSpec((tm, D_MODEL), lambda j, i: (i, 0)),
            pl.BlockSpec((None, D_MODEL, tn), lambda j, i: (layer, 0, j)),
            pl.BlockSpec((None, 1, tn), lambda j, i: (layer, 0, j)),
        ],
        out_specs=pl.BlockSpec((tm, tn), lambda j, i: (i, j)),
        out_shape=jax.ShapeDtypeStruct((NTOK, OFF_G), F32),
        scratch_shapes=[pltpu.VMEM((D_MODEL, tn), BF16)],
        compiler_params=_cparams(("arbitrary", "arbitrary")),
        name="in_proj",
    )(h, w_in, b_in.reshape(DEPTH, 1, in_w))


def _row_gate(g_ref, i, tm, tn):
    rows = i * tm + lax.broadcasted_iota(jnp.int32, (tm, tn), 0)
    return jnp.where(rows < CTX, g_ref[0, 1:2, :], g_ref[0, 0:1, :])


def _out_proj_body(s5_ref, cv_ref, ml_ref, w_ref, g_ref, x_ref, o_ref, wbf_ref):
    i = pl.program_id(1)

    @pl.when(i == 0)
    def _():
        wbf_ref[...] = w_ref[...].astype(BF16)

    acc = jnp.dot(s5_ref[...], wbf_ref[0:S5_W, :], preferred_element_type=F32)
    acc += jnp.dot(cv_ref[...], wbf_ref[S5_W:S5_W + CONV_W, :], preferred_element_type=F32)
    acc += jnp.dot(ml_ref[...], wbf_ref[S5_W + CONV_W:, :], preferred_element_type=F32)
    tm, tn = o_ref.shape
    o_ref[...] = x_ref[...] + _row_gate(g_ref, i, tm, tn) * acc


def _out_proj(xs, s5o, cvo, mlo, w_out, mod_all, layer):
    tm, tn = MM_TILE_M, MM_TILE_N
    return pl.pallas_call(
        _out_proj_body,
        grid=(D_MODEL // tn, NTOK // tm),
        in_specs=[
            pl.BlockSpec((tm, S5_W), lambda j, i: (i, 0)),
            pl.BlockSpec((tm, CONV_W), lambda j, i: (i, 0)),
            pl.BlockSpec((tm, ML_W), lambda j, i: (i, 0)),
            pl.BlockSpec((None, D_MODEL, tn), lambda j, i: (layer, 0, j)),
            pl.BlockSpec((1, SUBLANES, tn), lambda j, i: (layer, 0, 2 * (D_MODEL // tn) + j)),
            pl.BlockSpec((tm, tn), lambda j, i: (i, j)),
        ],
        out_specs=pl.BlockSpec((tm, tn), lambda j, i: (i, j)),
        out_shape=jax.ShapeDtypeStruct((NTOK, D_MODEL), F32),
        scratch_shapes=[pltpu.VMEM((D_MODEL, tn), BF16)],
        compiler_params=_cparams(("arbitrary", "arbitrary")),
        name="out_proj_residual",
    )(s5o, cvo, mlo, w_out, mod_all, xs)


def _ffn_in_body(a_ref, wg_ref, wu_ref, o_ref, wgbf_ref, wubf_ref):
    @pl.when(pl.program_id(1) == 0)
    def _():
        wgbf_ref[...] = wg_ref[...].astype(BF16)
        wubf_ref[...] = wu_ref[...].astype(BF16)

    a = a_ref[...]
    g = jnp.dot(a, wgbf_ref[...], preferred_element_type=F32)
    u = jnp.dot(a, wubf_ref[...], preferred_element_type=F32)
    o_ref[...] = (jax.nn.silu(g) * u).astype(BF16)


def _ffn_in(h, w_ffn_in, layer):
    tm, tn = MM_TILE_M, MM_TILE_N
    nj = D_FF // tn
    return pl.pallas_call(
        _ffn_in_body,
        grid=(nj, NTOK // tm),
        in_specs=[
            pl.BlockSpec((tm, D_MODEL), lambda j, i: (i, 0)),
            pl.BlockSpec((None, D_MODEL, tn), lambda j, i: (layer, 0, j)),
            pl.BlockSpec((None, D_MODEL, tn), lambda j, i: (layer, 0, nj + j)),
        ],
        out_specs=pl.BlockSpec((tm, tn), lambda j, i: (i, j)),
        out_shape=jax.ShapeDtypeStruct((NTOK, D_FF), BF16),
        scratch_shapes=[pltpu.VMEM((D_MODEL, tn), BF16), pltpu.VMEM((D_MODEL, tn), BF16)],
        compiler_params=_cparams(("arbitrary", "arbitrary")),
        name="ffn_in_swiglu",
    )(h, w_ffn_in, w_ffn_in)


def _ffn_out_body(a_ref, w_ref, g_ref, x_ref, o_ref, wbf_ref):
    i = pl.program_id(1)

    @pl.when(i == 0)
    def _():
        wbf_ref[...] = w_ref[...].astype(BF16)

    acc = jnp.dot(a_ref[...], wbf_ref[...], preferred_element_type=F32)
    tm, tn = o_ref.shape
    o_ref[...] = x_ref[...] + _row_gate(g_ref, i, tm, tn) * acc


def _ffn_out(xs, hid, w_ffn_out, mod_all, layer):
    tm, tn = MM_TILE_M // 2, MM_TILE_N
    return pl.pallas_call(
        _ffn_out_body,
        grid=(D_MODEL // tn, NTOK // tm),
        in_specs=[
            pl.BlockSpec((tm, D_FF), lambda j, i: (i, 0)),
            pl.BlockSpec((None, D_FF, tn), lambda j, i: (layer, 0, j)),
            pl.BlockSpec((1, SUBLANES, tn), lambda j, i: (layer, 0, 5 * (D_MODEL // tn) + j)),
            pl.BlockSpec((tm, tn), lambda j, i: (i, j)),
        ],
        out_specs=pl.BlockSpec((tm, tn), lambda j, i: (i, j)),
        out_shape=jax.ShapeDtypeStruct((NTOK, D_MODEL), F32),
        scratch_shapes=[pltpu.VMEM((D_FF, tn), BF16)],
        compiler_params=_cparams(("arbitrary", "arbitrary")),
        name="ffn_out_residual",
    )(hid, w_ffn_out, mod_all, xs)


def _s5_operators(lam_re, lam_im, log_step, b_re, b_im, c_re, c_im):
    lam = lax.complex(lam_re.astype(F32), lam_im.astype(F32))
    lam_bar = jnp.exp(lam * jnp.exp(log_step.astype(F32)))
    b_bar = ((lam_bar - 1.0) / lam)[..., None] * lax.complex(b_re.astype(F32), b_im.astype(F32))
    c_mat = lax.complex(c_re.astype(F32), c_im.astype(F32))

    def powers(base, count):
        out = [jnp.ones_like(base)]
        for _ in range(count - 1):
            out.append(out[-1] * base)
        return out

    pw = powers(lam_bar, S5_T + 1)
    pa = powers(pw[S5_T], SUBLANES + 1)
    take = lambda seq, idx: jnp.stack([seq[i] for i in idx])
    t_up = list(range(S5_T))
    t_down = t_up[::-1]

    gsets = (S5_SETS, S5_G // S5_SETS)
    eye = jnp.eye(S5_G // S5_SETS, dtype=F32)
    t_idx = jnp.arange(S5_T)

    kern = jnp.real(jnp.einsum('dgcp,kdgp,dgpa->kdgca', c_mat, take(pw, t_up), b_bar))
    diff = t_idx[None, :] - t_idx[:, None]
    kf = jnp.where((diff >= 0)[:, :, None, None, None], kern[jnp.clip(diff, 0), 0], 0.0)
    kb = jnp.where((diff <= 0)[:, :, None, None, None], kern[jnp.clip(-diff, 0), 1], 0.0)
    kdir = jnp.stack([kf, kb]).reshape((2, S5_T, S5_T) + gsets + (S5_CH, S5_CH))
    m_mat = jnp.einsum('dstjgca,gh->djsgathc', kdir, eye).reshape(2, S5_SETS, S5_FLAT, S5_FLAT)

    def wb_dir(d, pows):
        w = pows[:, d][..., None] * b_bar[d][None]
        w = w.reshape((S5_T,) + gsets + (S5_P, S5_CH))
        parts = [jnp.einsum('tjgpa,gh->jtgahp', part, eye) for part in (jnp.real(w), jnp.imag(w))]
        return jnp.stack(parts, axis=4).reshape(S5_SETS, S5_FLAT, 2 * S5_SET_ST)

    wb = jnp.stack([wb_dir(0, take(pw, t_down)), wb_dir(1, take(pw, t_up))])

    def wc_dir(d, pows):
        w = c_mat[d][None] * pows[:, d][:, :, None, :]
        w = w.reshape((S5_T,) + gsets + (S5_CH, S5_P))
        parts = [jnp.einsum('tjgcp,gh->jgpthc', part, eye) for part in (jnp.real(w), -jnp.imag(w))]
        return jnp.stack(parts, axis=1).reshape(S5_SETS, 2 * S5_SET_ST, S5_FLAT)

    wc = jnp.stack([wc_dir(0, take(pw, [t + 1 for t in t_up])),
                    wc_dir(1, take(pw, [S5_T - t for t in t_up]))])

    def per_set(v):
        v = v.reshape((v.shape[0], 2, S5_SETS, S5_SET_ST))
        v = jnp.concatenate([jnp.real(v), jnp.imag(v)], axis=-1)
        return jnp.transpose(v, (1, 2, 0, 3))

    amul = per_set(jnp.stack([pa[1], pa[2], pa[4], pa[8]] + [jnp.zeros_like(pa[0])] * 4))
    apow_f = per_set(take(pa, t_up))
    apow_b = per_set(take(pa, t_down))
    apow = jnp.stack([apow_f[0], apow_b[1]])
    return m_mat.astype(BF16), wb.astype(BF16), wc.astype(BF16), amul, apow


def _s5_row_scan(e_ref, amul, apow, block_lo, block_hi, reverse, carry):
    half = S5_SET_ST
    rows = lax.broadcasted_iota(jnp.int32, (SUBLANES, half), 0)
    mults = [(amul[k:k + 1, :half], amul[k:k + 1, half:]) for k in range(4)]
    p_re, p_im = apow[:, :half], apow[:, half:]

    def shifted(x, k):
        if reverse:
            return jnp.where(rows < SUBLANES - k, pltpu.roll(x, SUBLANES - k, 0), 0.0)
        return jnp.where(rows >= k, pltpu.roll(x, k, 0), 0.0)

    def body(step, carry):
        c_re, c_im = carry
        blk = (block_hi - 1 - step) if reverse else (block_lo + step)
        r0 = pl.multiple_of(blk * SUBLANES, SUBLANES)
        s_re = e_ref[pl.ds(r0, SUBLANES), 0:half]
        s_im = e_ref[pl.ds(r0, SUBLANES), half:2 * half]
        for k, (a_re, a_im) in zip((1, 2, 4), mults[:3]):
            t_re, t_im = shifted(s_re, k), shifted(s_im, k)
            s_re, s_im = (s_re + t_re * a_re - t_im * a_im, s_im + t_re * a_im + t_im * a_re)
        x_re, x_im = shifted(s_re, 1), shifted(s_im, 1)
        e_ref[pl.ds(r0, SUBLANES), 0:half] = x_re + p_re * c_re - p_im * c_im
        e_ref[pl.ds(r0, SUBLANES), half:2 * half] = x_im + p_re * c_im + p_im * c_re
        last = 0 if reverse else SUBLANES - 1
        a8_re, a8_im = mults[3]
        n_re = s_re[last:last + 1, :] + a8_re * c_re - a8_im * c_im
        n_im = s_im[last:last + 1, :] + a8_re * c_im + a8_im * c_re
        return n_re, n_im

    return lax.fori_loop(0, block_hi - block_lo, body, carry)


def _s5_body(u_ref, m_ref, wb_ref, wc_ref, amul_ref, apow_ref, d_ref, y_ref, e_ref):
    direction = pl.program_id(1)
    u = u_ref[0]
    ub = u.astype(BF16)
    e_ref[...] = jnp.dot(ub, wb_ref[0, 0], preferred_element_type=F32)
    zero = (jnp.zeros((1, S5_SET_ST), F32), jnp.zeros((1, S5_SET_ST), F32))
    n_blocks = S5_ROWS // SUBLANES
    ctx_blocks = S5_CTX_ROWS // SUBLANES
    amul = amul_ref[0, 0]
    apow = apow_ref[0, 0]

    @pl.when(direction == 0)
    def _():
        _s5_row_scan(e_ref, amul, apow, 0, n_blocks, False, zero)

    @pl.when(direction == 1)
    def _():
        carry = _s5_row_scan(e_ref, amul, apow, 0, ctx_blocks, True, zero)
        _s5_row_scan(e_ref, amul, apow, ctx_blocks, n_blocks, True, carry)

    y = jnp.dot(ub, m_ref[0, 0], preferred_element_type=F32)
    y += jnp.dot(e_ref[...].astype(BF16), wc_ref[0, 0], preferred_element_type=F32)

    @pl.when(direction == 0)
    def _():
        y_ref[0] = y + d_ref[0] * u

    @pl.when(direction == 1)
    def _():
        y_ref[0] += y


def _s5_glu_body(y_ref, w_ref, b_ref, o_ref):
    w = w_ref[...].astype(BF16)
    for t in range(S5_T):
        cols = slice(t * S5_SET_CH, (t + 1) * S5_SET_CH)
        y = jnp.concatenate([y_ref[j, :, cols] for j in range(S5_SETS)], axis=-1)
        g = jax.nn.gelu(y)
        o = g * jax.nn.sigmoid(jnp.dot(g.astype(BF16), w, preferred_element_type=F32) + b_ref[...])
        o_ref[:, t * S5_W:(t + 1) * S5_W] = o.astype(BF16)


def _s5_mixer(z, ops, d_skip, w_glu, b_glu):
    m_mat, wb, wc, amul, apow = ops
    u = z[:, :S5_W].reshape(S5_ROWS, S5_T, S5_SETS, S5_SET_CH)
    u = jnp.transpose(u, (2, 0, 1, 3)).reshape(S5_SETS, S5_ROWS, S5_FLAT)
    d_flat = jnp.tile(d_skip.astype(F32).reshape(S5_SETS, 1, 1, S5_SET_CH), (1, 1, S5_T, 1))
    d_flat = d_flat.reshape(S5_SETS, 1, S5_FLAT)
    y = pl.pallas_call(
        _s5_body,
        grid=(S5_SETS, 2),
        in_specs=[
            pl.BlockSpec((1, S5_ROWS, S5_FLAT), lambda j, d: (j, 0, 0)),
            pl.BlockSpec((1, 1, S5_FLAT, S5_FLAT), lambda j, d: (d, j, 0, 0)),
            pl.BlockSpec((1, 1, S5_FLAT, S5_FLAT), lambda j, d: (d, j, 0, 0)),
            pl.BlockSpec((1, 1, S5_FLAT, S5_FLAT), lambda j, d: (d, j, 0, 0)),
            pl.BlockSpec((1, 1, SUBLANES, S5_FLAT), lambda j, d: (d, j, 0, 0)),
            pl.BlockSpec((1, 1, SUBLANES, S5_FLAT), lambda j, d: (d, j, 0, 0)),
            pl.BlockSpec((1, 1, S5_FLAT), lambda j, d: (j, 0, 0)),
        ],
        out_specs=pl.BlockSpec((1, S5_ROWS, S5_FLAT), lambda j, d: (j, 0, 0)),
        out_shape=jax.ShapeDtypeStruct((S5_SETS, S5_ROWS, S5_FLAT), F32),
        scratch_shapes=[pltpu.VMEM((S5_ROWS, S5_FLAT), F32)],
        compiler_params=_cparams(("arbitrary", "arbitrary")),
        name="s5_scan",
    )(u, m_mat, wb, wc, amul, apow, d_flat)

    rt = S5_ROWS // 4
    out = pl.pallas_call(
        _s5_glu_body,
        grid=(S5_ROWS // rt,),
        in_specs=[
            pl.BlockSpec((S5_SETS, rt, S5_FLAT), lambda i: (0, i, 0)),
            pl.BlockSpec((S5_W, S5_W), lambda i: (0, 0)),
            pl.BlockSpec((1, S5_W), lambda i: (0, 0)),
        ],
        out_specs=pl.BlockSpec((rt, S5_T * S5_W), lambda i: (i, 0)),
        out_shape=jax.ShapeDtypeStruct((S5_ROWS, S5_T * S5_W), BF16),
        compiler_params=_cparams(("arbitrary",)),
        name="s5_glu",
    )(y, w_glu, b_glu.reshape(1, S5_W))
    return out.reshape(NTOK, S5_W)


CONV_PAD = 16
CONV_BLK = 64


def _conv_body(a_ref, b_ref, w_ref, db_ref, lg_ref, lb_ref, o_ref, pad_ref):
    g = a_ref[...] * jax.nn.sigmoid(b_ref[...])

    def run(seq_len):
        n_seq = ROW_TILE // seq_len
        pitch = seq_len + 2 * CONV_PAD
        zeros = jnp.zeros((CONV_PAD, CONV_W), F32)
        for s in range(n_seq):
            pad_ref[s * pitch:s * pitch + CONV_PAD, :] = zeros
            pad_ref[s * pitch + CONV_PAD:s * pitch + CONV_PAD + seq_len, :] = g[s * seq_len:(s + 1) * seq_len]
            pad_ref[s * pitch + CONV_PAD + seq_len:(s + 1) * pitch, :] = zeros
        for blk in range(ROW_TILE // CONV_BLK):
            row0 = blk * CONV_BLK
            s, q = divmod(row0, seq_len)
            base = s * pitch + CONV_PAD + q - CONV_K // 2
            acc = jnp.zeros((CONV_BLK, CONV_W), F32)
            for k in range(CONV_K):
                acc = acc + w_ref[k:k + 1, :] * pad_ref[base + k:base + k + CONV_BLK, :]
            y = acc + db_ref[...]
            yc = y - jnp.mean(y, axis=-1, keepdims=True)
            var = jnp.mean(yc * yc, axis=-1, keepdims=True)
            y = yc * lax.rsqrt(var + LN_EPS) * lg_ref[...] + lb_ref[...]
            o_ref[row0:row0 + CONV_BLK, :] = jax.nn.silu(y).astype(BF16)

    is_ctx = pl.program_id(0) == 0

    @pl.when(is_ctx)
    def _():
        run(CTX)

    @pl.when(jnp.logical_not(is_ctx))
    def _():
        run(GRID_W)


def _conv_module(z, dw_w, dw_b, ln_g, ln_b):
    w = jnp.concatenate([dw_w, jnp.zeros((1, CONV_W), dw_w.dtype)], axis=0)
    vec = lambda v: v.reshape(1, CONV_W)
    pad_rows = max(CTX + 2 * CONV_PAD, (ROW_TILE // GRID_W) * (GRID_W + 2 * CONV_PAD))
    return pl.pallas_call(
        _conv_body,
        grid=(NTOK // ROW_TILE,),
        in_specs=[
            pl.BlockSpec((ROW_TILE, CONV_W), lambda i: (i, OFF_CONV // CONV_W)),
            pl.BlockSpec((ROW_TILE, CONV_W), lambda i: (i, OFF_CONV // CONV_W + 1)),
            pl.BlockSpec((CONV_K + 1, CONV_W), lambda i: (0, 0)),
            pl.BlockSpec((1, CONV_W), lambda i: (0, 0)),
            pl.BlockSpec((1, CONV_W), lambda i: (0, 0)),
            pl.BlockSpec((1, CONV_W), lambda i: (0, 0)),
        ],
        out_specs=pl.BlockSpec((ROW_TILE, CONV_W), lambda i: (i, 0)),
        out_shape=jax.ShapeDtypeStruct((NTOK, CONV_W), BF16),
        scratch_shapes=[pltpu.VMEM((pad_rows, CONV_W), F32)],
        compiler_params=_cparams(("arbitrary",)),
        name="conv_module",
    )(z, z, w, vec(dw_b), vec(ln_g), vec(ln_b))


def _ml_chunk_index(ci, reverse):
    if not reverse:
        return ci
    return jnp.where(ci == 0, 0, ML_NCHUNK - ci)


def _ml_direction(q, k, v, li_col, li_row, b_col, b_row, cm_ref, n_ref, m_ref, reverse):
    c = ML_CHUNK
    t_idx = lax.broadcasted_iota(jnp.int32, (c, c), 0)
    s_idx = lax.broadcasted_iota(jnp.int32, (c, c), 1)
    mask = (s_idx >= t_idx) if reverse else (s_idx <= t_idx)
    m = m_ref[:, 0:1]
    cm = cm_ref[...]
    n = n_ref[...]

    d_log = jnp.where(mask, b_col - b_row + li_row, -jnp.inf)
    inter = b_col + m
    m_row = jnp.maximum(inter, jnp.max(d_log, axis=-1, keepdims=True))
    qb = q.astype(BF16)
    kb = k.astype(BF16)
    vb = v.astype(BF16)
    qk = lax.dot_general(qb, kb, (((1,), (1,)), ((), ())), preferred_element_type=F32)
    s = qk * jnp.exp(d_log - m_row)
    w_inter = jnp.exp(inter - m_row)
    num = jnp.dot(s.astype(BF16), vb, preferred_element_type=F32)
    num += w_inter * jnp.dot(qb, cm.astype(BF16), preferred_element_type=F32)
    den = jnp.sum(s, axis=-1, keepdims=True) + w_inter * jnp.sum(q * n, axis=-1, keepdims=True)
    h = num / jnp.maximum(jnp.abs(den), jnp.exp(-m_row))

    b_tot = b_row[:, 0:1] if reverse else b_row[:, c - 1:c]
    g = b_tot - b_col + li_col
    m_new = jnp.maximum(b_tot + m, jnp.max(g, axis=0, keepdims=True))
    kw = k * jnp.exp(g - m_new)
    decay = jnp.exp(b_tot + m - m_new)
    cm_ref[...] = decay * cm + lax.dot_general(kw.astype(BF16), vb, (((0,), (0,)), ((), ())),
                                               preferred_element_type=F32)
    n_ref[...] = decay * n + jnp.sum(kw, axis=0, keepdims=True)
    m_ref[...] = jnp.broadcast_to(m_new, m_ref.shape)
    return h


def _mlstm_body(qf_ref, kf_ref, vf_ref, gf_ref, gtf_ref, qb_ref, kb_ref, vb_ref, gb_ref, gtb_ref,
                hf_ref, hb_ref, cm_ref, n_ref, m_ref):
    head = pl.program_id(0)

    @pl.when(pl.program_id(1) == 0)
    def _():
        cm_ref[...] = jnp.zeros_like(cm_ref)
        n_ref[...] = jnp.zeros_like(n_ref)
        m_ref[...] = jnp.zeros_like(m_ref)

    c = ML_CHUNK
    r_idx = lax.broadcasted_iota(jnp.int32, (c, c), 0)
    c_idx = lax.broadcasted_iota(jnp.int32, (c, c), 1)
    lane = lax.broadcasted_iota(jnp.int32, (c, LANES), 1)
    sub = lax.broadcasted_iota(jnp.int32, (2 * SUBLANES, c), 0)

    for d, (q_ref, k_ref, v_ref, g_ref, gt_ref, h_ref) in enumerate(
            ((qf_ref, kf_ref, vf_ref, gf_ref, gtf_ref, hf_ref),
             (qb_ref, kb_ref, vb_ref, gb_ref, gtb_ref, hb_ref))):
        reverse = d == 1
        gates = g_ref[...]
        gates_t = gt_ref[...]
        incl = (c_idx >= r_idx) if reverse else (c_idx <= r_idx)
        tri = jnp.where(incl, 1.0, 0.0).astype(BF16)
        incl_t = (r_idx >= c_idx) if reverse else (r_idx <= c_idx)
        tri_t = jnp.where(incl_t, 1.0, 0.0).astype(BF16)
        lf = jax.nn.log_sigmoid(gates)
        lf_t = jax.nn.log_sigmoid(gates_t)
        b_all = sum(jnp.dot(tri, p, preferred_element_type=F32) for p in _split3(lf))
        b_all_t = sum(jnp.dot(p, tri_t, preferred_element_type=F32) for p in _split3(lf_t))
        i_col = 2 * ML_H * d + head
        f_col = i_col + ML_H
        pick_col = lambda a, idx: jnp.sum(jnp.where(lane == idx, a, 0.0), axis=-1, keepdims=True)
        pick_row = lambda a, idx: jnp.sum(jnp.where(sub == idx, a, 0.0), axis=0, keepdims=True)
        h = _ml_direction(
            q_ref[...], k_ref[...] * (ML_DH ** -0.5), v_ref[...],
            pick_col(gates, i_col), pick_row(gates_t, i_col),
            pick_col(b_all, f_col), pick_row(b_all_t, f_col),
            cm_ref.at[d], n_ref.at[d], m_ref.at[d], reverse)
        h_ref[...] = h


def _mlstm_readout_body(hf_ref, hb_ref, o_ref, g_ref, out_ref):
    h = hf_ref[...] + hb_ref[...]
    hc = h - jnp.mean(h, axis=-1, keepdims=True)
    var = jnp.mean(hc * hc, axis=-1, keepdims=True)
    h = hc * lax.rsqrt(var + LN_EPS) * g_ref[...]
    out_ref[...] = (jax.nn.sigmoid(o_ref[...]) * h).astype(BF16)


def _mlstm_mixer(z, gates, norm_g):
    gates_t = jnp.transpose(gates[:, :2 * SUBLANES])
    c = ML_CHUNK
    qcol, kcol, vcol = OFF_Q // ML_DH, OFF_K // ML_DH, OFF_V // ML_DH

    def specs(reverse):
        row = lambda ci: _ml_chunk_index(ci, reverse)
        return [
            pl.BlockSpec((c, ML_DH), lambda hh, ci: (row(ci), qcol + hh)),
            pl.BlockSpec((c, ML_DH), lambda hh, ci: (row(ci), kcol + hh)),
            pl.BlockSpec((c, ML_DH), lambda hh, ci: (row(ci), vcol + hh)),
            pl.BlockSpec((c, LANES), lambda hh, ci: (row(ci), 0)),
            pl.BlockSpec((2 * SUBLANES, c), lambda hh, ci: (0, row(ci))),
        ]

    out_spec = lambda reverse: pl.BlockSpec(
        (c, ML_DH), lambda hh, ci: (_ml_chunk_index(ci, reverse), hh))
    h_f, h_b = pl.pallas_call(
        _mlstm_body,
        grid=(ML_H, ML_NCHUNK),
        in_specs=specs(False) + specs(True),
        out_specs=[out_spec(False), out_spec(True)],
        out_shape=[jax.ShapeDtypeStruct((NTOK, ML_W), F32)] * 2,
        scratch_shapes=[pltpu.VMEM((2, ML_DH, ML_DH), F32), pltpu.VMEM((2, 1, ML_DH), F32),
                        pltpu.VMEM((2, 1, LANES), F32)],
        compiler_params=_cparams(("arbitrary", "arbitrary")),
        name="mlstm_chunks",
    )(z, z, z, gates, gates_t, z, z, z, gates, gates_t)

    tm = MM_TILE_M
    return pl.pallas_call(
        _mlstm_readout_body,
        grid=(NTOK // tm, ML_H),
        in_specs=[
            pl.BlockSpec((tm, ML_DH), lambda i, hh: (i, hh)),
            pl.BlockSpec((tm, ML_DH), lambda i, hh: (i, hh)),
            pl.BlockSpec((tm, ML_DH), lambda i, hh: (i, OFF_O // ML_DH + hh)),
            pl.BlockSpec((1, ML_DH), lambda i, hh: (0, hh)),
        ],
        out_specs=pl.BlockSpec((tm, ML_DH), lambda i, hh: (i, hh)),
        out_shape=jax.ShapeDtypeStruct((NTOK, ML_W), BF16),
        compiler_params=_cparams(("arbitrary", "arbitrary")),
        name="mlstm_readout",
    )(h_f, h_b, z, norm_g.reshape(1, ML_W))


def kernel(x, c, ctx, c_ctx, w_mod, b_mod, norm1_g, w_in, b_in, s5_lam_re, s5_lam_im, s5_log_step,
           s5_b_re, s5_b_im, s5_c_re, s5_c_im, s5_d, s5_w_glu, s5_b_glu, conv_dw_w, conv_dw_b,
           conv_ln_g, conv_ln_b, ml_norm_g, w_out, norm2_g, w_ffn_in, w_ffn_out, norm_f_g):
    assert x.shape == (1, SEQ, D_MODEL) and ctx.shape == (1, CTX, D_MODEL)
    xs = jnp.concatenate([ctx[0], x[0]], axis=0).astype(F32)
    cc = jnp.zeros((SUBLANES, D_MODEL), F32).at[0].set(c[0]).at[1].set(c_ctx)
    mod_all = _modulation(cc, w_mod, b_mod)
    s5_ops = jax.vmap(_s5_operators)(s5_lam_re, s5_lam_im, s5_log_step, s5_b_re, s5_b_im,
                                     s5_c_re, s5_c_im)
    n_gate = w_in.shape[2] - OFF_G
    w_gate = jnp.pad(w_in[:, :, OFF_G:], ((0, 0), (0, 0), (0, LANES - n_gate)))
    b_gate = jnp.pad(b_in[:, OFF_G:], ((0, 0), (0, LANES - n_gate))).reshape(DEPTH, 1, LANES)

    for l in range(DEPTH):
        h, gates = _norm_mod(xs, norm1_g[l], mod_all, l, 0, w_gate[l], b_gate[l])
        z = _in_proj(h, w_in, b_in, l)
        s5o = _s5_mixer(z, [op[l] for op in s5_ops], s5_d[l], s5_w_glu[l], s5_b_glu[l])
        cvo = _conv_module(z, conv_dw_w[l], conv_dw_b[l], conv_ln_g[l], conv_ln_b[l])
        mlo = _mlstm_mixer(z, gates, ml_norm_g[l])
        xs = _out_proj(xs, s5o, cvo, mlo, w_out, mod_all, l)
        (h2,) = _norm_mod(xs, norm2_g[l], mod_all, l, 1)
        hid = _ffn_in(h2, w_ffn_in, l)
        xs = _ffn_out(xs, hid, w_ffn_out, mod_all, l)
    return _final_norm(xs, norm_f_g)[None]
```

```python
import functools

import jax
import jax.numpy as jnp
from jax import lax
from jax.experimental import pallas as pl
from jax.experimental.pallas import tpu as pltpu

F32 = jnp.float32
BF16 = jnp.bfloat16

D_MODEL = 2048
SEQ = 8192
CTX = 256
NTOK = SEQ + CTX
DEPTH = 4
GRID_W = 64

S5_W = 512
S5_CH = 16
S5_G = 32
S5_P = 64
CONV_W = 512
CONV_K = 31
ML_W = 1024
ML_H = 4
ML_DH = 256
D_FF = 5632
EPS = 1e-6
LN_EPS = 1e-5

OFF_CONV = 512
OFF_Q = 1536
OFF_K = 2560
OFF_V = 3584
OFF_O = 4608
OFF_G = 5632

LANES = 128
SUBLANES = 8
VMEM_LIMIT = 56 * 1024 * 1024

ROW_TILE = 256
MM_TILE_M = 768
MM_TILE_N = 512

S5_T = 8
S5_ROWS = NTOK // S5_T
S5_CTX_ROWS = CTX // S5_T
S5_BLK = S5_T * S5_CH
S5_STEP_G = 8
S5_STEP_W = S5_STEP_G * S5_BLK
S5_FLAT = S5_G * S5_BLK

ML_CHUNK = 256
ML_NCHUNK = NTOK // ML_CHUNK


def _cparams(sem, vmem=VMEM_LIMIT):
    return pltpu.CompilerParams(dimension_semantics=sem, vmem_limit_bytes=vmem)


def _dot(a, b):
    return jnp.dot(a.astype(BF16), b.astype(BF16), preferred_element_type=F32)


def _split3(x):
    a = x.astype(BF16)
    r = x - a.astype(F32)
    b = r.astype(BF16)
    c = (r - b.astype(F32)).astype(BF16)
    return a, b, c


def _mod_body(cc_ref, w_ref, b_ref, o_ref):
    s = jax.nn.silu(cc_ref[...])
    o_ref[0] = _dot(s, w_ref[0]) + b_ref[0]


def _modulation(cc, w_mod, b_mod):
    depth, _, n = w_mod.shape
    tn = 1024
    return pl.pallas_call(
        _mod_body,
        grid=(depth, n // tn),
        in_specs=[
            pl.BlockSpec((SUBLANES, D_MODEL), lambda l, j: (0, 0)),
            pl.BlockSpec((1, D_MODEL, tn), lambda l, j: (l, 0, j)),
            pl.BlockSpec((1, 1, tn), lambda l, j: (l, 0, j)),
        ],
        out_specs=pl.BlockSpec((1, SUBLANES, tn), lambda l, j: (l, 0, j)),
        out_shape=jax.ShapeDtypeStruct((depth, SUBLANES, n), F32),
        compiler_params=_cparams(("arbitrary", "arbitrary")),
        name="adaln_modulation",
    )(cc, w_mod, b_mod.reshape(depth, 1, n))


def _mod_row(m_ref, is_ctx):
    return jnp.where(is_ctx, m_ref[0, 1:2, :], m_ref[0, 0:1, :])


def _norm_body(*refs, modulate, with_gates):
    it = iter(refs)
    x_ref, g_ref = next(it), next(it)
    sh_ref = sc_ref = wg_ref = bg_ref = gate_ref = None
    if modulate:
        sh_ref, sc_ref = next(it), next(it)
    if with_gates:
        wg_ref, bg_ref = next(it), next(it)
    h_ref = next(it)
    if with_gates:
        gate_ref = next(it)

    xf = x_ref[...]
    ms = jnp.mean(xf * xf, axis=-1, keepdims=True)
    h = xf * lax.rsqrt(ms + EPS) * g_ref[...]
    if modulate:
        is_ctx = pl.program_id(0) == 0
        h = h * (1.0 + _mod_row(sc_ref, is_ctx)) + _mod_row(sh_ref, is_ctx)
    h_ref[...] = h.astype(h_ref.dtype)
    if with_gates:
        hi = h.astype(BF16)
        lo = (h - hi.astype(F32)).astype(BF16)
        w = wg_ref[...]
        whi = w.astype(BF16)
        wlo = (w - whi.astype(F32)).astype(BF16)
        nt = lambda a, b: lax.dot_general(a, b, (((1,), (1,)), ((), ())), preferred_element_type=F32)
        gate_ref[...] = nt(hi, whi) + nt(hi, wlo) + nt(lo, whi) + bg_ref[...]


def _norm_mod(xs, g, mod_all, layer, phase, w_gate=None, b_gate=None):
    with_gates = w_gate is not None
    n_tiles = NTOK // ROW_TILE
    in_specs = [
        pl.BlockSpec((ROW_TILE, D_MODEL), lambda i: (i, 0)),
        pl.BlockSpec((1, D_MODEL), lambda i: (0, 0)),
        pl.BlockSpec((1, SUBLANES, D_MODEL), lambda i: (layer, 0, 3 * phase)),
        pl.BlockSpec((1, SUBLANES, D_MODEL), lambda i: (layer, 0, 3 * phase + 1)),
    ]
    args = [xs, g.reshape(1, D_MODEL), mod_all, mod_all]
    out_specs = [pl.BlockSpec((ROW_TILE, D_MODEL), lambda i: (i, 0))]
    out_shape = [jax.ShapeDtypeStruct((NTOK, D_MODEL), BF16)]
    if with_gates:
        in_specs += [pl.BlockSpec((LANES, D_MODEL), lambda i: (0, 0)),
                     pl.BlockSpec((1, LANES), lambda i: (0, 0))]
        args += [w_gate, b_gate]
        out_specs.append(pl.BlockSpec((ROW_TILE, LANES), lambda i: (i, 0)))
        out_shape.append(jax.ShapeDtypeStruct((NTOK, LANES), F32))
    return pl.pallas_call(
        functools.partial(_norm_body, modulate=True, with_gates=with_gates),
        grid=(n_tiles,),
        in_specs=in_specs,
        out_specs=out_specs,
        out_shape=out_shape,
        compiler_params=_cparams(("arbitrary",)),
        name="rmsnorm_modulate",
    )(*args)


def _final_norm(xs, g):
    skip = CTX // ROW_TILE
    return pl.pallas_call(
        functools.partial(_norm_body, modulate=False, with_gates=False),
        grid=(SEQ // ROW_TILE,),
        in_specs=[pl.BlockSpec((ROW_TILE, D_MODEL), lambda i: (i + skip, 0)),
                  pl.BlockSpec((1, D_MODEL), lambda i: (0, 0))],
        out_specs=[pl.BlockSpec((ROW_TILE, D_MODEL), lambda i: (i, 0))],
        out_shape=[jax.ShapeDtypeStruct((SEQ, D_MODEL), F32)],
        compiler_params=_cparams(("arbitrary",)),
        name="final_rmsnorm",
    )(xs, g.reshape(1, D_MODEL))[0]


def _in_proj_body(a_ref, wt_ref, b_ref, o_ref, wbf_ref):
    @pl.when(pl.program_id(1) == 0)
    def _():
        wbf_ref[...] = jnp.transpose(wt_ref[...]).astype(BF16)

    acc = jnp.dot(a_ref[...], wbf_ref[...], preferred_element_type=F32) + b_ref[...]
    o_ref[...] = acc.astype(o_ref.dtype)


def _in_proj(h, w_in_t, b_in, layer, col_tile, n_tiles, dtype):
    tm, tn = MM_TILE_M, MM_TILE_N
    in_w = w_in_t.shape[1]
    return pl.pallas_call(
        _in_proj_body,
        grid=(n_tiles, NTOK // tm),
        in_specs=[
            pl.BlockSpec((tm, D_MODEL), lambda j, i: (i, 0)),
            pl.BlockSpec((None, tn, D_MODEL), lambda j, i: (layer, col_tile(j), 0)),
            pl.BlockSpec((None, 1, tn), lambda j, i: (layer, 0, col_tile(j))),
        ],
        out_specs=pl.BlockSpec((tm, tn), lambda j, i: (i, j)),
        out_shape=jax.ShapeDtypeStruct((NTOK, n_tiles * tn), dtype),
        scratch_shapes=[pltpu.VMEM((D_MODEL, tn), BF16)],
        compiler_params=_cparams(("arbitrary", "arbitrary")),
        name="in_proj",
    )(h, w_in_t, b_in.reshape(DEPTH, 1, in_w))


def _row_gate(g_ref, i, tm, tn):
    rows = i * tm + lax.broadcasted_iota(jnp.int32, (tm, tn), 0)
    return jnp.where(rows < CTX, g_ref[0, 1:2, :], g_ref[0, 0:1, :])


def _out_proj_body(s5_ref, cv_ref, ml_ref, w_ref, g_ref, x_ref, o_ref, wbf_ref):
    i = pl.program_id(1)

    @pl.when(i == 0)
    def _():
        wbf_ref[...] = w_ref[...].astype(BF16)

    acc = jnp.dot(s5_ref[...], wbf_ref[0:S5_W, :], preferred_element_type=F32)
    acc += jnp.dot(cv_ref[...], wbf_ref[S5_W:S5_W + CONV_W, :], preferred_element_type=F32)
    acc += jnp.dot(ml_ref[...], wbf_ref[S5_W + CONV_W:, :], preferred_element_type=F32)
    tm, tn = o_ref.shape
    o_ref[...] = x_ref[...] + _row_gate(g_ref, i, tm, tn) * acc


def _out_proj(xs, s5o, cvo, mlo, w_out, mod_all, layer):
    tm, tn = MM_TILE_M, MM_TILE_N
    return pl.pallas_call(
        _out_proj_body,
        grid=(D_MODEL // tn, NTOK // tm),
        in_specs=[
            pl.BlockSpec((tm, S5_W), lambda j, i: (i, 0)),
            pl.BlockSpec((tm, CONV_W), lambda j, i: (i, 0)),
            pl.BlockSpec((tm, ML_W), lambda j, i: (i, 0)),
            pl.BlockSpec((None, D_MODEL, tn), lambda j, i: (layer, 0, j)),
            pl.BlockSpec((1, SUBLANES, tn), lambda j, i: (layer, 0, 2 * (D_MODEL // tn) + j)),
            pl.BlockSpec((tm, tn), lambda j, i: (i, j)),
        ],
        out_specs=pl.BlockSpec((tm, tn), lambda j, i: (i, j)),
        out_shape=jax.ShapeDtypeStruct((NTOK, D_MODEL), F32),
        scratch_shapes=[pltpu.VMEM((D_MODEL, tn), BF16)],
        compiler_params=_cparams(("arbitrary", "arbitrary")),
        name="out_proj_residual",
    )(s5o, cvo, mlo, w_out, mod_all, xs)


def _ffn_in_body(a_ref, wg_ref, wu_ref, o_ref, wgbf_ref, wubf_ref):
    @pl.when(pl.program_id(1) == 0)
    def _():
        wgbf_ref[...] = wg_ref[...].astype(BF16)
        wubf_ref[...] = wu_ref[...].astype(BF16)

    a = a_ref[...]
    g = jnp.dot(a, wgbf_ref[...], preferred_element_type=F32)
    u = jnp.dot(a, wubf_ref[...], preferred_element_type=F32)
    o_ref[...] = (jax.nn.silu(g) * u).astype(BF16)


def _ffn_in(h, w_ffn_in, layer):
    tm, tn = MM_TILE_M, MM_TILE_N
    nj = D_FF // tn
    return pl.pallas_call(
        _ffn_in_body,
        grid=(nj, NTOK // tm),
        in_specs=[
            pl.BlockSpec((tm, D_MODEL), lambda j, i: (i, 0)),
            pl.BlockSpec((None, D_MODEL, tn), lambda j, i: (layer, 0, j)),
            pl.BlockSpec((None, D_MODEL, tn), lambda j, i: (layer, 0, nj + j)),
        ],
        out_specs=pl.BlockSpec((tm, tn), lambda j, i: (i, j)),
        out_shape=jax.ShapeDtypeStruct((NTOK, D_FF), BF16),
        scratch_shapes=[pltpu.VMEM((D_MODEL, tn), BF16), pltpu.VMEM((D_MODEL, tn), BF16)],
        compiler_params=_cparams(("arbitrary", "arbitrary")),
        name="ffn_in_swiglu",
    )(h, w_ffn_in, w_ffn_in)


def _ffn_out_body(a_ref, w_ref, g_ref, x_ref, o_ref, wbf_ref):
    i = pl.program_id(1)

    @pl.when(i == 0)
    def _():
        wbf_ref[...] = w_ref[...].astype(BF16)

    acc = jnp.dot(a_ref[...], wbf_ref[...], preferred_element_type=F32)
    tm, tn = o_ref.shape
    o_ref[...] = x_ref[...] + _row_gate(g_ref, i, tm, tn) * acc


def _ffn_out(xs, hid, w_ffn_out, mod_all, layer):
    tm, tn = MM_TILE_M // 2, MM_TILE_N
    return pl.pallas_call(
        _ffn_out_body,
        grid=(D_MODEL // tn, NTOK // tm),
        in_specs=[
            pl.BlockSpec((tm, D_FF), lambda j, i: (i, 0)),
            pl.BlockSpec((None, D_FF, tn), lambda j, i: (layer, 0, j)),
            pl.BlockSpec((1, SUBLANES, tn), lambda j, i: (layer, 0, 5 * (D_MODEL // tn) + j)),
            pl.BlockSpec((tm, tn), lambda j, i: (i, j)),
        ],
        out_specs=pl.BlockSpec((tm, tn), lambda j, i: (i, j)),
        out_shape=jax.ShapeDtypeStruct((NTOK, D_MODEL), F32),
        scratch_shapes=[pltpu.VMEM((D_FF, tn), BF16)],
        compiler_params=_cparams(("arbitrary", "arbitrary")),
        name="ffn_out_residual",
    )(hid, w_ffn_out, mod_all, xs)


def _s5_operators(lam_re, lam_im, log_step, b_re, b_im, c_re, c_im):
    lam = lax.complex(lam_re.astype(F32), lam_im.astype(F32))
    lam_bar = jnp.exp(lam * jnp.exp(log_step.astype(F32)))
    b_bar = ((lam_bar - 1.0) / lam)[..., None] * lax.complex(b_re.astype(F32), b_im.astype(F32))
    c_mat = lax.complex(c_re.astype(F32), c_im.astype(F32))

    def powers(base, count):
        out = [jnp.ones_like(base)]
        for _ in range(count - 1):
            out.append(out[-1] * base)
        return out

    pw = powers(lam_bar, S5_T + 1)
    pa = powers(pw[S5_T], SUBLANES + 1)
    take = lambda seq, idx: jnp.stack([seq[i] for i in idx])
    t_up = list(range(S5_T))
    t_down = t_up[::-1]

    blk = (S5_G, S5_BLK, S5_BLK)

    kern = jnp.real(jnp.einsum('dgcp,kdgp,dgpa->kdgca', c_mat, take(pw, t_up), b_bar))
    zero_k = jnp.zeros_like(kern[0, 0])

    def toeplitz(d, lag):
        rows = [jnp.stack([kern[lag(s, t), d] if lag(s, t) >= 0 else zero_k for t in t_up])
                for s in t_up]
        k = jnp.stack(rows)
        return jnp.transpose(k, (2, 0, 4, 1, 3)).reshape(blk)

    kc = jnp.stack([toeplitz(0, lambda s, t: t - s), toeplitz(1, lambda s, t: s - t)])

    def wb_dir(d, pows):
        w = pows[:, d][..., None] * b_bar[d][None]
        w = jnp.stack([jnp.real(w), jnp.imag(w)])
        return jnp.transpose(w, (2, 1, 4, 0, 3)).reshape(blk)

    wb = jnp.stack([wb_dir(0, take(pw, t_down)), wb_dir(1, take(pw, t_up))])

    def wc_dir(d, pows):
        w = c_mat[d][None] * pows[:, d][:, :, None, :]
        w = jnp.stack([jnp.real(w), -jnp.imag(w)])
        return jnp.transpose(w, (2, 0, 4, 1, 3)).reshape(blk)

    wc = jnp.stack([wc_dir(0, take(pw, [t + 1 for t in t_up])),
                    wc_dir(1, take(pw, [S5_T - t for t in t_up]))])

    def lanes(v, sign):
        part = jnp.real(v) if sign is None else jnp.imag(v)
        both = jnp.concatenate([part if sign is None else -part, part], axis=-1)
        return jnp.transpose(both.reshape(v.shape[0], 2, S5_FLAT), (1, 0, 2))

    steps = jnp.stack([pa[1], pa[2], pa[4], pa[8]])
    amul = jnp.concatenate([lanes(steps, None), lanes(steps, -1)], axis=1)
    enter = [jnp.stack([lanes(take(pa, idx), s)[d] for d, idx in enumerate((t_up, t_down))])
             for s in (None, -1)]
    apow = jnp.concatenate(enter, axis=1)
    return kc.astype(BF16), wb.astype(BF16), wc.astype(BF16), amul, apow


def _s5_row_scan(e_ref, amul_ref, apow_ref, block_lo, block_hi, reverse, carry):
    rows = lax.broadcasted_iota(jnp.int32, (SUBLANES, S5_BLK), 0)
    swap = lambda x: pltpu.roll(x, S5_BLK // 2, 1)

    def shifted(x, k):
        if reverse:
            return jnp.where(rows < SUBLANES - k, pltpu.roll(x, SUBLANES - k, 0), 0.0)
        return jnp.where(rows >= k, pltpu.roll(x, k, 0), 0.0)

    def body(step, carry):
        blk = (block_hi - 1 - step) if reverse else (block_lo + step)
        r0 = pl.multiple_of(blk * SUBLANES, SUBLANES)
        last = 0 if reverse else SUBLANES - 1
        carry_out = []
        for g in range(S5_STEP_G):
            cols = slice(g * S5_BLK, (g + 1) * S5_BLK)
            s = e_ref[pl.ds(r0, SUBLANES), cols]
            for idx, k in enumerate((1, 2, 4)):
                t = shifted(s, k)
                s = s + amul_ref[idx:idx + 1, cols] * t + amul_ref[4 + idx:5 + idx, cols] * swap(t)
            c = carry[:, cols]
            c_sw = swap(c)
            e_ref[pl.ds(r0, SUBLANES), cols] = (shifted(s, 1) + apow_ref[0:SUBLANES, cols] * c
                                                + apow_ref[SUBLANES:2 * SUBLANES, cols] * c_sw)
            s_last = jnp.broadcast_to(s[last:last + 1, :], (SUBLANES, S5_BLK))
            carry_out.append(s_last + amul_ref[3:4, cols] * c + amul_ref[7:8, cols] * c_sw)
        return jnp.concatenate(carry_out, axis=1)

    return lax.fori_loop(0, block_hi - block_lo, body, carry)


def _s5_body(u_ref, kc_ref, wb_ref, wc_ref, amul_ref, apow_ref, d_ref, y_ref, e_ref):
    direction = pl.program_id(1)
    blocks = [slice(g * S5_BLK, (g + 1) * S5_BLK) for g in range(S5_STEP_G)]
    for g, cols in enumerate(blocks):
        e_ref[:, cols] = jnp.dot(u_ref[:, cols].astype(BF16), wb_ref[0, g], preferred_element_type=F32)

    zero = jnp.zeros((SUBLANES, S5_STEP_W), F32)
    n_blocks = S5_ROWS // SUBLANES
    ctx_blocks = S5_CTX_ROWS // SUBLANES
    amul = amul_ref.at[0]
    apow = apow_ref.at[0]

    @pl.when(direction == 0)
    def _():
        _s5_row_scan(e_ref, amul, apow, 0, n_blocks, False, zero)
        y_ref[...] = d_ref[...] * u_ref[...]

    @pl.when(direction == 1)
    def _():
        carry = _s5_row_scan(e_ref, amul, apow, 0, ctx_blocks, True, zero)
        _s5_row_scan(e_ref, amul, apow, ctx_blocks, n_blocks, True, carry)

    for g, cols in enumerate(blocks):
        y = jnp.dot(u_ref[:, cols].astype(BF16), kc_ref[0, g], preferred_element_type=F32)
        y += jnp.dot(e_ref[:, cols].astype(BF16), wc_ref[0, g], preferred_element_type=F32)
        y_ref[:, cols] += y


def _s5_glu_body(y_ref, w_ref, b_ref, o_ref):
    g = jax.nn.gelu(y_ref[...])
    gate = jnp.dot(g.astype(BF16), w_ref[...].astype(BF16), preferred_element_type=F32) + b_ref[...]
    o_ref[...] = (g * jax.nn.sigmoid(gate)).astype(BF16)


def _s5_mixer(z, ops, d_skip, w_glu, b_glu):
    kc, wb, wc, amul, apow = ops
    u = z[:, :S5_W].reshape(S5_ROWS, S5_T, S5_G, S5_CH)
    u = jnp.transpose(u, (0, 2, 1, 3)).reshape(S5_ROWS, S5_FLAT)
    d_flat = jnp.tile(d_skip.astype(F32).reshape(S5_G, 1, S5_CH), (1, S5_T, 1)).reshape(1, S5_FLAT)
    n_steps = S5_G // S5_STEP_G
    wspec = pl.BlockSpec((1, S5_STEP_G, S5_BLK, S5_BLK), lambda j, d: (d, j, 0, 0))
    y = pl.pallas_call(
        _s5_body,
        grid=(n_steps, 2),
        in_specs=[
            pl.BlockSpec((S5_ROWS, S5_STEP_W), lambda j, d: (0, j)),
            wspec, wspec, wspec,
            pl.BlockSpec((1, SUBLANES, S5_STEP_W), lambda j, d: (d, 0, j)),
            pl.BlockSpec((1, 2 * SUBLANES, S5_STEP_W), lambda j, d: (d, 0, j)),
            pl.BlockSpec((1, S5_STEP_W), lambda j, d: (0, j)),
        ],
        out_specs=pl.BlockSpec((S5_ROWS, S5_STEP_W), lambda j, d: (0, j)),
        out_shape=jax.ShapeDtypeStruct((S5_ROWS, S5_FLAT), F32),
        scratch_shapes=[pltpu.VMEM((S5_ROWS, S5_STEP_W), F32)],
        compiler_params=_cparams(("arbitrary", "arbitrary")),
        name="s5_scan",
    )(u, kc, wb, wc, amul, apow, d_flat)
    y = jnp.transpose(y.reshape(S5_ROWS, S5_G, S5_T, S5_CH), (0, 2, 1, 3)).reshape(NTOK, S5_W)

    tm = MM_TILE_M
    return pl.pallas_call(
        _s5_glu_body,
        grid=(NTOK // tm,),
        in_specs=[
            pl.BlockSpec((tm, S5_W), lambda i: (i, 0)),
            pl.BlockSpec((S5_W, S5_W), lambda i: (0, 0)),
            pl.BlockSpec((1, S5_W), lambda i: (0, 0)),
        ],
        out_specs=pl.BlockSpec((tm, S5_W), lambda i: (i, 0)),
        out_shape=jax.ShapeDtypeStruct((NTOK, S5_W), BF16),
        compiler_params=_cparams(("arbitrary",)),
        name="s5_glu",
    )(y, w_glu, b_glu.reshape(1, S5_W))


CONV_PAD = 16
CONV_BLK = 64


def _conv_body(a_ref, b_ref, w_ref, db_ref, lg_ref, lb_ref, o_ref, pad_ref):
    g = a_ref[...] * jax.nn.sigmoid(b_ref[...])

    def run(seq_len):
        n_seq = ROW_TILE // seq_len
        pitch = seq_len + 2 * CONV_PAD
        zeros = jnp.zeros((CONV_PAD, CONV_W), F32)
        for s in range(n_seq):
            pad_ref[s * pitch:s * pitch + CONV_PAD, :] = zeros
            pad_ref[s * pitch + CONV_PAD:s * pitch + CONV_PAD + seq_len, :] = g[s * seq_len:(s + 1) * seq_len]
            pad_ref[s * pitch + CONV_PAD + seq_len:(s + 1) * pitch, :] = zeros
        for blk in range(ROW_TILE // CONV_BLK):
            row0 = blk * CONV_BLK
            s, q = divmod(row0, seq_len)
            base = s * pitch + CONV_PAD + q - CONV_K // 2
            acc = jnp.zeros((CONV_BLK, CONV_W), F32)
            for k in range(CONV_K):
                acc = acc + w_ref[k:k + 1, :] * pad_ref[base + k:base + k + CONV_BLK, :]
            y = acc + db_ref[...]
            yc = y - jnp.mean(y, axis=-1, keepdims=True)
            var = jnp.mean(yc * yc, axis=-1, keepdims=True)
            y = yc * lax.rsqrt(var + LN_EPS) * lg_ref[...] + lb_ref[...]
            o_ref[row0:row0 + CONV_BLK, :] = jax.nn.silu(y).astype(BF16)

    is_ctx = pl.program_id(0) == 0

    @pl.when(is_ctx)
    def _():
        run(CTX)

    @pl.when(jnp.logical_not(is_ctx))
    def _():
        run(GRID_W)


def _conv_module(z, dw_w, dw_b, ln_g, ln_b):
    w = jnp.concatenate([dw_w, jnp.zeros((1, CONV_W), dw_w.dtype)], axis=0)
    vec = lambda v: v.reshape(1, CONV_W)
    pad_rows = max(CTX + 2 * CONV_PAD, (ROW_TILE // GRID_W) * (GRID_W + 2 * CONV_PAD))
    return pl.pallas_call(
        _conv_body,
        grid=(NTOK // ROW_TILE,),
        in_specs=[
            pl.BlockSpec((ROW_TILE, CONV_W), lambda i: (i, OFF_CONV // CONV_W)),
            pl.BlockSpec((ROW_TILE, CONV_W), lambda i: (i, OFF_CONV // CONV_W + 1)),
            pl.BlockSpec((CONV_K + 1, CONV_W), lambda i: (0, 0)),
            pl.BlockSpec((1, CONV_W), lambda i: (0, 0)),
            pl.BlockSpec((1, CONV_W), lambda i: (0, 0)),
            pl.BlockSpec((1, CONV_W), lambda i: (0, 0)),
        ],
        out_specs=pl.BlockSpec((ROW_TILE, CONV_W), lambda i: (i, 0)),
        out_shape=jax.ShapeDtypeStruct((NTOK, CONV_W), BF16),
        scratch_shapes=[pltpu.VMEM((pad_rows, CONV_W), F32)],
        compiler_params=_cparams(("arbitrary",)),
        name="conv_module",
    )(z, z, w, vec(dw_b), vec(ln_g), vec(ln_b))


def _ml_chunk_index(ci, reverse):
    if not reverse:
        return ci
    return jnp.where(ci == 0, 0, ML_NCHUNK - ci)


def _ml_direction(q, k, v, li_col, li_row, b_col, b_row, cm_ref, n_ref, m_ref, reverse):
    c = ML_CHUNK
    t_idx = lax.broadcasted_iota(jnp.int32, (c, c), 0)
    s_idx = lax.broadcasted_iota(jnp.int32, (c, c), 1)
    mask = (s_idx >= t_idx) if reverse else (s_idx <= t_idx)
    m = m_ref[:, 0:1]
    cm = cm_ref[...]
    n = n_ref[...]

    d_log = jnp.where(mask, b_col - b_row + li_row, -jnp.inf)
    inter = b_col + m
    m_row = jnp.maximum(inter, jnp.max(d_log, axis=-1, keepdims=True))
    qb, vb = q, v
    q = q.astype(F32)
    k = k.astype(F32) * (ML_DH ** -0.5)
    kb = k.astype(BF16)
    qk = lax.dot_general(qb, kb, (((1,), (1,)), ((), ())), preferred_element_type=F32)
    s = qk * jnp.exp(d_log - m_row)
    w_inter = jnp.exp(inter - m_row)
    num = jnp.dot(s.astype(BF16), vb, preferred_element_type=F32)
    num += w_inter * jnp.dot(qb, cm.astype(BF16), preferred_element_type=F32)
    den = jnp.sum(s, axis=-1, keepdims=True) + w_inter * jnp.sum(q * n, axis=-1, keepdims=True)
    h = num / jnp.maximum(jnp.abs(den), jnp.exp(-m_row))

    b_tot = b_row[:, 0:1] if reverse else b_row[:, c - 1:c]
    g = b_tot - b_col + li_col
    m_new = jnp.maximum(b_tot + m, jnp.max(g, axis=0, keepdims=True))
    kw = k * jnp.exp(g - m_new)
    decay = jnp.exp(b_tot + m - m_new)
    cm_ref[...] = decay * cm + lax.dot_general(kw.astype(BF16), vb, (((0,), (0,)), ((), ())),
                                               preferred_element_type=F32)
    n_ref[...] = decay * n + jnp.sum(kw, axis=0, keepdims=True)
    m_ref[...] = jnp.broadcast_to(m_new, m_ref.shape)
    return h


def _mlstm_body(qf_ref, kf_ref, vf_ref, gf_ref, gtf_ref, qb_ref, kb_ref, vb_ref, gb_ref, gtb_ref,
                hf_ref, hb_ref, cm_ref, n_ref, m_ref):
    head = pl.program_id(0)

    @pl.when(pl.program_id(1) == 0)
    def _():
        cm_ref[...] = jnp.zeros_like(cm_ref)
        n_ref[...] = jnp.zeros_like(n_ref)
        m_ref[...] = jnp.zeros_like(m_ref)

    c = ML_CHUNK
    r_idx = lax.broadcasted_iota(jnp.int32, (c, c), 0)
    c_idx = lax.broadcasted_iota(jnp.int32, (c, c), 1)
    lane = lax.broadcasted_iota(jnp.int32, (c, LANES), 1)
    sub = lax.broadcasted_iota(jnp.int32, (2 * SUBLANES, c), 0)

    for d, (q_ref, k_ref, v_ref, g_ref, gt_ref, h_ref) in enumerate(
            ((qf_ref, kf_ref, vf_ref, gf_ref, gtf_ref, hf_ref),
             (qb_ref, kb_ref, vb_ref, gb_ref, gtb_ref, hb_ref))):
        reverse = d == 1
        gates = g_ref[...]
        gates_t = gt_ref[...]
        incl = (c_idx >= r_idx) if reverse else (c_idx <= r_idx)
        tri = jnp.where(incl, 1.0, 0.0).astype(BF16)
        incl_t = (r_idx >= c_idx) if reverse else (r_idx <= c_idx)
        tri_t = jnp.where(incl_t, 1.0, 0.0).astype(BF16)
        lf = jax.nn.log_sigmoid(gates)
        lf_t = jax.nn.log_sigmoid(gates_t)
        b_all = sum(jnp.dot(tri, p, preferred_element_type=F32) for p in _split3(lf))
        b_all_t = sum(jnp.dot(p, tri_t, preferred_element_type=F32) for p in _split3(lf_t))
        i_col = 2 * ML_H * d + head
        f_col = i_col + ML_H
        pick_col = lambda a, idx: jnp.sum(jnp.where(lane == idx, a, 0.0), axis=-1, keepdims=True)
        pick_row = lambda a, idx: jnp.sum(jnp.where(sub == idx, a, 0.0), axis=0, keepdims=True)
        h = _ml_direction(
            q_ref[...], k_ref[...], v_ref[...],
            pick_col(gates, i_col), pick_row(gates_t, i_col),
            pick_col(b_all, f_col), pick_row(b_all_t, f_col),
            cm_ref.at[d], n_ref.at[d], m_ref.at[d], reverse)
        h_ref[...] = h


def _mlstm_readout_body(hf_ref, hb_ref, o_ref, g_ref, out_ref):
    h = hf_ref[...] + hb_ref[...]
    hc = h - jnp.mean(h, axis=-1, keepdims=True)
    var = jnp.mean(hc * hc, axis=-1, keepdims=True)
    h = hc * lax.rsqrt(var + LN_EPS) * g_ref[...]
    out_ref[...] = (jax.nn.sigmoid(o_ref[...]) * h).astype(BF16)


def _mlstm_mixer(qkv, z, o_col, gates, norm_g):
    gates_t = jnp.transpose(gates[:, :2 * SUBLANES])
    c = ML_CHUNK
    qcol, kcol, vcol = 0, ML_H, 2 * ML_H

    def specs(reverse):
        row = lambda ci: _ml_chunk_index(ci, reverse)
        return [
            pl.BlockSpec((c, ML_DH), lambda hh, ci: (row(ci), qcol + hh)),
            pl.BlockSpec((c, ML_DH), lambda hh, ci: (row(ci), kcol + hh)),
            pl.BlockSpec((c, ML_DH), lambda hh, ci: (row(ci), vcol + hh)),
            pl.BlockSpec((c, LANES), lambda hh, ci: (row(ci), 0)),
            pl.BlockSpec((2 * SUBLANES, c), lambda hh, ci: (0, row(ci))),
        ]

    out_spec = lambda reverse: pl.BlockSpec(
        (c, ML_DH), lambda hh, ci: (_ml_chunk_index(ci, reverse), hh))
    h_f, h_b = pl.pallas_call(
        _mlstm_body,
        grid=(ML_H, ML_NCHUNK),
        in_specs=specs(False) + specs(True),
        out_specs=[out_spec(False), out_spec(True)],
        out_shape=[jax.ShapeDtypeStruct((NTOK, ML_W), F32)] * 2,
        scratch_shapes=[pltpu.VMEM((2, ML_DH, ML_DH), F32), pltpu.VMEM((2, 1, ML_DH), F32),
                        pltpu.VMEM((2, 1, LANES), F32)],
        compiler_params=_cparams(("arbitrary", "arbitrary")),
        name="mlstm_chunks",
    )(qkv, qkv, qkv, gates, gates_t, qkv, qkv, qkv, gates, gates_t)

    tm = MM_TILE_M
    return pl.pallas_call(
        _mlstm_readout_body,
        grid=(NTOK // tm, ML_H),
        in_specs=[
            pl.BlockSpec((tm, ML_DH), lambda i, hh: (i, hh)),
            pl.BlockSpec((tm, ML_DH), lambda i, hh: (i, hh)),
            pl.BlockSpec((tm, ML_DH), lambda i, hh: (i, o_col // ML_DH + hh)),
            pl.BlockSpec((1, ML_DH), lambda i, hh: (0, hh)),
        ],
        out_specs=pl.BlockSpec((tm, ML_DH), lambda i, hh: (i, hh)),
        out_shape=jax.ShapeDtypeStruct((NTOK, ML_W), BF16),
        compiler_params=_cparams(("arbitrary", "arbitrary")),
        name="mlstm_readout",
    )(h_f, h_b, z, norm_g.reshape(1, ML_W))


def kernel(x, c, ctx, c_ctx, w_mod, b_mod, norm1_g, w_in, b_in, s5_lam_re, s5_lam_im, s5_log_step,
           s5_b_re, s5_b_im, s5_c_re, s5_c_im, s5_d, s5_w_glu, s5_b_glu, conv_dw_w, conv_dw_b,
           conv_ln_g, conv_ln_b, ml_norm_g, w_out, norm2_g, w_ffn_in, w_ffn_out, norm_f_g):
    assert x.shape == (1, SEQ, D_MODEL) and ctx.shape == (1, CTX, D_MODEL)
    xs = jnp.concatenate([ctx[0], x[0]], axis=0).astype(F32)
    cc = jnp.zeros((SUBLANES, D_MODEL), F32).at[0].set(c[0]).at[1].set(c_ctx)
    mod_all = _modulation(cc, w_mod, b_mod)
    s5_ops = jax.vmap(_s5_operators)(s5_lam_re, s5_lam_im, s5_log_step, s5_b_re, s5_b_im,
                                     s5_c_re, s5_c_im)
    w_in_t = jnp.swapaxes(w_in, 1, 2)
    n_gate = w_in.shape[2] - OFF_G
    w_gate = jnp.pad(w_in_t[:, OFF_G:, :], ((0, 0), (0, LANES - n_gate), (0, 0)))
    b_gate = jnp.pad(b_in[:, OFF_G:], ((0, 0), (0, LANES - n_gate))).reshape(DEPTH, 1, LANES)
    tn = MM_TILE_N
    f32_tiles = OFF_Q // tn
    o_col = f32_tiles * tn

    for l in range(DEPTH):
        h, gates = _norm_mod(xs, norm1_g[l], mod_all, l, 0, w_gate[l], b_gate[l])
        z = _in_proj(h, w_in_t, b_in, l, lambda j: jnp.where(j < f32_tiles, j, j + (OFF_O - OFF_Q) // tn),
                     f32_tiles + (OFF_G - OFF_O) // tn, F32)
        qkv = _in_proj(h, w_in_t, b_in, l, lambda j: j + OFF_Q // tn, (OFF_O - OFF_Q) // tn, BF16)
        s5o = _s5_mixer(z, [op[l] for op in s5_ops], s5_d[l], s5_w_glu[l], s5_b_glu[l])
        cvo = _conv_module(z, conv_dw_w[l], conv_dw_b[l], conv_ln_g[l], conv_ln_b[l])
        mlo = _mlstm_mixer(qkv, z, o_col, gates, ml_norm_g[l])
        xs = _out_proj(xs, s5o, cvo, mlo, w_out, mod_all, l)
        (h2,) = _norm_mod(xs, norm2_g[l], mod_all, l, 1)
        hid = _ffn_in(h2, w_ffn_in, l)
        xs = _ffn_out(xs, hid, w_ffn_out, mod_all, l)
    return _final_norm(xs, norm_f_g)[None]
```

```python
import functools

import jax
import jax.numpy as jnp
from jax import lax
from jax.experimental import pallas as pl
from jax.experimental.pallas import tpu as pltpu

F32 = jnp.float32
BF16 = jnp.bfloat16

D_MODEL = 2048
SEQ = 8192
CTX = 256
NTOK = SEQ + CTX
DEPTH = 4
GRID_W = 64

S5_W = 512
S5_CH = 16
S5_G = 32
S5_P = 64
CONV_W = 512
CONV_K = 31
ML_W = 1024
ML_H = 4
ML_DH = 256
D_FF = 5632
EPS = 1e-6
LN_EPS = 1e-5

OFF_CONV = 512
OFF_Q = 1536
OFF_K = 2560
OFF_V = 3584
OFF_O = 4608
OFF_G = 5632

LANES = 128
SUBLANES = 8
VMEM_LIMIT = 56 * 1024 * 1024

ROW_TILE = 256
MM_TILE_M = 768
MM_TILE_N = 512

S5_T = 8
S5_ROWS = NTOK // S5_T
S5_CTX_ROWS = CTX // S5_T
S5_BLK = S5_T * S5_CH
S5_STEP_G = 8
S5_STEP_PAIRS = S5_STEP_G // 2
S5_SET_CH = S5_STEP_G * S5_CH
S5_STEP_W = S5_STEP_G * S5_BLK
S5_FLAT = S5_G * S5_BLK

ML_CHUNK = 256
ML_NCHUNK = NTOK // ML_CHUNK


def _cparams(sem, vmem=VMEM_LIMIT):
    return pltpu.CompilerParams(dimension_semantics=sem, vmem_limit_bytes=vmem)


def _dot(a, b):
    return jnp.dot(a.astype(BF16), b.astype(BF16), preferred_element_type=F32)


def _split3(x):
    a = x.astype(BF16)
    r = x - a.astype(F32)
    b = r.astype(BF16)
    c = (r - b.astype(F32)).astype(BF16)
    return a, b, c


def _mod_body(cc_ref, w_ref, b_ref, o_ref):
    s = jax.nn.silu(cc_ref[...])
    o_ref[0] = _dot(s, w_ref[0]) + b_ref[0]


def _modulation(cc, w_mod, b_mod):
    depth, _, n = w_mod.shape
    tn = 1024
    return pl.pallas_call(
        _mod_body,
        grid=(depth, n // tn),
        in_specs=[
            pl.BlockSpec((SUBLANES, D_MODEL), lambda l, j: (0, 0)),
            pl.BlockSpec((1, D_MODEL, tn), lambda l, j: (l, 0, j)),
            pl.BlockSpec((1, 1, tn), lambda l, j: (l, 0, j)),
        ],
        out_specs=pl.BlockSpec((1, SUBLANES, tn), lambda l, j: (l, 0, j)),
        out_shape=jax.ShapeDtypeStruct((depth, SUBLANES, n), F32),
        compiler_params=_cparams(("arbitrary", "arbitrary")),
        name="adaln_modulation",
    )(cc, w_mod, b_mod.reshape(depth, 1, n))


def _mod_row(m_ref, is_ctx):
    return jnp.where(is_ctx, m_ref[0, 1:2, :], m_ref[0, 0:1, :])


def _norm_body(*refs, modulate, with_gates):
    it = iter(refs)
    x_ref, g_ref = next(it), next(it)
    sh_ref = sc_ref = wg_ref = bg_ref = gate_ref = None
    if modulate:
        sh_ref, sc_ref = next(it), next(it)
    if with_gates:
        wg_ref, bg_ref = next(it), next(it)
    h_ref = next(it)
    if with_gates:
        gate_ref = next(it)

    xf = x_ref[...]
    ms = jnp.mean(xf * xf, axis=-1, keepdims=True)
    h = xf * lax.rsqrt(ms + EPS) * g_ref[...]
    if modulate:
        is_ctx = pl.program_id(0) == 0
        h = h * (1.0 + _mod_row(sc_ref, is_ctx)) + _mod_row(sh_ref, is_ctx)
    h_ref[...] = h.astype(h_ref.dtype)
    if with_gates:
        hi = h.astype(BF16)
        lo = (h - hi.astype(F32)).astype(BF16)
        w = wg_ref[...]
        whi = w.astype(BF16)
        wlo = (w - whi.astype(F32)).astype(BF16)
        nt = lambda a, b: lax.dot_general(a, b, (((1,), (1,)), ((), ())), preferred_element_type=F32)
        gate_ref[...] = nt(hi, whi) + nt(hi, wlo) + nt(lo, whi) + bg_ref[...]


def _norm_mod(xs, g, mod_all, layer, phase, w_gate=None, b_gate=None):
    with_gates = w_gate is not None
    n_tiles = NTOK // ROW_TILE
    in_specs = [
        pl.BlockSpec((ROW_TILE, D_MODEL), lambda i: (i, 0)),
        pl.BlockSpec((1, D_MODEL), lambda i: (0, 0)),
        pl.BlockSpec((1, SUBLANES, D_MODEL), lambda i: (layer, 0, 3 * phase)),
        pl.BlockSpec((1, SUBLANES, D_MODEL), lambda i: (layer, 0, 3 * phase + 1)),
    ]
    args = [xs, g.reshape(1, D_MODEL), mod_all, mod_all]
    out_specs = [pl.BlockSpec((ROW_TILE, D_MODEL), lambda i: (i, 0))]
    out_shape = [jax.ShapeDtypeStruct((NTOK, D_MODEL), BF16)]
    if with_gates:
        in_specs += [pl.BlockSpec((LANES, D_MODEL), lambda i: (0, 0)),
                     pl.BlockSpec((1, LANES), lambda i: (0, 0))]
        args += [w_gate, b_gate]
        out_specs.append(pl.BlockSpec((ROW_TILE, LANES), lambda i: (i, 0)))
        out_shape.append(jax.ShapeDtypeStruct((NTOK, LANES), F32))
    return pl.pallas_call(
        functools.partial(_norm_body, modulate=True, with_gates=with_gates),
        grid=(n_tiles,),
        in_specs=in_specs,
        out_specs=out_specs,
        out_shape=out_shape,
        compiler_params=_cparams(("arbitrary",)),
        name="rmsnorm_modulate",
    )(*args)


def _final_norm(xs, g):
    skip = CTX // ROW_TILE
    return pl.pallas_call(
        functools.partial(_norm_body, modulate=False, with_gates=False),
        grid=(SEQ // ROW_TILE,),
        in_specs=[pl.BlockSpec((ROW_TILE, D_MODEL), lambda i: (i + skip, 0)),
                  pl.BlockSpec((1, D_MODEL), lambda i: (0, 0))],
        out_specs=[pl.BlockSpec((ROW_TILE, D_MODEL), lambda i: (i, 0))],
        out_shape=[jax.ShapeDtypeStruct((SEQ, D_MODEL), F32)],
        compiler_params=_cparams(("arbitrary",)),
        name="final_rmsnorm",
    )(xs, g.reshape(1, D_MODEL))[0]


def _in_proj_body(a_ref, wt_ref, b_ref, o_ref, wbf_ref):
    @pl.when(pl.program_id(1) == 0)
    def _():
        wbf_ref[...] = jnp.transpose(wt_ref[...]).astype(BF16)

    acc = jnp.dot(a_ref[...], wbf_ref[...], preferred_element_type=F32) + b_ref[...]
    o_ref[...] = acc.astype(o_ref.dtype)


def _in_proj(h, w_in_t, b_in, layer, col_tile, n_tiles, dtype):
    tm, tn = MM_TILE_M, MM_TILE_N
    in_w = w_in_t.shape[1]
    return pl.pallas_call(
        _in_proj_body,
        grid=(n_tiles, NTOK // tm),
        in_specs=[
            pl.BlockSpec((tm, D_MODEL), lambda j, i: (i, 0)),
            pl.BlockSpec((None, tn, D_MODEL), lambda j, i: (layer, col_tile(j), 0)),
            pl.BlockSpec((None, 1, tn), lambda j, i: (layer, 0, col_tile(j))),
        ],
        out_specs=pl.BlockSpec((tm, tn), lambda j, i: (i, j)),
        out_shape=jax.ShapeDtypeStruct((NTOK, n_tiles * tn), dtype),
        scratch_shapes=[pltpu.VMEM((D_MODEL, tn), BF16)],
        compiler_params=_cparams(("arbitrary", "arbitrary")),
        name="in_proj",
    )(h, w_in_t, b_in.reshape(DEPTH, 1, in_w))


def _row_gate(g_ref, i, tm, tn):
    rows = i * tm + lax.broadcasted_iota(jnp.int32, (tm, tn), 0)
    return jnp.where(rows < CTX, g_ref[0, 1:2, :], g_ref[0, 0:1, :])


def _out_proj_body(s5_ref, cv_ref, ml_ref, w_ref, g_ref, x_ref, o_ref, wbf_ref):
    i = pl.program_id(1)

    @pl.when(i == 0)
    def _():
        wbf_ref[...] = w_ref[...].astype(BF16)

    acc = jnp.dot(s5_ref[...], wbf_ref[0:S5_W, :], preferred_element_type=F32)
    acc += jnp.dot(cv_ref[...], wbf_ref[S5_W:S5_W + CONV_W, :], preferred_element_type=F32)
    acc += jnp.dot(ml_ref[...], wbf_ref[S5_W + CONV_W:, :], preferred_element_type=F32)
    tm, tn = o_ref.shape
    o_ref[...] = x_ref[...] + _row_gate(g_ref, i, tm, tn) * acc


def _out_proj(xs, s5o, cvo, mlo, w_out, mod_all, layer):
    tm, tn = MM_TILE_M, MM_TILE_N
    return pl.pallas_call(
        _out_proj_body,
        grid=(D_MODEL // tn, NTOK // tm),
        in_specs=[
            pl.BlockSpec((tm, S5_W), lambda j, i: (i, 0)),
            pl.BlockSpec((tm, CONV_W), lambda j, i: (i, 0)),
            pl.BlockSpec((tm, ML_W), lambda j, i: (i, 0)),
            pl.BlockSpec((None, D_MODEL, tn), lambda j, i: (layer, 0, j)),
            pl.BlockSpec((1, SUBLANES, tn), lambda j, i: (layer, 0, 2 * (D_MODEL // tn) + j)),
            pl.BlockSpec((tm, tn), lambda j, i: (i, j)),
        ],
        out_specs=pl.BlockSpec((tm, tn), lambda j, i: (i, j)),
        out_shape=jax.ShapeDtypeStruct((NTOK, D_MODEL), F32),
        scratch_shapes=[pltpu.VMEM((D_MODEL, tn), BF16)],
        compiler_params=_cparams(("arbitrary", "arbitrary")),
        name="out_proj_residual",
    )(s5o, cvo, mlo, w_out, mod_all, xs)


def _ffn_in_body(a_ref, wg_ref, wu_ref, o_ref, wgbf_ref, wubf_ref):
    @pl.when(pl.program_id(1) == 0)
    def _():
        wgbf_ref[...] = wg_ref[...].astype(BF16)
        wubf_ref[...] = wu_ref[...].astype(BF16)

    a = a_ref[...]
    g = jnp.dot(a, wgbf_ref[...], preferred_element_type=F32)
    u = jnp.dot(a, wubf_ref[...], preferred_element_type=F32)
    o_ref[...] = (jax.nn.silu(g) * u).astype(BF16)


def _ffn_in(h, w_ffn_in, layer):
    tm, tn = MM_TILE_M, MM_TILE_N
    nj = D_FF // tn
    return pl.pallas_call(
        _ffn_in_body,
        grid=(nj, NTOK // tm),
        in_specs=[
            pl.BlockSpec((tm, D_MODEL), lambda j, i: (i, 0)),
            pl.BlockSpec((None, D_MODEL, tn), lambda j, i: (layer, 0, j)),
            pl.BlockSpec((None, D_MODEL, tn), lambda j, i: (layer, 0, nj + j)),
        ],
        out_specs=pl.BlockSpec((tm, tn), lambda j, i: (i, j)),
        out_shape=jax.ShapeDtypeStruct((NTOK, D_FF), BF16),
        scratch_shapes=[pltpu.VMEM((D_MODEL, tn), BF16), pltpu.VMEM((D_MODEL, tn), BF16)],
        compiler_params=_cparams(("arbitrary", "arbitrary")),
        name="ffn_in_swiglu",
    )(h, w_ffn_in, w_ffn_in)


def _ffn_out_body(a_ref, w_ref, g_ref, x_ref, o_ref, wbf_ref):
    i = pl.program_id(1)

    @pl.when(i == 0)
    def _():
        wbf_ref[...] = w_ref[...].astype(BF16)

    acc = jnp.dot(a_ref[...], wbf_ref[...], preferred_element_type=F32)
    tm, tn = o_ref.shape
    o_ref[...] = x_ref[...] + _row_gate(g_ref, i, tm, tn) * acc


def _ffn_out(xs, hid, w_ffn_out, mod_all, layer):
    tm, tn = MM_TILE_M // 2, MM_TILE_N
    return pl.pallas_call(
        _ffn_out_body,
        grid=(D_MODEL // tn, NTOK // tm),
        in_specs=[
            pl.BlockSpec((tm, D_FF), lambda j, i: (i, 0)),
            pl.BlockSpec((None, D_FF, tn), lambda j, i: (layer, 0, j)),
            pl.BlockSpec((1, SUBLANES, tn), lambda j, i: (layer, 0, 5 * (D_MODEL // tn) + j)),
            pl.BlockSpec((tm, tn), lambda j, i: (i, j)),
        ],
        out_specs=pl.BlockSpec((tm, tn), lambda j, i: (i, j)),
        out_shape=jax.ShapeDtypeStruct((NTOK, D_MODEL), F32),
        scratch_shapes=[pltpu.VMEM((D_FF, tn), BF16)],
        compiler_params=_cparams(("arbitrary", "arbitrary")),
        name="ffn_out_residual",
    )(hid, w_ffn_out, mod_all, xs)


def _s5_operators(lam_re, lam_im, log_step, b_re, b_im, c_re, c_im):
    lam = lax.complex(lam_re.astype(F32), lam_im.astype(F32))
    lam_bar = jnp.exp(lam * jnp.exp(log_step.astype(F32)))
    b_bar = ((lam_bar - 1.0) / lam)[..., None] * lax.complex(b_re.astype(F32), b_im.astype(F32))
    c_mat = lax.complex(c_re.astype(F32), c_im.astype(F32))

    def powers(base, count):
        out = [jnp.ones_like(base)]
        for _ in range(count - 1):
            out.append(out[-1] * base)
        return out

    pw = powers(lam_bar, S5_T + 1)
    pa = powers(pw[S5_T], SUBLANES + 1)
    take = lambda seq, idx: jnp.stack([seq[i] for i in idx])
    t_up = list(range(S5_T))
    t_down = t_up[::-1]

    blk = (S5_G, S5_BLK, S5_BLK)

    kern = jnp.real(jnp.einsum('dgcp,kdgp,dgpa->kdgca', c_mat, take(pw, t_up), b_bar))
    zero_k = jnp.zeros_like(kern[0, 0])

    def toeplitz(d, lag):
        rows = [jnp.stack([kern[lag(s, t), d] if lag(s, t) >= 0 else zero_k for t in t_up])
                for s in t_up]
        k = jnp.stack(rows)
        return jnp.transpose(k, (2, 0, 4, 1, 3)).reshape(blk)

    kc = jnp.stack([toeplitz(0, lambda s, t: t - s), toeplitz(1, lambda s, t: s - t)])

    def wb_dir(d, pows):
        w = pows[:, d][..., None] * b_bar[d][None]
        w = jnp.stack([jnp.real(w), jnp.imag(w)])
        return jnp.transpose(w, (2, 1, 4, 0, 3)).reshape(blk)

    wb = jnp.stack([wb_dir(0, take(pw, t_down)), wb_dir(1, take(pw, t_up))])

    def wc_dir(d, pows):
        w = c_mat[d][None] * pows[:, d][:, :, None, :]
        w = jnp.stack([jnp.real(w), -jnp.imag(w)])
        return jnp.transpose(w, (2, 0, 4, 1, 3)).reshape(blk)

    wc = jnp.stack([wc_dir(0, take(pw, [t + 1 for t in t_up])),
                    wc_dir(1, take(pw, [S5_T - t for t in t_up]))])

    n_pair = S5_G // 2
    eye2 = jnp.eye(2, dtype=F32)
    pair = (2, n_pair, 2 * S5_BLK, 2 * S5_BLK)
    kc_p = jnp.einsum('dqgab,gh->dqgahb', kc.reshape(2, n_pair, 2, S5_BLK, S5_BLK), eye2).reshape(pair)
    wb_p = jnp.einsum('dqgarp,gh->dqgarhp', wb.reshape(2, n_pair, 2, S5_BLK, 2, S5_P), eye2).reshape(pair)
    wc_p = jnp.einsum('dqgrpb,gh->dqrgphb', wc.reshape(2, n_pair, 2, 2, S5_P, S5_BLK), eye2).reshape(pair)

    def lanes(v):
        parts = jnp.stack([jnp.real(v), jnp.imag(v)], axis=3)
        parts = parts.reshape(v.shape[0], 2, n_pair, 2, 2, S5_P)
        parts = jnp.swapaxes(parts, 3, 4)
        return jnp.transpose(parts.reshape(v.shape[0], 2, S5_FLAT), (1, 0, 2))

    zero_p = jnp.zeros_like(pa[0])
    amul = lanes(jnp.stack([pa[1], pa[2], pa[4], pa[8]] + [zero_p] * 4))
    apow = jnp.stack([lanes(take(pa, t_up))[0], lanes(take(pa, t_down))[1]])
    return kc_p.astype(BF16), wb_p.astype(BF16), wc_p.astype(BF16), amul, apow


def _s5_row_scan(e_ref, amul_ref, apow_ref, block_lo, block_hi, reverse, carry):
    rows = lax.broadcasted_iota(jnp.int32, (SUBLANES, S5_BLK), 0)

    def shifted(x, k):
        if reverse:
            return jnp.where(rows < SUBLANES - k, pltpu.roll(x, SUBLANES - k, 0), 0.0)
        return jnp.where(rows >= k, pltpu.roll(x, k, 0), 0.0)

    def body(step, carry):
        blk = (block_hi - 1 - step) if reverse else (block_lo + step)
        r0 = pl.multiple_of(blk * SUBLANES, SUBLANES)
        last = 0 if reverse else SUBLANES - 1
        carry_out = []
        for q in range(S5_STEP_PAIRS):
            re_c = slice(2 * q * S5_BLK, (2 * q + 1) * S5_BLK)
            im_c = slice((2 * q + 1) * S5_BLK, (2 * q + 2) * S5_BLK)
            s_re = e_ref[pl.ds(r0, SUBLANES), re_c]
            s_im = e_ref[pl.ds(r0, SUBLANES), im_c]
            for idx, k in enumerate((1, 2, 4)):
                a_re, a_im = amul_ref[idx:idx + 1, re_c], amul_ref[idx:idx + 1, im_c]
                t_re, t_im = shifted(s_re, k), shifted(s_im, k)
                s_re, s_im = s_re + t_re * a_re - t_im * a_im, s_im + t_re * a_im + t_im * a_re
            c_re, c_im = carry[:, re_c], carry[:, im_c]
            p_re, p_im = apow_ref[:, re_c], apow_ref[:, im_c]
            e_ref[pl.ds(r0, SUBLANES), re_c] = shifted(s_re, 1) + p_re * c_re - p_im * c_im
            e_ref[pl.ds(r0, SUBLANES), im_c] = shifted(s_im, 1) + p_re * c_im + p_im * c_re
            a_re, a_im = amul_ref[3:4, re_c], amul_ref[3:4, im_c]
            l_re = jnp.broadcast_to(s_re[last:last + 1, :], (SUBLANES, S5_BLK))
            l_im = jnp.broadcast_to(s_im[last:last + 1, :], (SUBLANES, S5_BLK))
            carry_out += [l_re + a_re * c_re - a_im * c_im, l_im + a_re * c_im + a_im * c_re]
        return jnp.concatenate(carry_out, axis=1)

    return lax.fori_loop(0, block_hi - block_lo, body, carry)


def _s5_body(u_ref, p_ref, pt_ref, kc_ref, wb_ref, wc_ref, amul_ref, apow_ref, d_ref, y_ref,
             up_ref, e_ref, acc_ref):
    direction = pl.program_id(1)
    token_rows = lambda t: pl.ds(t, S5_ROWS, stride=S5_T)
    pairs = [slice(2 * q * S5_BLK, 2 * (q + 1) * S5_BLK) for q in range(S5_STEP_PAIRS)]

    @pl.when(direction == 0)
    def _():
        u_nat = jnp.concatenate([u_ref[token_rows(t), :] for t in range(S5_T)], axis=1)
        up_ref[...] = jnp.dot(u_nat.astype(BF16), p_ref[...], preferred_element_type=F32).astype(BF16)
        acc_ref[...] = jnp.zeros_like(acc_ref)

    for q, cols in enumerate(pairs):
        e_ref[:, cols] = jnp.dot(up_ref[:, cols], wb_ref[0, q], preferred_element_type=F32)

    zero = jnp.zeros((SUBLANES, S5_STEP_W), F32)
    n_blocks = S5_ROWS // SUBLANES
    ctx_blocks = S5_CTX_ROWS // SUBLANES
    amul = amul_ref.at[0]
    apow = apow_ref.at[0]

    @pl.when(direction == 0)
    def _():
        _s5_row_scan(e_ref, amul, apow, 0, n_blocks, False, zero)

    @pl.when(direction == 1)
    def _():
        carry = _s5_row_scan(e_ref, amul, apow, 0, ctx_blocks, True, zero)
        _s5_row_scan(e_ref, amul, apow, ctx_blocks, n_blocks, True, carry)

    for q, cols in enumerate(pairs):
        y = jnp.dot(up_ref[:, cols], kc_ref[0, q], preferred_element_type=F32)
        y += jnp.dot(e_ref[:, cols].astype(BF16), wc_ref[0, q], preferred_element_type=F32)
        acc_ref[:, cols] += y

    @pl.when(direction == 1)
    def _():
        acc = acc_ref[...]
        hi = acc.astype(BF16)
        lo = (acc - hi.astype(F32)).astype(BF16)
        y_nat = (jnp.dot(hi, pt_ref[...], preferred_element_type=F32)
                 + jnp.dot(lo, pt_ref[...], preferred_element_type=F32))
        for t in range(S5_T):
            y_ref[token_rows(t), :] = (y_nat[:, t * S5_SET_CH:(t + 1) * S5_SET_CH]
                                       + d_ref[...] * u_ref[token_rows(t), :])


def _s5_glu_body(y_ref, w_ref, b_ref, o_ref):
    g = jax.nn.gelu(y_ref[...])
    gate = jnp.dot(g.astype(BF16), w_ref[...].astype(BF16), preferred_element_type=F32) + b_ref[...]
    o_ref[...] = (g * jax.nn.sigmoid(gate)).astype(BF16)


def _s5_mixer(z, ops, d_skip, w_glu, b_glu):
    kc, wb, wc, amul, apow = ops
    src = jnp.arange(S5_STEP_W)
    dst = ((src // S5_CH) % S5_STEP_G) * S5_BLK + (src // S5_SET_CH) * S5_CH + src % S5_CH
    perm = (dst[:, None] == jnp.arange(S5_STEP_W)[None, :]).astype(BF16)
    n_steps = S5_G // S5_STEP_G
    wspec = pl.BlockSpec((1, S5_STEP_PAIRS, 2 * S5_BLK, 2 * S5_BLK), lambda j, d: (d, j, 0, 0))
    pspec = pl.BlockSpec((S5_STEP_W, S5_STEP_W), lambda j, d: (0, 0))
    y = pl.pallas_call(
        _s5_body,
        grid=(n_steps, 2),
        in_specs=[
            pl.BlockSpec((NTOK, S5_SET_CH), lambda j, d: (0, j)),
            pspec, pspec, wspec, wspec, wspec,
            pl.BlockSpec((1, SUBLANES, S5_STEP_W), lambda j, d: (d, 0, j)),
            pl.BlockSpec((1, SUBLANES, S5_STEP_W), lambda j, d: (d, 0, j)),
            pl.BlockSpec((1, S5_SET_CH), lambda j, d: (0, j)),
        ],
        out_specs=pl.BlockSpec((NTOK, S5_SET_CH), lambda j, d: (0, j)),
        out_shape=jax.ShapeDtypeStruct((NTOK, S5_W), F32),
        scratch_shapes=[pltpu.VMEM((S5_ROWS, S5_STEP_W), BF16), pltpu.VMEM((S5_ROWS, S5_STEP_W), F32),
                        pltpu.VMEM((S5_ROWS, S5_STEP_W), F32)],
        compiler_params=_cparams(("arbitrary", "arbitrary")),
        name="s5_scan",
    )(z, perm, perm.T, kc, wb, wc, amul, apow, d_skip.astype(F32).reshape(1, S5_W))

    tm = MM_TILE_M
    return pl.pallas_call(
        _s5_glu_body,
        grid=(NTOK // tm,),
        in_specs=[
            pl.BlockSpec((tm, S5_W), lambda i: (i, 0)),
            pl.BlockSpec((S5_W, S5_W), lambda i: (0, 0)),
            pl.BlockSpec((1, S5_W), lambda i: (0, 0)),
        ],
        out_specs=pl.BlockSpec((tm, S5_W), lambda i: (i, 0)),
        out_shape=jax.ShapeDtypeStruct((NTOK, S5_W), BF16),
        compiler_params=_cparams(("arbitrary",)),
        name="s5_glu",
    )(y, w_glu, b_glu.reshape(1, S5_W))


CONV_PAD = 16
CONV_BLK = 64


def _conv_body(a_ref, b_ref, w_ref, db_ref, lg_ref, lb_ref, o_ref, pad_ref):
    g = a_ref[...] * jax.nn.sigmoid(b_ref[...])

    def run(seq_len):
        n_seq = ROW_TILE // seq_len
        pitch = seq_len + 2 * CONV_PAD
        zeros = jnp.zeros((CONV_PAD, CONV_W), F32)
        for s in range(n_seq):
            pad_ref[s * pitch:s * pitch + CONV_PAD, :] = zeros
            pad_ref[s * pitch + CONV_PAD:s * pitch + CONV_PAD + seq_len, :] = g[s * seq_len:(s + 1) * seq_len]
            pad_ref[s * pitch + CONV_PAD + seq_len:(s + 1) * pitch, :] = zeros
        for blk in range(ROW_TILE // CONV_BLK):
            row0 = blk * CONV_BLK
            s, q = divmod(row0, seq_len)
            base = s * pitch + CONV_PAD + q - CONV_K // 2
            acc = jnp.zeros((CONV_BLK, CONV_W), F32)
            for k in range(CONV_K):
                acc = acc + w_ref[k:k + 1, :] * pad_ref[base + k:base + k + CONV_BLK, :]
            y = acc + db_ref[...]
            yc = y - jnp.mean(y, axis=-1, keepdims=True)
            var = jnp.mean(yc * yc, axis=-1, keepdims=True)
            y = yc * lax.rsqrt(var + LN_EPS) * lg_ref[...] + lb_ref[...]
            o_ref[row0:row0 + CONV_BLK, :] = jax.nn.silu(y).astype(BF16)

    is_ctx = pl.program_id(0) == 0

    @pl.when(is_ctx)
    def _():
        run(CTX)

    @pl.when(jnp.logical_not(is_ctx))
    def _():
        run(GRID_W)


def _conv_module(z, dw_w, dw_b, ln_g, ln_b):
    w = jnp.concatenate([dw_w, jnp.zeros((1, CONV_W), dw_w.dtype)], axis=0)
    vec = lambda v: v.reshape(1, CONV_W)
    pad_rows = max(CTX + 2 * CONV_PAD, (ROW_TILE // GRID_W) * (GRID_W + 2 * CONV_PAD))
    return pl.pallas_call(
        _conv_body,
        grid=(NTOK // ROW_TILE,),
        in_specs=[
            pl.BlockSpec((ROW_TILE, CONV_W), lambda i: (i, OFF_CONV // CONV_W)),
            pl.BlockSpec((ROW_TILE, CONV_W), lambda i: (i, OFF_CONV // CONV_W + 1)),
            pl.BlockSpec((CONV_K + 1, CONV_W), lambda i: (0, 0)),
            pl.BlockSpec((1, CONV_W), lambda i: (0, 0)),
            pl.BlockSpec((1, CONV_W), lambda i: (0, 0)),
            pl.BlockSpec((1, CONV_W), lambda i: (0, 0)),
        ],
        out_specs=pl.BlockSpec((ROW_TILE, CONV_W), lambda i: (i, 0)),
        out_shape=jax.ShapeDtypeStruct((NTOK, CONV_W), BF16),
        scratch_shapes=[pltpu.VMEM((pad_rows, CONV_W), F32)],
        compiler_params=_cparams(("arbitrary",)),
        name="conv_module",
    )(z, z, w, vec(dw_b), vec(ln_g), vec(ln_b))


def _ml_chunk_index(ci, reverse):
    if not reverse:
        return ci
    return jnp.where(ci == 0, 0, ML_NCHUNK - ci)


def _ml_direction(qb, kb, vx, li_col, li_row, b_col, b_row, cx_ref, m_ref, mask, reverse):
    c = ML_CHUNK
    scale = ML_DH ** 0.5
    m = m_ref[:, 0:1]
    cx = cx_ref[...]

    d_log = jnp.where(mask, b_col + (li_row - b_row), -jnp.inf)
    inter = b_col + m
    m_row = jnp.maximum(inter, jnp.max(d_log, axis=-1, keepdims=True))
    qk = lax.dot_general(qb, kb, (((1,), (1,)), ((), ())), preferred_element_type=F32)
    s = qk * jnp.exp(d_log - m_row)
    w_inter = jnp.exp(inter - m_row) * scale
    num = jnp.dot(s.astype(BF16), vx, preferred_element_type=F32)
    num += w_inter * jnp.dot(qb, cx.astype(BF16), preferred_element_type=F32)
    den = num[:, ML_DH:ML_DH + 1]
    h = num[:, :ML_DH] / jnp.maximum(jnp.abs(den), jnp.exp(-m_row) * scale)

    b_tot = b_row[:, 0:1] if reverse else b_row[:, c - 1:c]
    g = b_tot - b_col + li_col
    m_new = jnp.maximum(b_tot + m, jnp.max(g, axis=0, keepdims=True))
    kw = kb * (jnp.exp(g - m_new) * (1.0 / scale)).astype(BF16)
    decay = jnp.exp(b_tot + m - m_new)
    cx_ref[...] = decay * cx + lax.dot_general(kw, vx, (((0,), (0,)), ((), ())),
                                               preferred_element_type=F32)
    m_ref[...] = jnp.broadcast_to(m_new, m_ref.shape)
    return h


def _mlstm_body(qkvf_ref, gf_ref, gtf_ref, qkvb_ref, gb_ref, gtb_ref, hf_ref, hb_ref, cx_ref, m_ref):
    @pl.when(pl.program_id(0) == 0)
    def _():
        cx_ref[...] = jnp.zeros_like(cx_ref)
        m_ref[...] = jnp.zeros_like(m_ref)

    c = ML_CHUNK
    r_idx = lax.broadcasted_iota(jnp.int32, (c, c), 0)
    c_idx = lax.broadcasted_iota(jnp.int32, (c, c), 1)
    ones_col = jnp.where(lax.broadcasted_iota(jnp.int32, (c, LANES), 1) == 0, 1.0, 0.0).astype(BF16)

    for d, (qkv_ref, g_ref, gt_ref, h_ref) in enumerate(
            ((qkvf_ref, gf_ref, gtf_ref, hf_ref), (qkvb_ref, gb_ref, gtb_ref, hb_ref))):
        reverse = d == 1
        gates = g_ref[...]
        gates_t = gt_ref[...]
        incl = (c_idx >= r_idx) if reverse else (c_idx <= r_idx)
        tri = jnp.where(incl, 1.0, 0.0).astype(BF16)
        incl_t = (r_idx >= c_idx) if reverse else (r_idx <= c_idx)
        tri_t = jnp.where(incl_t, 1.0, 0.0).astype(BF16)
        lf = jax.nn.log_sigmoid(gates)
        lf_t = jax.nn.log_sigmoid(gates_t)
        b_all = sum(jnp.dot(tri, p, preferred_element_type=F32) for p in _split3(lf))
        b_all_t = sum(jnp.dot(p, tri_t, preferred_element_type=F32) for p in _split3(lf_t))
        for head in range(ML_H):
            i_col = 2 * ML_H * d + head
            f_col = i_col + ML_H
            state = d * ML_H + head
            col = lambda part: slice((part * ML_H + head) * ML_DH, (part * ML_H + head + 1) * ML_DH)
            vx = jnp.concatenate([qkv_ref[:, col(2)], ones_col], axis=1)
            h = _ml_direction(
                qkv_ref[:, col(0)], qkv_ref[:, col(1)], vx,
                gates[:, i_col:i_col + 1], gates_t[i_col:i_col + 1, :],
                b_all[:, f_col:f_col + 1], b_all_t[f_col:f_col + 1, :],
                cx_ref.at[state], m_ref.at[state], incl, reverse)
            h_ref[:, head * ML_DH:(head + 1) * ML_DH] = h


def _mlstm_readout_body(hf_ref, hb_ref, o_ref, g_ref, out_ref):
    h = hf_ref[...] + hb_ref[...]
    hc = h - jnp.mean(h, axis=-1, keepdims=True)
    var = jnp.mean(hc * hc, axis=-1, keepdims=True)
    h = hc * lax.rsqrt(var + LN_EPS) * g_ref[...]
    out_ref[...] = (jax.nn.sigmoid(o_ref[...]) * h).astype(BF16)


def _mlstm_mixer(qkv, z, o_col, gates, norm_g):
    gates_t = jnp.transpose(gates[:, :2 * SUBLANES])
    c = ML_CHUNK

    def specs(reverse):
        row = lambda ci: _ml_chunk_index(ci, reverse)
        return [
            pl.BlockSpec((c, 3 * ML_W), lambda ci: (row(ci), 0)),
            pl.BlockSpec((c, LANES), lambda ci: (row(ci), 0)),
            pl.BlockSpec((2 * SUBLANES, c), lambda ci: (0, row(ci))),
        ]

    out_spec = lambda reverse: pl.BlockSpec((c, ML_W), lambda ci: (_ml_chunk_index(ci, reverse), 0))
    n_state = 2 * ML_H
    h_f, h_b = pl.pallas_call(
        _mlstm_body,
        grid=(ML_NCHUNK,),
        in_specs=specs(False) + specs(True),
        out_specs=[out_spec(False), out_spec(True)],
        out_shape=[jax.ShapeDtypeStruct((NTOK, ML_W), F32)] * 2,
        scratch_shapes=[pltpu.VMEM((n_state, ML_DH, ML_DH + LANES), F32),
                        pltpu.VMEM((n_state, 1, LANES), F32)],
        compiler_params=_cparams(("arbitrary",)),
        name="mlstm_chunks",
    )(qkv, gates, gates_t, qkv, gates, gates_t)

    tm = MM_TILE_M
    return pl.pallas_call(
        _mlstm_readout_body,
        grid=(NTOK // tm, ML_H),
        in_specs=[
            pl.BlockSpec((tm, ML_DH), lambda i, hh: (i, hh)),
            pl.BlockSpec((tm, ML_DH), lambda i, hh: (i, hh)),
            pl.BlockSpec((tm, ML_DH), lambda i, hh: (i, o_col // ML_DH + hh)),
            pl.BlockSpec((1, ML_DH), lambda i, hh: (0, hh)),
        ],
        out_specs=pl.BlockSpec((tm, ML_DH), lambda i, hh: (i, hh)),
        out_shape=jax.ShapeDtypeStruct((NTOK, ML_W), BF16),
        compiler_params=_cparams(("arbitrary", "arbitrary")),
        name="mlstm_readout",
    )(h_f, h_b, z, norm_g.reshape(1, ML_W))


def kernel(x, c, ctx, c_ctx, w_mod, b_mod, norm1_g, w_in, b_in, s5_lam_re, s5_lam_im, s5_log_step,
           s5_b_re, s5_b_im, s5_c_re, s5_c_im, s5_d, s5_w_glu, s5_b_glu, conv_dw_w, conv_dw_b,
           conv_ln_g, conv_ln_b, ml_norm_g, w_out, norm2_g, w_ffn_in, w_ffn_out, norm_f_g):
    assert x.shape == (1, SEQ, D_MODEL) and ctx.shape == (1, CTX, D_MODEL)
    xs = jnp.concatenate([ctx[0], x[0]], axis=0).astype(F32)
    cc = jnp.zeros((SUBLANES, D_MODEL), F32).at[0].set(c[0]).at[1].set(c_ctx)
    mod_all = _modulation(cc, w_mod, b_mod)
    s5_ops = jax.vmap(_s5_operators)(s5_lam_re, s5_lam_im, s5_log_step, s5_b_re, s5_b_im,
                                     s5_c_re, s5_c_im)
    w_in_t = jnp.swapaxes(w_in, 1, 2)
    n_gate = w_in.shape[2] - OFF_G
    w_gate = jnp.pad(w_in_t[:, OFF_G:, :], ((0, 0), (0, LANES - n_gate), (0, 0)))
    b_gate = jnp.pad(b_in[:, OFF_G:], ((0, 0), (0, LANES - n_gate))).reshape(DEPTH, 1, LANES)
    tn = MM_TILE_N
    f32_tiles = OFF_Q // tn
    o_col = f32_tiles * tn

    for l in range(DEPTH):
        h, gates = _norm_mod(xs, norm1_g[l], mod_all, l, 0, w_gate[l], b_gate[l])
        z = _in_proj(h, w_in_t, b_in, l, lambda j: jnp.where(j < f32_tiles, j, j + (OFF_O - OFF_Q) // tn),
                     f32_tiles + (OFF_G - OFF_O) // tn, F32)
        qkv = _in_proj(h, w_in_t, b_in, l, lambda j: j + OFF_Q // tn, (OFF_O - OFF_Q) // tn, BF16)
        s5o = _s5_mixer(z, [op[l] for op in s5_ops], s5_d[l], s5_w_glu[l], s5_b_glu[l])
        cvo = _conv_module(z, conv_dw_w[l], conv_dw_b[l], conv_ln_g[l], conv_ln_b[l])
        mlo = _mlstm_mixer(qkv, z, o_col, gates, ml_norm_g[l])
        xs = _out_proj(xs, s5o, cvo, mlo, w_out, mod_all, l)
        (h2,) = _norm_mod(xs, norm2_g[l], mod_all, l, 1)
        hid = _ffn_in(h2, w_ffn_in, l)
        xs = _ffn_out(xs, hid, w_ffn_out, mod_all, l)
    return _final_norm(xs, norm_f_g)[None]
```

```python
import functools

import numpy as np
import jax
import jax.numpy as jnp
from jax import lax
from jax.experimental import pallas as pl
from jax.experimental.pallas import tpu as pltpu

F32 = jnp.float32
BF16 = jnp.bfloat16

D_MODEL = 2048
SEQ = 8192
CTX = 256
NTOK = SEQ + CTX
DEPTH = 4
GRID_W = 64

S5_W = 512
S5_CH = 16
S5_G = 32
S5_P = 64
CONV_W = 512
CONV_K = 31
ML_W = 1024
ML_H = 4
ML_DH = 256
D_FF = 5632
EPS = 1e-6
LN_EPS = 1e-5

OFF_CONV = 512
OFF_Q = 1536
OFF_K = 2560
OFF_V = 3584
OFF_O = 4608
OFF_G = 5632

LANES = 128
SUBLANES = 8
VMEM_LIMIT = 56 * 1024 * 1024

ROW_TILE = 256
MM_TILE_M = 768
MM_TILE_N = 512

S5_T = 8
S5_ROWS = NTOK // S5_T
S5_CTX_ROWS = CTX // S5_T
S5_BLK = S5_T * S5_CH
S5_STEP_G = 8
S5_STEP_PAIRS = S5_STEP_G // 2
S5_SET_CH = S5_STEP_G * S5_CH
S5_STEP_W = S5_STEP_G * S5_BLK
S5_FLAT = S5_G * S5_BLK

ML_CHUNK = 256
ML_NCHUNK = NTOK // ML_CHUNK


def _cparams(sem, vmem=VMEM_LIMIT):
    return pltpu.CompilerParams(dimension_semantics=sem, vmem_limit_bytes=vmem)


def _dot(a, b):
    return jnp.dot(a.astype(BF16), b.astype(BF16), preferred_element_type=F32)


def _split3(x):
    a = x.astype(BF16)
    r = x - a.astype(F32)
    b = r.astype(BF16)
    c = (r - b.astype(F32)).astype(BF16)
    return a, b, c


def _mod_body(cc_ref, w_ref, b_ref, o_ref):
    s = jax.nn.silu(cc_ref[...])
    o_ref[0] = _dot(s, w_ref[0]) + b_ref[0]


def _modulation(cc, w_mod, b_mod):
    depth, _, n = w_mod.shape
    tn = 1024
    return pl.pallas_call(
        _mod_body,
        grid=(depth, n // tn),
        in_specs=[
            pl.BlockSpec((SUBLANES, D_MODEL), lambda l, j: (0, 0)),
            pl.BlockSpec((1, D_MODEL, tn), lambda l, j: (l, 0, j)),
            pl.BlockSpec((1, 1, tn), lambda l, j: (l, 0, j)),
        ],
        out_specs=pl.BlockSpec((1, SUBLANES, tn), lambda l, j: (l, 0, j)),
        out_shape=jax.ShapeDtypeStruct((depth, SUBLANES, n), F32),
        compiler_params=_cparams(("arbitrary", "arbitrary")),
        name="adaln_modulation",
    )(cc, w_mod, b_mod.reshape(depth, 1, n))


def _mod_row(m_ref, is_ctx):
    return jnp.where(is_ctx, m_ref[0, 1:2, :], m_ref[0, 0:1, :])


def _norm_body(*refs, modulate, with_gates):
    it = iter(refs)
    x_ref, g_ref = next(it), next(it)
    sh_ref = sc_ref = wg_ref = bg_ref = gate_ref = None
    if modulate:
        sh_ref, sc_ref = next(it), next(it)
    if with_gates:
        wg_ref, bg_ref = next(it), next(it)
    h_ref = next(it)
    if with_gates:
        gate_ref = next(it)

    xf = x_ref[...]
    ms = jnp.mean(xf * xf, axis=-1, keepdims=True)
    h = xf * lax.rsqrt(ms + EPS) * g_ref[...]
    if modulate:
        is_ctx = pl.program_id(0) == 0
        h = h * (1.0 + _mod_row(sc_ref, is_ctx)) + _mod_row(sh_ref, is_ctx)
    h_ref[...] = h.astype(h_ref.dtype)
    if with_gates:
        hi = h.astype(BF16)
        lo = (h - hi.astype(F32)).astype(BF16)
        w = wg_ref[...]
        whi = w.astype(BF16)
        wlo = (w - whi.astype(F32)).astype(BF16)
        nt = lambda a, b: lax.dot_general(a, b, (((1,), (1,)), ((), ())), preferred_element_type=F32)
        gate_ref[...] = nt(hi, whi) + nt(hi, wlo) + nt(lo, whi) + bg_ref[...]


def _norm_mod(xs, g, mod_all, layer, phase, w_gate=None, b_gate=None):
    with_gates = w_gate is not None
    n_tiles = NTOK // ROW_TILE
    in_specs = [
        pl.BlockSpec((ROW_TILE, D_MODEL), lambda i: (i, 0)),
        pl.BlockSpec((1, D_MODEL), lambda i: (0, 0)),
        pl.BlockSpec((1, SUBLANES, D_MODEL), lambda i: (layer, 0, 3 * phase)),
        pl.BlockSpec((1, SUBLANES, D_MODEL), lambda i: (layer, 0, 3 * phase + 1)),
    ]
    args = [xs, g.reshape(1, D_MODEL), mod_all, mod_all]
    out_specs = [pl.BlockSpec((ROW_TILE, D_MODEL), lambda i: (i, 0))]
    out_shape = [jax.ShapeDtypeStruct((NTOK, D_MODEL), BF16)]
    if with_gates:
        in_specs += [pl.BlockSpec((LANES, D_MODEL), lambda i: (0, 0)),
                     pl.BlockSpec((1, LANES), lambda i: (0, 0))]
        args += [w_gate, b_gate]
        out_specs.append(pl.BlockSpec((ROW_TILE, LANES), lambda i: (i, 0)))
        out_shape.append(jax.ShapeDtypeStruct((NTOK, LANES), F32))
    return pl.pallas_call(
        functools.partial(_norm_body, modulate=True, with_gates=with_gates),
        grid=(n_tiles,),
        in_specs=in_specs,
        out_specs=out_specs,
        out_shape=out_shape,
        compiler_params=_cparams(("arbitrary",)),
        name="rmsnorm_modulate",
    )(*args)


def _final_norm(xs, g):
    skip = CTX // ROW_TILE
    return pl.pallas_call(
        functools.partial(_norm_body, modulate=False, with_gates=False),
        grid=(SEQ // ROW_TILE,),
        in_specs=[pl.BlockSpec((ROW_TILE, D_MODEL), lambda i: (i + skip, 0)),
                  pl.BlockSpec((1, D_MODEL), lambda i: (0, 0))],
        out_specs=[pl.BlockSpec((ROW_TILE, D_MODEL), lambda i: (i, 0))],
        out_shape=[jax.ShapeDtypeStruct((SEQ, D_MODEL), F32)],
        compiler_params=_cparams(("arbitrary",)),
        name="final_rmsnorm",
    )(xs, g.reshape(1, D_MODEL))[0]


def _in_proj_body(*refs, n_w):
    a_ref, wt_refs, b_refs = refs[0], refs[1:1 + n_w], refs[1 + n_w:1 + 2 * n_w]
    o_ref, wbf_ref = refs[1 + 2 * n_w:]
    tn = MM_TILE_N

    @pl.when(pl.program_id(1) == 0)
    def _():
        for k, wt_ref in enumerate(wt_refs):
            wbf_ref[:, k * tn:(k + 1) * tn] = jnp.transpose(wt_ref[...]).astype(BF16)

    bias = jnp.concatenate([b_ref[...] for b_ref in b_refs], axis=1)
    acc = jnp.dot(a_ref[...], wbf_ref[...], preferred_element_type=F32) + bias
    o_ref[...] = acc.astype(o_ref.dtype)


def _in_proj(h, w_in_t, b_in, layer, first_tile, n_steps, n_w, dtype):
    tm, tn = MM_TILE_M, MM_TILE_N
    in_w = w_in_t.shape[1]
    tile = lambda j, k: first_tile + n_w * j + k
    w_specs = [pl.BlockSpec((None, tn, D_MODEL), functools.partial(lambda j, i, k: (layer, tile(j, k), 0), k=k))
               for k in range(n_w)]
    b_specs = [pl.BlockSpec((None, 1, tn), functools.partial(lambda j, i, k: (layer, 0, tile(j, k)), k=k))
               for k in range(n_w)]
    b3 = b_in.reshape(DEPTH, 1, in_w)
    return pl.pallas_call(
        functools.partial(_in_proj_body, n_w=n_w),
        grid=(n_steps, NTOK // tm),
        in_specs=[pl.BlockSpec((tm, D_MODEL), lambda j, i: (i, 0))] + w_specs + b_specs,
        out_specs=pl.BlockSpec((tm, n_w * tn), lambda j, i: (i, j)),
        out_shape=jax.ShapeDtypeStruct((NTOK, n_steps * n_w * tn), dtype),
        scratch_shapes=[pltpu.VMEM((D_MODEL, n_w * tn), BF16)],
        compiler_params=_cparams(("arbitrary", "arbitrary")),
        name="in_proj",
    )(h, *([w_in_t] * n_w), *([b3] * n_w))


def _row_gate(g_ref, i, tm, tn):
    rows = i * tm + lax.broadcasted_iota(jnp.int32, (tm, tn), 0)
    return jnp.where(rows < CTX, g_ref[0, 1:2, :], g_ref[0, 0:1, :])


def _out_proj_body(s5_ref, cv_ref, ml_ref, w_ref, gate_ref, sh_ref, sc_ref, ng_ref, x_ref,
                   xo_ref, h_ref, wbf_ref):
    i = pl.program_id(0)

    @pl.when(i == 0)
    def _():
        wbf_ref[...] = w_ref[...].astype(BF16)

    acc = jnp.dot(s5_ref[...], wbf_ref[0:S5_W, :], preferred_element_type=F32)
    acc += jnp.dot(cv_ref[...], wbf_ref[S5_W:S5_W + CONV_W, :], preferred_element_type=F32)
    acc += jnp.dot(ml_ref[...], wbf_ref[S5_W + CONV_W:, :], preferred_element_type=F32)
    is_ctx = i == 0
    xn = x_ref[...] + _mod_row(gate_ref, is_ctx) * acc
    xo_ref[...] = xn
    ms = jnp.mean(xn * xn, axis=-1, keepdims=True)
    h = xn * lax.rsqrt(ms + EPS) * ng_ref[...]
    h_ref[...] = (h * (1.0 + _mod_row(sc_ref, is_ctx)) + _mod_row(sh_ref, is_ctx)).astype(BF16)


def _out_proj(xs, s5o, cvo, mlo, w_out, mod_all, norm_g, layer):
    tm = ROW_TILE
    mod = lambda k: pl.BlockSpec((1, SUBLANES, D_MODEL), lambda i: (layer, 0, k))
    return pl.pallas_call(
        _out_proj_body,
        grid=(NTOK // tm,),
        in_specs=[
            pl.BlockSpec((tm, S5_W), lambda i: (i, 0)),
            pl.BlockSpec((tm, CONV_W), lambda i: (i, 0)),
            pl.BlockSpec((tm, ML_W), lambda i: (i, 0)),
            pl.BlockSpec((None, D_MODEL, D_MODEL), lambda i: (layer, 0, 0), pipeline_mode=pl.Buffered(1)),
            mod(2), mod(3), mod(4),
            pl.BlockSpec((1, D_MODEL), lambda i: (0, 0)),
            pl.BlockSpec((tm, D_MODEL), lambda i: (i, 0)),
        ],
        out_specs=[pl.BlockSpec((tm, D_MODEL), lambda i: (i, 0)), pl.BlockSpec((tm, D_MODEL), lambda i: (i, 0))],
        out_shape=[jax.ShapeDtypeStruct((NTOK, D_MODEL), F32), jax.ShapeDtypeStruct((NTOK, D_MODEL), BF16)],
        scratch_shapes=[pltpu.VMEM((D_MODEL, D_MODEL), BF16)],
        compiler_params=_cparams(("arbitrary",)),
        name="out_proj_residual",
    )(s5o, cvo, mlo, w_out, mod_all, mod_all, mod_all, norm_g.reshape(1, D_MODEL), xs)


def _ffn_in_body(a_ref, wg_ref, wu_ref, o_ref, wgbf_ref, wubf_ref):
    @pl.when(pl.program_id(1) == 0)
    def _():
        wgbf_ref[...] = wg_ref[...].astype(BF16)
        wubf_ref[...] = wu_ref[...].astype(BF16)

    a = a_ref[...]
    g = jnp.dot(a, wgbf_ref[...], preferred_element_type=F32)
    u = jnp.dot(a, wubf_ref[...], preferred_element_type=F32)
    o_ref[...] = (jax.nn.silu(g) * u).astype(BF16)


def _ffn_in(h, w_ffn_in, layer):
    tm, tn = MM_TILE_M, MM_TILE_N
    nj = D_FF // tn
    return pl.pallas_call(
        _ffn_in_body,
        grid=(nj, NTOK // tm),
        in_specs=[
            pl.BlockSpec((tm, D_MODEL), lambda j, i: (i, 0)),
            pl.BlockSpec((None, D_MODEL, tn), lambda j, i: (layer, 0, j)),
            pl.BlockSpec((None, D_MODEL, tn), lambda j, i: (layer, 0, nj + j)),
        ],
        out_specs=pl.BlockSpec((tm, tn), lambda j, i: (i, j)),
        out_shape=jax.ShapeDtypeStruct((NTOK, D_FF), BF16),
        scratch_shapes=[pltpu.VMEM((D_MODEL, tn), BF16), pltpu.VMEM((D_MODEL, tn), BF16)],
        compiler_params=_cparams(("arbitrary", "arbitrary")),
        name="ffn_in_swiglu",
    )(h, w_ffn_in, w_ffn_in)


def _ffn_out_body(a_ref, w_ref, g_ref, x_ref, o_ref, wbf_ref):
    i = pl.program_id(1)

    @pl.when(i == 0)
    def _():
        wbf_ref[...] = w_ref[...].astype(BF16)

    acc = jnp.dot(a_ref[...], wbf_ref[...], preferred_element_type=F32)
    tm, tn = o_ref.shape
    o_ref[...] = x_ref[...] + _row_gate(g_ref, i, tm, tn) * acc


def _ffn_out(xs, hid, w_ffn_out, mod_all, layer):
    tm, tn = MM_TILE_M // 2, MM_TILE_N
    return pl.pallas_call(
        _ffn_out_body,
        grid=(D_MODEL // tn, NTOK // tm),
        in_specs=[
            pl.BlockSpec((tm, D_FF), lambda j, i: (i, 0)),
            pl.BlockSpec((None, D_FF, tn), lambda j, i: (layer, 0, j)),
            pl.BlockSpec((1, SUBLANES, tn), lambda j, i: (layer, 0, 5 * (D_MODEL // tn) + j)),
            pl.BlockSpec((tm, tn), lambda j, i: (i, j)),
        ],
        out_specs=pl.BlockSpec((tm, tn), lambda j, i: (i, j)),
        out_shape=jax.ShapeDtypeStruct((NTOK, D_MODEL), F32),
        scratch_shapes=[pltpu.VMEM((D_FF, tn), BF16)],
        compiler_params=_cparams(("arbitrary", "arbitrary")),
        name="ffn_out_residual",
    )(hid, w_ffn_out, mod_all, xs)


def _s5_selectors():
    tau = np.arange(S5_BLK)[:, None] // S5_CH
    ch_r = np.arange(S5_BLK)[:, None] % S5_CH
    t = np.arange(S5_BLK)[None, :] // S5_CH
    ch_c = np.arange(S5_BLK)[None, :] % S5_CH
    sel = np.zeros((2, S5_T, S5_BLK, S5_BLK), np.float32)
    for s in range(S5_T):
        sel[0, s] = (tau == t - s) & (ch_r == ch_c)
        sel[1, s] = (tau == s - t) & (ch_r == ch_c)
    rp = np.arange(S5_BLK)[:, None]
    col = np.arange(2 * S5_BLK)[None, :]
    pair_map = np.stack([(col == (rp // S5_P) * S5_BLK + gi * S5_P + rp % S5_P) for gi in range(2)])
    src = np.arange(S5_STEP_W)
    dst = ((src // S5_CH) % S5_STEP_G) * S5_BLK + (src // S5_SET_CH) * S5_CH + src % S5_CH
    perm = dst[:, None] == np.arange(S5_STEP_W)[None, :]
    return (jnp.asarray(sel), jnp.asarray(pair_map, BF16), jnp.asarray(perm, BF16),
            jnp.asarray(perm.T, BF16))


def _s5_operators(lam_re, lam_im, log_step, b_re, b_im, c_re, c_im, sel):
    lam = lax.complex(lam_re.astype(F32), lam_im.astype(F32))
    lam_bar = jnp.exp(lam * jnp.exp(log_step.astype(F32)))
    b_bar = ((lam_bar - 1.0) / lam)[..., None] * lax.complex(b_re.astype(F32), b_im.astype(F32))
    c_mat = lax.complex(c_re.astype(F32), c_im.astype(F32))
    depth = lam.shape[0]

    def powers(base, count):
        out = [jnp.ones_like(base)]
        for _ in range(count - 1):
            out.append(out[-1] * base)
        return out

    pw = powers(lam_bar, S5_T + 1)
    pa = powers(pw[S5_T], SUBLANES + 1)
    t_up = list(range(S5_T))

    def table(seq, fwd_idx, bwd_idx):
        return jnp.stack([jnp.stack([seq[f][:, 0], seq[b][:, 1]], axis=1)
                          for f, b in zip(fwd_idx, bwd_idx)])

    blocks = (depth, 2, S5_G, S5_BLK, S5_BLK)

    def token_rows(w, imag_sign):
        w = jnp.moveaxis(w, 0, 3)
        return jnp.concatenate([jnp.real(w), imag_sign * jnp.imag(w)], axis=-1).reshape(blocks)

    pb = table(pw, [S5_T - 1 - t for t in t_up], t_up)
    wb = token_rows(pb[:, :, :, :, None, :] * jnp.swapaxes(b_bar, -1, -2)[None], 1.0)
    pc = table(pw, [t + 1 for t in t_up], [S5_T - t for t in t_up])
    wct = token_rows(pc[:, :, :, :, None, :] * c_mat[None], -1.0)

    kern = jnp.real(jnp.einsum('ldgcp,kldgp,ldgpa->ldgakc', c_mat, jnp.stack(pw[:S5_T]), b_bar))
    kern = kern.reshape(depth, 2, S5_G * S5_CH, S5_BLK)
    kc = jnp.einsum('ldxk,dskn->ldsxn', kern, sel, precision=lax.Precision.HIGHEST)
    kc = kc.reshape(depth, 2, S5_T, S5_G, S5_CH, S5_BLK)
    kc = jnp.transpose(kc, (0, 1, 3, 2, 4, 5)).reshape(blocks)

    def lanes(v):
        parts = jnp.stack([jnp.real(v), jnp.imag(v)], axis=4)
        parts = parts.reshape(v.shape[0], depth, 2, S5_G // 2, 2, 2, S5_P)
        parts = jnp.swapaxes(parts, 4, 5).reshape(v.shape[0], depth, 2, S5_FLAT)
        return jnp.transpose(parts, (1, 2, 0, 3))

    zero_p = jnp.zeros_like(pa[0])
    amul = lanes(jnp.stack([pa[1], pa[2], pa[4], pa[8]] + [zero_p] * 4))
    apow = lanes(table(pa, t_up, t_up[::-1]))
    return kc.astype(BF16), wb.astype(BF16), wct.astype(BF16), amul, apow


def _s5_row_scan(e_ref, amul_ref, apow_ref, block_lo, block_hi, reverse, carry):
    rows = lax.broadcasted_iota(jnp.int32, (SUBLANES, S5_BLK), 0)

    def shifted(x, k):
        if reverse:
            return jnp.where(rows < SUBLANES - k, pltpu.roll(x, SUBLANES - k, 0), 0.0)
        return jnp.where(rows >= k, pltpu.roll(x, k, 0), 0.0)

    def body(step, carry):
        blk = (block_hi - 1 - step) if reverse else (block_lo + step)
        r0 = pl.multiple_of(blk * SUBLANES, SUBLANES)
        last = 0 if reverse else SUBLANES - 1
        carry_out = []
        for q in range(S5_STEP_PAIRS):
            re_c = slice(2 * q * S5_BLK, (2 * q + 1) * S5_BLK)
            im_c = slice((2 * q + 1) * S5_BLK, (2 * q + 2) * S5_BLK)
            s_re = e_ref[pl.ds(r0, SUBLANES), re_c]
            s_im = e_ref[pl.ds(r0, SUBLANES), im_c]
            for idx, k in enumerate((1, 2, 4)):
                a_re, a_im = amul_ref[idx:idx + 1, re_c], amul_ref[idx:idx + 1, im_c]
                t_re, t_im = shifted(s_re, k), shifted(s_im, k)
                s_re, s_im = s_re + t_re * a_re - t_im * a_im, s_im + t_re * a_im + t_im * a_re
            c_re, c_im = carry[:, re_c], carry[:, im_c]
            p_re, p_im = apow_ref[:, re_c], apow_ref[:, im_c]
            e_ref[pl.ds(r0, SUBLANES), re_c] = shifted(s_re, 1) + p_re * c_re - p_im * c_im
            e_ref[pl.ds(r0, SUBLANES), im_c] = shifted(s_im, 1) + p_re * c_im + p_im * c_re
            a_re, a_im = amul_ref[3:4, re_c], amul_ref[3:4, im_c]
            l_re = jnp.broadcast_to(s_re[last:last + 1, :], (SUBLANES, S5_BLK))
            l_im = jnp.broadcast_to(s_im[last:last + 1, :], (SUBLANES, S5_BLK))
            carry_out += [l_re + a_re * c_re - a_im * c_im, l_im + a_re * c_im + a_im * c_re]
        return jnp.concatenate(carry_out, axis=1)

    return lax.fori_loop(0, block_hi - block_lo, body, carry)


def _s5_pair_operators(kc_ref, wb_ref, wct_ref, map_ref, q):
    g0, g1 = 2 * q, 2 * q + 1
    zero = jnp.zeros((S5_BLK, S5_BLK), BF16)
    kc = jnp.concatenate([jnp.concatenate([kc_ref[g0], zero], axis=1),
                          jnp.concatenate([zero, kc_ref[g1]], axis=1)], axis=0)
    spread = lambda ref: jnp.concatenate(
        [jnp.dot(ref[g0], map_ref[0], preferred_element_type=F32),
         jnp.dot(ref[g1], map_ref[1], preferred_element_type=F32)], axis=0).astype(BF16)
    return kc, spread(wb_ref), spread(wct_ref)


def _s5_body(u_ref, p_ref, pt_ref, map_ref, kc_ref, wb_ref, wct_ref, amul_ref, apow_ref, d_ref, y_ref,
             up_ref, e_ref, acc_ref):
    direction = pl.program_id(1)
    token_rows = lambda t: pl.ds(t, S5_ROWS, stride=S5_T)
    pairs = [slice(2 * q * S5_BLK, 2 * (q + 1) * S5_BLK) for q in range(S5_STEP_PAIRS)]

    @pl.when(direction == 0)
    def _():
        u_nat = jnp.concatenate([u_ref[token_rows(t), :] for t in range(S5_T)], axis=1)
        up_ref[...] = jnp.dot(u_nat.astype(BF16), p_ref[...], preferred_element_type=F32).astype(BF16)
        acc_ref[...] = jnp.zeros_like(acc_ref)

    ops = [_s5_pair_operators(kc_ref, wb_ref, wct_ref, map_ref, q) for q in range(S5_STEP_PAIRS)]
    for (_, wb, _), cols in zip(ops, pairs):
        e_ref[:, cols] = jnp.dot(up_ref[:, cols], wb, preferred_element_type=F32)

    zero = jnp.zeros((SUBLANES, S5_STEP_W), F32)
    n_blocks = S5_ROWS // SUBLANES
    ctx_blocks = S5_CTX_ROWS // SUBLANES
    amul, apow = amul_ref, apow_ref

    @pl.when(direction == 0)
    def _():
        _s5_row_scan(e_ref, amul, apow, 0, n_blocks, False, zero)

    @pl.when(direction == 1)
    def _():
        carry = _s5_row_scan(e_ref, amul, apow, 0, ctx_blocks, True, zero)
        _s5_row_scan(e_ref, amul, apow, ctx_blocks, n_blocks, True, carry)

    for (kc, _, wct), cols in zip(ops, pairs):
        y = jnp.dot(up_ref[:, cols], kc, preferred_element_type=F32)
        y += lax.dot_general(e_ref[:, cols].astype(BF16), wct, (((1,), (1,)), ((), ())),
                             preferred_element_type=F32)
        acc_ref[:, cols] += y

    @pl.when(direction == 1)
    def _():
        acc = acc_ref[...]
        hi = acc.astype(BF16)
        lo = (acc - hi.astype(F32)).astype(BF16)
        y_nat = (jnp.dot(hi, pt_ref[...], preferred_element_type=F32)
                 + jnp.dot(lo, pt_ref[...], preferred_element_type=F32))
        for t in range(S5_T):
            y_ref[token_rows(t), :] = (y_nat[:, t * S5_SET_CH:(t + 1) * S5_SET_CH]
                                       + d_ref[...] * u_ref[token_rows(t), :])


def _s5_glu_body(y_ref, w_ref, b_ref, o_ref):
    g = jax.nn.gelu(y_ref[...])
    gate = jnp.dot(g.astype(BF16), w_ref[...].astype(BF16), preferred_element_type=F32) + b_ref[...]
    o_ref[...] = (g * jax.nn.sigmoid(gate)).astype(BF16)


def _s5_mixer(z, ops, consts, layer, d_skip, w_glu, b_glu):
    kc, wb, wct, amul, apow = ops
    _, pair_map, perm, perm_t = consts
    n_steps = S5_G // S5_STEP_G
    wspec = pl.BlockSpec((None, None, S5_STEP_G, S5_BLK, S5_BLK), lambda j, d: (layer, d, j, 0, 0))
    cspec = pl.BlockSpec((None, None, SUBLANES, S5_STEP_W), lambda j, d: (layer, d, 0, j))
    pspec = pl.BlockSpec((S5_STEP_W, S5_STEP_W), lambda j, d: (0, 0))
    y = pl.pallas_call(
        _s5_body,
        grid=(n_steps, 2),
        in_specs=[
            pl.BlockSpec((NTOK, S5_SET_CH), lambda j, d: (0, j)),
            pspec, pspec,
            pl.BlockSpec((2, S5_BLK, 2 * S5_BLK), lambda j, d: (0, 0, 0)),
            wspec, wspec, wspec, cspec, cspec,
            pl.BlockSpec((1, S5_SET_CH), lambda j, d: (0, j)),
        ],
        out_specs=pl.BlockSpec((NTOK, S5_SET_CH), lambda j, d: (0, j)),
        out_shape=jax.ShapeDtypeStruct((NTOK, S5_W), F32),
        scratch_shapes=[pltpu.VMEM((S5_ROWS, S5_STEP_W), BF16), pltpu.VMEM((S5_ROWS, S5_STEP_W), F32),
                        pltpu.VMEM((S5_ROWS, S5_STEP_W), F32)],
        compiler_params=_cparams(("arbitrary", "arbitrary")),
        name="s5_scan",
    )(z, perm, perm_t, pair_map, kc, wb, wct, amul, apow, d_skip.astype(F32).reshape(1, S5_W))

    tm = MM_TILE_M
    return pl.pallas_call(
        _s5_glu_body,
        grid=(NTOK // tm,),
        in_specs=[
            pl.BlockSpec((tm, S5_W), lambda i: (i, 0)),
            pl.BlockSpec((S5_W, S5_W), lambda i: (0, 0)),
            pl.BlockSpec((1, S5_W), lambda i: (0, 0)),
        ],
        out_specs=pl.BlockSpec((tm, S5_W), lambda i: (i, 0)),
        out_shape=jax.ShapeDtypeStruct((NTOK, S5_W), BF16),
        compiler_params=_cparams(("arbitrary",)),
        name="s5_glu",
    )(y, w_glu, b_glu.reshape(1, S5_W))


CONV_PAD = 16
CONV_BLK = 64


def _conv_body(a_ref, b_ref, w_ref, db_ref, lg_ref, lb_ref, o_ref, pad_ref):
    g = a_ref[...] * jax.nn.sigmoid(b_ref[...])

    def run(seq_len):
        n_seq = ROW_TILE // seq_len
        pitch = seq_len + 2 * CONV_PAD
        zeros = jnp.zeros((CONV_PAD, CONV_W), F32)
        for s in range(n_seq):
            pad_ref[s * pitch:s * pitch + CONV_PAD, :] = zeros
            pad_ref[s * pitch + CONV_PAD:s * pitch + CONV_PAD + seq_len, :] = g[s * seq_len:(s + 1) * seq_len]
            pad_ref[s * pitch + CONV_PAD + seq_len:(s + 1) * pitch, :] = zeros
        for blk in range(ROW_TILE // CONV_BLK):
            row0 = blk * CONV_BLK
            s, q = divmod(row0, seq_len)
            base = s * pitch + CONV_PAD + q - CONV_K // 2
            acc = jnp.zeros((CONV_BLK, CONV_W), F32)
            for k in range(CONV_K):
                acc = acc + w_ref[k:k + 1, :] * pad_ref[base + k:base + k + CONV_BLK, :]
            y = acc + db_ref[...]
            yc = y - jnp.mean(y, axis=-1, keepdims=True)
            var = jnp.mean(yc * yc, axis=-1, keepdims=True)
            y = yc * lax.rsqrt(var + LN_EPS) * lg_ref[...] + lb_ref[...]
            o_ref[row0:row0 + CONV_BLK, :] = jax.nn.silu(y).astype(BF16)

    is_ctx = pl.program_id(0) == 0

    @pl.when(is_ctx)
    def _():
        run(CTX)

    @pl.when(jnp.logical_not(is_ctx))
    def _():
        run(GRID_W)


def _conv_module(z, dw_w, dw_b, ln_g, ln_b):
    w = jnp.concatenate([dw_w, jnp.zeros((1, CONV_W), dw_w.dtype)], axis=0)
    vec = lambda v: v.reshape(1, CONV_W)
    pad_rows = max(CTX + 2 * CONV_PAD, (ROW_TILE // GRID_W) * (GRID_W + 2 * CONV_PAD))
    return pl.pallas_call(
        _conv_body,
        grid=(NTOK // ROW_TILE,),
        in_specs=[
            pl.BlockSpec((ROW_TILE, CONV_W), lambda i: (i, OFF_CONV // CONV_W)),
            pl.BlockSpec((ROW_TILE, CONV_W), lambda i: (i, OFF_CONV // CONV_W + 1)),
            pl.BlockSpec((CONV_K + 1, CONV_W), lambda i: (0, 0)),
            pl.BlockSpec((1, CONV_W), lambda i: (0, 0)),
            pl.BlockSpec((1, CONV_W), lambda i: (0, 0)),
            pl.BlockSpec((1, CONV_W), lambda i: (0, 0)),
        ],
        out_specs=pl.BlockSpec((ROW_TILE, CONV_W), lambda i: (i, 0)),
        out_shape=jax.ShapeDtypeStruct((NTOK, CONV_W), BF16),
        scratch_shapes=[pltpu.VMEM((pad_rows, CONV_W), F32)],
        compiler_params=_cparams(("arbitrary",)),
        name="conv_module",
    )(z, z, w, vec(dw_b), vec(ln_g), vec(ln_b))


def _ml_chunk_index(ci, reverse):
    if not reverse:
        return ci
    return jnp.where(ci == 0, 0, ML_NCHUNK - ci)


def _ml_direction(qb, kb, vx, li_col, li_row, b_col, b_row, cx_ref, m_ref, mask, reverse):
    c = ML_CHUNK
    scale = ML_DH ** 0.5
    m = m_ref[:, 0:1]
    cx = cx_ref[...]

    d_log = jnp.where(mask, b_col + (li_row - b_row), -jnp.inf)
    inter = b_col + m
    m_row = jnp.maximum(inter, jnp.max(d_log, axis=-1, keepdims=True))
    qk = lax.dot_general(qb, kb, (((1,), (1,)), ((), ())), preferred_element_type=F32)
    s = qk * jnp.exp(d_log - m_row)
    w_inter = jnp.exp(inter - m_row) * scale
    num = jnp.dot(s.astype(BF16), vx, preferred_element_type=F32)
    num += w_inter * jnp.dot(qb, cx.astype(BF16), preferred_element_type=F32)
    den = num[:, ML_DH:ML_DH + 1]
    h = num[:, :ML_DH] / jnp.maximum(jnp.abs(den), jnp.exp(-m_row) * scale)

    b_tot = b_row[:, 0:1] if reverse else b_row[:, c - 1:c]
    g = b_tot - b_col + li_col
    m_new = jnp.maximum(b_tot + m, jnp.max(g, axis=0, keepdims=True))
    kw = kb * (jnp.exp(g - m_new) * (1.0 / scale)).astype(BF16)
    decay = jnp.exp(b_tot + m - m_new)
    cx_ref[...] = decay * cx + lax.dot_general(kw, vx, (((0,), (0,)), ((), ())),
                                               preferred_element_type=F32)
    m_ref[...] = jnp.broadcast_to(m_new, m_ref.shape)
    return h


def _mlstm_body(qkvf_ref, gf_ref, gtf_ref, qkvb_ref, gb_ref, gtb_ref, hf_ref, hb_ref, cx_ref, m_ref):
    @pl.when(pl.program_id(0) == 0)
    def _():
        cx_ref[...] = jnp.zeros_like(cx_ref)
        m_ref[...] = jnp.zeros_like(m_ref)

    c = ML_CHUNK
    r_idx = lax.broadcasted_iota(jnp.int32, (c, c), 0)
    c_idx = lax.broadcasted_iota(jnp.int32, (c, c), 1)
    ones_col = jnp.where(lax.broadcasted_iota(jnp.int32, (c, LANES), 1) == 0, 1.0, 0.0).astype(BF16)

    for d, (qkv_ref, g_ref, gt_ref, h_ref) in enumerate(
            ((qkvf_ref, gf_ref, gtf_ref, hf_ref), (qkvb_ref, gb_ref, gtb_ref, hb_ref))):
        reverse = d == 1
        gates = g_ref[...]
        gates_t = gt_ref[...]
        incl = (c_idx >= r_idx) if reverse else (c_idx <= r_idx)
        tri = jnp.where(incl, 1.0, 0.0).astype(BF16)
        incl_t = (r_idx >= c_idx) if reverse else (r_idx <= c_idx)
        tri_t = jnp.where(incl_t, 1.0, 0.0).astype(BF16)
        lf = jax.nn.log_sigmoid(gates)
        lf_t = jax.nn.log_sigmoid(gates_t)
        b_all = sum(jnp.dot(tri, p, preferred_element_type=F32) for p in _split3(lf))
        b_all_t = sum(jnp.dot(p, tri_t, preferred_element_type=F32) for p in _split3(lf_t))
        for head in range(ML_H):
            i_col = 2 * ML_H * d + head
            f_col = i_col + ML_H
            state = d * ML_H + head
            col = lambda part: slice((part * ML_H + head) * ML_DH, (part * ML_H + head + 1) * ML_DH)
            vx = jnp.concatenate([qkv_ref[:, col(2)], ones_col], axis=1)
            h = _ml_direction(
                qkv_ref[:, col(0)], qkv_ref[:, col(1)], vx,
                gates[:, i_col:i_col + 1], gates_t[i_col:i_col + 1, :],
                b_all[:, f_col:f_col + 1], b_all_t[f_col:f_col + 1, :],
                cx_ref.at[state], m_ref.at[state], incl, reverse)
            h_ref[:, head * ML_DH:(head + 1) * ML_DH] = h


def _mlstm_readout_body(hf_ref, hb_ref, o_ref, g_ref, out_ref):
    h = hf_ref[...] + hb_ref[...]
    hc = h - jnp.mean(h, axis=-1, keepdims=True)
    var = jnp.mean(hc * hc, axis=-1, keepdims=True)
    h = hc * lax.rsqrt(var + LN_EPS) * g_ref[...]
    out_ref[...] = (jax.nn.sigmoid(o_ref[...].astype(F32)) * h).astype(BF16)


def _mlstm_mixer(qkvo, gates, norm_g):
    gates_t = jnp.transpose(gates[:, :2 * SUBLANES])
    c = ML_CHUNK

    def specs(reverse):
        row = lambda ci: _ml_chunk_index(ci, reverse)
        return [
            pl.BlockSpec((c, 3 * ML_W), lambda ci: (row(ci), 0)),
            pl.BlockSpec((c, LANES), lambda ci: (row(ci), 0)),
            pl.BlockSpec((2 * SUBLANES, c), lambda ci: (0, row(ci))),
        ]

    out_spec = lambda reverse: pl.BlockSpec((c, ML_W), lambda ci: (_ml_chunk_index(ci, reverse), 0))
    n_state = 2 * ML_H
    h_f, h_b = pl.pallas_call(
        _mlstm_body,
        grid=(ML_NCHUNK,),
        in_specs=specs(False) + specs(True),
        out_specs=[out_spec(False), out_spec(True)],
        out_shape=[jax.ShapeDtypeStruct((NTOK, ML_W), F32)] * 2,
        scratch_shapes=[pltpu.VMEM((n_state, ML_DH, ML_DH + LANES), F32),
                        pltpu.VMEM((n_state, 1, LANES), F32)],
        compiler_params=_cparams(("arbitrary",)),
        name="mlstm_chunks",
    )(qkvo, gates, gates_t, qkvo, gates, gates_t)

    tm = MM_TILE_M
    return pl.pallas_call(
        _mlstm_readout_body,
        grid=(NTOK // tm, ML_H),
        in_specs=[
            pl.BlockSpec((tm, ML_DH), lambda i, hh: (i, hh)),
            pl.BlockSpec((tm, ML_DH), lambda i, hh: (i, hh)),
            pl.BlockSpec((tm, ML_DH), lambda i, hh: (i, 3 * ML_H + hh)),
            pl.BlockSpec((1, ML_DH), lambda i, hh: (0, hh)),
        ],
        out_specs=pl.BlockSpec((tm, ML_DH), lambda i, hh: (i, hh)),
        out_shape=jax.ShapeDtypeStruct((NTOK, ML_W), BF16),
        compiler_params=_cparams(("arbitrary", "arbitrary")),
        name="mlstm_readout",
    )(h_f, h_b, qkvo, norm_g.reshape(1, ML_W))


def kernel(x, c, ctx, c_ctx, w_mod, b_mod, norm1_g, w_in, b_in, s5_lam_re, s5_lam_im, s5_log_step,
           s5_b_re, s5_b_im, s5_c_re, s5_c_im, s5_d, s5_w_glu, s5_b_glu, conv_dw_w, conv_dw_b,
           conv_ln_g, conv_ln_b, ml_norm_g, w_out, norm2_g, w_ffn_in, w_ffn_out, norm_f_g):
    assert x.shape == (1, SEQ, D_MODEL) and ctx.shape == (1, CTX, D_MODEL)
    xs = jnp.concatenate([ctx[0], x[0]], axis=0).astype(F32)
    cc = jnp.zeros((SUBLANES, D_MODEL), F32).at[0].set(c[0]).at[1].set(c_ctx)
    mod_all = _modulation(cc, w_mod, b_mod)
    s5_consts = _s5_selectors()
    s5_ops = _s5_operators(s5_lam_re, s5_lam_im, s5_log_step, s5_b_re, s5_b_im, s5_c_re, s5_c_im,
                           s5_consts[0])
    w_in_t = jnp.swapaxes(w_in, 1, 2)
    n_gate = w_in.shape[2] - OFF_G
    w_gate = jnp.pad(w_in_t[:, OFF_G:, :], ((0, 0), (0, LANES - n_gate), (0, 0)))
    b_gate = jnp.pad(b_in[:, OFF_G:], ((0, 0), (0, LANES - n_gate))).reshape(DEPTH, 1, LANES)
    tn = MM_TILE_N

    h, gates = _norm_mod(xs, norm1_g[0], mod_all, 0, 0, w_gate[0], b_gate[0])
    for l in range(DEPTH):
        z = _in_proj(h, w_in_t, b_in, l, 0, OFF_Q // tn, 1, F32)
        qkvo = _in_proj(h, w_in_t, b_in, l, OFF_Q // tn, (OFF_G - OFF_Q) // (2 * tn), 2, BF16)
        s5o = _s5_mixer(z, s5_ops, s5_consts, l, s5_d[l], s5_w_glu[l], s5_b_glu[l])
        cvo = _conv_module(z, conv_dw_w[l], conv_dw_b[l], conv_ln_g[l], conv_ln_b[l])
        mlo = _mlstm_mixer(qkvo, gates, ml_norm_g[l])
        xs, h2 = _out_proj(xs, s5o, cvo, mlo, w_out, mod_all, norm2_g[l], l)
        hid = _ffn_in(h2, w_ffn_in, l)
        xs = _ffn_out(xs, hid, w_ffn_out, mod_all, l)
        if l + 1 < DEPTH:
            h, gates = _norm_mod(xs, norm1_g[l + 1], mod_all, l + 1, 0, w_gate[l + 1], b_gate[l + 1])
    return _final_norm(xs, norm_f_g)[None]
```

```python
import functools

import numpy as np
import jax
import jax.numpy as jnp
from jax import lax
from jax.experimental import pallas as pl
from jax.experimental.pallas import tpu as pltpu

F32 = jnp.float32
BF16 = jnp.bfloat16

D_MODEL = 2048
SEQ = 8192
CTX = 256
NTOK = SEQ + CTX
DEPTH = 4
GRID_W = 64

S5_W = 512
S5_CH = 16
S5_G = 32
S5_P = 64
CONV_W = 512
CONV_K = 31
ML_W = 1024
ML_H = 4
ML_DH = 256
D_FF = 5632
EPS = 1e-6
LN_EPS = 1e-5

OFF_CONV = 512
OFF_Q = 1536
OFF_K = 2560
OFF_V = 3584
OFF_O = 4608
OFF_G = 5632

LANES = 128
SUBLANES = 8
VMEM_LIMIT = 56 * 1024 * 1024

ROW_TILE = 256
MM_TILE_M = 1056
MM_TILE_N = 512

S5_T = 8
S5_ROWS = NTOK // S5_T
S5_CTX_ROWS = CTX // S5_T
S5_BLK = S5_T * S5_CH
S5_STEP_G = 8
S5_STEP_PAIRS = S5_STEP_G // 2
S5_SET_CH = S5_STEP_G * S5_CH
S5_STEP_W = S5_STEP_G * S5_BLK
S5_FLAT = S5_G * S5_BLK

ML_CHUNK = 256
ML_NCHUNK = NTOK // ML_CHUNK


def _cparams(sem, vmem=VMEM_LIMIT):
    return pltpu.CompilerParams(dimension_semantics=sem, vmem_limit_bytes=vmem)


def _dot(a, b):
    return jnp.dot(a.astype(BF16), b.astype(BF16), preferred_element_type=F32)


def _split3(x):
    a = x.astype(BF16)
    r = x - a.astype(F32)
    b = r.astype(BF16)
    c = (r - b.astype(F32)).astype(BF16)
    return a, b, c


def _mod_body(cc_ref, w_ref, b_ref, o_ref):
    s = jax.nn.silu(cc_ref[...])
    o_ref[0] = _dot(s, w_ref[0]) + b_ref[0]


def _modulation(cc, w_mod, b_mod):
    depth, _, n = w_mod.shape
    tn = 1024
    return pl.pallas_call(
        _mod_body,
        grid=(depth, n // tn),
        in_specs=[
            pl.BlockSpec((SUBLANES, D_MODEL), lambda l, j: (0, 0)),
            pl.BlockSpec((1, D_MODEL, tn), lambda l, j: (l, 0, j)),
            pl.BlockSpec((1, 1, tn), lambda l, j: (l, 0, j)),
        ],
        out_specs=pl.BlockSpec((1, SUBLANES, tn), lambda l, j: (l, 0, j)),
        out_shape=jax.ShapeDtypeStruct((depth, SUBLANES, n), F32),
        compiler_params=_cparams(("arbitrary", "arbitrary")),
        name="adaln_modulation",
    )(cc, w_mod, b_mod.reshape(depth, 1, n))


def _mod_row(m_ref, is_ctx):
    return jnp.where(is_ctx, m_ref[0, 1:2, :], m_ref[0, 0:1, :])


def _norm_body(*refs, modulate, with_gates):
    it = iter(refs)
    x_ref, g_ref = next(it), next(it)
    sh_ref = sc_ref = wg_ref = bg_ref = gate_ref = None
    if modulate:
        sh_ref, sc_ref = next(it), next(it)
    if with_gates:
        wg_ref, bg_ref = next(it), next(it)
    h_ref = next(it)
    if with_gates:
        gate_ref = next(it)

    xf = x_ref[...]
    ms = jnp.mean(xf * xf, axis=-1, keepdims=True)
    h = xf * lax.rsqrt(ms + EPS) * g_ref[...]
    if modulate:
        is_ctx = pl.program_id(0) == 0
        h = h * (1.0 + _mod_row(sc_ref, is_ctx)) + _mod_row(sh_ref, is_ctx)
    h_ref[...] = h.astype(h_ref.dtype)
    if with_gates:
        hi = h.astype(BF16)
        lo = (h - hi.astype(F32)).astype(BF16)
        whi, wlo = wg_ref[0], wg_ref[1]
        nt = lambda a, b: lax.dot_general(a, b, (((1,), (1,)), ((), ())), preferred_element_type=F32)
        gate_ref[...] = nt(hi, whi) + nt(hi, wlo) + nt(lo, whi) + bg_ref[...]


def _norm_mod(xs, g, mod_all, layer, phase, w_gate=None, b_gate=None):
    with_gates = w_gate is not None
    n_tiles = NTOK // ROW_TILE
    in_specs = [
        pl.BlockSpec((ROW_TILE, D_MODEL), lambda i: (i, 0)),
        pl.BlockSpec((1, D_MODEL), lambda i: (0, 0)),
        pl.BlockSpec((1, SUBLANES, D_MODEL), lambda i: (layer, 0, 3 * phase)),
        pl.BlockSpec((1, SUBLANES, D_MODEL), lambda i: (layer, 0, 3 * phase + 1)),
    ]
    args = [xs, g.reshape(1, D_MODEL), mod_all, mod_all]
    out_specs = [pl.BlockSpec((ROW_TILE, D_MODEL), lambda i: (i, 0))]
    out_shape = [jax.ShapeDtypeStruct((NTOK, D_MODEL), BF16)]
    if with_gates:
        in_specs += [pl.BlockSpec((2, LANES, D_MODEL), lambda i: (0, 0, 0)),
                     pl.BlockSpec((1, LANES), lambda i: (0, 0))]
        args += [w_gate, b_gate]
        out_specs.append(pl.BlockSpec((ROW_TILE, LANES), lambda i: (i, 0)))
        out_shape.append(jax.ShapeDtypeStruct((NTOK, LANES), F32))
    return pl.pallas_call(
        functools.partial(_norm_body, modulate=True, with_gates=with_gates),
        grid=(n_tiles,),
        in_specs=in_specs,
        out_specs=out_specs,
        out_shape=out_shape,
        compiler_params=_cparams(("arbitrary",)),
        name="rmsnorm_modulate",
    )(*args)


def _final_norm(xs, g):
    skip = CTX // ROW_TILE
    return pl.pallas_call(
        functools.partial(_norm_body, modulate=False, with_gates=False),
        grid=(SEQ // ROW_TILE,),
        in_specs=[pl.BlockSpec((ROW_TILE, D_MODEL), lambda i: (i + skip, 0)),
                  pl.BlockSpec((1, D_MODEL), lambda i: (0, 0))],
        out_specs=[pl.BlockSpec((ROW_TILE, D_MODEL), lambda i: (i, 0))],
        out_shape=[jax.ShapeDtypeStruct((SEQ, D_MODEL), F32)],
        compiler_params=_cparams(("arbitrary",)),
        name="final_rmsnorm",
    )(xs, g.reshape(1, D_MODEL))[0]


def _in_proj_body(*refs, n_w):
    a_ref, wt_refs, b_refs = refs[0], refs[1:1 + n_w], refs[1 + n_w:1 + 2 * n_w]
    o_ref, wbf_ref = refs[1 + 2 * n_w:]
    tn = MM_TILE_N

    @pl.when(pl.program_id(1) == 0)
    def _():
        for k, wt_ref in enumerate(wt_refs):
            wbf_ref[:, k * tn:(k + 1) * tn] = jnp.transpose(wt_ref[...]).astype(BF16)

    bias = jnp.concatenate([b_ref[...] for b_ref in b_refs], axis=1)
    acc = jnp.dot(a_ref[...], wbf_ref[...], preferred_element_type=F32) + bias
    o_ref[...] = acc.astype(o_ref.dtype)


def _in_proj(h, w_in_t, b_in, layer, first_tile, n_steps, n_w, dtype):
    tm, tn = MM_TILE_M, MM_TILE_N
    in_w = w_in_t.shape[1]
    tile = lambda j, k: first_tile + n_w * j + k
    w_specs = [pl.BlockSpec((None, tn, D_MODEL), functools.partial(lambda j, i, k: (layer, tile(j, k), 0), k=k))
               for k in range(n_w)]
    b_specs = [pl.BlockSpec((None, 1, tn), functools.partial(lambda j, i, k: (layer, 0, tile(j, k)), k=k))
               for k in range(n_w)]
    b3 = b_in.reshape(DEPTH, 1, in_w)
    return pl.pallas_call(
        functools.partial(_in_proj_body, n_w=n_w),
        grid=(n_steps, NTOK // tm),
        in_specs=[pl.BlockSpec((tm, D_MODEL), lambda j, i: (i, 0))] + w_specs + b_specs,
        out_specs=pl.BlockSpec((tm, n_w * tn), lambda j, i: (i, j)),
        out_shape=jax.ShapeDtypeStruct((NTOK, n_steps * n_w * tn), dtype),
        scratch_shapes=[pltpu.VMEM((D_MODEL, n_w * tn), BF16)],
        compiler_params=_cparams(("arbitrary", "arbitrary")),
        name="in_proj",
    )(h, *([w_in_t] * n_w), *([b3] * n_w))


def _row_gate(g_ref, i, tm, tn):
    rows = i * tm + lax.broadcasted_iota(jnp.int32, (tm, tn), 0)
    return jnp.where(rows < CTX, g_ref[0, 1:2, :], g_ref[0, 0:1, :])


def _out_proj_body(s5_ref, cv_ref, ml_ref, w_ref, gate_ref, sh_ref, sc_ref, ng_ref, x_ref,
                   xo_ref, h_ref, wbf_ref):
    i = pl.program_id(0)

    @pl.when(i == 0)
    def _():
        wbf_ref[...] = w_ref[...].astype(BF16)

    acc = jnp.dot(s5_ref[...], wbf_ref[0:S5_W, :], preferred_element_type=F32)
    acc += jnp.dot(cv_ref[...], wbf_ref[S5_W:S5_W + CONV_W, :], preferred_element_type=F32)
    acc += jnp.dot(ml_ref[...], wbf_ref[S5_W + CONV_W:, :], preferred_element_type=F32)
    is_ctx = i == 0
    xn = x_ref[...] + _mod_row(gate_ref, is_ctx) * acc
    xo_ref[...] = xn
    ms = jnp.mean(xn * xn, axis=-1, keepdims=True)
    h = xn * lax.rsqrt(ms + EPS) * ng_ref[...]
    h_ref[...] = (h * (1.0 + _mod_row(sc_ref, is_ctx)) + _mod_row(sh_ref, is_ctx)).astype(BF16)


def _out_proj(xs, s5o, cvo, mlo, w_out, mod_all, norm_g, layer):
    tm = ROW_TILE
    mod = lambda k: pl.BlockSpec((1, SUBLANES, D_MODEL), lambda i: (layer, 0, k))
    return pl.pallas_call(
        _out_proj_body,
        grid=(NTOK // tm,),
        in_specs=[
            pl.BlockSpec((tm, S5_W), lambda i: (i, 0)),
            pl.BlockSpec((tm, CONV_W), lambda i: (i, 0)),
            pl.BlockSpec((tm, ML_W), lambda i: (i, 0)),
            pl.BlockSpec((None, D_MODEL, D_MODEL), lambda i: (layer, 0, 0), pipeline_mode=pl.Buffered(1)),
            mod(2), mod(3), mod(4),
            pl.BlockSpec((1, D_MODEL), lambda i: (0, 0)),
            pl.BlockSpec((tm, D_MODEL), lambda i: (i, 0)),
        ],
        out_specs=[pl.BlockSpec((tm, D_MODEL), lambda i: (i, 0)), pl.BlockSpec((tm, D_MODEL), lambda i: (i, 0))],
        out_shape=[jax.ShapeDtypeStruct((NTOK, D_MODEL), F32), jax.ShapeDtypeStruct((NTOK, D_MODEL), BF16)],
        scratch_shapes=[pltpu.VMEM((D_MODEL, D_MODEL), BF16)],
        compiler_params=_cparams(("arbitrary",)),
        name="out_proj_residual",
    )(s5o, cvo, mlo, w_out, mod_all, mod_all, mod_all, norm_g.reshape(1, D_MODEL), xs)


def _ffn_in_body(a_ref, wg_ref, wu_ref, o_ref, wgbf_ref, wubf_ref):
    @pl.when(pl.program_id(1) == 0)
    def _():
        wgbf_ref[...] = wg_ref[...].astype(BF16)
        wubf_ref[...] = wu_ref[...].astype(BF16)

    a = a_ref[...]
    g = jnp.dot(a, wgbf_ref[...], preferred_element_type=F32)
    u = jnp.dot(a, wubf_ref[...], preferred_element_type=F32)
    o_ref[...] = (jax.nn.silu(g) * u).astype(BF16)


def _ffn_in(h, w_ffn_in, layer):
    tm, tn = MM_TILE_M, MM_TILE_N
    nj = D_FF // tn
    return pl.pallas_call(
        _ffn_in_body,
        grid=(nj, NTOK // tm),
        in_specs=[
            pl.BlockSpec((tm, D_MODEL), lambda j, i: (i, 0)),
            pl.BlockSpec((None, D_MODEL, tn), lambda j, i: (layer, 0, j)),
            pl.BlockSpec((None, D_MODEL, tn), lambda j, i: (layer, 0, nj + j)),
        ],
        out_specs=pl.BlockSpec((tm, tn), lambda j, i: (i, j)),
        out_shape=jax.ShapeDtypeStruct((NTOK, D_FF), BF16),
        scratch_shapes=[pltpu.VMEM((D_MODEL, tn), BF16), pltpu.VMEM((D_MODEL, tn), BF16)],
        compiler_params=_cparams(("arbitrary", "arbitrary")),
        name="ffn_in_swiglu",
    )(h, w_ffn_in, w_ffn_in)


def _ffn_out_body(a_ref, w_ref, g_ref, x_ref, o_ref, wbf_ref):
    i = pl.program_id(1)

    @pl.when(i == 0)
    def _():
        wbf_ref[...] = w_ref[...].astype(BF16)

    acc = jnp.dot(a_ref[...], wbf_ref[...], preferred_element_type=F32)
    tm, tn = o_ref.shape
    o_ref[...] = x_ref[...] + _row_gate(g_ref, i, tm, tn) * acc


def _ffn_out(xs, hid, w_ffn_out, mod_all, layer):
    tm, tn = MM_TILE_M // 2, MM_TILE_N
    return pl.pallas_call(
        _ffn_out_body,
        grid=(D_MODEL // tn, NTOK // tm),
        in_specs=[
            pl.BlockSpec((tm, D_FF), lambda j, i: (i, 0)),
            pl.BlockSpec((None, D_FF, tn), lambda j, i: (layer, 0, j)),
            pl.BlockSpec((1, SUBLANES, tn), lambda j, i: (layer, 0, 5 * (D_MODEL // tn) + j)),
            pl.BlockSpec((tm, tn), lambda j, i: (i, j)),
        ],
        out_specs=pl.BlockSpec((tm, tn), lambda j, i: (i, j)),
        out_shape=jax.ShapeDtypeStruct((NTOK, D_MODEL), F32),
        scratch_shapes=[pltpu.VMEM((D_FF, tn), BF16)],
        compiler_params=_cparams(("arbitrary", "arbitrary")),
        name="ffn_out_residual",
    )(hid, w_ffn_out, mod_all, xs)


def _s5_selectors():
    tau = np.arange(S5_BLK)[:, None] // S5_CH
    ch_r = np.arange(S5_BLK)[:, None] % S5_CH
    t = np.arange(S5_BLK)[None, :] // S5_CH
    ch_c = np.arange(S5_BLK)[None, :] % S5_CH
    sel = np.zeros((2, S5_T, S5_BLK, S5_BLK), np.float32)
    for s in range(S5_T):
        sel[0, s] = (tau == t - s) & (ch_r == ch_c)
        sel[1, s] = (tau == s - t) & (ch_r == ch_c)
    rp = np.arange(S5_BLK)[:, None]
    col = np.arange(2 * S5_BLK)[None, :]
    pair_map = np.stack([(col == (rp // S5_P) * S5_BLK + gi * S5_P + rp % S5_P) for gi in range(2)])
    src = np.arange(S5_STEP_W)
    dst = ((src // S5_CH) % S5_STEP_G) * S5_BLK + (src // S5_SET_CH) * S5_CH + src % S5_CH
    perm = dst[:, None] == np.arange(S5_STEP_W)[None, :]
    return (jnp.asarray(sel), jnp.asarray(pair_map, BF16), jnp.asarray(perm, BF16),
            jnp.asarray(perm.T, BF16))


def _s5_operators(lam_re, lam_im, log_step, b_re, b_im, c_re, c_im, sel):
    lam = lax.complex(lam_re.astype(F32), lam_im.astype(F32))
    lam_bar = jnp.exp(lam * jnp.exp(log_step.astype(F32)))
    b_bar = ((lam_bar - 1.0) / lam)[..., None] * lax.complex(b_re.astype(F32), b_im.astype(F32))
    c_mat = lax.complex(c_re.astype(F32), c_im.astype(F32))
    depth = lam.shape[0]

    def powers(base, count):
        out = [jnp.ones_like(base)]
        for _ in range(count - 1):
            out.append(out[-1] * base)
        return out

    pw = powers(lam_bar, S5_T + 1)
    pa = powers(pw[S5_T], SUBLANES + 1)
    t_up = list(range(S5_T))

    def table(seq, fwd_idx, bwd_idx):
        return jnp.stack([jnp.stack([seq[f][:, 0], seq[b][:, 1]], axis=1)
                          for f, b in zip(fwd_idx, bwd_idx)])

    blocks = (depth, 2, S5_G, S5_BLK, S5_BLK)

    def token_rows(w, imag_sign):
        w = jnp.moveaxis(w, 0, 3)
        return jnp.concatenate([jnp.real(w), imag_sign * jnp.imag(w)], axis=-1).reshape(blocks)

    pb = table(pw, [S5_T - 1 - t for t in t_up], t_up)
    wb = token_rows(pb[:, :, :, :, None, :] * jnp.swapaxes(b_bar, -1, -2)[None], 1.0)
    pc = table(pw, [t + 1 for t in t_up], [S5_T - t for t in t_up])
    wct = token_rows(pc[:, :, :, :, None, :] * c_mat[None], -1.0)

    kern = jnp.real(jnp.einsum('ldgcp,kldgp,ldgpa->ldgakc', c_mat, jnp.stack(pw[:S5_T]), b_bar))
    kern = kern.reshape(depth, 2, S5_G * S5_CH, S5_BLK)
    kc = jnp.einsum('ldxk,dskn->ldsxn', kern, sel, precision=lax.Precision.HIGHEST)
    kc = kc.reshape(depth, 2, S5_T, S5_G, S5_CH, S5_BLK)
    kc = jnp.transpose(kc, (0, 1, 3, 2, 4, 5)).reshape(blocks)

    def lanes(v):
        parts = jnp.stack([jnp.real(v), jnp.imag(v)], axis=4)
        parts = parts.reshape(v.shape[0], depth, 2, S5_G // 2, 2, 2, S5_P)
        parts = jnp.swapaxes(parts, 4, 5).reshape(v.shape[0], depth, 2, S5_FLAT)
        return jnp.transpose(parts, (1, 2, 0, 3))

    zero_p = jnp.zeros_like(pa[0])
    amul = lanes(jnp.stack([pa[1], pa[2], pa[4], pa[8]] + [zero_p] * 4))
    apow = lanes(table(pa, t_up, t_up[::-1]))
    return kc.astype(BF16), wb.astype(BF16), wct.astype(BF16), amul, apow


def _s5_row_scan(e_ref, amul_ref, apow_ref, block_lo, block_hi, reverse, carry):
    rows = lax.broadcasted_iota(jnp.int32, (SUBLANES, S5_BLK), 0)

    def shifted(x, k):
        if reverse:
            return jnp.where(rows < SUBLANES - k, pltpu.roll(x, SUBLANES - k, 0), 0.0)
        return jnp.where(rows >= k, pltpu.roll(x, k, 0), 0.0)

    def body(step, carry):
        blk = (block_hi - 1 - step) if reverse else (block_lo + step)
        r0 = pl.multiple_of(blk * SUBLANES, SUBLANES)
        last = 0 if reverse else SUBLANES - 1
        carry_out = []
        for q in range(S5_STEP_PAIRS):
            re_c = slice(2 * q * S5_BLK, (2 * q + 1) * S5_BLK)
            im_c = slice((2 * q + 1) * S5_BLK, (2 * q + 2) * S5_BLK)
            s_re = e_ref[pl.ds(r0, SUBLANES), re_c]
            s_im = e_ref[pl.ds(r0, SUBLANES), im_c]
            for idx, k in enumerate((1, 2, 4)):
                a_re, a_im = amul_ref[idx:idx + 1, re_c], amul_ref[idx:idx + 1, im_c]
                t_re, t_im = shifted(s_re, k), shifted(s_im, k)
                s_re, s_im = s_re + t_re * a_re - t_im * a_im, s_im + t_re * a_im + t_im * a_re
            c_re, c_im = carry[:, re_c], carry[:, im_c]
            p_re, p_im = apow_ref[:, re_c], apow_ref[:, im_c]
            e_ref[pl.ds(r0, SUBLANES), re_c] = shifted(s_re, 1) + p_re * c_re - p_im * c_im
            e_ref[pl.ds(r0, SUBLANES), im_c] = shifted(s_im, 1) + p_re * c_im + p_im * c_re
            a_re, a_im = amul_ref[3:4, re_c], amul_ref[3:4, im_c]
            l_re = jnp.broadcast_to(s_re[last:last + 1, :], (SUBLANES, S5_BLK))
            l_im = jnp.broadcast_to(s_im[last:last + 1, :], (SUBLANES, S5_BLK))
            carry_out += [l_re + a_re * c_re - a_im * c_im, l_im + a_re * c_im + a_im * c_re]
        return jnp.concatenate(carry_out, axis=1)

    return lax.fori_loop(0, block_hi - block_lo, body, carry)


def _s5_pair_operators(kc_ref, wb_ref, wct_ref, map_ref, q):
    g0, g1 = 2 * q, 2 * q + 1
    zero = jnp.zeros((S5_BLK, S5_BLK), BF16)
    kc = jnp.concatenate([jnp.concatenate([kc_ref[g0], zero], axis=1),
                          jnp.concatenate([zero, kc_ref[g1]], axis=1)], axis=0)
    spread = lambda ref: jnp.concatenate(
        [jnp.dot(ref[g0], map_ref[0], preferred_element_type=F32),
         jnp.dot(ref[g1], map_ref[1], preferred_element_type=F32)], axis=0).astype(BF16)
    return kc, spread(wb_ref), spread(wct_ref)


def _s5_body(u_ref, p_ref, pt_ref, map_ref, kc_ref, wb_ref, wct_ref, amul_ref, apow_ref, d_ref, y_ref,
             up_ref, e_ref, acc_ref):
    direction = pl.program_id(1)
    token_rows = lambda t: pl.ds(t, S5_ROWS, stride=S5_T)
    pairs = [slice(2 * q * S5_BLK, 2 * (q + 1) * S5_BLK) for q in range(S5_STEP_PAIRS)]

    @pl.when(direction == 0)
    def _():
        u_nat = jnp.concatenate([u_ref[token_rows(t), :] for t in range(S5_T)], axis=1)
        up_ref[...] = jnp.dot(u_nat.astype(BF16), p_ref[...], preferred_element_type=F32).astype(BF16)
        acc_ref[...] = jnp.zeros_like(acc_ref)

    ops = [_s5_pair_operators(kc_ref, wb_ref, wct_ref, map_ref, q) for q in range(S5_STEP_PAIRS)]
    for (_, wb, _), cols in zip(ops, pairs):
        e_ref[:, cols] = jnp.dot(up_ref[:, cols], wb, preferred_element_type=F32)

    zero = jnp.zeros((SUBLANES, S5_STEP_W), F32)
    n_blocks = S5_ROWS // SUBLANES
    ctx_blocks = S5_CTX_ROWS // SUBLANES
    amul, apow = amul_ref, apow_ref

    @pl.when(direction == 0)
    def _():
        _s5_row_scan(e_ref, amul, apow, 0, n_blocks, False, zero)

    @pl.when(direction == 1)
    def _():
        carry = _s5_row_scan(e_ref, amul, apow, 0, ctx_blocks, True, zero)
        _s5_row_scan(e_ref, amul, apow, ctx_blocks, n_blocks, True, carry)

    for (kc, _, wct), cols in zip(ops, pairs):
        y = jnp.dot(up_ref[:, cols], kc, preferred_element_type=F32)
        y += lax.dot_general(e_ref[:, cols].astype(BF16), wct, (((1,), (1,)), ((), ())),
                             preferred_element_type=F32)
        acc_ref[:, cols] += y

    @pl.when(direction == 1)
    def _():
        acc = acc_ref[...]
        hi = acc.astype(BF16)
        lo = (acc - hi.astype(F32)).astype(BF16)
        y_nat = (jnp.dot(hi, pt_ref[...], preferred_element_type=F32)
                 + jnp.dot(lo, pt_ref[...], preferred_element_type=F32))
        for t in range(S5_T):
            y_ref[token_rows(t), :] = (y_nat[:, t * S5_SET_CH:(t + 1) * S5_SET_CH]
                                       + d_ref[...] * u_ref[token_rows(t), :])


def _s5_glu_body(y_ref, w_ref, b_ref, o_ref):
    g = jax.nn.gelu(y_ref[...])
    gate = jnp.dot(g.astype(BF16), w_ref[...].astype(BF16), preferred_element_type=F32) + b_ref[...]
    o_ref[...] = (g * jax.nn.sigmoid(gate)).astype(BF16)


def _s5_mixer(z, ops, consts, layer, d_skip, w_glu, b_glu):
    kc, wb, wct, amul, apow = ops
    _, pair_map, perm, perm_t = consts
    n_steps = S5_G // S5_STEP_G
    wspec = pl.BlockSpec((None, None, S5_STEP_G, S5_BLK, S5_BLK), lambda j, d: (layer, d, j, 0, 0))
    cspec = pl.BlockSpec((None, None, SUBLANES, S5_STEP_W), lambda j, d: (layer, d, 0, j))
    pspec = pl.BlockSpec((S5_STEP_W, S5_STEP_W), lambda j, d: (0, 0))
    y = pl.pallas_call(
        _s5_body,
        grid=(n_steps, 2),
        in_specs=[
            pl.BlockSpec((NTOK, S5_SET_CH), lambda j, d: (0, j)),
            pspec, pspec,
            pl.BlockSpec((2, S5_BLK, 2 * S5_BLK), lambda j, d: (0, 0, 0)),
            wspec, wspec, wspec, cspec, cspec,
            pl.BlockSpec((1, S5_SET_CH), lambda j, d: (0, j)),
        ],
        out_specs=pl.BlockSpec((NTOK, S5_SET_CH), lambda j, d: (0, j)),
        out_shape=jax.ShapeDtypeStruct((NTOK, S5_W), F32),
        scratch_shapes=[pltpu.VMEM((S5_ROWS, S5_STEP_W), BF16), pltpu.VMEM((S5_ROWS, S5_STEP_W), F32),
                        pltpu.VMEM((S5_ROWS, S5_STEP_W), F32)],
        compiler_params=_cparams(("arbitrary", "arbitrary")),
        name="s5_scan",
    )(z, perm, perm_t, pair_map, kc, wb, wct, amul, apow, d_skip.astype(F32).reshape(1, S5_W))

    tm = MM_TILE_M
    return pl.pallas_call(
        _s5_glu_body,
        grid=(NTOK // tm,),
        in_specs=[
            pl.BlockSpec((tm, S5_W), lambda i: (i, 0)),
            pl.BlockSpec((S5_W, S5_W), lambda i: (0, 0)),
            pl.BlockSpec((1, S5_W), lambda i: (0, 0)),
        ],
        out_specs=pl.BlockSpec((tm, S5_W), lambda i: (i, 0)),
        out_shape=jax.ShapeDtypeStruct((NTOK, S5_W), BF16),
        compiler_params=_cparams(("arbitrary",)),
        name="s5_glu",
    )(y, w_glu, b_glu.reshape(1, S5_W))


CONV_PAD = 16
CONV_BLK = 64


def _conv_body(a_ref, b_ref, w_ref, db_ref, lg_ref, lb_ref, o_ref, pad_ref, sh_ref):
    g = a_ref[...] * jax.nn.sigmoid(b_ref[...])

    def run(seq_len):
        n_seq = ROW_TILE // seq_len
        pitch = seq_len + 2 * CONV_PAD
        zeros = jnp.zeros((CONV_PAD, CONV_W), F32)
        for s in range(n_seq):
            pad_ref[s * pitch:s * pitch + CONV_PAD, :] = zeros
            pad_ref[s * pitch + CONV_PAD:s * pitch + CONV_PAD + seq_len, :] = g[s * seq_len:(s + 1) * seq_len]
            pad_ref[s * pitch + CONV_PAD + seq_len:(s + 1) * pitch, :] = zeros
        used = n_seq * pitch
        for r in range(1, SUBLANES):
            sh_ref[r, 0:used - SUBLANES, :] = pad_ref[r:r + used - SUBLANES, :]
        for blk in range(ROW_TILE // CONV_BLK):
            row0 = blk * CONV_BLK
            s, q = divmod(row0, seq_len)
            base = s * pitch + CONV_PAD + q - CONV_K // 2
            acc = jnp.zeros((CONV_BLK, CONV_W), F32)
            for k in range(CONV_K):
                r = (base + k) % SUBLANES
                row = base + k - r
                tap = pad_ref[row:row + CONV_BLK, :] if r == 0 else sh_ref[r, row:row + CONV_BLK, :]
                acc = acc + w_ref[k:k + 1, :] * tap
            y = acc + db_ref[...]
            yc = y - jnp.mean(y, axis=-1, keepdims=True)
            var = jnp.mean(yc * yc, axis=-1, keepdims=True)
            y = yc * lax.rsqrt(var + LN_EPS) * lg_ref[...] + lb_ref[...]
            o_ref[row0:row0 + CONV_BLK, :] = jax.nn.silu(y).astype(BF16)

    is_ctx = pl.program_id(0) == 0

    @pl.when(is_ctx)
    def _():
        run(CTX)

    @pl.when(jnp.logical_not(is_ctx))
    def _():
        run(GRID_W)


def _conv_module(z, dw_w, dw_b, ln_g, ln_b):
    w = jnp.concatenate([dw_w, jnp.zeros((1, CONV_W), dw_w.dtype)], axis=0)
    vec = lambda v: v.reshape(1, CONV_W)
    pad_rows = max(CTX + 2 * CONV_PAD, (ROW_TILE // GRID_W) * (GRID_W + 2 * CONV_PAD))
    return pl.pallas_call(
        _conv_body,
        grid=(NTOK // ROW_TILE,),
        in_specs=[
            pl.BlockSpec((ROW_TILE, CONV_W), lambda i: (i, OFF_CONV // CONV_W)),
            pl.BlockSpec((ROW_TILE, CONV_W), lambda i: (i, OFF_CONV // CONV_W + 1)),
            pl.BlockSpec((CONV_K + 1, CONV_W), lambda i: (0, 0)),
            pl.BlockSpec((1, CONV_W), lambda i: (0, 0)),
            pl.BlockSpec((1, CONV_W), lambda i: (0, 0)),
            pl.BlockSpec((1, CONV_W), lambda i: (0, 0)),
        ],
        out_specs=pl.BlockSpec((ROW_TILE, CONV_W), lambda i: (i, 0)),
        out_shape=jax.ShapeDtypeStruct((NTOK, CONV_W), BF16),
        scratch_shapes=[pltpu.VMEM((pad_rows, CONV_W), F32), pltpu.VMEM((SUBLANES, pad_rows, CONV_W), F32)],
        compiler_params=_cparams(("arbitrary",)),
        name="conv_module",
    )(z, z, w, vec(dw_b), vec(ln_g), vec(ln_b))


def _ml_chunk_index(ci, reverse):
    if not reverse:
        return ci
    return jnp.where(ci == 0, 0, ML_NCHUNK - ci)


def _ml_direction(qb, kb, vx, li_col, li_row, b_col, b_row, cx_ref, m_ref, mask, reverse):
    c = ML_CHUNK
    scale = ML_DH ** 0.5
    m = m_ref[:, 0:1]
    cx = cx_ref[...]

    d_log = jnp.where(mask, b_col + (li_row - b_row), -jnp.inf)
    inter = b_col + m
    m_row = jnp.maximum(inter, jnp.max(d_log, axis=-1, keepdims=True))
    qk = lax.dot_general(qb, kb, (((1,), (1,)), ((), ())), preferred_element_type=F32)
    s = qk * jnp.exp(d_log - m_row)
    w_inter = jnp.exp(inter - m_row) * scale
    num = jnp.dot(s.astype(BF16), vx, preferred_element_type=F32)
    num += w_inter * jnp.dot(qb, cx.astype(BF16), preferred_element_type=F32)
    den = num[:, ML_DH:ML_DH + 1]
    h = num[:, :ML_DH] / jnp.maximum(jnp.abs(den), jnp.exp(-m_row) * scale)

    b_tot = b_row[:, 0:1] if reverse else b_row[:, c - 1:c]
    g = b_tot - b_col + li_col
    m_new = jnp.maximum(b_tot + m, jnp.max(g, axis=0, keepdims=True))
    kw = kb * (jnp.exp(g - m_new) * (1.0 / scale)).astype(BF16)
    decay = jnp.exp(b_tot + m - m_new)
    cx_ref[...] = decay * cx + lax.dot_general(kw, vx, (((0,), (0,)), ((), ())),
                                               preferred_element_type=F32)
    m_ref[...] = jnp.broadcast_to(m_new, m_ref.shape)
    return h


def _mlstm_body(qkvf_ref, gf_ref, gtf_ref, qkvb_ref, gb_ref, gtb_ref, hf_ref, hb_ref, cx_ref, m_ref):
    @pl.when(pl.program_id(0) == 0)
    def _():
        cx_ref[...] = jnp.zeros_like(cx_ref)
        m_ref[...] = jnp.zeros_like(m_ref)

    c = ML_CHUNK
    r_idx = lax.broadcasted_iota(jnp.int32, (c, c), 0)
    c_idx = lax.broadcasted_iota(jnp.int32, (c, c), 1)
    ones_col = jnp.where(lax.broadcasted_iota(jnp.int32, (c, LANES), 1) == 0, 1.0, 0.0).astype(BF16)

    for d, (qkv_ref, g_ref, gt_ref, h_ref) in enumerate(
            ((qkvf_ref, gf_ref, gtf_ref, hf_ref), (qkvb_ref, gb_ref, gtb_ref, hb_ref))):
        reverse = d == 1
        gates = g_ref[...]
        gates_t = gt_ref[...]
        incl = (c_idx >= r_idx) if reverse else (c_idx <= r_idx)
        tri = jnp.where(incl, 1.0, 0.0).astype(BF16)
        incl_t = (r_idx >= c_idx) if reverse else (r_idx <= c_idx)
        tri_t = jnp.where(incl_t, 1.0, 0.0).astype(BF16)
        lf = jax.nn.log_sigmoid(gates)
        lf_t = jax.nn.log_sigmoid(gates_t)
        b_all = sum(jnp.dot(tri, p, preferred_element_type=F32) for p in _split3(lf))
        b_all_t = sum(jnp.dot(p, tri_t, preferred_element_type=F32) for p in _split3(lf_t))
        for head in range(ML_H):
            i_col = 2 * ML_H * d + head
            f_col = i_col + ML_H
            state = d * ML_H + head
            col = lambda part: slice((part * ML_H + head) * ML_DH, (part * ML_H + head + 1) * ML_DH)
            vx = jnp.concatenate([qkv_ref[:, col(2)], ones_col], axis=1)
            h = _ml_direction(
                qkv_ref[:, col(0)], qkv_ref[:, col(1)], vx,
                gates[:, i_col:i_col + 1], gates_t[i_col:i_col + 1, :],
                b_all[:, f_col:f_col + 1], b_all_t[f_col:f_col + 1, :],
                cx_ref.at[state], m_ref.at[state], incl, reverse)
            h_ref[:, head * ML_DH:(head + 1) * ML_DH] = h.astype(h_ref.dtype)


def _mlstm_readout_body(hf_ref, hb_ref, o_ref, g_ref, out_ref):
    h = hf_ref[...].astype(F32) + hb_ref[...].astype(F32)
    hc = h - jnp.mean(h, axis=-1, keepdims=True)
    var = jnp.mean(hc * hc, axis=-1, keepdims=True)
    h = hc * lax.rsqrt(var + LN_EPS) * g_ref[...]
    out_ref[...] = (jax.nn.sigmoid(o_ref[...].astype(F32)) * h).astype(BF16)


def _mlstm_mixer(qkvo, gates, norm_g):
    gates_t = jnp.transpose(gates[:, :2 * SUBLANES])
    c = ML_CHUNK

    def specs(reverse):
        row = lambda ci: _ml_chunk_index(ci, reverse)
        return [
            pl.BlockSpec((c, 3 * ML_W), lambda ci: (row(ci), 0)),
            pl.BlockSpec((c, LANES), lambda ci: (row(ci), 0)),
            pl.BlockSpec((2 * SUBLANES, c), lambda ci: (0, row(ci))),
        ]

    out_spec = lambda reverse: pl.BlockSpec((c, ML_W), lambda ci: (_ml_chunk_index(ci, reverse), 0))
    n_state = 2 * ML_H
    h_f, h_b = pl.pallas_call(
        _mlstm_body,
        grid=(ML_NCHUNK,),
        in_specs=specs(False) + specs(True),
        out_specs=[out_spec(False), out_spec(True)],
        out_shape=[jax.ShapeDtypeStruct((NTOK, ML_W), BF16)] * 2,
        scratch_shapes=[pltpu.VMEM((n_state, ML_DH, ML_DH + LANES), F32),
                        pltpu.VMEM((n_state, 1, LANES), F32)],
        compiler_params=_cparams(("arbitrary",)),
        name="mlstm_chunks",
    )(qkvo, gates, gates_t, qkvo, gates, gates_t)

    tm = MM_TILE_M
    return pl.pallas_call(
        _mlstm_readout_body,
        grid=(NTOK // tm, ML_H),
        in_specs=[
            pl.BlockSpec((tm, ML_DH), lambda i, hh: (i, hh)),
            pl.BlockSpec((tm, ML_DH), lambda i, hh: (i, hh)),
            pl.BlockSpec((tm, ML_DH), lambda i, hh: (i, 3 * ML_H + hh)),
            pl.BlockSpec((1, ML_DH), lambda i, hh: (0, hh)),
        ],
        out_specs=pl.BlockSpec((tm, ML_DH), lambda i, hh: (i, hh)),
        out_shape=jax.ShapeDtypeStruct((NTOK, ML_W), BF16),
        compiler_params=_cparams(("arbitrary", "arbitrary")),
        name="mlstm_readout",
    )(h_f, h_b, qkvo, norm_g.reshape(1, ML_W))


def kernel(x, c, ctx, c_ctx, w_mod, b_mod, norm1_g, w_in, b_in, s5_lam_re, s5_lam_im, s5_log_step,
           s5_b_re, s5_b_im, s5_c_re, s5_c_im, s5_d, s5_w_glu, s5_b_glu, conv_dw_w, conv_dw_b,
           conv_ln_g, conv_ln_b, ml_norm_g, w_out, norm2_g, w_ffn_in, w_ffn_out, norm_f_g):
    assert x.shape == (1, SEQ, D_MODEL) and ctx.shape == (1, CTX, D_MODEL)
    xs = jnp.concatenate([ctx[0], x[0]], axis=0).astype(F32)
    cc = jnp.zeros((SUBLANES, D_MODEL), F32).at[0].set(c[0]).at[1].set(c_ctx)
    mod_all = _modulation(cc, w_mod, b_mod)
    s5_consts = _s5_selectors()
    s5_ops = _s5_operators(s5_lam_re, s5_lam_im, s5_log_step, s5_b_re, s5_b_im, s5_c_re, s5_c_im,
                           s5_consts[0])
    w_in_t = jnp.swapaxes(w_in, 1, 2)
    n_gate = w_in.shape[2] - OFF_G
    w_gate = jnp.pad(w_in_t[:, OFF_G:, :], ((0, 0), (0, LANES - n_gate), (0, 0)))
    w_gate_hi = w_gate.astype(BF16)
    w_gate = jnp.stack([w_gate_hi, (w_gate - w_gate_hi.astype(F32)).astype(BF16)], axis=1)
    b_gate = jnp.pad(b_in[:, OFF_G:], ((0, 0), (0, LANES - n_gate))).reshape(DEPTH, 1, LANES)
    tn = MM_TILE_N

    h, gates = _norm_mod(xs, norm1_g[0], mod_all, 0, 0, w_gate[0], b_gate[0])
    for l in range(DEPTH):
        z = _in_proj(h, w_in_t, b_in, l, 0, OFF_Q // tn, 1, F32)
        qkvo = _in_proj(h, w_in_t, b_in, l, OFF_Q // tn, (OFF_G - OFF_Q) // (2 * tn), 2, BF16)
        s5o = _s5_mixer(z, s5_ops, s5_consts, l, s5_d[l], s5_w_glu[l], s5_b_glu[l])
        cvo = _conv_module(z, conv_dw_w[l], conv_dw_b[l], conv_ln_g[l], conv_ln_b[l])
        mlo = _mlstm_mixer(qkvo, gates, ml_norm_g[l])
        xs, h2 = _out_proj(xs, s5o, cvo, mlo, w_out, mod_all, norm2_g[l], l)
        hid = _ffn_in(h2, w_ffn_in, l)
        xs = _ffn_out(xs, hid, w_ffn_out, mod_all, l)
        if l + 1 < DEPTH:
            h, gates = _norm_mod(xs, norm1_g[l + 1], mod_all, l + 1, 0, w_gate[l + 1], b_gate[l + 1])
    return _final_norm(xs, norm_f_g)[None]
```

```python
import functools

import numpy as np
import jax
import jax.numpy as jnp
from jax import lax
from jax.experimental import pallas as pl
from jax.experimental.pallas import tpu as pltpu

F32 = jnp.float32
BF16 = jnp.bfloat16

D_MODEL = 2048
SEQ = 8192
CTX = 256
NTOK = SEQ + CTX
DEPTH = 4
GRID_W = 64

S5_W = 512
S5_CH = 16
S5_G = 32
S5_P = 64
CONV_W = 512
CONV_K = 31
ML_W = 1024
ML_H = 4
ML_DH = 256
D_FF = 5632
EPS = 1e-6
LN_EPS = 1e-5

OFF_CONV = 512
OFF_Q = 1536
OFF_K = 2560
OFF_V = 3584
OFF_O = 4608
OFF_G = 5632

LANES = 128
SUBLANES = 8
VMEM_LIMIT = 56 * 1024 * 1024

ROW_TILE = 256
MM_TILE_M = 1056
MM_TILE_N = 512

S5_T = 8
S5_ROWS = NTOK // S5_T
S5_CTX_ROWS = CTX // S5_T
S5_BLK = S5_T * S5_CH
S5_STEP_G = 8
S5_STEP_PAIRS = S5_STEP_G // 2
S5_SET_CH = S5_STEP_G * S5_CH
S5_STEP_W = S5_STEP_G * S5_BLK
S5_FLAT = S5_G * S5_BLK

ML_CHUNK = 256
ML_NCHUNK = NTOK // ML_CHUNK


def _cparams(sem, vmem=VMEM_LIMIT):
    return pltpu.CompilerParams(dimension_semantics=sem, vmem_limit_bytes=vmem)


def _dot(a, b):
    return jnp.dot(a.astype(BF16), b.astype(BF16), preferred_element_type=F32)


def _split3(x):
    a = x.astype(BF16)
    r = x - a.astype(F32)
    b = r.astype(BF16)
    c = (r - b.astype(F32)).astype(BF16)
    return a, b, c


def _mod_body(cc_ref, w_ref, b_ref, o_ref):
    s = jax.nn.silu(cc_ref[...])
    o_ref[0] = _dot(s, w_ref[0]) + b_ref[0]


def _modulation(cc, w_mod, b_mod):
    depth, _, n = w_mod.shape
    tn = 1024
    return pl.pallas_call(
        _mod_body,
        grid=(depth, n // tn),
        in_specs=[
            pl.BlockSpec((SUBLANES, D_MODEL), lambda l, j: (0, 0)),
            pl.BlockSpec((1, D_MODEL, tn), lambda l, j: (l, 0, j)),
            pl.BlockSpec((1, 1, tn), lambda l, j: (l, 0, j)),
        ],
        out_specs=pl.BlockSpec((1, SUBLANES, tn), lambda l, j: (l, 0, j)),
        out_shape=jax.ShapeDtypeStruct((depth, SUBLANES, n), F32),
        compiler_params=_cparams(("arbitrary", "arbitrary")),
        name="adaln_modulation",
    )(cc, w_mod, b_mod.reshape(depth, 1, n))


def _mod_row(m_ref, is_ctx):
    return jnp.where(is_ctx, m_ref[0, 1:2, :], m_ref[0, 0:1, :])


def _norm_body(*refs, modulate, with_gates):
    it = iter(refs)
    x_ref, g_ref = next(it), next(it)
    sh_ref = sc_ref = wg_ref = bg_ref = gate_ref = None
    if modulate:
        sh_ref, sc_ref = next(it), next(it)
    if with_gates:
        wg_ref, bg_ref = next(it), next(it)
    h_ref = next(it)
    if with_gates:
        gate_ref = next(it)

    xf = x_ref[...]
    ms = jnp.mean(xf * xf, axis=-1, keepdims=True)
    h = xf * lax.rsqrt(ms + EPS) * g_ref[...]
    if modulate:
        is_ctx = pl.program_id(0) == 0
        h = h * (1.0 + _mod_row(sc_ref, is_ctx)) + _mod_row(sh_ref, is_ctx)
    h_ref[...] = h.astype(h_ref.dtype)
    if with_gates:
        hi = h.astype(BF16)
        lo = (h - hi.astype(F32)).astype(BF16)
        whi, wlo = wg_ref[0], wg_ref[1]
        nt = lambda a, b: lax.dot_general(a, b, (((1,), (1,)), ((), ())), preferred_element_type=F32)
        gate_ref[...] = nt(hi, whi) + nt(hi, wlo) + nt(lo, whi) + bg_ref[...]


def _norm_mod(xs, g, mod_all, layer, phase, w_gate=None, b_gate=None):
    with_gates = w_gate is not None
    n_tiles = NTOK // ROW_TILE
    in_specs = [
        pl.BlockSpec((ROW_TILE, D_MODEL), lambda i: (i, 0)),
        pl.BlockSpec((1, D_MODEL), lambda i: (0, 0)),
        pl.BlockSpec((1, SUBLANES, D_MODEL), lambda i: (layer, 0, 3 * phase)),
        pl.BlockSpec((1, SUBLANES, D_MODEL), lambda i: (layer, 0, 3 * phase + 1)),
    ]
    args = [xs, g.reshape(1, D_MODEL), mod_all, mod_all]
    out_specs = [pl.BlockSpec((ROW_TILE, D_MODEL), lambda i: (i, 0))]
    out_shape = [jax.ShapeDtypeStruct((NTOK, D_MODEL), BF16)]
    if with_gates:
        in_specs += [pl.BlockSpec((2, LANES, D_MODEL), lambda i: (0, 0, 0)),
                     pl.BlockSpec((1, LANES), lambda i: (0, 0))]
        args += [w_gate, b_gate]
        out_specs.append(pl.BlockSpec((ROW_TILE, LANES), lambda i: (i, 0)))
        out_shape.append(jax.ShapeDtypeStruct((NTOK, LANES), F32))
    return pl.pallas_call(
        functools.partial(_norm_body, modulate=True, with_gates=with_gates),
        grid=(n_tiles,),
        in_specs=in_specs,
        out_specs=out_specs,
        out_shape=out_shape,
        compiler_params=_cparams(("arbitrary",)),
        name="rmsnorm_modulate",
    )(*args)


def _final_norm(xs, g):
    skip = CTX // ROW_TILE
    return pl.pallas_call(
        functools.partial(_norm_body, modulate=False, with_gates=False),
        grid=(SEQ // ROW_TILE,),
        in_specs=[pl.BlockSpec((ROW_TILE, D_MODEL), lambda i: (i + skip, 0)),
                  pl.BlockSpec((1, D_MODEL), lambda i: (0, 0))],
        out_specs=[pl.BlockSpec((ROW_TILE, D_MODEL), lambda i: (i, 0))],
        out_shape=[jax.ShapeDtypeStruct((SEQ, D_MODEL), F32)],
        compiler_params=_cparams(("arbitrary",)),
        name="final_rmsnorm",
    )(xs, g.reshape(1, D_MODEL))[0]


def _in_proj_body(*refs, n_w):
    a_ref, wt_refs, b_refs = refs[0], refs[1:1 + n_w], refs[1 + n_w:1 + 2 * n_w]
    o_ref, wbf_ref = refs[1 + 2 * n_w:]
    tn = MM_TILE_N

    @pl.when(pl.program_id(1) == 0)
    def _():
        for k, wt_ref in enumerate(wt_refs):
            wbf_ref[:, k * tn:(k + 1) * tn] = jnp.transpose(wt_ref[...]).astype(BF16)

    bias = jnp.concatenate([b_ref[...] for b_ref in b_refs], axis=1)
    acc = jnp.dot(a_ref[...], wbf_ref[...], preferred_element_type=F32) + bias
    o_ref[...] = acc.astype(o_ref.dtype)


def _in_proj(h, w_in_t, b_in, layer, first_tile, n_steps, n_w, dtype):
    tm, tn = MM_TILE_M, MM_TILE_N
    in_w = w_in_t.shape[1]
    tile = lambda j, k: first_tile + n_w * j + k
    w_specs = [pl.BlockSpec((None, tn, D_MODEL), functools.partial(lambda j, i, k: (layer, tile(j, k), 0), k=k))
               for k in range(n_w)]
    b_specs = [pl.BlockSpec((None, 1, tn), functools.partial(lambda j, i, k: (layer, 0, tile(j, k)), k=k))
               for k in range(n_w)]
    b3 = b_in.reshape(DEPTH, 1, in_w)
    return pl.pallas_call(
        functools.partial(_in_proj_body, n_w=n_w),
        grid=(n_steps, NTOK // tm),
        in_specs=[pl.BlockSpec((tm, D_MODEL), lambda j, i: (i, 0))] + w_specs + b_specs,
        out_specs=pl.BlockSpec((tm, n_w * tn), lambda j, i: (i, j)),
        out_shape=jax.ShapeDtypeStruct((NTOK, n_steps * n_w * tn), dtype),
        scratch_shapes=[pltpu.VMEM((D_MODEL, n_w * tn), BF16)],
        compiler_params=_cparams(("arbitrary", "arbitrary")),
        name="in_proj",
    )(h, *([w_in_t] * n_w), *([b3] * n_w))


def _row_gate(g_ref, i, tm, tn):
    rows = i * tm + lax.broadcasted_iota(jnp.int32, (tm, tn), 0)
    return jnp.where(rows < CTX, g_ref[0, 1:2, :], g_ref[0, 0:1, :])


def _out_proj_body(y_ref, wglu_ref, bglu_ref, cv_ref, hf_ref, hb_ref, o_ref, mlg_ref, w_ref,
                   gate_ref, sh_ref, sc_ref, ng_ref, x_ref, xo_ref, h_ref, wbf_ref, wglu_bf_ref):
    i = pl.program_id(0)

    @pl.when(i == 0)
    def _():
        wbf_ref[...] = w_ref[...].astype(BF16)
        wglu_bf_ref[...] = wglu_ref[...].astype(BF16)

    g = jax.nn.gelu(y_ref[...])
    glu = jnp.dot(g.astype(BF16), wglu_bf_ref[...], preferred_element_type=F32) + bglu_ref[...]
    s5 = (g * jax.nn.sigmoid(glu)).astype(BF16)
    acc = jnp.dot(s5, wbf_ref[0:S5_W, :], preferred_element_type=F32)
    acc += jnp.dot(cv_ref[...], wbf_ref[S5_W:S5_W + CONV_W, :], preferred_element_type=F32)
    for head in range(ML_H):
        cols = slice(head * ML_DH, (head + 1) * ML_DH)
        hh = hf_ref[:, cols].astype(F32) + hb_ref[:, cols].astype(F32)
        hc = hh - jnp.mean(hh, axis=-1, keepdims=True)
        var = jnp.mean(hc * hc, axis=-1, keepdims=True)
        ml = jax.nn.sigmoid(o_ref[:, cols].astype(F32)) * (hc * lax.rsqrt(var + LN_EPS) * mlg_ref[:, cols])
        row0 = S5_W + CONV_W + head * ML_DH
        acc += jnp.dot(ml.astype(BF16), wbf_ref[row0:row0 + ML_DH, :], preferred_element_type=F32)
    is_ctx = i == 0
    xn = x_ref[...] + _mod_row(gate_ref, is_ctx) * acc
    xo_ref[...] = xn
    ms = jnp.mean(xn * xn, axis=-1, keepdims=True)
    h = xn * lax.rsqrt(ms + EPS) * ng_ref[...]
    h_ref[...] = (h * (1.0 + _mod_row(sc_ref, is_ctx)) + _mod_row(sh_ref, is_ctx)).astype(BF16)


def _out_proj(xs, y_s5, w_glu, b_glu, cvo, h_f, h_b, qkvo, ml_norm_g, w_out, mod_all, norm_g, layer):
    tm = ROW_TILE
    mod = lambda k: pl.BlockSpec((1, SUBLANES, D_MODEL), lambda i: (layer, 0, k))
    rows = lambda width, col=0: pl.BlockSpec((tm, width), lambda i: (i, col))
    const = lambda shape: pl.BlockSpec(shape, lambda i: (0,) * len(shape))
    return pl.pallas_call(
        _out_proj_body,
        grid=(NTOK // tm,),
        in_specs=[
            rows(S5_W), const((S5_W, S5_W)), const((1, S5_W)),
            rows(CONV_W),
            rows(ML_W), rows(ML_W), rows(ML_W, 3), const((1, ML_W)),
            pl.BlockSpec((None, D_MODEL, D_MODEL), lambda i: (layer, 0, 0), pipeline_mode=pl.Buffered(1)),
            mod(2), mod(3), mod(4),
            const((1, D_MODEL)),
            rows(D_MODEL),
        ],
        out_specs=[rows(D_MODEL), rows(D_MODEL)],
        out_shape=[jax.ShapeDtypeStruct((NTOK, D_MODEL), F32), jax.ShapeDtypeStruct((NTOK, D_MODEL), BF16)],
        scratch_shapes=[pltpu.VMEM((D_MODEL, D_MODEL), BF16), pltpu.VMEM((S5_W, S5_W), BF16)],
        compiler_params=_cparams(("arbitrary",)),
        name="out_proj_residual",
    )(y_s5, w_glu, b_glu.reshape(1, S5_W), cvo, h_f, h_b, qkvo, ml_norm_g.reshape(1, ML_W), w_out,
      mod_all, mod_all, mod_all, norm_g.reshape(1, D_MODEL), xs)


def _ffn_in_body(a_ref, wg_ref, wu_ref, o_ref, wgbf_ref, wubf_ref):
    @pl.when(pl.program_id(1) == 0)
    def _():
        wgbf_ref[...] = wg_ref[...].astype(BF16)
        wubf_ref[...] = wu_ref[...].astype(BF16)

    a = a_ref[...]
    g = jnp.dot(a, wgbf_ref[...], preferred_element_type=F32)
    u = jnp.dot(a, wubf_ref[...], preferred_element_type=F32)
    o_ref[...] = (jax.nn.silu(g) * u).astype(BF16)


def _ffn_in(h, w_ffn_in, layer):
    tm, tn = MM_TILE_M, MM_TILE_N
    nj = D_FF // tn
    return pl.pallas_call(
        _ffn_in_body,
        grid=(nj, NTOK // tm),
        in_specs=[
            pl.BlockSpec((tm, D_MODEL), lambda j, i: (i, 0)),
            pl.BlockSpec((None, D_MODEL, tn), lambda j, i: (layer, 0, j)),
            pl.BlockSpec((None, D_MODEL, tn), lambda j, i: (layer, 0, nj + j)),
        ],
        out_specs=pl.BlockSpec((tm, tn), lambda j, i: (i, j)),
        out_shape=jax.ShapeDtypeStruct((NTOK, D_FF), BF16),
        scratch_shapes=[pltpu.VMEM((D_MODEL, tn), BF16), pltpu.VMEM((D_MODEL, tn), BF16)],
        compiler_params=_cparams(("arbitrary", "arbitrary")),
        name="ffn_in_swiglu",
    )(h, w_ffn_in, w_ffn_in)


def _ffn_out_body(a_ref, w_ref, g_ref, x_ref, o_ref, wbf_ref):
    i = pl.program_id(1)

    @pl.when(i == 0)
    def _():
        wbf_ref[...] = w_ref[...].astype(BF16)

    acc = jnp.dot(a_ref[...], wbf_ref[...], preferred_element_type=F32)
    tm, tn = o_ref.shape
    o_ref[...] = x_ref[...] + _row_gate(g_ref, i, tm, tn) * acc


def _ffn_out(xs, hid, w_ffn_out, mod_all, layer):
    tm, tn = MM_TILE_M // 2, MM_TILE_N
    return pl.pallas_call(
        _ffn_out_body,
        grid=(D_MODEL // tn, NTOK // tm),
        in_specs=[
            pl.BlockSpec((tm, D_FF), lambda j, i: (i, 0)),
            pl.BlockSpec((None, D_FF, tn), lambda j, i: (layer, 0, j)),
            pl.BlockSpec((1, SUBLANES, tn), lambda j, i: (layer, 0, 5 * (D_MODEL // tn) + j)),
            pl.BlockSpec((tm, tn), lambda j, i: (i, j)),
        ],
        out_specs=pl.BlockSpec((tm, tn), lambda j, i: (i, j)),
        out_shape=jax.ShapeDtypeStruct((NTOK, D_MODEL), F32),
        scratch_shapes=[pltpu.VMEM((D_FF, tn), BF16)],
        compiler_params=_cparams(("arbitrary", "arbitrary")),
        name="ffn_out_residual",
    )(hid, w_ffn_out, mod_all, xs)


def _s5_selectors():
    tau = np.arange(S5_BLK)[:, None] // S5_CH
    ch_r = np.arange(S5_BLK)[:, None] % S5_CH
    t = np.arange(S5_BLK)[None, :] // S5_CH
    ch_c = np.arange(S5_BLK)[None, :] % S5_CH
    sel = np.zeros((2, S5_T, S5_BLK, S5_BLK), np.float32)
    for s in range(S5_T):
        sel[0, s] = (tau == t - s) & (ch_r == ch_c)
        sel[1, s] = (tau == s - t) & (ch_r == ch_c)
    rp = np.arange(S5_BLK)[:, None]
    col = np.arange(2 * S5_BLK)[None, :]
    pair_map = np.stack([(col == (rp // S5_P) * S5_BLK + gi * S5_P + rp % S5_P) for gi in range(2)])
    src = np.arange(S5_STEP_W)
    dst = ((src // S5_CH) % S5_STEP_G) * S5_BLK + (src // S5_SET_CH) * S5_CH + src % S5_CH
    perm = dst[:, None] == np.arange(S5_STEP_W)[None, :]
    return (jnp.asarray(sel), jnp.asarray(pair_map, BF16), jnp.asarray(perm, BF16),
            jnp.asarray(perm.T, BF16))


def _s5_operators(lam_re, lam_im, log_step, b_re, b_im, c_re, c_im, sel):
    lam = lax.complex(lam_re.astype(F32), lam_im.astype(F32))
    lam_bar = jnp.exp(lam * jnp.exp(log_step.astype(F32)))
    b_bar = ((lam_bar - 1.0) / lam)[..., None] * lax.complex(b_re.astype(F32), b_im.astype(F32))
    c_mat = lax.complex(c_re.astype(F32), c_im.astype(F32))
    depth = lam.shape[0]

    def powers(base, count):
        out = [jnp.ones_like(base)]
        for _ in range(count - 1):
            out.append(out[-1] * base)
        return out

    pw = powers(lam_bar, S5_T + 1)
    pa = powers(pw[S5_T], SUBLANES + 1)
    t_up = list(range(S5_T))

    def table(seq, fwd_idx, bwd_idx):
        return jnp.stack([jnp.stack([seq[f][:, 0], seq[b][:, 1]], axis=1)
                          for f, b in zip(fwd_idx, bwd_idx)])

    blocks = (depth, 2, S5_G, S5_BLK, S5_BLK)

    def token_rows(w, imag_sign):
        w = jnp.moveaxis(w, 0, 3)
        return jnp.concatenate([jnp.real(w), imag_sign * jnp.imag(w)], axis=-1).reshape(blocks)

    pb = table(pw, [S5_T - 1 - t for t in t_up], t_up)
    wb = token_rows(pb[:, :, :, :, None, :] * jnp.swapaxes(b_bar, -1, -2)[None], 1.0)
    pc = table(pw, [t + 1 for t in t_up], [S5_T - t for t in t_up])
    wct = token_rows(pc[:, :, :, :, None, :] * c_mat[None], -1.0)

    kern = jnp.real(jnp.einsum('ldgcp,kldgp,ldgpa->ldgakc', c_mat, jnp.stack(pw[:S5_T]), b_bar))
    kern = kern.reshape(depth, 2, S5_G * S5_CH, S5_BLK)
    kc = jnp.einsum('ldxk,dskn->ldsxn', kern, sel, precision=lax.Precision.HIGHEST)
    kc = kc.reshape(depth, 2, S5_T, S5_G, S5_CH, S5_BLK)
    kc = jnp.transpose(kc, (0, 1, 3, 2, 4, 5)).reshape(blocks)

    def lanes(v):
        parts = jnp.stack([jnp.real(v), jnp.imag(v)], axis=4)
        parts = parts.reshape(v.shape[0], depth, 2, S5_G // 2, 2, 2, S5_P)
        parts = jnp.swapaxes(parts, 4, 5).reshape(v.shape[0], depth, 2, S5_FLAT)
        return jnp.transpose(parts, (1, 2, 0, 3))

    zero_p = jnp.zeros_like(pa[0])
    amul = lanes(jnp.stack([pa[1], pa[2], pa[4], pa[8]] + [zero_p] * 4))
    apow = lanes(table(pa, t_up, t_up[::-1]))
    return kc.astype(BF16), wb.astype(BF16), wct.astype(BF16), amul, apow


def _s5_row_scan(e_ref, amul_ref, apow_ref, block_lo, block_hi, reverse, carry):
    rows = lax.broadcasted_iota(jnp.int32, (SUBLANES, S5_BLK), 0)

    def shifted(x, k):
        if reverse:
            return jnp.where(rows < SUBLANES - k, pltpu.roll(x, SUBLANES - k, 0), 0.0)
        return jnp.where(rows >= k, pltpu.roll(x, k, 0), 0.0)

    def body(step, carry):
        blk = (block_hi - 1 - step) if reverse else (block_lo + step)
        r0 = pl.multiple_of(blk * SUBLANES, SUBLANES)
        last = 0 if reverse else SUBLANES - 1
        carry_out = []
        for q in range(S5_STEP_PAIRS):
            re_c = slice(2 * q * S5_BLK, (2 * q + 1) * S5_BLK)
            im_c = slice((2 * q + 1) * S5_BLK, (2 * q + 2) * S5_BLK)
            s_re = e_ref[pl.ds(r0, SUBLANES), re_c]
            s_im = e_ref[pl.ds(r0, SUBLANES), im_c]
            for idx, k in enumerate((1, 2, 4)):
                a_re, a_im = amul_ref[idx:idx + 1, re_c], amul_ref[idx:idx + 1, im_c]
                t_re, t_im = shifted(s_re, k), shifted(s_im, k)
                s_re, s_im = s_re + t_re * a_re - t_im * a_im, s_im + t_re * a_im + t_im * a_re
            c_re, c_im = carry[:, re_c], carry[:, im_c]
            p_re, p_im = apow_ref[:, re_c], apow_ref[:, im_c]
            e_ref[pl.ds(r0, SUBLANES), re_c] = shifted(s_re, 1) + p_re * c_re - p_im * c_im
            e_ref[pl.ds(r0, SUBLANES), im_c] = shifted(s_im, 1) + p_re * c_im + p_im * c_re
            a_re, a_im = amul_ref[3:4, re_c], amul_ref[3:4, im_c]
            l_re = jnp.broadcast_to(s_re[last:last + 1, :], (SUBLANES, S5_BLK))
            l_im = jnp.broadcast_to(s_im[last:last + 1, :], (SUBLANES, S5_BLK))
            carry_out += [l_re + a_re * c_re - a_im * c_im, l_im + a_re * c_im + a_im * c_re]
        return jnp.concatenate(carry_out, axis=1)

    return lax.fori_loop(0, block_hi - block_lo, body, carry)


def _s5_pair_operators(kc_ref, wb_ref, wct_ref, map_ref, q):
    g0, g1 = 2 * q, 2 * q + 1
    zero = jnp.zeros((S5_BLK, S5_BLK), BF16)
    kc = jnp.concatenate([jnp.concatenate([kc_ref[g0], zero], axis=1),
                          jnp.concatenate([zero, kc_ref[g1]], axis=1)], axis=0)
    spread = lambda ref: jnp.concatenate(
        [jnp.dot(ref[g0], map_ref[0], preferred_element_type=F32),
         jnp.dot(ref[g1], map_ref[1], preferred_element_type=F32)], axis=0).astype(BF16)
    return kc, spread(wb_ref), spread(wct_ref)


def _s5_body(u_ref, p_ref, pt_ref, map_ref, kc_ref, wb_ref, wct_ref, amul_ref, apow_ref, d_ref, y_ref,
             up_ref, e_ref, acc_ref):
    direction = pl.program_id(1)
    token_rows = lambda t: pl.ds(t, S5_ROWS, stride=S5_T)
    pairs = [slice(2 * q * S5_BLK, 2 * (q + 1) * S5_BLK) for q in range(S5_STEP_PAIRS)]

    @pl.when(direction == 0)
    def _():
        u_nat = jnp.concatenate([u_ref[token_rows(t), :] for t in range(S5_T)], axis=1)
        up_ref[...] = jnp.dot(u_nat.astype(BF16), p_ref[...], preferred_element_type=F32).astype(BF16)
        acc_ref[...] = jnp.zeros_like(acc_ref)

    ops = [_s5_pair_operators(kc_ref, wb_ref, wct_ref, map_ref, q) for q in range(S5_STEP_PAIRS)]
    for (_, wb, _), cols in zip(ops, pairs):
        e_ref[:, cols] = jnp.dot(up_ref[:, cols], wb, preferred_element_type=F32)

    zero = jnp.zeros((SUBLANES, S5_STEP_W), F32)
    n_blocks = S5_ROWS // SUBLANES
    ctx_blocks = S5_CTX_ROWS // SUBLANES
    amul, apow = amul_ref, apow_ref

    @pl.when(direction == 0)
    def _():
        _s5_row_scan(e_ref, amul, apow, 0, n_blocks, False, zero)

    @pl.when(direction == 1)
    def _():
        carry = _s5_row_scan(e_ref, amul, apow, 0, ctx_blocks, True, zero)
        _s5_row_scan(e_ref, amul, apow, ctx_blocks, n_blocks, True, carry)

    for (kc, _, wct), cols in zip(ops, pairs):
        y = jnp.dot(up_ref[:, cols], kc, preferred_element_type=F32)
        y += lax.dot_general(e_ref[:, cols].astype(BF16), wct, (((1,), (1,)), ((), ())),
                             preferred_element_type=F32)
        acc_ref[:, cols] += y

    @pl.when(direction == 1)
    def _():
        acc = acc_ref[...]
        hi = acc.astype(BF16)
        lo = (acc - hi.astype(F32)).astype(BF16)
        y_nat = (jnp.dot(hi, pt_ref[...], preferred_element_type=F32)
                 + jnp.dot(lo, pt_ref[...], preferred_element_type=F32))
        for t in range(S5_T):
            y_ref[token_rows(t), :] = (y_nat[:, t * S5_SET_CH:(t + 1) * S5_SET_CH]
                                       + d_ref[...] * u_ref[token_rows(t), :])


def _s5_mixer(z, ops, consts, layer, d_skip):
    kc, wb, wct, amul, apow = ops
    _, pair_map, perm, perm_t = consts
    n_steps = S5_G // S5_STEP_G
    wspec = pl.BlockSpec((None, None, S5_STEP_G, S5_BLK, S5_BLK), lambda j, d: (layer, d, j, 0, 0))
    cspec = pl.BlockSpec((None, None, SUBLANES, S5_STEP_W), lambda j, d: (layer, d, 0, j))
    pspec = pl.BlockSpec((S5_STEP_W, S5_STEP_W), lambda j, d: (0, 0))
    return pl.pallas_call(
        _s5_body,
        grid=(n_steps, 2),
        in_specs=[
            pl.BlockSpec((NTOK, S5_SET_CH), lambda j, d: (0, j)),
            pspec, pspec,
            pl.BlockSpec((2, S5_BLK, 2 * S5_BLK), lambda j, d: (0, 0, 0)),
            wspec, wspec, wspec, cspec, cspec,
            pl.BlockSpec((1, S5_SET_CH), lambda j, d: (0, j)),
        ],
        out_specs=pl.BlockSpec((NTOK, S5_SET_CH), lambda j, d: (0, j)),
        out_shape=jax.ShapeDtypeStruct((NTOK, S5_W), F32),
        scratch_shapes=[pltpu.VMEM((S5_ROWS, S5_STEP_W), BF16), pltpu.VMEM((S5_ROWS, S5_STEP_W), F32),
                        pltpu.VMEM((S5_ROWS, S5_STEP_W), F32)],
        compiler_params=_cparams(("arbitrary", "arbitrary")),
        name="s5_scan",
    )(z, perm, perm_t, pair_map, kc, wb, wct, amul, apow, d_skip.astype(F32).reshape(1, S5_W))


CONV_PAD = 16
CONV_BLK = 64


def _conv_body(a_ref, b_ref, w_ref, db_ref, lg_ref, lb_ref, o_ref, pad_ref, sh_ref):
    g = a_ref[...] * jax.nn.sigmoid(b_ref[...])

    def run(seq_len):
        n_seq = ROW_TILE // seq_len
        pitch = seq_len + 2 * CONV_PAD
        zeros = jnp.zeros((CONV_PAD, CONV_W), F32)
        for s in range(n_seq):
            pad_ref[s * pitch:s * pitch + CONV_PAD, :] = zeros
            pad_ref[s * pitch + CONV_PAD:s * pitch + CONV_PAD + seq_len, :] = g[s * seq_len:(s + 1) * seq_len]
            pad_ref[s * pitch + CONV_PAD + seq_len:(s + 1) * pitch, :] = zeros
        used = n_seq * pitch
        for r in range(1, SUBLANES):
            sh_ref[r, 0:used - SUBLANES, :] = pad_ref[r:r + used - SUBLANES, :]
        for blk in range(ROW_TILE // CONV_BLK):
            row0 = blk * CONV_BLK
            s, q = divmod(row0, seq_len)
            base = s * pitch + CONV_PAD + q - CONV_K // 2
            acc = jnp.zeros((CONV_BLK, CONV_W), F32)
            for k in range(CONV_K):
                r = (base + k) % SUBLANES
                row = base + k - r
                tap = pad_ref[row:row + CONV_BLK, :] if r == 0 else sh_ref[r, row:row + CONV_BLK, :]
                acc = acc + w_ref[k:k + 1, :] * tap
            y = acc + db_ref[...]
            yc = y - jnp.mean(y, axis=-1, keepdims=True)
            var = jnp.mean(yc * yc, axis=-1, keepdims=True)
            y = yc * lax.rsqrt(var + LN_EPS) * lg_ref[...] + lb_ref[...]
            o_ref[row0:row0 + CONV_BLK, :] = jax.nn.silu(y).astype(BF16)

    is_ctx = pl.program_id(0) == 0

    @pl.when(is_ctx)
    def _():
        run(CTX)

    @pl.when(jnp.logical_not(is_ctx))
    def _():
        run(GRID_W)


def _conv_module(z, dw_w, dw_b, ln_g, ln_b):
    w = jnp.concatenate([dw_w, jnp.zeros((1, CONV_W), dw_w.dtype)], axis=0)
    vec = lambda v: v.reshape(1, CONV_W)
    pad_rows = max(CTX + 2 * CONV_PAD, (ROW_TILE // GRID_W) * (GRID_W + 2 * CONV_PAD))
    return pl.pallas_call(
        _conv_body,
        grid=(NTOK // ROW_TILE,),
        in_specs=[
            pl.BlockSpec((ROW_TILE, CONV_W), lambda i: (i, OFF_CONV // CONV_W)),
            pl.BlockSpec((ROW_TILE, CONV_W), lambda i: (i, OFF_CONV // CONV_W + 1)),
            pl.BlockSpec((CONV_K + 1, CONV_W), lambda i: (0, 0)),
            pl.BlockSpec((1, CONV_W), lambda i: (0, 0)),
            pl.BlockSpec((1, CONV_W), lambda i: (0, 0)),
            pl.BlockSpec((1, CONV_W), lambda i: (0, 0)),
        ],
        out_specs=pl.BlockSpec((ROW_TILE, CONV_W), lambda i: (i, 0)),
        out_shape=jax.ShapeDtypeStruct((NTOK, CONV_W), BF16),
        scratch_shapes=[pltpu.VMEM((pad_rows, CONV_W), F32), pltpu.VMEM((SUBLANES, pad_rows, CONV_W), F32)],
        compiler_params=_cparams(("arbitrary",)),
        name="conv_module",
    )(z, z, w, vec(dw_b), vec(ln_g), vec(ln_b))


def _ml_chunk_index(ci, reverse):
    if not reverse:
        return ci
    return jnp.where(ci == 0, 0, ML_NCHUNK - ci)


def _ml_direction(qb, kb, vx, li_col, li_row, b_col, b_row, cx_ref, m_ref, mask, reverse):
    c = ML_CHUNK
    scale = ML_DH ** 0.5
    m = m_ref[:, 0:1]
    cx = cx_ref[...]

    d_log = jnp.where(mask, b_col + (li_row - b_row), -jnp.inf)
    inter = b_col + m
    m_row = jnp.maximum(inter, jnp.max(d_log, axis=-1, keepdims=True))
    qk = lax.dot_general(qb, kb, (((1,), (1,)), ((), ())), preferred_element_type=F32)
    s = qk * jnp.exp(d_log - m_row)
    w_inter = jnp.exp(inter - m_row) * scale
    lhs = jnp.concatenate([s.astype(BF16), qb * w_inter.astype(BF16)], axis=1)
    rhs = jnp.concatenate([vx, cx.astype(BF16)], axis=0)
    num = jnp.dot(lhs, rhs, preferred_element_type=F32)
    den = num[:, ML_DH:ML_DH + 1]
    h = num[:, :ML_DH] / jnp.maximum(jnp.abs(den), jnp.exp(-m_row) * scale)

    b_tot = b_row[:, 0:1] if reverse else b_row[:, c - 1:c]
    g = b_tot - b_col + li_col
    m_new = jnp.maximum(b_tot + m, jnp.max(g, axis=0, keepdims=True))
    kw = kb * (jnp.exp(g - m_new) * (1.0 / scale)).astype(BF16)
    decay = jnp.exp(b_tot + m - m_new)
    cx_ref[...] = decay * cx + lax.dot_general(kw, vx, (((0,), (0,)), ((), ())),
                                               preferred_element_type=F32)
    m_ref[...] = jnp.broadcast_to(m_new, m_ref.shape)
    return h


def _mlstm_body(qkvf_ref, gf_ref, gtf_ref, qkvb_ref, gb_ref, gtb_ref, hf_ref, hb_ref, cx_ref, m_ref):
    @pl.when(pl.program_id(0) == 0)
    def _():
        cx_ref[...] = jnp.zeros_like(cx_ref)
        m_ref[...] = jnp.zeros_like(m_ref)

    c = ML_CHUNK
    r_idx = lax.broadcasted_iota(jnp.int32, (c, c), 0)
    c_idx = lax.broadcasted_iota(jnp.int32, (c, c), 1)
    ones_col = jnp.where(lax.broadcasted_iota(jnp.int32, (c, LANES), 1) == 0, 1.0, 0.0).astype(BF16)

    for d, (qkv_ref, g_ref, gt_ref, h_ref) in enumerate(
            ((qkvf_ref, gf_ref, gtf_ref, hf_ref), (qkvb_ref, gb_ref, gtb_ref, hb_ref))):
        reverse = d == 1
        gates = g_ref[...]
        gates_t = gt_ref[...]
        incl = (c_idx >= r_idx) if reverse else (c_idx <= r_idx)
        tri = jnp.where(incl, 1.0, 0.0).astype(BF16)
        incl_t = (r_idx >= c_idx) if reverse else (r_idx <= c_idx)
        tri_t = jnp.where(incl_t, 1.0, 0.0).astype(BF16)
        lf = jax.nn.log_sigmoid(gates)
        lf_t = jax.nn.log_sigmoid(gates_t)
        b_all = sum(jnp.dot(tri, p, preferred_element_type=F32) for p in _split3(lf))
        b_all_t = sum(jnp.dot(p, tri_t, preferred_element_type=F32) for p in _split3(lf_t))
        for head in range(ML_H):
            i_col = 2 * ML_H * d + head
            f_col = i_col + ML_H
            state = d * ML_H + head
            col = lambda part: slice((part * ML_H + head) * ML_DH, (part * ML_H + head + 1) * ML_DH)
            vx = jnp.concatenate([qkv_ref[:, col(2)], ones_col], axis=1)
            h = _ml_direction(
                qkv_ref[:, col(0)], qkv_ref[:, col(1)], vx,
                gates[:, i_col:i_col + 1], gates_t[i_col:i_col + 1, :],
                b_all[:, f_col:f_col + 1], b_all_t[f_col:f_col + 1, :],
                cx_ref.at[state], m_ref.at[state], incl, reverse)
            h_ref[:, head * ML_DH:(head + 1) * ML_DH] = h.astype(h_ref.dtype)


def _mlstm_mixer(qkvo, gates):
    gates_t = jnp.transpose(gates[:, :2 * SUBLANES])
    c = ML_CHUNK

    def specs(reverse):
        row = lambda ci: _ml_chunk_index(ci, reverse)
        return [
            pl.BlockSpec((c, 3 * ML_W), lambda ci: (row(ci), 0)),
            pl.BlockSpec((c, LANES), lambda ci: (row(ci), 0)),
            pl.BlockSpec((2 * SUBLANES, c), lambda ci: (0, row(ci))),
        ]

    out_spec = lambda reverse: pl.BlockSpec((c, ML_W), lambda ci: (_ml_chunk_index(ci, reverse), 0))
    n_state = 2 * ML_H
    return pl.pallas_call(
        _mlstm_body,
        grid=(ML_NCHUNK,),
        in_specs=specs(False) + specs(True),
        out_specs=[out_spec(False), out_spec(True)],
        out_shape=[jax.ShapeDtypeStruct((NTOK, ML_W), BF16)] * 2,
        scratch_shapes=[pltpu.VMEM((n_state, ML_DH, ML_DH + LANES), F32),
                        pltpu.VMEM((n_state, 1, LANES), F32)],
        compiler_params=_cparams(("arbitrary",)),
        name="mlstm_chunks",
    )(qkvo, gates, gates_t, qkvo, gates, gates_t)


def kernel(x, c, ctx, c_ctx, w_mod, b_mod, norm1_g, w_in, b_in, s5_lam_re, s5_lam_im, s5_log_step,
           s5_b_re, s5_b_im, s5_c_re, s5_c_im, s5_d, s5_w_glu, s5_b_glu, conv_dw_w, conv_dw_b,
           conv_ln_g, conv_ln_b, ml_norm_g, w_out, norm2_g, w_ffn_in, w_ffn_out, norm_f_g):
    assert x.shape == (1, SEQ, D_MODEL) and ctx.shape == (1, CTX, D_MODEL)
    xs = jnp.concatenate([ctx[0], x[0]], axis=0).astype(F32)
    cc = jnp.zeros((SUBLANES, D_MODEL), F32).at[0].set(c[0]).at[1].set(c_ctx)
    mod_all = _modulation(cc, w_mod, b_mod)
    s5_consts = _s5_selectors()
    s5_ops = _s5_operators(s5_lam_re, s5_lam_im, s5_log_step, s5_b_re, s5_b_im, s5_c_re, s5_c_im,
                           s5_consts[0])
    w_in_t = jnp.swapaxes(w_in, 1, 2)
    n_gate = w_in.shape[2] - OFF_G
    w_gate = jnp.pad(w_in_t[:, OFF_G:, :], ((0, 0), (0, LANES - n_gate), (0, 0)))
    w_gate_hi = w_gate.astype(BF16)
    w_gate = jnp.stack([w_gate_hi, (w_gate - w_gate_hi.astype(F32)).astype(BF16)], axis=1)
    b_gate = jnp.pad(b_in[:, OFF_G:], ((0, 0), (0, LANES - n_gate))).reshape(DEPTH, 1, LANES)
    tn = MM_TILE_N

    h, gates = _norm_mod(xs, norm1_g[0], mod_all, 0, 0, w_gate[0], b_gate[0])
    for l in range(DEPTH):
        z = _in_proj(h, w_in_t, b_in, l, 0, OFF_Q // tn, 1, F32)
        qkvo = _in_proj(h, w_in_t, b_in, l, OFF_Q // tn, (OFF_G - OFF_Q) // (2 * tn), 2, BF16)
        y_s5 = _s5_mixer(z, s5_ops, s5_consts, l, s5_d[l])
        cvo = _conv_module(z, conv_dw_w[l], conv_dw_b[l], conv_ln_g[l], conv_ln_b[l])
        h_f, h_b = _mlstm_mixer(qkvo, gates)
        xs, h2 = _out_proj(xs, y_s5, s5_w_glu[l], s5_b_glu[l], cvo, h_f, h_b, qkvo, ml_norm_g[l], w_out,
                           mod_all, norm2_g[l], l)
        hid = _ffn_in(h2, w_ffn_in, l)
        xs = _ffn_out(xs, hid, w_ffn_out, mod_all, l)
        if l + 1 < DEPTH:
            h, gates = _norm_mod(xs, norm1_g[l + 1], mod_all, l + 1, 0, w_gate[l + 1], b_gate[l + 1])
    return _final_norm(xs, norm_f_g)[None]
```

```python
import functools

import numpy as np
import jax
import jax.numpy as jnp
from jax import lax
from jax.experimental import pallas as pl
from jax.experimental.pallas import tpu as pltpu

F32 = jnp.float32
BF16 = jnp.bfloat16

D_MODEL = 2048
SEQ = 8192
CTX = 256
NTOK = SEQ + CTX
DEPTH = 4
GRID_W = 64

S5_W = 512
S5_CH = 16
S5_G = 32
S5_P = 64
CONV_W = 512
CONV_K = 31
ML_W = 1024
ML_H = 4
ML_DH = 256
D_FF = 5632
EPS = 1e-6
LN_EPS = 1e-5

OFF_CONV = 512
OFF_Q = 1536
OFF_K = 2560
OFF_V = 3584
OFF_O = 4608
OFF_G = 5632

LANES = 128
SUBLANES = 8
VMEM_LIMIT = 56 * 1024 * 1024

ROW_TILE = 256
MM_TILE_M = 1056
MM_TILE_N = 512
FFN_OUT_TILE_M = 704

S5_T = 8
S5_ROWS = NTOK // S5_T
S5_CTX_ROWS = CTX // S5_T
S5_BLK = S5_T * S5_CH
S5_STEP_G = 8
S5_STEP_PAIRS = S5_STEP_G // 2
S5_SET_CH = S5_STEP_G * S5_CH
S5_STEP_W = S5_STEP_G * S5_BLK
S5_FLAT = S5_G * S5_BLK

ML_CHUNK = 256
ML_NCHUNK = NTOK // ML_CHUNK


def _cparams(sem, vmem=VMEM_LIMIT):
    return pltpu.CompilerParams(dimension_semantics=sem, vmem_limit_bytes=vmem)


def _dot(a, b):
    return jnp.dot(a.astype(BF16), b.astype(BF16), preferred_element_type=F32)


def _split3(x):
    a = x.astype(BF16)
    r = x - a.astype(F32)
    b = r.astype(BF16)
    c = (r - b.astype(F32)).astype(BF16)
    return a, b, c


def _mod_body(cc_ref, w_ref, b_ref, o_ref):
    s = jax.nn.silu(cc_ref[...])
    o_ref[0] = _dot(s, w_ref[0]) + b_ref[0]


def _modulation(cc, w_mod, b_mod):
    depth, _, n = w_mod.shape
    tn = 1024
    return pl.pallas_call(
        _mod_body,
        grid=(depth, n // tn),
        in_specs=[
            pl.BlockSpec((SUBLANES, D_MODEL), lambda l, j: (0, 0)),
            pl.BlockSpec((1, D_MODEL, tn), lambda l, j: (l, 0, j)),
            pl.BlockSpec((1, 1, tn), lambda l, j: (l, 0, j)),
        ],
        out_specs=pl.BlockSpec((1, SUBLANES, tn), lambda l, j: (l, 0, j)),
        out_shape=jax.ShapeDtypeStruct((depth, SUBLANES, n), F32),
        compiler_params=_cparams(("arbitrary", "arbitrary")),
        name="adaln_modulation",
    )(cc, w_mod, b_mod.reshape(depth, 1, n))


def _mod_row(m_ref, is_ctx):
    return jnp.where(is_ctx, m_ref[0, 1:2, :], m_ref[0, 0:1, :])


def _stream_specs(xs, tm):
    if isinstance(xs, tuple):
        assert tm == CTX
        return [pl.BlockSpec((tm, D_MODEL), lambda i: (0, 0)),
                pl.BlockSpec((tm, D_MODEL), lambda i: (jnp.maximum(i - 1, 0), 0))], list(xs)
    return [pl.BlockSpec((tm, D_MODEL), lambda i: (i, 0))], [xs]


def _stream_tile(x_refs):
    if len(x_refs) == 1:
        return x_refs[0][...]
    return jnp.where(pl.program_id(0) == 0, x_refs[0][...], x_refs[1][...])


def _norm_body(*refs, modulate, with_gates, n_stream=1):
    it = iter(refs)
    x_refs = [next(it) for _ in range(n_stream)]
    g_ref = next(it)
    sh_ref = sc_ref = wg_ref = bg_ref = gate_ref = None
    if modulate:
        sh_ref, sc_ref = next(it), next(it)
    if with_gates:
        wg_ref, bg_ref = next(it), next(it)
    h_ref = next(it)
    if with_gates:
        gate_ref = next(it)

    xf = _stream_tile(x_refs)
    ms = jnp.mean(xf * xf, axis=-1, keepdims=True)
    h = xf * lax.rsqrt(ms + EPS) * g_ref[...]
    if modulate:
        is_ctx = pl.program_id(0) == 0
        h = h * (1.0 + _mod_row(sc_ref, is_ctx)) + _mod_row(sh_ref, is_ctx)
    h_ref[...] = h.astype(h_ref.dtype)
    if with_gates:
        hi = h.astype(BF16)
        lo = (h - hi.astype(F32)).astype(BF16)
        whi, wlo = wg_ref[0], wg_ref[1]
        nt = lambda a, b: lax.dot_general(a, b, (((1,), (1,)), ((), ())), preferred_element_type=F32)
        gate_ref[...] = nt(hi, whi) + nt(hi, wlo) + nt(lo, whi) + bg_ref[...]


def _norm_mod(xs, g, mod_all, layer, phase, w_gate=None, b_gate=None):
    with_gates = w_gate is not None
    n_tiles = NTOK // ROW_TILE
    x_specs, x_args = _stream_specs(xs, ROW_TILE)
    in_specs = x_specs + [
        pl.BlockSpec((1, D_MODEL), lambda i: (0, 0)),
        pl.BlockSpec((1, SUBLANES, D_MODEL), lambda i: (layer, 0, 3 * phase)),
        pl.BlockSpec((1, SUBLANES, D_MODEL), lambda i: (layer, 0, 3 * phase + 1)),
    ]
    args = x_args + [g.reshape(1, D_MODEL), mod_all, mod_all]
    out_specs = [pl.BlockSpec((ROW_TILE, D_MODEL), lambda i: (i, 0))]
    out_shape = [jax.ShapeDtypeStruct((NTOK, D_MODEL), BF16)]
    if with_gates:
        in_specs += [pl.BlockSpec((2, LANES, D_MODEL), lambda i: (0, 0, 0)),
                     pl.BlockSpec((1, LANES), lambda i: (0, 0))]
        args += [w_gate, b_gate]
        out_specs.append(pl.BlockSpec((ROW_TILE, LANES), lambda i: (i, 0)))
        out_shape.append(jax.ShapeDtypeStruct((NTOK, LANES), F32))
    return pl.pallas_call(
        functools.partial(_norm_body, modulate=True, with_gates=with_gates, n_stream=len(x_args)),
        grid=(n_tiles,),
        in_specs=in_specs,
        out_specs=out_specs,
        out_shape=out_shape,
        compiler_params=_cparams(("arbitrary",)),
        name="rmsnorm_modulate",
    )(*args)


def _final_norm(xs, g):
    skip = CTX // ROW_TILE
    return pl.pallas_call(
        functools.partial(_norm_body, modulate=False, with_gates=False),
        grid=(SEQ // ROW_TILE,),
        in_specs=[pl.BlockSpec((ROW_TILE, D_MODEL), lambda i: (i + skip, 0)),
                  pl.BlockSpec((1, D_MODEL), lambda i: (0, 0))],
        out_specs=[pl.BlockSpec((ROW_TILE, D_MODEL), lambda i: (i, 0))],
        out_shape=[jax.ShapeDtypeStruct((SEQ, D_MODEL), F32)],
        compiler_params=_cparams(("arbitrary",)),
        name="final_rmsnorm",
    )(xs, g.reshape(1, D_MODEL))[0]


def _in_proj_body(*refs, n_w):
    a_ref, wt_refs, b_refs = refs[0], refs[1:1 + n_w], refs[1 + n_w:1 + 2 * n_w]
    o_ref, wbf_ref = refs[1 + 2 * n_w:]
    tn = MM_TILE_N

    @pl.when(pl.program_id(1) == 0)
    def _():
        for k, wt_ref in enumerate(wt_refs):
            wbf_ref[:, k * tn:(k + 1) * tn] = jnp.transpose(wt_ref[...]).astype(BF16)

    bias = jnp.concatenate([b_ref[...] for b_ref in b_refs], axis=1)
    acc = jnp.dot(a_ref[...], wbf_ref[...], preferred_element_type=F32) + bias
    o_ref[...] = acc.astype(o_ref.dtype)


def _in_proj(h, w_in_t, b_in, layer, first_tile, n_steps, n_w, dtype):
    tm, tn = MM_TILE_M, MM_TILE_N
    in_w = w_in_t.shape[1]
    tile = lambda j, k: first_tile + n_w * j + k
    w_specs = [pl.BlockSpec((None, tn, D_MODEL), functools.partial(lambda j, i, k: (layer, tile(j, k), 0), k=k))
               for k in range(n_w)]
    b_specs = [pl.BlockSpec((None, 1, tn), functools.partial(lambda j, i, k: (layer, 0, tile(j, k)), k=k))
               for k in range(n_w)]
    b3 = b_in.reshape(DEPTH, 1, in_w)
    return pl.pallas_call(
        functools.partial(_in_proj_body, n_w=n_w),
        grid=(n_steps, NTOK // tm),
        in_specs=[pl.BlockSpec((tm, D_MODEL), lambda j, i: (i, 0))] + w_specs + b_specs,
        out_specs=pl.BlockSpec((tm, n_w * tn), lambda j, i: (i, j)),
        out_shape=jax.ShapeDtypeStruct((NTOK, n_steps * n_w * tn), dtype),
        scratch_shapes=[pltpu.VMEM((D_MODEL, n_w * tn), BF16)],
        compiler_params=_cparams(("arbitrary", "arbitrary")),
        name="in_proj",
    )(h, *([w_in_t] * n_w), *([b3] * n_w))


def _row_gate(g_ref, i, tm, tn):
    rows = i * tm + lax.broadcasted_iota(jnp.int32, (tm, tn), 0)
    return jnp.where(rows < CTX, g_ref[0, 1:2, :], g_ref[0, 0:1, :])


def _out_proj_body(y_ref, wglu_ref, bglu_ref, cv_ref, hf_ref, hb_ref, o_ref, mlg_ref, w_ref,
                   gate_ref, sh_ref, sc_ref, ng_ref, *rest):
    x_refs, (xo_ref, h_ref, wbf_ref, wglu_bf_ref) = rest[:-4], rest[-4:]
    i = pl.program_id(0)

    @pl.when(i == 0)
    def _():
        wbf_ref[...] = w_ref[...].astype(BF16)
        wglu_bf_ref[...] = wglu_ref[...].astype(BF16)

    g = jax.nn.gelu(y_ref[...])
    glu = jnp.dot(g.astype(BF16), wglu_bf_ref[...], preferred_element_type=F32) + bglu_ref[...]
    s5 = (g * jax.nn.sigmoid(glu)).astype(BF16)
    acc = jnp.dot(s5, wbf_ref[0:S5_W, :], preferred_element_type=F32)
    acc += jnp.dot(cv_ref[...], wbf_ref[S5_W:S5_W + CONV_W, :], preferred_element_type=F32)
    for head in range(ML_H):
        cols = slice(head * ML_DH, (head + 1) * ML_DH)
        hh = hf_ref[:, cols].astype(F32) + hb_ref[:, cols].astype(F32)
        hc = hh - jnp.mean(hh, axis=-1, keepdims=True)
        var = jnp.mean(hc * hc, axis=-1, keepdims=True)
        ml = jax.nn.sigmoid(o_ref[:, cols].astype(F32)) * (hc * lax.rsqrt(var + LN_EPS) * mlg_ref[:, cols])
        row0 = S5_W + CONV_W + head * ML_DH
        acc += jnp.dot(ml.astype(BF16), wbf_ref[row0:row0 + ML_DH, :], preferred_element_type=F32)
    is_ctx = i == 0
    xn = _stream_tile(x_refs) + _mod_row(gate_ref, is_ctx) * acc
    xo_ref[...] = xn
    ms = jnp.mean(xn * xn, axis=-1, keepdims=True)
    h = xn * lax.rsqrt(ms + EPS) * ng_ref[...]
    h_ref[...] = (h * (1.0 + _mod_row(sc_ref, is_ctx)) + _mod_row(sh_ref, is_ctx)).astype(BF16)


def _out_proj(xs, y_s5, w_glu, b_glu, cvo, h_f, h_b, qkvo, ml_norm_g, w_out, mod_all, norm_g, layer):
    tm = ROW_TILE
    mod = lambda k: pl.BlockSpec((1, SUBLANES, D_MODEL), lambda i: (layer, 0, k))
    rows = lambda width, col=0: pl.BlockSpec((tm, width), lambda i: (i, col))
    const = lambda shape: pl.BlockSpec(shape, lambda i: (0,) * len(shape))
    x_specs, x_args = _stream_specs(xs, tm)
    return pl.pallas_call(
        _out_proj_body,
        grid=(NTOK // tm,),
        in_specs=[
            rows(S5_W), const((S5_W, S5_W)), const((1, S5_W)),
            rows(CONV_W),
            rows(ML_W), rows(ML_W), rows(ML_W, 3), const((1, ML_W)),
            pl.BlockSpec((None, D_MODEL, D_MODEL), lambda i: (layer, 0, 0), pipeline_mode=pl.Buffered(1)),
            mod(2), mod(3), mod(4),
            const((1, D_MODEL)),
        ] + x_specs,
        out_specs=[rows(D_MODEL), rows(D_MODEL)],
        out_shape=[jax.ShapeDtypeStruct((NTOK, D_MODEL), F32), jax.ShapeDtypeStruct((NTOK, D_MODEL), BF16)],
        scratch_shapes=[pltpu.VMEM((D_MODEL, D_MODEL), BF16), pltpu.VMEM((S5_W, S5_W), BF16)],
        compiler_params=_cparams(("arbitrary",)),
        name="out_proj_residual",
    )(y_s5, w_glu, b_glu.reshape(1, S5_W), cvo, h_f, h_b, qkvo, ml_norm_g.reshape(1, ML_W), w_out,
      mod_all, mod_all, mod_all, norm_g.reshape(1, D_MODEL), *x_args)


def _ffn_in_body(a_ref, wg_ref, wu_ref, o_ref, wgbf_ref, wubf_ref):
    @pl.when(pl.program_id(1) == 0)
    def _():
        wgbf_ref[...] = wg_ref[...].astype(BF16)
        wubf_ref[...] = wu_ref[...].astype(BF16)

    a = a_ref[...]
    g = jnp.dot(a, wgbf_ref[...], preferred_element_type=F32)
    u = jnp.dot(a, wubf_ref[...], preferred_element_type=F32)
    o_ref[...] = (jax.nn.silu(g) * u).astype(BF16)


def _ffn_in(h, w_ffn_in, layer):
    tm, tn = MM_TILE_M, MM_TILE_N
    nj = D_FF // tn
    return pl.pallas_call(
        _ffn_in_body,
        grid=(nj, NTOK // tm),
        in_specs=[
            pl.BlockSpec((tm, D_MODEL), lambda j, i: (i, 0)),
            pl.BlockSpec((None, D_MODEL, tn), lambda j, i: (layer, 0, j)),
            pl.BlockSpec((None, D_MODEL, tn), lambda j, i: (layer, 0, nj + j)),
        ],
        out_specs=pl.BlockSpec((tm, tn), lambda j, i: (i, j)),
        out_shape=jax.ShapeDtypeStruct((NTOK, D_FF), BF16),
        scratch_shapes=[pltpu.VMEM((D_MODEL, tn), BF16), pltpu.VMEM((D_MODEL, tn), BF16)],
        compiler_params=_cparams(("arbitrary", "arbitrary")),
        name="ffn_in_swiglu",
    )(h, w_ffn_in, w_ffn_in)


def _ffn_out_body(a_ref, w_ref, g_ref, x_ref, o_ref, wbf_ref):
    i = pl.program_id(1)

    @pl.when(i == 0)
    def _():
        wbf_ref[...] = w_ref[...].astype(BF16)

    acc = jnp.dot(a_ref[...], wbf_ref[...], preferred_element_type=F32)
    tm, tn = o_ref.shape
    o_ref[...] = x_ref[...] + _row_gate(g_ref, i, tm, tn) * acc


def _ffn_out(xs, hid, w_ffn_out, mod_all, layer):
    tm, tn = FFN_OUT_TILE_M, MM_TILE_N
    return pl.pallas_call(
        _ffn_out_body,
        grid=(D_MODEL // tn, NTOK // tm),
        in_specs=[
            pl.BlockSpec((tm, D_FF), lambda j, i: (i, 0)),
            pl.BlockSpec((None, D_FF, tn), lambda j, i: (layer, 0, j)),
            pl.BlockSpec((1, SUBLANES, tn), lambda j, i: (layer, 0, 5 * (D_MODEL // tn) + j)),
            pl.BlockSpec((tm, tn), lambda j, i: (i, j)),
        ],
        out_specs=pl.BlockSpec((tm, tn), lambda j, i: (i, j)),
        out_shape=jax.ShapeDtypeStruct((NTOK, D_MODEL), F32),
        scratch_shapes=[pltpu.VMEM((D_FF, tn), BF16)],
        compiler_params=_cparams(("arbitrary", "arbitrary")),
        name="ffn_out_residual",
    )(hid, w_ffn_out, mod_all, xs)


def _s5_selectors():
    tau = np.arange(S5_BLK)[:, None] // S5_CH
    ch_r = np.arange(S5_BLK)[:, None] % S5_CH
    t = np.arange(S5_BLK)[None, :] // S5_CH
    ch_c = np.arange(S5_BLK)[None, :] % S5_CH
    sel = np.zeros((2, S5_T, S5_BLK, S5_BLK), np.float32)
    for s in range(S5_T):
        sel[0, s] = (tau == t - s) & (ch_r == ch_c)
        sel[1, s] = (tau == s - t) & (ch_r == ch_c)
    rp = np.arange(S5_BLK)[:, None]
    col = np.arange(2 * S5_BLK)[None, :]
    pair_map = np.stack([(col == (rp // S5_P) * S5_BLK + gi * S5_P + rp % S5_P) for gi in range(2)])
    src = np.arange(S5_STEP_W)
    dst = ((src // S5_CH) % S5_STEP_G) * S5_BLK + (src // S5_SET_CH) * S5_CH + src % S5_CH
    perm = dst[:, None] == np.arange(S5_STEP_W)[None, :]
    return (jnp.asarray(sel), jnp.asarray(pair_map, BF16), jnp.asarray(perm, BF16),
            jnp.asarray(perm.T, BF16))


def _s5_operators(lam_re, lam_im, log_step, b_re, b_im, c_re, c_im, sel):
    lam = lax.complex(lam_re.astype(F32), lam_im.astype(F32))
    lam_bar = jnp.exp(lam * jnp.exp(log_step.astype(F32)))
    b_bar = ((lam_bar - 1.0) / lam)[..., None] * lax.complex(b_re.astype(F32), b_im.astype(F32))
    c_mat = lax.complex(c_re.astype(F32), c_im.astype(F32))
    depth = lam.shape[0]

    def powers(base, count):
        out = [jnp.ones_like(base)]
        for _ in range(count - 1):
            out.append(out[-1] * base)
        return out

    pw = powers(lam_bar, S5_T + 1)
    pa = powers(pw[S5_T], SUBLANES + 1)
    t_up = list(range(S5_T))

    def table(seq, fwd_idx, bwd_idx):
        return jnp.stack([jnp.stack([seq[f][:, 0], seq[b][:, 1]], axis=1)
                          for f, b in zip(fwd_idx, bwd_idx)])

    blocks = (depth, 2, S5_G, S5_BLK, S5_BLK)

    def token_rows(w, imag_sign):
        w = jnp.moveaxis(w, 0, 3)
        return jnp.concatenate([jnp.real(w), imag_sign * jnp.imag(w)], axis=-1).reshape(blocks)

    pb = table(pw, [S5_T - 1 - t for t in t_up], t_up)
    wb = token_rows(pb[:, :, :, :, None, :] * jnp.swapaxes(b_bar, -1, -2)[None], 1.0)
    pc = table(pw, [t + 1 for t in t_up], [S5_T - t for t in t_up])
    wct = token_rows(pc[:, :, :, :, None, :] * c_mat[None], -1.0)

    kern = jnp.real(jnp.einsum('ldgcp,kldgp,ldgpa->ldgakc', c_mat, jnp.stack(pw[:S5_T]), b_bar))
    kern = kern.reshape(depth, 2, S5_G * S5_CH, S5_BLK)
    kc = jnp.einsum('ldxk,dskn->ldsxn', kern, sel, precision=lax.Precision.HIGHEST)
    kc = kc.reshape(depth, 2, S5_T, S5_G, S5_CH, S5_BLK)
    kc = jnp.transpose(kc, (0, 1, 3, 2, 4, 5)).reshape(blocks)

    def lanes(v):
        parts = jnp.stack([jnp.real(v), jnp.imag(v)], axis=4)
        parts = parts.reshape(v.shape[0], depth, 2, S5_G // 2, 2, 2, S5_P)
        parts = jnp.swapaxes(parts, 4, 5).reshape(v.shape[0], depth, 2, S5_FLAT)
        return jnp.transpose(parts, (1, 2, 0, 3))

    zero_p = jnp.zeros_like(pa[0])
    amul = lanes(jnp.stack([pa[1], pa[2], pa[4], pa[8]] + [zero_p] * 4))
    apow = lanes(table(pa, t_up, t_up[::-1]))
    return kc.astype(BF16), wb.astype(BF16), wct.astype(BF16), amul, apow


def _s5_row_scan(e_ref, amul_ref, apow_ref, block_lo, block_hi, reverse, carry):
    rows = lax.broadcasted_iota(jnp.int32, (SUBLANES, S5_BLK), 0)

    def shifted(x, k):
        if reverse:
            return jnp.where(rows < SUBLANES - k, pltpu.roll(x, SUBLANES - k, 0), 0.0)
        return jnp.where(rows >= k, pltpu.roll(x, k, 0), 0.0)

    def body(step, carry):
        blk = (block_hi - 1 - step) if reverse else (block_lo + step)
        r0 = pl.multiple_of(blk * SUBLANES, SUBLANES)
        last = 0 if reverse else SUBLANES - 1
        carry_out = []
        for q in range(S5_STEP_PAIRS):
            re_c = slice(2 * q * S5_BLK, (2 * q + 1) * S5_BLK)
            im_c = slice((2 * q + 1) * S5_BLK, (2 * q + 2) * S5_BLK)
            s_re = e_ref[pl.ds(r0, SUBLANES), re_c]
            s_im = e_ref[pl.ds(r0, SUBLANES), im_c]
            for idx, k in enumerate((1, 2, 4)):
                a_re, a_im = amul_ref[idx:idx + 1, re_c], amul_ref[idx:idx + 1, im_c]
                t_re, t_im = shifted(s_re, k), shifted(s_im, k)
                s_re, s_im = s_re + t_re * a_re - t_im * a_im, s_im + t_re * a_im + t_im * a_re
            c_re, c_im = carry[:, re_c], carry[:, im_c]
            p_re, p_im = apow_ref[:, re_c], apow_ref[:, im_c]
            e_ref[pl.ds(r0, SUBLANES), re_c] = shifted(s_re, 1) + p_re * c_re - p_im * c_im
            e_ref[pl.ds(r0, SUBLANES), im_c] = shifted(s_im, 1) + p_re * c_im + p_im * c_re
            a_re, a_im = amul_ref[3:4, re_c], amul_ref[3:4, im_c]
            l_re = jnp.broadcast_to(s_re[last:last + 1, :], (SUBLANES, S5_BLK))
            l_im = jnp.broadcast_to(s_im[last:last + 1, :], (SUBLANES, S5_BLK))
            carry_out += [l_re + a_re * c_re - a_im * c_im, l_im + a_re * c_im + a_im * c_re]
        return jnp.concatenate(carry_out, axis=1)

    return lax.fori_loop(0, block_hi - block_lo, body, carry)


def _s5_pair_operators(kc_ref, wb_ref, wct_ref, map_ref, q):
    g0, g1 = 2 * q, 2 * q + 1
    zero = jnp.zeros((S5_BLK, S5_BLK), BF16)
    kc = jnp.concatenate([jnp.concatenate([kc_ref[g0], zero], axis=1),
                          jnp.concatenate([zero, kc_ref[g1]], axis=1)], axis=0)
    spread = lambda ref: jnp.concatenate(
        [jnp.dot(ref[g0], map_ref[0], preferred_element_type=F32),
         jnp.dot(ref[g1], map_ref[1], preferred_element_type=F32)], axis=0).astype(BF16)
    return kc, spread(wb_ref), spread(wct_ref)


def _s5_body(u_ref, p_ref, pt_ref, map_ref, kc_ref, wb_ref, wct_ref, amul_ref, apow_ref, d_ref, y_ref,
             up_ref, e_ref, acc_ref):
    direction = pl.program_id(1)
    token_rows = lambda t: pl.ds(t, S5_ROWS, stride=S5_T)
    pairs = [slice(2 * q * S5_BLK, 2 * (q + 1) * S5_BLK) for q in range(S5_STEP_PAIRS)]

    @pl.when(direction == 0)
    def _():
        u_nat = jnp.concatenate([u_ref[token_rows(t), :] for t in range(S5_T)], axis=1)
        up_ref[...] = jnp.dot(u_nat.astype(BF16), p_ref[...], preferred_element_type=F32).astype(BF16)
        acc_ref[...] = jnp.zeros_like(acc_ref)

    ops = [_s5_pair_operators(kc_ref, wb_ref, wct_ref, map_ref, q) for q in range(S5_STEP_PAIRS)]
    for (_, wb, _), cols in zip(ops, pairs):
        e_ref[:, cols] = jnp.dot(up_ref[:, cols], wb, preferred_element_type=F32)

    zero = jnp.zeros((SUBLANES, S5_STEP_W), F32)
    n_blocks = S5_ROWS // SUBLANES
    ctx_blocks = S5_CTX_ROWS // SUBLANES
    amul, apow = amul_ref, apow_ref

    @pl.when(direction == 0)
    def _():
        _s5_row_scan(e_ref, amul, apow, 0, n_blocks, False, zero)

    @pl.when(direction == 1)
    def _():
        carry = _s5_row_scan(e_ref, amul, apow, 0, ctx_blocks, True, zero)
        _s5_row_scan(e_ref, amul, apow, ctx_blocks, n_blocks, True, carry)

    for (kc, _, wct), cols in zip(ops, pairs):
        y = jnp.dot(up_ref[:, cols], kc, preferred_element_type=F32)
        y += lax.dot_general(e_ref[:, cols].astype(BF16), wct, (((1,), (1,)), ((), ())),
                             preferred_element_type=F32)
        acc_ref[:, cols] += y

    @pl.when(direction == 1)
    def _():
        acc = acc_ref[...]
        hi = acc.astype(BF16)
        lo = (acc - hi.astype(F32)).astype(BF16)
        y_nat = (jnp.dot(hi, pt_ref[...], preferred_element_type=F32)
                 + jnp.dot(lo, pt_ref[...], preferred_element_type=F32))
        for t in range(S5_T):
            y_ref[token_rows(t), :] = (y_nat[:, t * S5_SET_CH:(t + 1) * S5_SET_CH]
                                       + d_ref[...] * u_ref[token_rows(t), :])


def _s5_mixer(z, ops, consts, layer, d_skip):
    kc, wb, wct, amul, apow = ops
    _, pair_map, perm, perm_t = consts
    n_steps = S5_G // S5_STEP_G
    wspec = pl.BlockSpec((None, None, S5_STEP_G, S5_BLK, S5_BLK), lambda j, d: (layer, d, j, 0, 0))
    cspec = pl.BlockSpec((None, None, SUBLANES, S5_STEP_W), lambda j, d: (layer, d, 0, j))
    pspec = pl.BlockSpec((S5_STEP_W, S5_STEP_W), lambda j, d: (0, 0))
    return pl.pallas_call(
        _s5_body,
        grid=(n_steps, 2),
        in_specs=[
            pl.BlockSpec((NTOK, S5_SET_CH), lambda j, d: (0, j)),
            pspec, pspec,
            pl.BlockSpec((2, S5_BLK, 2 * S5_BLK), lambda j, d: (0, 0, 0)),
            wspec, wspec, wspec, cspec, cspec,
            pl.BlockSpec((1, S5_SET_CH), lambda j, d: (0, j)),
        ],
        out_specs=pl.BlockSpec((NTOK, S5_SET_CH), lambda j, d: (0, j)),
        out_shape=jax.ShapeDtypeStruct((NTOK, S5_W), F32),
        scratch_shapes=[pltpu.VMEM((S5_ROWS, S5_STEP_W), BF16), pltpu.VMEM((S5_ROWS, S5_STEP_W), F32),
                        pltpu.VMEM((S5_ROWS, S5_STEP_W), F32)],
        compiler_params=_cparams(("arbitrary", "arbitrary")),
        name="s5_scan",
    )(z, perm, perm_t, pair_map, kc, wb, wct, amul, apow, d_skip.astype(F32).reshape(1, S5_W))


CONV_PAD = 16
CONV_BLK = 64


def _conv_body(a_ref, b_ref, w_ref, db_ref, lg_ref, lb_ref, o_ref, pad_ref, sh_ref):
    g = a_ref[...] * jax.nn.sigmoid(b_ref[...])

    def run(seq_len):
        n_seq = ROW_TILE // seq_len
        pitch = seq_len + 2 * CONV_PAD
        zeros = jnp.zeros((CONV_PAD, CONV_W), F32)
        for s in range(n_seq):
            pad_ref[s * pitch:s * pitch + CONV_PAD, :] = zeros
            pad_ref[s * pitch + CONV_PAD:s * pitch + CONV_PAD + seq_len, :] = g[s * seq_len:(s + 1) * seq_len]
            pad_ref[s * pitch + CONV_PAD + seq_len:(s + 1) * pitch, :] = zeros
        used = n_seq * pitch
        for r in range(1, SUBLANES):
            sh_ref[r, 0:used - SUBLANES, :] = pad_ref[r:r + used - SUBLANES, :]
        for blk in range(ROW_TILE // CONV_BLK):
            row0 = blk * CONV_BLK
            s, q = divmod(row0, seq_len)
            base = s * pitch + CONV_PAD + q - CONV_K // 2
            acc = jnp.zeros((CONV_BLK, CONV_W), F32)
            for k in range(CONV_K):
                r = (base + k) % SUBLANES
                row = base + k - r
                tap = pad_ref[row:row + CONV_BLK, :] if r == 0 else sh_ref[r, row:row + CONV_BLK, :]
                acc = acc + w_ref[k:k + 1, :] * tap
            y = acc + db_ref[...]
            yc = y - jnp.mean(y, axis=-1, keepdims=True)
            var = jnp.mean(yc * yc, axis=-1, keepdims=True)
            y = yc * lax.rsqrt(var + LN_EPS) * lg_ref[...] + lb_ref[...]
            o_ref[row0:row0 + CONV_BLK, :] = jax.nn.silu(y).astype(BF16)

    is_ctx = pl.program_id(0) == 0

    @pl.when(is_ctx)
    def _():
        run(CTX)

    @pl.when(jnp.logical_not(is_ctx))
    def _():
        run(GRID_W)


def _conv_module(z, dw_w, dw_b, ln_g, ln_b):
    w = jnp.concatenate([dw_w, jnp.zeros((1, CONV_W), dw_w.dtype)], axis=0)
    vec = lambda v: v.reshape(1, CONV_W)
    pad_rows = max(CTX + 2 * CONV_PAD, (ROW_TILE // GRID_W) * (GRID_W + 2 * CONV_PAD))
    return pl.pallas_call(
        _conv_body,
        grid=(NTOK // ROW_TILE,),
        in_specs=[
            pl.BlockSpec((ROW_TILE, CONV_W), lambda i: (i, OFF_CONV // CONV_W)),
            pl.BlockSpec((ROW_TILE, CONV_W), lambda i: (i, OFF_CONV // CONV_W + 1)),
            pl.BlockSpec((CONV_K + 1, CONV_W), lambda i: (0, 0)),
            pl.BlockSpec((1, CONV_W), lambda i: (0, 0)),
            pl.BlockSpec((1, CONV_W), lambda i: (0, 0)),
            pl.BlockSpec((1, CONV_W), lambda i: (0, 0)),
        ],
        out_specs=pl.BlockSpec((ROW_TILE, CONV_W), lambda i: (i, 0)),
        out_shape=jax.ShapeDtypeStruct((NTOK, CONV_W), BF16),
        scratch_shapes=[pltpu.VMEM((pad_rows, CONV_W), F32), pltpu.VMEM((SUBLANES, pad_rows, CONV_W), F32)],
        compiler_params=_cparams(("arbitrary",)),
        name="conv_module",
    )(z, z, w, vec(dw_b), vec(ln_g), vec(ln_b))


def _ml_chunk_index(ci, reverse):
    if not reverse:
        return ci
    return jnp.where(ci == 0, 0, ML_NCHUNK - ci)


def _ml_direction(qb, kb, vx, li_col, li_row, b_col, b_row, cx_ref, m_ref, mask, reverse):
    c = ML_CHUNK
    scale = ML_DH ** 0.5
    m = m_ref[:, 0:1]
    cx = cx_ref[...]

    d_log = jnp.where(mask, b_col + (li_row - b_row), -jnp.inf)
    inter = b_col + m
    m_row = jnp.maximum(inter, jnp.max(d_log, axis=-1, keepdims=True))
    qk = lax.dot_general(qb, kb, (((1,), (1,)), ((), ())), preferred_element_type=F32)
    s = qk * jnp.exp(d_log - m_row)
    w_inter = jnp.exp(inter - m_row) * scale
    lhs = jnp.concatenate([s.astype(BF16), qb * w_inter.astype(BF16)], axis=1)
    rhs = jnp.concatenate([vx, cx.astype(BF16)], axis=0)
    num = jnp.dot(lhs, rhs, preferred_element_type=F32)
    den = num[:, ML_DH:ML_DH + 1]
    h = num[:, :ML_DH] / jnp.maximum(jnp.abs(den), jnp.exp(-m_row) * scale)

    b_tot = b_row[:, 0:1] if reverse else b_row[:, c - 1:c]
    g = b_tot - b_col + li_col
    m_new = jnp.maximum(b_tot + m, jnp.max(g, axis=0, keepdims=True))
    kw = kb * (jnp.exp(g - m_new) * (1.0 / scale)).astype(BF16)
    decay = jnp.exp(b_tot + m - m_new)
    cx_ref[...] = decay * cx + lax.dot_general(kw, vx, (((0,), (0,)), ((), ())),
                                               preferred_element_type=F32)
    m_ref[...] = jnp.broadcast_to(m_new, m_ref.shape)
    return h


def _mlstm_body(qkvf_ref, gf_ref, gtf_ref, qkvb_ref, gb_ref, gtb_ref, hf_ref, hb_ref, cx_ref, m_ref):
    @pl.when(pl.program_id(0) == 0)
    def _():
        cx_ref[...] = jnp.zeros_like(cx_ref)
        m_ref[...] = jnp.zeros_like(m_ref)

    c = ML_CHUNK
    r_idx = lax.broadcasted_iota(jnp.int32, (c, c), 0)
    c_idx = lax.broadcasted_iota(jnp.int32, (c, c), 1)
    ones_col = jnp.where(lax.broadcasted_iota(jnp.int32, (c, LANES), 1) == 0, 1.0, 0.0).astype(BF16)

    for d, (qkv_ref, g_ref, gt_ref, h_ref) in enumerate(
            ((qkvf_ref, gf_ref, gtf_ref, hf_ref), (qkvb_ref, gb_ref, gtb_ref, hb_ref))):
        reverse = d == 1
        gates = g_ref[...]
        gates_t = gt_ref[...]
        incl = (c_idx >= r_idx) if reverse else (c_idx <= r_idx)
        tri = jnp.where(incl, 1.0, 0.0).astype(BF16)
        incl_t = (r_idx >= c_idx) if reverse else (r_idx <= c_idx)
        tri_t = jnp.where(incl_t, 1.0, 0.0).astype(BF16)
        lf = jax.nn.log_sigmoid(gates)
        lf_t = jax.nn.log_sigmoid(gates_t)
        b_all = sum(jnp.dot(tri, p, preferred_element_type=F32) for p in _split3(lf))
        b_all_t = sum(jnp.dot(p, tri_t, preferred_element_type=F32) for p in _split3(lf_t))
        for head in range(ML_H):
            i_col = 2 * ML_H * d + head
            f_col = i_col + ML_H
            state = d * ML_H + head
            col = lambda part: slice((part * ML_H + head) * ML_DH, (part * ML_H + head + 1) * ML_DH)
            vx = jnp.concatenate([qkv_ref[:, col(2)], ones_col], axis=1)
            h = _ml_direction(
                qkv_ref[:, col(0)], qkv_ref[:, col(1)], vx,
                gates[:, i_col:i_col + 1], gates_t[i_col:i_col + 1, :],
                b_all[:, f_col:f_col + 1], b_all_t[f_col:f_col + 1, :],
                cx_ref.at[state], m_ref.at[state], incl, reverse)
            h_ref[:, head * ML_DH:(head + 1) * ML_DH] = h.astype(h_ref.dtype)


def _mlstm_mixer(qkvo, gates):
    gates_t = jnp.transpose(gates[:, :2 * SUBLANES])
    c = ML_CHUNK

    def specs(reverse):
        row = lambda ci: _ml_chunk_index(ci, reverse)
        return [
            pl.BlockSpec((c, 3 * ML_W), lambda ci: (row(ci), 0)),
            pl.BlockSpec((c, LANES), lambda ci: (row(ci), 0)),
            pl.BlockSpec((2 * SUBLANES, c), lambda ci: (0, row(ci))),
        ]

    out_spec = lambda reverse: pl.BlockSpec((c, ML_W), lambda ci: (_ml_chunk_index(ci, reverse), 0))
    n_state = 2 * ML_H
    return pl.pallas_call(
        _mlstm_body,
        grid=(ML_NCHUNK,),
        in_specs=specs(False) + specs(True),
        out_specs=[out_spec(False), out_spec(True)],
        out_shape=[jax.ShapeDtypeStruct((NTOK, ML_W), BF16)] * 2,
        scratch_shapes=[pltpu.VMEM((n_state, ML_DH, ML_DH + LANES), F32),
                        pltpu.VMEM((n_state, 1, LANES), F32)],
        compiler_params=_cparams(("arbitrary",)),
        name="mlstm_chunks",
    )(qkvo, gates, gates_t, qkvo, gates, gates_t)


def kernel(x, c, ctx, c_ctx, w_mod, b_mod, norm1_g, w_in, b_in, s5_lam_re, s5_lam_im, s5_log_step,
           s5_b_re, s5_b_im, s5_c_re, s5_c_im, s5_d, s5_w_glu, s5_b_glu, conv_dw_w, conv_dw_b,
           conv_ln_g, conv_ln_b, ml_norm_g, w_out, norm2_g, w_ffn_in, w_ffn_out, norm_f_g):
    assert x.shape == (1, SEQ, D_MODEL) and ctx.shape == (1, CTX, D_MODEL)
    xs = (ctx[0].astype(F32), x[0].astype(F32))
    cc = jnp.zeros((SUBLANES, D_MODEL), F32).at[0].set(c[0]).at[1].set(c_ctx)
    mod_all = _modulation(cc, w_mod, b_mod)
    s5_consts = _s5_selectors()
    s5_ops = _s5_operators(s5_lam_re, s5_lam_im, s5_log_step, s5_b_re, s5_b_im, s5_c_re, s5_c_im,
                           s5_consts[0])
    w_in_t = jnp.swapaxes(w_in, 1, 2)
    n_gate = w_in.shape[2] - OFF_G
    w_gate = jnp.pad(w_in_t[:, OFF_G:, :], ((0, 0), (0, LANES - n_gate), (0, 0)))
    w_gate_hi = w_gate.astype(BF16)
    w_gate = jnp.stack([w_gate_hi, (w_gate - w_gate_hi.astype(F32)).astype(BF16)], axis=1)
    b_gate = jnp.pad(b_in[:, OFF_G:], ((0, 0), (0, LANES - n_gate))).reshape(DEPTH, 1, LANES)
    tn = MM_TILE_N

    h, gates = _norm_mod(xs, norm1_g[0], mod_all, 0, 0, w_gate[0], b_gate[0])
    for l in range(DEPTH):
        z = _in_proj(h, w_in_t, b_in, l, 0, OFF_Q // tn, 1, F32)
        qkvo = _in_proj(h, w_in_t, b_in, l, OFF_Q // tn, (OFF_G - OFF_Q) // (2 * tn), 2, BF16)
        y_s5 = _s5_mixer(z, s5_ops, s5_consts, l, s5_d[l])
        cvo = _conv_module(z, conv_dw_w[l], conv_dw_b[l], conv_ln_g[l], conv_ln_b[l])
        h_f, h_b = _mlstm_mixer(qkvo, gates)
        xs, h2 = _out_proj(xs, y_s5, s5_w_glu[l], s5_b_glu[l], cvo, h_f, h_b, qkvo, ml_norm_g[l], w_out,
                           mod_all, norm2_g[l], l)
        hid = _ffn_in(h2, w_ffn_in, l)
        xs = _ffn_out(xs, hid, w_ffn_out, mod_all, l)
        if l + 1 < DEPTH:
            h, gates = _norm_mod(xs, norm1_g[l + 1], mod_all, l + 1, 0, w_gate[l + 1], b_gate[l + 1])
    return _final_norm(xs, norm_f_g)[None]
```

```python
import functools

import numpy as np
import jax
import jax.numpy as jnp
from jax import lax
from jax.experimental import pallas as pl
from jax.experimental.pallas import tpu as pltpu

F32 = jnp.float32
BF16 = jnp.bfloat16

D_MODEL = 2048
SEQ = 8192
CTX = 256
NTOK = SEQ + CTX
DEPTH = 4
GRID_W = 64

S5_W = 512
S5_CH = 16
S5_G = 32
S5_P = 64
CONV_W = 512
CONV_K = 31
ML_W = 1024
ML_H = 4
ML_DH = 256
D_FF = 5632
EPS = 1e-6
LN_EPS = 1e-5

OFF_CONV = 512
OFF_Q = 1536
OFF_K = 2560
OFF_V = 3584
OFF_O = 4608
OFF_G = 5632

LANES = 128
SUBLANES = 8
VMEM_LIMIT = 56 * 1024 * 1024

ROW_TILE = 256
MM_TILE_M = 1056
MM_TILE_N = 512
FFN_OUT_TILE_M = 704

S5_T = 8
S5_ROWS = NTOK // S5_T
S5_CTX_ROWS = CTX // S5_T
S5_BLK = S5_T * S5_CH
S5_STEP_G = 8
S5_STEP_PAIRS = S5_STEP_G // 2
S5_SET_CH = S5_STEP_G * S5_CH
S5_STEP_W = S5_STEP_G * S5_BLK
S5_FLAT = S5_G * S5_BLK

ML_CHUNK = 256
ML_NCHUNK = NTOK // ML_CHUNK


def _cparams(sem, vmem=VMEM_LIMIT):
    return pltpu.CompilerParams(dimension_semantics=sem, vmem_limit_bytes=vmem)


def _dot(a, b):
    return jnp.dot(a.astype(BF16), b.astype(BF16), preferred_element_type=F32)


def _split3(x):
    a = x.astype(BF16)
    r = x - a.astype(F32)
    b = r.astype(BF16)
    c = (r - b.astype(F32)).astype(BF16)
    return a, b, c


def _mod_body(cc_ref, w_ref, b_ref, o_ref):
    s = jax.nn.silu(cc_ref[...])
    o_ref[0] = _dot(s, w_ref[0]) + b_ref[0]


def _modulation(cc, w_mod, b_mod):
    depth, _, n = w_mod.shape
    tn = 1024
    return pl.pallas_call(
        _mod_body,
        grid=(depth, n // tn),
        in_specs=[
            pl.BlockSpec((SUBLANES, D_MODEL), lambda l, j: (0, 0)),
            pl.BlockSpec((1, D_MODEL, tn), lambda l, j: (l, 0, j)),
            pl.BlockSpec((1, 1, tn), lambda l, j: (l, 0, j)),
        ],
        out_specs=pl.BlockSpec((1, SUBLANES, tn), lambda l, j: (l, 0, j)),
        out_shape=jax.ShapeDtypeStruct((depth, SUBLANES, n), F32),
        compiler_params=_cparams(("arbitrary", "arbitrary")),
        name="adaln_modulation",
    )(cc, w_mod, b_mod.reshape(depth, 1, n))


def _mod_row(m_ref, is_ctx):
    return jnp.where(is_ctx, m_ref[0, 1:2, :], m_ref[0, 0:1, :])


def _stream_specs(xs, tm):
    if isinstance(xs, tuple):
        assert tm == CTX
        return [pl.BlockSpec((tm, D_MODEL), lambda i: (0, 0)),
                pl.BlockSpec((tm, D_MODEL), lambda i: (jnp.maximum(i - 1, 0), 0))], list(xs)
    return [pl.BlockSpec((tm, D_MODEL), lambda i: (i, 0))], [xs]


def _stream_tile(x_refs):
    if len(x_refs) == 1:
        return x_refs[0][...]
    return jnp.where(pl.program_id(0) == 0, x_refs[0][...], x_refs[1][...])


def _norm_body(*refs, modulate, with_gates, n_stream=1):
    it = iter(refs)
    x_refs = [next(it) for _ in range(n_stream)]
    g_ref = next(it)
    sh_ref = sc_ref = wg_ref = bg_ref = gate_ref = None
    if modulate:
        sh_ref, sc_ref = next(it), next(it)
    if with_gates:
        wg_ref, bg_ref = next(it), next(it)
    h_ref = next(it)
    if with_gates:
        gate_ref = next(it)

    xf = _stream_tile(x_refs)
    ms = jnp.mean(xf * xf, axis=-1, keepdims=True)
    h = xf * lax.rsqrt(ms + EPS) * g_ref[...]
    if modulate:
        is_ctx = pl.program_id(0) == 0
        h = h * (1.0 + _mod_row(sc_ref, is_ctx)) + _mod_row(sh_ref, is_ctx)
    h_ref[...] = h.astype(h_ref.dtype)
    if with_gates:
        hi = h.astype(BF16)
        lo = (h - hi.astype(F32)).astype(BF16)
        whi, wlo = wg_ref[0], wg_ref[1]
        nt = lambda a, b: lax.dot_general(a, b, (((1,), (1,)), ((), ())), preferred_element_type=F32)
        gate_ref[...] = nt(hi, whi) + nt(hi, wlo) + nt(lo, whi) + bg_ref[...]


def _norm_mod(xs, g, mod_all, layer, phase, w_gate=None, b_gate=None):
    with_gates = w_gate is not None
    n_tiles = NTOK // ROW_TILE
    x_specs, x_args = _stream_specs(xs, ROW_TILE)
    in_specs = x_specs + [
        pl.BlockSpec((1, D_MODEL), lambda i: (0, 0)),
        pl.BlockSpec((1, SUBLANES, D_MODEL), lambda i: (layer, 0, 3 * phase)),
        pl.BlockSpec((1, SUBLANES, D_MODEL), lambda i: (layer, 0, 3 * phase + 1)),
    ]
    args = x_args + [g.reshape(1, D_MODEL), mod_all, mod_all]
    out_specs = [pl.BlockSpec((ROW_TILE, D_MODEL), lambda i: (i, 0))]
    out_shape = [jax.ShapeDtypeStruct((NTOK, D_MODEL), BF16)]
    if with_gates:
        in_specs += [pl.BlockSpec((2, LANES, D_MODEL), lambda i: (0, 0, 0)),
                     pl.BlockSpec((1, LANES), lambda i: (0, 0))]
        args += [w_gate, b_gate]
        out_specs.append(pl.BlockSpec((ROW_TILE, LANES), lambda i: (i, 0)))
        out_shape.append(jax.ShapeDtypeStruct((NTOK, LANES), F32))
    return pl.pallas_call(
        functools.partial(_norm_body, modulate=True, with_gates=with_gates, n_stream=len(x_args)),
        grid=(n_tiles,),
        in_specs=in_specs,
        out_specs=out_specs,
        out_shape=out_shape,
        compiler_params=_cparams(("arbitrary",)),
        name="rmsnorm_modulate",
    )(*args)


def _final_norm(xs, g):
    skip = CTX // ROW_TILE
    return pl.pallas_call(
        functools.partial(_norm_body, modulate=False, with_gates=False),
        grid=(SEQ // ROW_TILE,),
        in_specs=[pl.BlockSpec((ROW_TILE, D_MODEL), lambda i: (i + skip, 0)),
                  pl.BlockSpec((1, D_MODEL), lambda i: (0, 0))],
        out_specs=[pl.BlockSpec((ROW_TILE, D_MODEL), lambda i: (i, 0))],
        out_shape=[jax.ShapeDtypeStruct((SEQ, D_MODEL), F32)],
        compiler_params=_cparams(("arbitrary",)),
        name="final_rmsnorm",
    )(xs, g.reshape(1, D_MODEL))[0]


def _in_proj_body(*refs, n_w):
    a_ref, wt_refs, b_refs = refs[0], refs[1:1 + n_w], refs[1 + n_w:1 + 2 * n_w]
    o_ref, wbf_ref = refs[1 + 2 * n_w:]
    tn = MM_TILE_N

    @pl.when(pl.program_id(1) == 0)
    def _():
        for k, wt_ref in enumerate(wt_refs):
            wbf_ref[:, k * tn:(k + 1) * tn] = jnp.transpose(wt_ref[...]).astype(BF16)

    bias = jnp.concatenate([b_ref[...] for b_ref in b_refs], axis=1)
    acc = jnp.dot(a_ref[...], wbf_ref[...], preferred_element_type=F32) + bias
    o_ref[...] = acc.astype(o_ref.dtype)


def _in_proj(h, w_in_t, b_in, layer, first_tile, n_steps, n_w, dtype):
    tm, tn = MM_TILE_M, MM_TILE_N
    in_w = w_in_t.shape[1]
    tile = lambda j, k: first_tile + n_w * j + k
    once = dict(pipeline_mode=pl.Buffered(1)) if n_steps == 1 else {}
    w_specs = [pl.BlockSpec((None, tn, D_MODEL), functools.partial(lambda j, i, k: (layer, tile(j, k), 0), k=k),
                            **once)
               for k in range(n_w)]
    b_specs = [pl.BlockSpec((None, 1, tn), functools.partial(lambda j, i, k: (layer, 0, tile(j, k)), k=k))
               for k in range(n_w)]
    b3 = b_in.reshape(DEPTH, 1, in_w)
    return pl.pallas_call(
        functools.partial(_in_proj_body, n_w=n_w),
        grid=(n_steps, NTOK // tm),
        in_specs=[pl.BlockSpec((tm, D_MODEL), lambda j, i: (i, 0))] + w_specs + b_specs,
        out_specs=pl.BlockSpec((tm, n_w * tn), lambda j, i: (i, j)),
        out_shape=jax.ShapeDtypeStruct((NTOK, n_steps * n_w * tn), dtype),
        scratch_shapes=[pltpu.VMEM((D_MODEL, n_w * tn), BF16)],
        compiler_params=_cparams(("arbitrary", "arbitrary")),
        name="in_proj",
    )(h, *([w_in_t] * n_w), *([b3] * n_w))


def _row_gate(g_ref, i, tm, tn):
    rows = i * tm + lax.broadcasted_iota(jnp.int32, (tm, tn), 0)
    return jnp.where(rows < CTX, g_ref[0, 1:2, :], g_ref[0, 0:1, :])


def _out_proj_body(y_ref, wglu_ref, bglu_ref, cv_ref, hf_ref, hb_ref, o_ref, mlg_ref, w_ref,
                   gate_ref, sh_ref, sc_ref, ng_ref, *rest):
    x_refs, (xo_ref, h_ref, wbf_ref, wglu_bf_ref) = rest[:-4], rest[-4:]
    i = pl.program_id(0)

    @pl.when(i == 0)
    def _():
        wbf_ref[...] = w_ref[...].astype(BF16)
        wglu_bf_ref[...] = wglu_ref[...].astype(BF16)

    g = jax.nn.gelu(y_ref[...])
    glu = jnp.dot(g.astype(BF16), wglu_bf_ref[...], preferred_element_type=F32) + bglu_ref[...]
    s5 = (g * jax.nn.sigmoid(glu)).astype(BF16)
    acc = jnp.dot(s5, wbf_ref[0:S5_W, :], preferred_element_type=F32)
    acc += jnp.dot(cv_ref[...], wbf_ref[S5_W:S5_W + CONV_W, :], preferred_element_type=F32)
    for head in range(ML_H):
        cols = slice(head * ML_DH, (head + 1) * ML_DH)
        hh = hf_ref[:, cols].astype(F32) + hb_ref[:, cols].astype(F32)
        hc = hh - jnp.mean(hh, axis=-1, keepdims=True)
        var = jnp.mean(hc * hc, axis=-1, keepdims=True)
        ml = jax.nn.sigmoid(o_ref[:, cols].astype(F32)) * (hc * lax.rsqrt(var + LN_EPS) * mlg_ref[:, cols])
        row0 = S5_W + CONV_W + head * ML_DH
        acc += jnp.dot(ml.astype(BF16), wbf_ref[row0:row0 + ML_DH, :], preferred_element_type=F32)
    is_ctx = i == 0
    xn = _stream_tile(x_refs) + _mod_row(gate_ref, is_ctx) * acc
    xo_ref[...] = xn
    ms = jnp.mean(xn * xn, axis=-1, keepdims=True)
    h = xn * lax.rsqrt(ms + EPS) * ng_ref[...]
    h_ref[...] = (h * (1.0 + _mod_row(sc_ref, is_ctx)) + _mod_row(sh_ref, is_ctx)).astype(BF16)


def _out_proj(xs, y_s5, w_glu, b_glu, cvo, h_f, h_b, qkvo, ml_norm_g, w_out, mod_all, norm_g, layer):
    tm = ROW_TILE
    mod = lambda k: pl.BlockSpec((1, SUBLANES, D_MODEL), lambda i: (layer, 0, k))
    rows = lambda width, col=0: pl.BlockSpec((tm, width), lambda i: (i, col))
    const = lambda shape: pl.BlockSpec(shape, lambda i: (0,) * len(shape))
    x_specs, x_args = _stream_specs(xs, tm)
    return pl.pallas_call(
        _out_proj_body,
        grid=(NTOK // tm,),
        in_specs=[
            rows(S5_W), const((S5_W, S5_W)), const((1, S5_W)),
            rows(CONV_W),
            rows(ML_W), rows(ML_W), rows(ML_W, 3), const((1, ML_W)),
            pl.BlockSpec((None, D_MODEL, D_MODEL), lambda i: (layer, 0, 0), pipeline_mode=pl.Buffered(1)),
            mod(2), mod(3), mod(4),
            const((1, D_MODEL)),
        ] + x_specs,
        out_specs=[rows(D_MODEL), rows(D_MODEL)],
        out_shape=[jax.ShapeDtypeStruct((NTOK, D_MODEL), F32), jax.ShapeDtypeStruct((NTOK, D_MODEL), BF16)],
        scratch_shapes=[pltpu.VMEM((D_MODEL, D_MODEL), BF16), pltpu.VMEM((S5_W, S5_W), BF16)],
        compiler_params=_cparams(("arbitrary",)),
        name="out_proj_residual",
    )(y_s5, w_glu, b_glu.reshape(1, S5_W), cvo, h_f, h_b, qkvo, ml_norm_g.reshape(1, ML_W), w_out,
      mod_all, mod_all, mod_all, norm_g.reshape(1, D_MODEL), *x_args)


def _ffn_in_body(a_ref, wg_ref, wu_ref, o_ref, wgbf_ref, wubf_ref):
    @pl.when(pl.program_id(1) == 0)
    def _():
        wgbf_ref[...] = wg_ref[...].astype(BF16)
        wubf_ref[...] = wu_ref[...].astype(BF16)

    a = a_ref[...]
    g = jnp.dot(a, wgbf_ref[...], preferred_element_type=F32)
    u = jnp.dot(a, wubf_ref[...], preferred_element_type=F32)
    o_ref[...] = (jax.nn.silu(g) * u).astype(BF16)


def _ffn_in(h, w_ffn_in, layer):
    tm, tn = MM_TILE_M, MM_TILE_N
    nj = D_FF // tn
    return pl.pallas_call(
        _ffn_in_body,
        grid=(nj, NTOK // tm),
        in_specs=[
            pl.BlockSpec((tm, D_MODEL), lambda j, i: (i, 0)),
            pl.BlockSpec((None, D_MODEL, tn), lambda j, i: (layer, 0, j)),
            pl.BlockSpec((None, D_MODEL, tn), lambda j, i: (layer, 0, nj + j)),
        ],
        out_specs=pl.BlockSpec((tm, tn), lambda j, i: (i, j)),
        out_shape=jax.ShapeDtypeStruct((NTOK, D_FF), BF16),
        scratch_shapes=[pltpu.VMEM((D_MODEL, tn), BF16), pltpu.VMEM((D_MODEL, tn), BF16)],
        compiler_params=_cparams(("arbitrary", "arbitrary")),
        name="ffn_in_swiglu",
    )(h, w_ffn_in, w_ffn_in)


def _ffn_out_body(a_ref, w_ref, g_ref, x_ref, o_ref, wbf_ref):
    i = pl.program_id(1)

    @pl.when(i == 0)
    def _():
        wbf_ref[...] = w_ref[...].astype(BF16)

    acc = jnp.dot(a_ref[...], wbf_ref[...], preferred_element_type=F32)
    tm, tn = o_ref.shape
    o_ref[...] = x_ref[...] + _row_gate(g_ref, i, tm, tn) * acc


def _ffn_out(xs, hid, w_ffn_out, mod_all, layer):
    tm, tn = FFN_OUT_TILE_M, MM_TILE_N
    return pl.pallas_call(
        _ffn_out_body,
        grid=(D_MODEL // tn, NTOK // tm),
        in_specs=[
            pl.BlockSpec((tm, D_FF), lambda j, i: (i, 0)),
            pl.BlockSpec((None, D_FF, tn), lambda j, i: (layer, 0, j)),
            pl.BlockSpec((1, SUBLANES, tn), lambda j, i: (layer, 0, 5 * (D_MODEL // tn) + j)),
            pl.BlockSpec((tm, tn), lambda j, i: (i, j)),
        ],
        out_specs=pl.BlockSpec((tm, tn), lambda j, i: (i, j)),
        out_shape=jax.ShapeDtypeStruct((NTOK, D_MODEL), F32),
        scratch_shapes=[pltpu.VMEM((D_FF, tn), BF16)],
        compiler_params=_cparams(("arbitrary", "arbitrary")),
        name="ffn_out_residual",
    )(hid, w_ffn_out, mod_all, xs)


def _s5_selectors():
    tau = np.arange(S5_BLK)[:, None] // S5_CH
    ch_r = np.arange(S5_BLK)[:, None] % S5_CH
    t = np.arange(S5_BLK)[None, :] // S5_CH
    ch_c = np.arange(S5_BLK)[None, :] % S5_CH
    sel = np.zeros((2, S5_T, S5_BLK, S5_BLK), np.float32)
    for s in range(S5_T):
        sel[0, s] = (tau == t - s) & (ch_r == ch_c)
        sel[1, s] = (tau == s - t) & (ch_r == ch_c)
    rp = np.arange(S5_BLK)[:, None]
    col = np.arange(2 * S5_BLK)[None, :]
    pair_map = np.stack([(col == (rp // S5_P) * S5_BLK + gi * S5_P + rp % S5_P) for gi in range(2)])
    src = np.arange(S5_STEP_W)
    dst = ((src // S5_CH) % S5_STEP_G) * S5_BLK + (src // S5_SET_CH) * S5_CH + src % S5_CH
    perm = dst[:, None] == np.arange(S5_STEP_W)[None, :]
    return (jnp.asarray(sel), jnp.asarray(pair_map, BF16), jnp.asarray(perm, BF16),
            jnp.asarray(perm.T, BF16))


def _s5_operators(lam_re, lam_im, log_step, b_re, b_im, c_re, c_im, sel):
    lam = lax.complex(lam_re.astype(F32), lam_im.astype(F32))
    lam_bar = jnp.exp(lam * jnp.exp(log_step.astype(F32)))
    b_bar = ((lam_bar - 1.0) / lam)[..., None] * lax.complex(b_re.astype(F32), b_im.astype(F32))
    c_mat = lax.complex(c_re.astype(F32), c_im.astype(F32))
    depth = lam.shape[0]

    def powers(base, count):
        out = [jnp.ones_like(base)]
        for _ in range(count - 1):
            out.append(out[-1] * base)
        return out

    pw = powers(lam_bar, S5_T + 1)
    pa = powers(pw[S5_T], SUBLANES + 1)
    t_up = list(range(S5_T))

    def table(seq, fwd_idx, bwd_idx):
        return jnp.stack([jnp.stack([seq[f][:, 0], seq[b][:, 1]], axis=1)
                          for f, b in zip(fwd_idx, bwd_idx)])

    blocks = (depth, 2, S5_G, S5_BLK, S5_BLK)

    def token_rows(w, imag_sign):
        w = jnp.moveaxis(w, 0, 3)
        return jnp.concatenate([jnp.real(w), imag_sign * jnp.imag(w)], axis=-1).reshape(blocks)

    pb = table(pw, [S5_T - 1 - t for t in t_up], t_up)
    wb = token_rows(pb[:, :, :, :, None, :] * jnp.swapaxes(b_bar, -1, -2)[None], 1.0)
    pc = table(pw, [t + 1 for t in t_up], [S5_T - t for t in t_up])
    wct = token_rows(pc[:, :, :, :, None, :] * c_mat[None], -1.0)

    kern = jnp.real(jnp.einsum('ldgcp,kldgp,ldgpa->ldgakc', c_mat, jnp.stack(pw[:S5_T]), b_bar))
    kern = kern.reshape(depth, 2, S5_G * S5_CH, S5_BLK)
    kc = jnp.einsum('ldxk,dskn->ldsxn', kern, sel, precision=lax.Precision.HIGHEST)
    kc = kc.reshape(depth, 2, S5_T, S5_G, S5_CH, S5_BLK)
    kc = jnp.transpose(kc, (0, 1, 3, 2, 4, 5)).reshape(blocks)

    def lanes(v):
        parts = jnp.stack([jnp.real(v), jnp.imag(v)], axis=4)
        parts = parts.reshape(v.shape[0], depth, 2, S5_G // 2, 2, 2, S5_P)
        parts = jnp.swapaxes(parts, 4, 5).reshape(v.shape[0], depth, 2, S5_FLAT)
        return jnp.transpose(parts, (1, 2, 0, 3))

    zero_p = jnp.zeros_like(pa[0])
    amul = lanes(jnp.stack([pa[1], pa[2], pa[4], pa[8]] + [zero_p] * 4))
    apow = lanes(table(pa, t_up, t_up[::-1]))
    return kc.astype(BF16), wb.astype(BF16), wct.astype(BF16), amul, apow


def _s5_row_scan(e_ref, amul_ref, apow_ref, block_lo, block_hi, reverse, carry):
    rows = lax.broadcasted_iota(jnp.int32, (SUBLANES, S5_BLK), 0)

    def shifted(x, k):
        if reverse:
            return jnp.where(rows < SUBLANES - k, pltpu.roll(x, SUBLANES - k, 0), 0.0)
        return jnp.where(rows >= k, pltpu.roll(x, k, 0), 0.0)

    def body(step, carry):
        blk = (block_hi - 1 - step) if reverse else (block_lo + step)
        r0 = pl.multiple_of(blk * SUBLANES, SUBLANES)
        last = 0 if reverse else SUBLANES - 1
        carry_out = []
        for q in range(S5_STEP_PAIRS):
            re_c = slice(2 * q * S5_BLK, (2 * q + 1) * S5_BLK)
            im_c = slice((2 * q + 1) * S5_BLK, (2 * q + 2) * S5_BLK)
            s_re = e_ref[pl.ds(r0, SUBLANES), re_c]
            s_im = e_ref[pl.ds(r0, SUBLANES), im_c]
            for idx, k in enumerate((1, 2, 4)):
                a_re, a_im = amul_ref[idx:idx + 1, re_c], amul_ref[idx:idx + 1, im_c]
                t_re, t_im = shifted(s_re, k), shifted(s_im, k)
                s_re, s_im = s_re + t_re * a_re - t_im * a_im, s_im + t_re * a_im + t_im * a_re
            c_re, c_im = carry[:, re_c], carry[:, im_c]
            p_re, p_im = apow_ref[:, re_c], apow_ref[:, im_c]
            e_ref[pl.ds(r0, SUBLANES), re_c] = shifted(s_re, 1) + p_re * c_re - p_im * c_im
            e_ref[pl.ds(r0, SUBLANES), im_c] = shifted(s_im, 1) + p_re * c_im + p_im * c_re
            a_re, a_im = amul_ref[3:4, re_c], amul_ref[3:4, im_c]
            l_re = jnp.broadcast_to(s_re[last:last + 1, :], (SUBLANES, S5_BLK))
            l_im = jnp.broadcast_to(s_im[last:last + 1, :], (SUBLANES, S5_BLK))
            carry_out += [l_re + a_re * c_re - a_im * c_im, l_im + a_re * c_im + a_im * c_re]
        return jnp.concatenate(carry_out, axis=1)

    return lax.fori_loop(0, block_hi - block_lo, body, carry)


def _s5_pair_operators(kc_ref, wb_ref, wct_ref, map_ref, q):
    g0, g1 = 2 * q, 2 * q + 1
    zero = jnp.zeros((S5_BLK, S5_BLK), BF16)
    kc = jnp.concatenate([jnp.concatenate([kc_ref[g0], zero], axis=1),
                          jnp.concatenate([zero, kc_ref[g1]], axis=1)], axis=0)
    spread = lambda ref: jnp.concatenate(
        [jnp.dot(ref[g0], map_ref[0], preferred_element_type=F32),
         jnp.dot(ref[g1], map_ref[1], preferred_element_type=F32)], axis=0).astype(BF16)
    return kc, spread(wb_ref), spread(wct_ref)


def _s5_body(u_ref, p_ref, pt_ref, map_ref, kc_ref, wb_ref, wct_ref, amul_ref, apow_ref, d_ref, y_ref,
             up_ref, e_ref, acc_ref):
    direction = pl.program_id(1)
    token_rows = lambda t: pl.ds(t, S5_ROWS, stride=S5_T)
    pairs = [slice(2 * q * S5_BLK, 2 * (q + 1) * S5_BLK) for q in range(S5_STEP_PAIRS)]

    @pl.when(direction == 0)
    def _():
        u_nat = jnp.concatenate([u_ref[token_rows(t), :] for t in range(S5_T)], axis=1)
        up_ref[...] = jnp.dot(u_nat.astype(BF16), p_ref[...], preferred_element_type=F32).astype(BF16)
        acc_ref[...] = jnp.zeros_like(acc_ref)

    ops = [_s5_pair_operators(kc_ref, wb_ref, wct_ref, map_ref, q) for q in range(S5_STEP_PAIRS)]
    for (_, wb, _), cols in zip(ops, pairs):
        e_ref[:, cols] = jnp.dot(up_ref[:, cols], wb, preferred_element_type=F32)

    zero = jnp.zeros((SUBLANES, S5_STEP_W), F32)
    n_blocks = S5_ROWS // SUBLANES
    ctx_blocks = S5_CTX_ROWS // SUBLANES
    amul, apow = amul_ref, apow_ref

    @pl.when(direction == 0)
    def _():
        _s5_row_scan(e_ref, amul, apow, 0, n_blocks, False, zero)

    @pl.when(direction == 1)
    def _():
        carry = _s5_row_scan(e_ref, amul, apow, 0, ctx_blocks, True, zero)
        _s5_row_scan(e_ref, amul, apow, ctx_blocks, n_blocks, True, carry)

    for (kc, _, wct), cols in zip(ops, pairs):
        y = jnp.dot(up_ref[:, cols], kc, preferred_element_type=F32)
        y += lax.dot_general(e_ref[:, cols].astype(BF16), wct, (((1,), (1,)), ((), ())),
                             preferred_element_type=F32)
        acc_ref[:, cols] += y

    @pl.when(direction == 1)
    def _():
        acc = acc_ref[...]
        hi = acc.astype(BF16)
        lo = (acc - hi.astype(F32)).astype(BF16)
        y_nat = (jnp.dot(hi, pt_ref[...], preferred_element_type=F32)
                 + jnp.dot(lo, pt_ref[...], preferred_element_type=F32))
        for t in range(S5_T):
            y_ref[token_rows(t), :] = (y_nat[:, t * S5_SET_CH:(t + 1) * S5_SET_CH]
                                       + d_ref[...] * u_ref[token_rows(t), :])


def _s5_mixer(z, ops, consts, layer, d_skip):
    kc, wb, wct, amul, apow = ops
    _, pair_map, perm, perm_t = consts
    n_steps = S5_G // S5_STEP_G
    wspec = pl.BlockSpec((None, None, S5_STEP_G, S5_BLK, S5_BLK), lambda j, d: (layer, d, j, 0, 0))
    cspec = pl.BlockSpec((None, None, SUBLANES, S5_STEP_W), lambda j, d: (layer, d, 0, j))
    pspec = pl.BlockSpec((S5_STEP_W, S5_STEP_W), lambda j, d: (0, 0))
    return pl.pallas_call(
        _s5_body,
        grid=(n_steps, 2),
        in_specs=[
            pl.BlockSpec((NTOK, S5_SET_CH), lambda j, d: (0, j)),
            pspec, pspec,
            pl.BlockSpec((2, S5_BLK, 2 * S5_BLK), lambda j, d: (0, 0, 0)),
            wspec, wspec, wspec, cspec, cspec,
            pl.BlockSpec((1, S5_SET_CH), lambda j, d: (0, j)),
        ],
        out_specs=pl.BlockSpec((NTOK, S5_SET_CH), lambda j, d: (0, j)),
        out_shape=jax.ShapeDtypeStruct((NTOK, S5_W), F32),
        scratch_shapes=[pltpu.VMEM((S5_ROWS, S5_STEP_W), BF16), pltpu.VMEM((S5_ROWS, S5_STEP_W), F32),
                        pltpu.VMEM((S5_ROWS, S5_STEP_W), F32)],
        compiler_params=_cparams(("arbitrary", "arbitrary")),
        name="s5_scan",
    )(z, perm, perm_t, pair_map, kc, wb, wct, amul, apow, d_skip.astype(F32).reshape(1, S5_W))


CONV_PAD = 16
CONV_BLK = 64


def _conv_body(a_ref, b_ref, w_ref, db_ref, lg_ref, lb_ref, o_ref, pad_ref, sh_ref):
    g = a_ref[...] * jax.nn.sigmoid(b_ref[...])

    def run(seq_len):
        n_seq = ROW_TILE // seq_len
        pitch = seq_len + 2 * CONV_PAD
        zeros = jnp.zeros((CONV_PAD, CONV_W), F32)
        for s in range(n_seq):
            pad_ref[s * pitch:s * pitch + CONV_PAD, :] = zeros
            pad_ref[s * pitch + CONV_PAD:s * pitch + CONV_PAD + seq_len, :] = g[s * seq_len:(s + 1) * seq_len]
            pad_ref[s * pitch + CONV_PAD + seq_len:(s + 1) * pitch, :] = zeros
        used = n_seq * pitch
        for r in range(1, SUBLANES):
            sh_ref[r, 0:used - SUBLANES, :] = pad_ref[r:r + used - SUBLANES, :]
        for blk in range(ROW_TILE // CONV_BLK):
            row0 = blk * CONV_BLK
            s, q = divmod(row0, seq_len)
            base = s * pitch + CONV_PAD + q - CONV_K // 2
            acc = jnp.zeros((CONV_BLK, CONV_W), F32)
            for k in range(CONV_K):
                r = (base + k) % SUBLANES
                row = base + k - r
                tap = pad_ref[row:row + CONV_BLK, :] if r == 0 else sh_ref[r, row:row + CONV_BLK, :]
                acc = acc + w_ref[k:k + 1, :] * tap
            y = acc + db_ref[...]
            yc = y - jnp.mean(y, axis=-1, keepdims=True)
            var = jnp.mean(yc * yc, axis=-1, keepdims=True)
            y = yc * lax.rsqrt(var + LN_EPS) * lg_ref[...] + lb_ref[...]
            o_ref[row0:row0 + CONV_BLK, :] = jax.nn.silu(y).astype(BF16)

    is_ctx = pl.program_id(0) == 0

    @pl.when(is_ctx)
    def _():
        run(CTX)

    @pl.when(jnp.logical_not(is_ctx))
    def _():
        run(GRID_W)


def _conv_module(z, dw_w, dw_b, ln_g, ln_b):
    w = jnp.concatenate([dw_w, jnp.zeros((1, CONV_W), dw_w.dtype)], axis=0)
    vec = lambda v: v.reshape(1, CONV_W)
    pad_rows = max(CTX + 2 * CONV_PAD, (ROW_TILE // GRID_W) * (GRID_W + 2 * CONV_PAD))
    return pl.pallas_call(
        _conv_body,
        grid=(NTOK // ROW_TILE,),
        in_specs=[
            pl.BlockSpec((ROW_TILE, CONV_W), lambda i: (i, OFF_CONV // CONV_W)),
            pl.BlockSpec((ROW_TILE, CONV_W), lambda i: (i, OFF_CONV // CONV_W + 1)),
            pl.BlockSpec((CONV_K + 1, CONV_W), lambda i: (0, 0)),
            pl.BlockSpec((1, CONV_W), lambda i: (0, 0)),
            pl.BlockSpec((1, CONV_W), lambda i: (0, 0)),
            pl.BlockSpec((1, CONV_W), lambda i: (0, 0)),
        ],
        out_specs=pl.BlockSpec((ROW_TILE, CONV_W), lambda i: (i, 0)),
        out_shape=jax.ShapeDtypeStruct((NTOK, CONV_W), BF16),
        scratch_shapes=[pltpu.VMEM((pad_rows, CONV_W), F32), pltpu.VMEM((SUBLANES, pad_rows, CONV_W), F32)],
        compiler_params=_cparams(("arbitrary",)),
        name="conv_module",
    )(z, z, w, vec(dw_b), vec(ln_g), vec(ln_b))


def _ml_chunk_index(ci, reverse):
    if not reverse:
        return ci
    return jnp.where(ci == 0, 0, ML_NCHUNK - ci)


def _ml_chain_stages(qb, kb, vx, li_col, li_row, b_col, b_row, cx_ref, m_ref, mask, reverse, store_h):
    c = ML_CHUNK
    scale = ML_DH ** 0.5
    v = {}

    def scores():
        v['m'] = m_ref[:, 0:1]
        v['d_log'] = jnp.where(mask, b_col + (li_row - b_row), -jnp.inf)
        v['inter'] = b_col + v['m']
        v['m_row'] = jnp.maximum(v['inter'], jnp.max(v['d_log'], axis=-1, keepdims=True))
        v['qk'] = lax.dot_general(qb, kb, (((1,), (1,)), ((), ())), preferred_element_type=F32)

    def numerator():
        s = v['qk'] * jnp.exp(v['d_log'] - v['m_row'])
        w_inter = jnp.exp(v['inter'] - v['m_row']) * scale
        v['cx'] = cx_ref[...]
        lhs = jnp.concatenate([s.astype(BF16), qb * w_inter.astype(BF16)], axis=1)
        rhs = jnp.concatenate([vx, v['cx'].astype(BF16)], axis=0)
        v['num'] = jnp.dot(lhs, rhs, preferred_element_type=F32)

    def output():
        num = v['num']
        den = num[:, ML_DH:ML_DH + 1]
        store_h(num[:, :ML_DH] / jnp.maximum(jnp.abs(den), jnp.exp(-v['m_row']) * scale))

    def state():
        m = v['m']
        b_tot = b_row[:, 0:1] if reverse else b_row[:, c - 1:c]
        g = b_tot - b_col + li_col
        m_new = jnp.maximum(b_tot + m, jnp.max(g, axis=0, keepdims=True))
        kw = kb * (jnp.exp(g - m_new) * (1.0 / scale)).astype(BF16)
        decay = jnp.exp(b_tot + m - m_new)
        cx_ref[...] = decay * v['cx'] + lax.dot_general(kw, vx, (((0,), (0,)), ((), ())),
                                                       preferred_element_type=F32)
        m_ref[...] = jnp.broadcast_to(m_new, m_ref.shape)

    return [scores, numerator, output, state]


def _mlstm_body(qkvf_ref, gf_ref, gtf_ref, qkvb_ref, gb_ref, gtb_ref, hf_ref, hb_ref, cx_ref, m_ref):
    @pl.when(pl.program_id(0) == 0)
    def _():
        cx_ref[...] = jnp.zeros_like(cx_ref)
        m_ref[...] = jnp.zeros_like(m_ref)

    c = ML_CHUNK
    r_idx = lax.broadcasted_iota(jnp.int32, (c, c), 0)
    c_idx = lax.broadcasted_iota(jnp.int32, (c, c), 1)
    ones_col = jnp.where(lax.broadcasted_iota(jnp.int32, (c, LANES), 1) == 0, 1.0, 0.0).astype(BF16)

    chains = []
    for d, (qkv_ref, g_ref, gt_ref, h_ref) in enumerate(
            ((qkvf_ref, gf_ref, gtf_ref, hf_ref), (qkvb_ref, gb_ref, gtb_ref, hb_ref))):
        reverse = d == 1
        gates = g_ref[...]
        gates_t = gt_ref[...]
        incl = (c_idx >= r_idx) if reverse else (c_idx <= r_idx)
        tri = jnp.where(incl, 1.0, 0.0).astype(BF16)
        incl_t = (r_idx >= c_idx) if reverse else (r_idx <= c_idx)
        tri_t = jnp.where(incl_t, 1.0, 0.0).astype(BF16)
        lf = jax.nn.log_sigmoid(gates)
        lf_t = jax.nn.log_sigmoid(gates_t)
        b_all = sum(jnp.dot(tri, p, preferred_element_type=F32) for p in _split3(lf))
        b_all_t = sum(jnp.dot(p, tri_t, preferred_element_type=F32) for p in _split3(lf_t))
        for head in range(ML_H):
            i_col = 2 * ML_H * d + head
            f_col = i_col + ML_H
            state = d * ML_H + head
            col = lambda part: slice((part * ML_H + head) * ML_DH, (part * ML_H + head + 1) * ML_DH)
            vx = jnp.concatenate([qkv_ref[:, col(2)], ones_col], axis=1)

            def store_h(h, h_ref=h_ref, head=head):
                h_ref[:, head * ML_DH:(head + 1) * ML_DH] = h.astype(h_ref.dtype)

            chains.append(_ml_chain_stages(
                qkv_ref[:, col(0)], qkv_ref[:, col(1)], vx,
                gates[:, i_col:i_col + 1], gates_t[i_col:i_col + 1, :],
                b_all[:, f_col:f_col + 1], b_all_t[f_col:f_col + 1, :],
                cx_ref.at[state], m_ref.at[state], incl, reverse, store_h))
    for stage in zip(*chains):
        for run in stage:
            run()


def _mlstm_mixer(qkvo, gates):
    gates_t = jnp.transpose(gates[:, :2 * SUBLANES])
    c = ML_CHUNK

    def specs(reverse):
        row = lambda ci: _ml_chunk_index(ci, reverse)
        return [
            pl.BlockSpec((c, 3 * ML_W), lambda ci: (row(ci), 0)),
            pl.BlockSpec((c, LANES), lambda ci: (row(ci), 0)),
            pl.BlockSpec((2 * SUBLANES, c), lambda ci: (0, row(ci))),
        ]

    out_spec = lambda reverse: pl.BlockSpec((c, ML_W), lambda ci: (_ml_chunk_index(ci, reverse), 0))
    n_state = 2 * ML_H
    return pl.pallas_call(
        _mlstm_body,
        grid=(ML_NCHUNK,),
        in_specs=specs(False) + specs(True),
        out_specs=[out_spec(False), out_spec(True)],
        out_shape=[jax.ShapeDtypeStruct((NTOK, ML_W), BF16)] * 2,
        scratch_shapes=[pltpu.VMEM((n_state, ML_DH, ML_DH + LANES), F32),
                        pltpu.VMEM((n_state, 1, LANES), F32)],
        compiler_params=_cparams(("arbitrary",)),
        name="mlstm_chunks",
    )(qkvo, gates, gates_t, qkvo, gates, gates_t)


def kernel(x, c, ctx, c_ctx, w_mod, b_mod, norm1_g, w_in, b_in, s5_lam_re, s5_lam_im, s5_log_step,
           s5_b_re, s5_b_im, s5_c_re, s5_c_im, s5_d, s5_w_glu, s5_b_glu, conv_dw_w, conv_dw_b,
           conv_ln_g, conv_ln_b, ml_norm_g, w_out, norm2_g, w_ffn_in, w_ffn_out, norm_f_g):
    assert x.shape == (1, SEQ, D_MODEL) and ctx.shape == (1, CTX, D_MODEL)
    xs = (ctx[0].astype(F32), x[0].astype(F32))
    cc = jnp.zeros((SUBLANES, D_MODEL), F32).at[0].set(c[0]).at[1].set(c_ctx)
    mod_all = _modulation(cc, w_mod, b_mod)
    s5_consts = _s5_selectors()
    s5_ops = _s5_operators(s5_lam_re, s5_lam_im, s5_log_step, s5_b_re, s5_b_im, s5_c_re, s5_c_im,
                           s5_consts[0])
    w_in_t = jnp.swapaxes(w_in, 1, 2)
    n_gate = w_in.shape[2] - OFF_G
    w_gate = jnp.pad(w_in_t[:, OFF_G:, :], ((0, 0), (0, LANES - n_gate), (0, 0)))
    w_gate_hi = w_gate.astype(BF16)
    w_gate = jnp.stack([w_gate_hi, (w_gate - w_gate_hi.astype(F32)).astype(BF16)], axis=1)
    b_gate = jnp.pad(b_in[:, OFF_G:], ((0, 0), (0, LANES - n_gate))).reshape(DEPTH, 1, LANES)
    tn = MM_TILE_N

    h, gates = _norm_mod(xs, norm1_g[0], mod_all, 0, 0, w_gate[0], b_gate[0])
    for l in range(DEPTH):
        z = _in_proj(h, w_in_t, b_in, l, 0, 1, OFF_Q // tn, F32)
        qkvo = _in_proj(h, w_in_t, b_in, l, OFF_Q // tn, (OFF_G - OFF_Q) // (2 * tn), 2, BF16)
        y_s5 = _s5_mixer(z, s5_ops, s5_consts, l, s5_d[l])
        cvo = _conv_module(z, conv_dw_w[l], conv_dw_b[l], conv_ln_g[l], conv_ln_b[l])
        h_f, h_b = _mlstm_mixer(qkvo, gates)
        xs, h2 = _out_proj(xs, y_s5, s5_w_glu[l], s5_b_glu[l], cvo, h_f, h_b, qkvo, ml_norm_g[l], w_out,
                           mod_all, norm2_g[l], l)
        hid = _ffn_in(h2, w_ffn_in, l)
        xs = _ffn_out(xs, hid, w_ffn_out, mod_all, l)
        if l + 1 < DEPTH:
            h, gates = _norm_mod(xs, norm1_g[l + 1], mod_all, l + 1, 0, w_gate[l + 1], b_gate[l + 1])
    return _final_norm(xs, norm_f_g)[None]
```

```python
import functools

import numpy as np
import jax
import jax.numpy as jnp
from jax import lax
from jax.experimental import pallas as pl
from jax.experimental.pallas import tpu as pltpu

F32 = jnp.float32
BF16 = jnp.bfloat16

D_MODEL = 2048
SEQ = 8192
CTX = 256
NTOK = SEQ + CTX
DEPTH = 4
GRID_W = 64

S5_W = 512
S5_CH = 16
S5_G = 32
S5_P = 64
CONV_W = 512
CONV_K = 31
ML_W = 1024
ML_H = 4
ML_DH = 256
D_FF = 5632
EPS = 1e-6
LN_EPS = 1e-5

OFF_CONV = 512
OFF_Q = 1536
OFF_K = 2560
OFF_V = 3584
OFF_O = 4608
OFF_G = 5632

LANES = 128
SUBLANES = 8
VMEM_LIMIT = 56 * 1024 * 1024

ROW_TILE = 256
MM_TILE_M = 1056
MM_TILE_N = 512
FFN_OUT_TILE_M = 704

S5_T = 8
S5_ROWS = NTOK // S5_T
S5_CTX_ROWS = CTX // S5_T
S5_BLK = S5_T * S5_CH
S5_STEP_G = 8
S5_STEP_PAIRS = S5_STEP_G // 2
S5_SET_CH = S5_STEP_G * S5_CH
S5_STEP_W = S5_STEP_G * S5_BLK
S5_FLAT = S5_G * S5_BLK

ML_CHUNK = 256
ML_NCHUNK = NTOK // ML_CHUNK


def _cparams(sem, vmem=VMEM_LIMIT):
    return pltpu.CompilerParams(dimension_semantics=sem, vmem_limit_bytes=vmem)


def _dot(a, b):
    return jnp.dot(a.astype(BF16), b.astype(BF16), preferred_element_type=F32)


def _split3(x):
    a = x.astype(BF16)
    r = x - a.astype(F32)
    b = r.astype(BF16)
    c = (r - b.astype(F32)).astype(BF16)
    return a, b, c


def _mod_body(cc_ref, w_ref, b_ref, o_ref):
    s = jax.nn.silu(cc_ref[...])
    o_ref[0] = _dot(s, w_ref[0]) + b_ref[0]


def _modulation(cc, w_mod, b_mod):
    depth, _, n = w_mod.shape
    tn = 1024
    return pl.pallas_call(
        _mod_body,
        grid=(depth, n // tn),
        in_specs=[
            pl.BlockSpec((SUBLANES, D_MODEL), lambda l, j: (0, 0)),
            pl.BlockSpec((1, D_MODEL, tn), lambda l, j: (l, 0, j)),
            pl.BlockSpec((1, 1, tn), lambda l, j: (l, 0, j)),
        ],
        out_specs=pl.BlockSpec((1, SUBLANES, tn), lambda l, j: (l, 0, j)),
        out_shape=jax.ShapeDtypeStruct((depth, SUBLANES, n), F32),
        compiler_params=_cparams(("arbitrary", "arbitrary")),
        name="adaln_modulation",
    )(cc, w_mod, b_mod.reshape(depth, 1, n))


def _mod_row(m_ref, is_ctx):
    return jnp.where(is_ctx, m_ref[0, 1:2, :], m_ref[0, 0:1, :])


def _stream_specs(xs, tm):
    if isinstance(xs, tuple):
        assert tm == CTX
        return [pl.BlockSpec((tm, D_MODEL), lambda i: (0, 0)),
                pl.BlockSpec((tm, D_MODEL), lambda i: (jnp.maximum(i - 1, 0), 0))], list(xs)
    return [pl.BlockSpec((tm, D_MODEL), lambda i: (i, 0))], [xs]


def _stream_tile(x_refs):
    if len(x_refs) == 1:
        return x_refs[0][...]
    return jnp.where(pl.program_id(0) == 0, x_refs[0][...], x_refs[1][...])


def _norm_body(*refs, modulate, with_gates, n_stream=1):
    it = iter(refs)
    x_refs = [next(it) for _ in range(n_stream)]
    g_ref = next(it)
    sh_ref = sc_ref = wg_ref = bg_ref = gate_ref = None
    if modulate:
        sh_ref, sc_ref = next(it), next(it)
    if with_gates:
        wg_ref, bg_ref = next(it), next(it)
    h_ref = next(it)
    if with_gates:
        gate_ref = next(it)

    xf = _stream_tile(x_refs)
    ms = jnp.mean(xf * xf, axis=-1, keepdims=True)
    if modulate:
        is_ctx = pl.program_id(0) == 0
        gain = g_ref[...] * (1.0 + _mod_row(sc_ref, is_ctx))
        h = xf * lax.rsqrt(ms + EPS) * gain + _mod_row(sh_ref, is_ctx)
    else:
        h = xf * lax.rsqrt(ms + EPS) * g_ref[...]
    h_ref[...] = h.astype(h_ref.dtype)
    if with_gates:
        hi = h.astype(BF16)
        lo = (h - hi.astype(F32)).astype(BF16)
        whi, wlo = wg_ref[0], wg_ref[1]
        nt = lambda a, b: lax.dot_general(a, b, (((1,), (1,)), ((), ())), preferred_element_type=F32)
        gate_ref[...] = nt(hi, whi) + nt(hi, wlo) + nt(lo, whi) + bg_ref[...]


def _norm_mod(xs, g, mod_all, layer, phase, w_gate=None, b_gate=None):
    with_gates = w_gate is not None
    n_tiles = NTOK // ROW_TILE
    x_specs, x_args = _stream_specs(xs, ROW_TILE)
    in_specs = x_specs + [
        pl.BlockSpec((1, D_MODEL), lambda i: (0, 0)),
        pl.BlockSpec((1, SUBLANES, D_MODEL), lambda i: (layer, 0, 3 * phase)),
        pl.BlockSpec((1, SUBLANES, D_MODEL), lambda i: (layer, 0, 3 * phase + 1)),
    ]
    args = x_args + [g.reshape(1, D_MODEL), mod_all, mod_all]
    out_specs = [pl.BlockSpec((ROW_TILE, D_MODEL), lambda i: (i, 0))]
    out_shape = [jax.ShapeDtypeStruct((NTOK, D_MODEL), BF16)]
    if with_gates:
        in_specs += [pl.BlockSpec((2, LANES, D_MODEL), lambda i: (0, 0, 0)),
                     pl.BlockSpec((1, LANES), lambda i: (0, 0))]
        args += [w_gate, b_gate]
        out_specs.append(pl.BlockSpec((ROW_TILE, LANES), lambda i: (i, 0)))
        out_shape.append(jax.ShapeDtypeStruct((NTOK, LANES), F32))
    return pl.pallas_call(
        functools.partial(_norm_body, modulate=True, with_gates=with_gates, n_stream=len(x_args)),
        grid=(n_tiles,),
        in_specs=in_specs,
        out_specs=out_specs,
        out_shape=out_shape,
        compiler_params=_cparams(("arbitrary",)),
        name="rmsnorm_modulate",
    )(*args)


def _final_norm(xs, g):
    skip = CTX // ROW_TILE
    return pl.pallas_call(
        functools.partial(_norm_body, modulate=False, with_gates=False),
        grid=(SEQ // ROW_TILE,),
        in_specs=[pl.BlockSpec((ROW_TILE, D_MODEL), lambda i: (i + skip, 0)),
                  pl.BlockSpec((1, D_MODEL), lambda i: (0, 0))],
        out_specs=[pl.BlockSpec((ROW_TILE, D_MODEL), lambda i: (i, 0))],
        out_shape=[jax.ShapeDtypeStruct((SEQ, D_MODEL), F32)],
        compiler_params=_cparams(("arbitrary",)),
        name="final_rmsnorm",
    )(xs, g.reshape(1, D_MODEL))[0]


def _in_proj_body(*refs, n_w):
    a_ref, wt_refs, b_refs = refs[0], refs[1:1 + n_w], refs[1 + n_w:1 + 2 * n_w]
    o_ref, wbf_ref = refs[1 + 2 * n_w:]
    tn = MM_TILE_N

    @pl.when(pl.program_id(1) == 0)
    def _():
        for k, wt_ref in enumerate(wt_refs):
            wbf_ref[:, k * tn:(k + 1) * tn] = jnp.transpose(wt_ref[...]).astype(BF16)

    bias = jnp.concatenate([b_ref[...] for b_ref in b_refs], axis=1)
    acc = jnp.dot(a_ref[...], wbf_ref[...], preferred_element_type=F32) + bias
    o_ref[...] = acc.astype(o_ref.dtype)


def _in_proj(h, w_in_t, b_in, layer, first_tile, n_steps, n_w, dtype):
    tm, tn = MM_TILE_M, MM_TILE_N
    in_w = w_in_t.shape[1]
    tile = lambda j, k: first_tile + n_w * j + k
    once = dict(pipeline_mode=pl.Buffered(1)) if n_steps == 1 else {}
    w_specs = [pl.BlockSpec((None, tn, D_MODEL), functools.partial(lambda j, i, k: (layer, tile(j, k), 0), k=k),
                            **once)
               for k in range(n_w)]
    b_specs = [pl.BlockSpec((None, 1, tn), functools.partial(lambda j, i, k: (layer, 0, tile(j, k)), k=k))
               for k in range(n_w)]
    b3 = b_in.reshape(DEPTH, 1, in_w)
    return pl.pallas_call(
        functools.partial(_in_proj_body, n_w=n_w),
        grid=(n_steps, NTOK // tm),
        in_specs=[pl.BlockSpec((tm, D_MODEL), lambda j, i: (i, 0))] + w_specs + b_specs,
        out_specs=pl.BlockSpec((tm, n_w * tn), lambda j, i: (i, j)),
        out_shape=jax.ShapeDtypeStruct((NTOK, n_steps * n_w * tn), dtype),
        scratch_shapes=[pltpu.VMEM((D_MODEL, n_w * tn), BF16)],
        compiler_params=_cparams(("arbitrary", "arbitrary")),
        name="in_proj",
    )(h, *([w_in_t] * n_w), *([b3] * n_w))


def _row_gate(g_ref, i, tm, tn):
    rows = i * tm + lax.broadcasted_iota(jnp.int32, (tm, tn), 0)
    return jnp.where(rows < CTX, g_ref[0, 1:2, :], g_ref[0, 0:1, :])


def _out_proj_body(y_ref, wglu_ref, bglu_ref, cv_ref, hf_ref, hb_ref, o_ref, mlg_ref, w_ref,
                   gate_ref, sh_ref, sc_ref, ng_ref, *rest):
    x_refs, (xo_ref, h_ref, wbf_ref, wglu_bf_ref) = rest[:-4], rest[-4:]
    i = pl.program_id(0)

    @pl.when(i == 0)
    def _():
        wbf_ref[...] = w_ref[...].astype(BF16)
        wglu_bf_ref[...] = wglu_ref[...].astype(BF16)

    g = jax.nn.gelu(y_ref[...])
    glu = jnp.dot(g.astype(BF16), wglu_bf_ref[...], preferred_element_type=F32) + bglu_ref[...]
    s5 = (g * jax.nn.sigmoid(glu)).astype(BF16)
    acc = jnp.dot(s5, wbf_ref[0:S5_W, :], preferred_element_type=F32)
    acc += jnp.dot(cv_ref[...], wbf_ref[S5_W:S5_W + CONV_W, :], preferred_element_type=F32)
    for head in range(ML_H):
        cols = slice(head * ML_DH, (head + 1) * ML_DH)
        hh = hf_ref[:, cols].astype(F32) + hb_ref[:, cols].astype(F32)
        hc = hh - jnp.mean(hh, axis=-1, keepdims=True)
        var = jnp.mean(hc * hc, axis=-1, keepdims=True)
        ml = jax.nn.sigmoid(o_ref[:, cols].astype(F32)) * (hc * lax.rsqrt(var + LN_EPS) * mlg_ref[:, cols])
        row0 = S5_W + CONV_W + head * ML_DH
        acc += jnp.dot(ml.astype(BF16), wbf_ref[row0:row0 + ML_DH, :], preferred_element_type=F32)
    is_ctx = i == 0
    xn = _stream_tile(x_refs) + _mod_row(gate_ref, is_ctx) * acc
    xo_ref[...] = xn
    ms = jnp.mean(xn * xn, axis=-1, keepdims=True)
    gain = ng_ref[...] * (1.0 + _mod_row(sc_ref, is_ctx))
    h_ref[...] = (xn * lax.rsqrt(ms + EPS) * gain + _mod_row(sh_ref, is_ctx)).astype(BF16)


def _out_proj(xs, y_s5, w_glu, b_glu, cvo, h_f, h_b, qkvo, ml_norm_g, w_out, mod_all, norm_g, layer):
    tm = ROW_TILE
    mod = lambda k: pl.BlockSpec((1, SUBLANES, D_MODEL), lambda i: (layer, 0, k))
    rows = lambda width, col=0: pl.BlockSpec((tm, width), lambda i: (i, col))
    const = lambda shape: pl.BlockSpec(shape, lambda i: (0,) * len(shape))
    x_specs, x_args = _stream_specs(xs, tm)
    return pl.pallas_call(
        _out_proj_body,
        grid=(NTOK // tm,),
        in_specs=[
            rows(S5_W), const((S5_W, S5_W)), const((1, S5_W)),
            rows(CONV_W),
            rows(ML_W), rows(ML_W), rows(ML_W, 3), const((1, ML_W)),
            pl.BlockSpec((None, D_MODEL, D_MODEL), lambda i: (layer, 0, 0), pipeline_mode=pl.Buffered(1)),
            mod(2), mod(3), mod(4),
            const((1, D_MODEL)),
        ] + x_specs,
        out_specs=[rows(D_MODEL), rows(D_MODEL)],
        out_shape=[jax.ShapeDtypeStruct((NTOK, D_MODEL), F32), jax.ShapeDtypeStruct((NTOK, D_MODEL), BF16)],
        scratch_shapes=[pltpu.VMEM((D_MODEL, D_MODEL), BF16), pltpu.VMEM((S5_W, S5_W), BF16)],
        compiler_params=_cparams(("arbitrary",)),
        name="out_proj_residual",
    )(y_s5, w_glu, b_glu.reshape(1, S5_W), cvo, h_f, h_b, qkvo, ml_norm_g.reshape(1, ML_W), w_out,
      mod_all, mod_all, mod_all, norm_g.reshape(1, D_MODEL), *x_args)


def _ffn_in_body(a_ref, wg_ref, wu_ref, o_ref, wgbf_ref, wubf_ref):
    @pl.when(pl.program_id(1) == 0)
    def _():
        wgbf_ref[...] = wg_ref[...].astype(BF16)
        wubf_ref[...] = wu_ref[...].astype(BF16)

    a = a_ref[...]
    g = jnp.dot(a, wgbf_ref[...], preferred_element_type=F32)
    u = jnp.dot(a, wubf_ref[...], preferred_element_type=F32)
    o_ref[...] = (jax.nn.silu(g) * u).astype(BF16)


def _ffn_in(h, w_ffn_in, layer):
    tm, tn = MM_TILE_M, MM_TILE_N
    nj = D_FF // tn
    return pl.pallas_call(
        _ffn_in_body,
        grid=(nj, NTOK // tm),
        in_specs=[
            pl.BlockSpec((tm, D_MODEL), lambda j, i: (i, 0)),
            pl.BlockSpec((None, D_MODEL, tn), lambda j, i: (layer, 0, j)),
            pl.BlockSpec((None, D_MODEL, tn), lambda j, i: (layer, 0, nj + j)),
        ],
        out_specs=pl.BlockSpec((tm, tn), lambda j, i: (i, j)),
        out_shape=jax.ShapeDtypeStruct((NTOK, D_FF), BF16),
        scratch_shapes=[pltpu.VMEM((D_MODEL, tn), BF16), pltpu.VMEM((D_MODEL, tn), BF16)],
        compiler_params=_cparams(("arbitrary", "arbitrary")),
        name="ffn_in_swiglu",
    )(h, w_ffn_in, w_ffn_in)


def _ffn_out_body(a_ref, w_ref, g_ref, x_ref, o_ref, wbf_ref):
    i = pl.program_id(1)

    @pl.when(i == 0)
    def _():
        wbf_ref[...] = w_ref[...].astype(BF16)

    acc = jnp.dot(a_ref[...], wbf_ref[...], preferred_element_type=F32)
    tm, tn = o_ref.shape
    o_ref[...] = x_ref[...] + _row_gate(g_ref, i, tm, tn) * acc


def _ffn_out(xs, hid, w_ffn_out, mod_all, layer):
    tm, tn = FFN_OUT_TILE_M, MM_TILE_N
    return pl.pallas_call(
        _ffn_out_body,
        grid=(D_MODEL // tn, NTOK // tm),
        in_specs=[
            pl.BlockSpec((tm, D_FF), lambda j, i: (i, 0)),
            pl.BlockSpec((None, D_FF, tn), lambda j, i: (layer, 0, j)),
            pl.BlockSpec((1, SUBLANES, tn), lambda j, i: (layer, 0, 5 * (D_MODEL // tn) + j)),
            pl.BlockSpec((tm, tn), lambda j, i: (i, j)),
        ],
        out_specs=pl.BlockSpec((tm, tn), lambda j, i: (i, j)),
        out_shape=jax.ShapeDtypeStruct((NTOK, D_MODEL), F32),
        scratch_shapes=[pltpu.VMEM((D_FF, tn), BF16)],
        compiler_params=_cparams(("arbitrary", "arbitrary")),
        name="ffn_out_residual",
    )(hid, w_ffn_out, mod_all, xs)


def _s5_selectors():
    tau = np.arange(S5_BLK)[:, None] // S5_CH
    ch_r = np.arange(S5_BLK)[:, None] % S5_CH
    t = np.arange(S5_BLK)[None, :] // S5_CH
    ch_c = np.arange(S5_BLK)[None, :] % S5_CH
    sel = np.zeros((2, S5_T, S5_BLK, S5_BLK), np.float32)
    for s in range(S5_T):
        sel[0, s] = (tau == t - s) & (ch_r == ch_c)
        sel[1, s] = (tau == s - t) & (ch_r == ch_c)
    rp = np.arange(S5_BLK)[:, None]
    col = np.arange(2 * S5_BLK)[None, :]
    pair_map = np.stack([(col == (rp // S5_P) * S5_BLK + gi * S5_P + rp % S5_P) for gi in range(2)])
    src = np.arange(S5_STEP_W)
    dst = ((src // S5_CH) % S5_STEP_G) * S5_BLK + (src // S5_SET_CH) * S5_CH + src % S5_CH
    perm = dst[:, None] == np.arange(S5_STEP_W)[None, :]
    return (jnp.asarray(sel), jnp.asarray(pair_map, BF16), jnp.asarray(perm, BF16),
            jnp.asarray(perm.T, BF16))


def _s5_operators(lam_re, lam_im, log_step, b_re, b_im, c_re, c_im, sel):
    lam = lax.complex(lam_re.astype(F32), lam_im.astype(F32))
    lam_bar = jnp.exp(lam * jnp.exp(log_step.astype(F32)))
    b_bar = ((lam_bar - 1.0) / lam)[..., None] * lax.complex(b_re.astype(F32), b_im.astype(F32))
    c_mat = lax.complex(c_re.astype(F32), c_im.astype(F32))
    depth = lam.shape[0]

    def powers(base, count):
        out = [jnp.ones_like(base)]
        for _ in range(count - 1):
            out.append(out[-1] * base)
        return out

    pw = powers(lam_bar, S5_T + 1)
    pa = powers(pw[S5_T], SUBLANES + 1)
    t_up = list(range(S5_T))

    def table(seq, fwd_idx, bwd_idx):
        return jnp.stack([jnp.stack([seq[f][:, 0], seq[b][:, 1]], axis=1)
                          for f, b in zip(fwd_idx, bwd_idx)])

    blocks = (depth, 2, S5_G, S5_BLK, S5_BLK)

    def token_rows(w, imag_sign):
        w = jnp.moveaxis(w, 0, 3)
        return jnp.concatenate([jnp.real(w), imag_sign * jnp.imag(w)], axis=-1).reshape(blocks)

    pb = table(pw, [S5_T - 1 - t for t in t_up], t_up)
    wb = token_rows(pb[:, :, :, :, None, :] * jnp.swapaxes(b_bar, -1, -2)[None], 1.0)
    pc = table(pw, [t + 1 for t in t_up], [S5_T - t for t in t_up])
    wct = token_rows(pc[:, :, :, :, None, :] * c_mat[None], -1.0)

    lag_c = token_rows(jnp.stack(pw[:S5_T])[:, :, :, :, None, :] * c_mat[None], -1.0)
    b_t = jnp.swapaxes(b_bar, -1, -2)
    b_ri = jnp.concatenate([jnp.real(b_t), jnp.imag(b_t)], axis=-1)
    kern = jnp.einsum('ldgar,ldgxr->ldgax', b_ri, lag_c, precision=lax.Precision.HIGHEST)
    kern = kern.reshape(depth, 2, S5_G * S5_CH, S5_BLK)
    kc = jnp.einsum('ldxk,dskn->ldsxn', kern, sel, precision=lax.Precision.HIGHEST)
    kc = kc.reshape(depth, 2, S5_T, S5_G, S5_CH, S5_BLK)
    kc = jnp.transpose(kc, (0, 1, 3, 2, 4, 5)).reshape(blocks)

    def lanes(v):
        parts = jnp.stack([jnp.real(v), jnp.imag(v)], axis=4)
        parts = parts.reshape(v.shape[0], depth, 2, S5_G // 2, 2, 2, S5_P)
        parts = jnp.swapaxes(parts, 4, 5).reshape(v.shape[0], depth, 2, S5_FLAT)
        return jnp.transpose(parts, (1, 2, 0, 3))

    zero_p = jnp.zeros_like(pa[0])
    amul = lanes(jnp.stack([pa[1], pa[2], pa[4], pa[8]] + [zero_p] * 4))
    apow = lanes(table(pa, t_up, t_up[::-1]))
    return kc.astype(BF16), wb.astype(BF16), wct.astype(BF16), amul, apow


def _s5_row_scan(e_ref, amul_ref, apow_ref, block_lo, block_hi, reverse, carry):
    rows = lax.broadcasted_iota(jnp.int32, (SUBLANES, S5_BLK), 0)

    def shifted(x, k):
        if reverse:
            return jnp.where(rows < SUBLANES - k, pltpu.roll(x, SUBLANES - k, 0), 0.0)
        return jnp.where(rows >= k, pltpu.roll(x, k, 0), 0.0)

    def body(step, carry):
        blk = (block_hi - 1 - step) if reverse else (block_lo + step)
        r0 = pl.multiple_of(blk * SUBLANES, SUBLANES)
        last = 0 if reverse else SUBLANES - 1
        carry_out = []
        for q in range(S5_STEP_PAIRS):
            re_c = slice(2 * q * S5_BLK, (2 * q + 1) * S5_BLK)
            im_c = slice((2 * q + 1) * S5_BLK, (2 * q + 2) * S5_BLK)
            s_re = e_ref[pl.ds(r0, SUBLANES), re_c]
            s_im = e_ref[pl.ds(r0, SUBLANES), im_c]
            for idx, k in enumerate((1, 2, 4)):
                a_re, a_im = amul_ref[idx:idx + 1, re_c], amul_ref[idx:idx + 1, im_c]
                t_re, t_im = shifted(s_re, k), shifted(s_im, k)
                s_re, s_im = s_re + t_re * a_re - t_im * a_im, s_im + t_re * a_im + t_im * a_re
            c_re, c_im = carry[:, re_c], carry[:, im_c]
            p_re, p_im = apow_ref[:, re_c], apow_ref[:, im_c]
            e_ref[pl.ds(r0, SUBLANES), re_c] = shifted(s_re, 1) + p_re * c_re - p_im * c_im
            e_ref[pl.ds(r0, SUBLANES), im_c] = shifted(s_im, 1) + p_re * c_im + p_im * c_re
            a_re, a_im = amul_ref[3:4, re_c], amul_ref[3:4, im_c]
            l_re = jnp.broadcast_to(s_re[last:last + 1, :], (SUBLANES, S5_BLK))
            l_im = jnp.broadcast_to(s_im[last:last + 1, :], (SUBLANES, S5_BLK))
            carry_out += [l_re + a_re * c_re - a_im * c_im, l_im + a_re * c_im + a_im * c_re]
        return jnp.concatenate(carry_out, axis=1)

    return lax.fori_loop(0, block_hi - block_lo, body, carry)


def _s5_pair_operators(kc_ref, wb_ref, wct_ref, map_ref, q):
    g0, g1 = 2 * q, 2 * q + 1
    zero = jnp.zeros((S5_BLK, S5_BLK), BF16)
    kc = jnp.concatenate([jnp.concatenate([kc_ref[g0], zero], axis=1),
                          jnp.concatenate([zero, kc_ref[g1]], axis=1)], axis=0)
    spread = lambda ref: jnp.concatenate(
        [jnp.dot(ref[g0], map_ref[0], preferred_element_type=F32),
         jnp.dot(ref[g1], map_ref[1], preferred_element_type=F32)], axis=0).astype(BF16)
    return kc, spread(wb_ref), spread(wct_ref)


def _s5_body(u_ref, p_ref, pt_ref, map_ref, kc_ref, wb_ref, wct_ref, amul_ref, apow_ref, d_ref, y_ref,
             up_ref, e_ref, acc_ref):
    direction = pl.program_id(1)
    token_rows = lambda t: pl.ds(t, S5_ROWS, stride=S5_T)
    pairs = [slice(2 * q * S5_BLK, 2 * (q + 1) * S5_BLK) for q in range(S5_STEP_PAIRS)]

    @pl.when(direction == 0)
    def _():
        u_nat = jnp.concatenate([u_ref[token_rows(t), :] for t in range(S5_T)], axis=1)
        up_ref[...] = jnp.dot(u_nat.astype(BF16), p_ref[...], preferred_element_type=F32).astype(BF16)
        acc_ref[...] = jnp.zeros_like(acc_ref)

    ops = [_s5_pair_operators(kc_ref, wb_ref, wct_ref, map_ref, q) for q in range(S5_STEP_PAIRS)]
    for (_, wb, _), cols in zip(ops, pairs):
        e_ref[:, cols] = jnp.dot(up_ref[:, cols], wb, preferred_element_type=F32)

    zero = jnp.zeros((SUBLANES, S5_STEP_W), F32)
    n_blocks = S5_ROWS // SUBLANES
    ctx_blocks = S5_CTX_ROWS // SUBLANES
    amul, apow = amul_ref, apow_ref

    @pl.when(direction == 0)
    def _():
        _s5_row_scan(e_ref, amul, apow, 0, n_blocks, False, zero)

    @pl.when(direction == 1)
    def _():
        carry = _s5_row_scan(e_ref, amul, apow, 0, ctx_blocks, True, zero)
        _s5_row_scan(e_ref, amul, apow, ctx_blocks, n_blocks, True, carry)

    for (kc, _, wct), cols in zip(ops, pairs):
        y = jnp.dot(up_ref[:, cols], kc, preferred_element_type=F32)
        y += lax.dot_general(e_ref[:, cols].astype(BF16), wct, (((1,), (1,)), ((), ())),
                             preferred_element_type=F32)
        acc_ref[:, cols] += y

    @pl.when(direction == 1)
    def _():
        acc = acc_ref[...]
        hi = acc.astype(BF16)
        lo = (acc - hi.astype(F32)).astype(BF16)
        y_nat = (jnp.dot(hi, pt_ref[...], preferred_element_type=F32)
                 + jnp.dot(lo, pt_ref[...], preferred_element_type=F32))
        for t in range(S5_T):
            y_ref[token_rows(t), :] = (y_nat[:, t * S5_SET_CH:(t + 1) * S5_SET_CH]
                                       + d_ref[...] * u_ref[token_rows(t), :])


def _s5_mixer(z, ops, consts, layer, d_skip):
    kc, wb, wct, amul, apow = ops
    _, pair_map, perm, perm_t = consts
    n_steps = S5_G // S5_STEP_G
    wspec = pl.BlockSpec((None, None, S5_STEP_G, S5_BLK, S5_BLK), lambda j, d: (layer, d, j, 0, 0))
    cspec = pl.BlockSpec((None, None, SUBLANES, S5_STEP_W), lambda j, d: (layer, d, 0, j))
    pspec = pl.BlockSpec((S5_STEP_W, S5_STEP_W), lambda j, d: (0, 0))
    return pl.pallas_call(
        _s5_body,
        grid=(n_steps, 2),
        in_specs=[
            pl.BlockSpec((NTOK, S5_SET_CH), lambda j, d: (0, j)),
            pspec, pspec,
            pl.BlockSpec((2, S5_BLK, 2 * S5_BLK), lambda j, d: (0, 0, 0)),
            wspec, wspec, wspec, cspec, cspec,
            pl.BlockSpec((1, S5_SET_CH), lambda j, d: (0, j)),
        ],
        out_specs=pl.BlockSpec((NTOK, S5_SET_CH), lambda j, d: (0, j)),
        out_shape=jax.ShapeDtypeStruct((NTOK, S5_W), F32),
        scratch_shapes=[pltpu.VMEM((S5_ROWS, S5_STEP_W), BF16), pltpu.VMEM((S5_ROWS, S5_STEP_W), F32),
                        pltpu.VMEM((S5_ROWS, S5_STEP_W), F32)],
        compiler_params=_cparams(("arbitrary", "arbitrary")),
        name="s5_scan",
    )(z, perm, perm_t, pair_map, kc, wb, wct, amul, apow, d_skip.astype(F32).reshape(1, S5_W))


CONV_PAD = 16
CONV_BLK = 64


def _conv_body(a_ref, b_ref, w_ref, db_ref, lg_ref, lb_ref, o_ref, pad_ref, sh_ref):
    g = a_ref[...] * jax.nn.sigmoid(b_ref[...])

    def run(seq_len):
        n_seq = ROW_TILE // seq_len
        pitch = seq_len + 2 * CONV_PAD
        zeros = jnp.zeros((CONV_PAD, CONV_W), F32)
        for s in range(n_seq):
            pad_ref[s * pitch:s * pitch + CONV_PAD, :] = zeros
            pad_ref[s * pitch + CONV_PAD:s * pitch + CONV_PAD + seq_len, :] = g[s * seq_len:(s + 1) * seq_len]
            pad_ref[s * pitch + CONV_PAD + seq_len:(s + 1) * pitch, :] = zeros
        used = n_seq * pitch
        for r in range(1, SUBLANES):
            sh_ref[r, 0:used - SUBLANES, :] = pad_ref[r:r + used - SUBLANES, :]
        for blk in range(ROW_TILE // CONV_BLK):
            row0 = blk * CONV_BLK
            s, q = divmod(row0, seq_len)
            base = s * pitch + CONV_PAD + q - CONV_K // 2
            acc = jnp.zeros((CONV_BLK, CONV_W), F32)
            for k in range(CONV_K):
                r = (base + k) % SUBLANES
                row = base + k - r
                tap = pad_ref[row:row + CONV_BLK, :] if r == 0 else sh_ref[r, row:row + CONV_BLK, :]
                acc = acc + w_ref[k:k + 1, :] * tap
            y = acc + db_ref[...]
            yc = y - jnp.mean(y, axis=-1, keepdims=True)
            var = jnp.mean(yc * yc, axis=-1, keepdims=True)
            y = yc * lax.rsqrt(var + LN_EPS) * lg_ref[...] + lb_ref[...]
            o_ref[row0:row0 + CONV_BLK, :] = jax.nn.silu(y).astype(BF16)

    is_ctx = pl.program_id(0) == 0

    @pl.when(is_ctx)
    def _():
        run(CTX)

    @pl.when(jnp.logical_not(is_ctx))
    def _():
        run(GRID_W)


def _conv_module(z, dw_w, dw_b, ln_g, ln_b):
    w = jnp.concatenate([dw_w, jnp.zeros((1, CONV_W), dw_w.dtype)], axis=0)
    vec = lambda v: v.reshape(1, CONV_W)
    pad_rows = max(CTX + 2 * CONV_PAD, (ROW_TILE // GRID_W) * (GRID_W + 2 * CONV_PAD))
    return pl.pallas_call(
        _conv_body,
        grid=(NTOK // ROW_TILE,),
        in_specs=[
            pl.BlockSpec((ROW_TILE, CONV_W), lambda i: (i, OFF_CONV // CONV_W)),
            pl.BlockSpec((ROW_TILE, CONV_W), lambda i: (i, OFF_CONV // CONV_W + 1)),
            pl.BlockSpec((CONV_K + 1, CONV_W), lambda i: (0, 0)),
            pl.BlockSpec((1, CONV_W), lambda i: (0, 0)),
            pl.BlockSpec((1, CONV_W), lambda i: (0, 0)),
            pl.BlockSpec((1, CONV_W), lambda i: (0, 0)),
        ],
        out_specs=pl.BlockSpec((ROW_TILE, CONV_W), lambda i: (i, 0)),
        out_shape=jax.ShapeDtypeStruct((NTOK, CONV_W), BF16),
        scratch_shapes=[pltpu.VMEM((pad_rows, CONV_W), F32), pltpu.VMEM((SUBLANES, pad_rows, CONV_W), F32)],
        compiler_params=_cparams(("arbitrary",)),
        name="conv_module",
    )(z, z, w, vec(dw_b), vec(ln_g), vec(ln_b))


def _ml_chunk_index(ci, reverse):
    if not reverse:
        return ci
    return jnp.where(ci == 0, 0, ML_NCHUNK - ci)


def _ml_chain_stages(qb, kb, vx, li_rep, li_row, b_rep, b_row, cx_ref, m_ref, mask, reverse, store_h):
    c = ML_CHUNK
    scale = ML_DH ** 0.5
    wide = lambda a: jnp.concatenate([a] * (ML_DH // LANES), axis=1)
    v = {}

    def scores():
        v['m'] = m_ref[...]
        v['d_log'] = jnp.where(mask, wide(b_rep) + (li_row - b_row), -jnp.inf)
        v['inter'] = b_rep + v['m']
        v['m_row'] = jnp.maximum(v['inter'], jnp.max(v['d_log'], axis=-1, keepdims=True))
        v['qk'] = lax.dot_general(qb, kb, (((1,), (1,)), ((), ())), preferred_element_type=F32)

    def numerator():
        s = v['qk'] * jnp.exp(v['d_log'] - wide(v['m_row']))
        w_inter = jnp.exp(v['inter'] - v['m_row']) * scale
        v['cx'] = cx_ref[...]
        lhs = jnp.concatenate([s.astype(BF16), qb * wide(w_inter.astype(BF16))], axis=1)
        rhs = jnp.concatenate([vx, v['cx'].astype(BF16)], axis=0)
        v['num'] = jnp.dot(lhs, rhs, preferred_element_type=F32)

    def output():
        num = v['num']
        den = num[:, ML_DH:]
        recip = 1.0 / jnp.maximum(jnp.abs(den), jnp.exp(-v['m_row']) * scale)
        store_h(num[:, :ML_DH] * wide(recip))

    def state():
        m = v['m']
        b_tot = b_row[:, 0:1] if reverse else b_row[:, c - 1:c]
        g = b_tot - b_rep + li_rep
        m_new = jnp.maximum(b_tot + m, jnp.max(g, axis=0, keepdims=True))
        kw = kb * wide((jnp.exp(g - m_new) * (1.0 / scale)).astype(BF16))
        decay = jnp.exp(b_tot + m - m_new)[:, 0:1]
        cx_ref[...] = decay * v['cx'] + lax.dot_general(kw, vx, (((0,), (0,)), ((), ())),
                                                       preferred_element_type=F32)
        m_ref[...] = m_new

    return [scores, numerator, output, state]


def _mlstm_body(qkvf_ref, gf_ref, gtf_ref, qkvb_ref, gb_ref, gtb_ref, hf_ref, hb_ref, cx_ref, m_ref):
    @pl.when(pl.program_id(0) == 0)
    def _():
        cx_ref[...] = jnp.zeros_like(cx_ref)
        m_ref[...] = jnp.zeros_like(m_ref)

    c = ML_CHUNK
    r_idx = lax.broadcasted_iota(jnp.int32, (c, c), 0)
    c_idx = lax.broadcasted_iota(jnp.int32, (c, c), 1)
    ones_col = jnp.ones((c, LANES), BF16)
    pick_r = lax.broadcasted_iota(jnp.int32, (2 * LANES, 2 * LANES), 0)
    pick_c = lax.broadcasted_iota(jnp.int32, (2 * LANES, 2 * LANES), 1)

    def replicate(parts, col_a, col_b):
        row = jnp.where(pick_c < LANES, col_a, LANES + col_b)
        pick = jnp.where(pick_r == row, 1.0, 0.0).astype(BF16)
        both = sum(jnp.dot(p, pick, preferred_element_type=F32) for p in parts)
        return both[:, :LANES], both[:, LANES:]

    chains = []
    for d, (qkv_ref, g_ref, gt_ref, h_ref) in enumerate(
            ((qkvf_ref, gf_ref, gtf_ref, hf_ref), (qkvb_ref, gb_ref, gtb_ref, hb_ref))):
        reverse = d == 1
        gates = g_ref[...]
        gates_t = gt_ref[...]
        incl = (c_idx >= r_idx) if reverse else (c_idx <= r_idx)
        tri = jnp.where(incl, 1.0, 0.0).astype(BF16)
        incl_t = (r_idx >= c_idx) if reverse else (r_idx <= c_idx)
        tri_t = jnp.where(incl_t, 1.0, 0.0).astype(BF16)
        lf = jax.nn.log_sigmoid(gates)
        lf_t = jax.nn.log_sigmoid(gates_t)
        b_all = sum(jnp.dot(tri, p, preferred_element_type=F32) for p in _split3(lf))
        b_all_t = sum(jnp.dot(p, tri_t, preferred_element_type=F32) for p in _split3(lf_t))
        gate_b_parts = _split3(jnp.concatenate([gates, b_all], axis=1))
        for head in range(ML_H):
            i_col = 2 * ML_H * d + head
            f_col = i_col + ML_H
            state = d * ML_H + head
            col = lambda part: slice((part * ML_H + head) * ML_DH, (part * ML_H + head + 1) * ML_DH)
            vx = jnp.concatenate([qkv_ref[:, col(2)], ones_col], axis=1)

            def store_h(h, h_ref=h_ref, head=head):
                h_ref[:, head * ML_DH:(head + 1) * ML_DH] = h.astype(h_ref.dtype)

            li_rep, b_rep = replicate(gate_b_parts, i_col, f_col)
            chains.append(_ml_chain_stages(
                qkv_ref[:, col(0)], qkv_ref[:, col(1)], vx,
                li_rep, gates_t[i_col:i_col + 1, :],
                b_rep, b_all_t[f_col:f_col + 1, :],
                cx_ref.at[state], m_ref.at[state], incl, reverse, store_h))
    for stage in zip(*chains):
        for run in stage:
            run()


def _mlstm_mixer(qkvo, gates):
    gates_t = jnp.transpose(gates[:, :2 * SUBLANES])
    c = ML_CHUNK

    def specs(reverse):
        row = lambda ci: _ml_chunk_index(ci, reverse)
        return [
            pl.BlockSpec((c, 3 * ML_W), lambda ci: (row(ci), 0)),
            pl.BlockSpec((c, LANES), lambda ci: (row(ci), 0)),
            pl.BlockSpec((2 * SUBLANES, c), lambda ci: (0, row(ci))),
        ]

    out_spec = lambda reverse: pl.BlockSpec((c, ML_W), lambda ci: (_ml_chunk_index(ci, reverse), 0))
    n_state = 2 * ML_H
    return pl.pallas_call(
        _mlstm_body,
        grid=(ML_NCHUNK,),
        in_specs=specs(False) + specs(True),
        out_specs=[out_spec(False), out_spec(True)],
        out_shape=[jax.ShapeDtypeStruct((NTOK, ML_W), BF16)] * 2,
        scratch_shapes=[pltpu.VMEM((n_state, ML_DH, ML_DH + LANES), F32),
                        pltpu.VMEM((n_state, 1, LANES), F32)],
        compiler_params=_cparams(("arbitrary",)),
        name="mlstm_chunks",
    )(qkvo, gates, gates_t, qkvo, gates, gates_t)


def kernel(x, c, ctx, c_ctx, w_mod, b_mod, norm1_g, w_in, b_in, s5_lam_re, s5_lam_im, s5_log_step,
           s5_b_re, s5_b_im, s5_c_re, s5_c_im, s5_d, s5_w_glu, s5_b_glu, conv_dw_w, conv_dw_b,
           conv_ln_g, conv_ln_b, ml_norm_g, w_out, norm2_g, w_ffn_in, w_ffn_out, norm_f_g):
    assert x.shape == (1, SEQ, D_MODEL) and ctx.shape == (1, CTX, D_MODEL)
    xs = (ctx[0].astype(F32), x[0].astype(F32))
    cc = jnp.zeros((SUBLANES, D_MODEL), F32).at[0].set(c[0]).at[1].set(c_ctx)
    mod_all = _modulation(cc, w_mod, b_mod)
    s5_consts = _s5_selectors()
    s5_ops = _s5_operators(s5_lam_re, s5_lam_im, s5_log_step, s5_b_re, s5_b_im, s5_c_re, s5_c_im,
                           s5_consts[0])
    w_in_t = jnp.swapaxes(w_in, 1, 2)
    n_gate = w_in.shape[2] - OFF_G
    w_gate = jnp.pad(w_in_t[:, OFF_G:, :], ((0, 0), (0, LANES - n_gate), (0, 0)))
    w_gate_hi = w_gate.astype(BF16)
    w_gate = jnp.stack([w_gate_hi, (w_gate - w_gate_hi.astype(F32)).astype(BF16)], axis=1)
    b_gate = jnp.pad(b_in[:, OFF_G:], ((0, 0), (0, LANES - n_gate))).reshape(DEPTH, 1, LANES)
    tn = MM_TILE_N

    h, gates = _norm_mod(xs, norm1_g[0], mod_all, 0, 0, w_gate[0], b_gate[0])
    for l in range(DEPTH):
        z = _in_proj(h, w_in_t, b_in, l, 0, 1, OFF_Q // tn, F32)
        qkvo = _in_proj(h, w_in_t, b_in, l, OFF_Q // tn, (OFF_G - OFF_Q) // (2 * tn), 2, BF16)
        y_s5 = _s5_mixer(z, s5_ops, s5_consts, l, s5_d[l])
        cvo = _conv_module(z, conv_dw_w[l], conv_dw_b[l], conv_ln_g[l], conv_ln_b[l])
        h_f, h_b = _mlstm_mixer(qkvo, gates)
        xs, h2 = _out_proj(xs, y_s5, s5_w_glu[l], s5_b_glu[l], cvo, h_f, h_b, qkvo, ml_norm_g[l], w_out,
                           mod_all, norm2_g[l], l)
        hid = _ffn_in(h2, w_ffn_in, l)
        xs = _ffn_out(xs, hid, w_ffn_out, mod_all, l)
        if l + 1 < DEPTH:
            h, gates = _norm_mod(xs, norm1_g[l + 1], mod_all, l + 1, 0, w_gate[l + 1], b_gate[l + 1])
    return _final_norm(xs, norm_f_g)[None]
```

```python
import functools

import numpy as np
import jax
import jax.numpy as jnp
from jax import lax
from jax.experimental import pallas as pl
from jax.experimental.pallas import tpu as pltpu

F32 = jnp.float32
BF16 = jnp.bfloat16

D_MODEL = 2048
SEQ = 8192
CTX = 256
NTOK = SEQ + CTX
DEPTH = 4
GRID_W = 64

S5_W = 512
S5_CH = 16
S5_G = 32
S5_P = 64
CONV_W = 512
CONV_K = 31
ML_W = 1024
ML_H = 4
ML_DH = 256
D_FF = 5632
EPS = 1e-6
LN_EPS = 1e-5

OFF_CONV = 512
OFF_Q = 1536
OFF_K = 2560
OFF_V = 3584
OFF_O = 4608
OFF_G = 5632

LANES = 128
SUBLANES = 8
VMEM_LIMIT = 56 * 1024 * 1024

ROW_TILE = 256
MM_TILE_M = 1056
MM_TILE_N = 512
IN_NORM_TILE_M = 528
FFN_OUT_TILE_M = 704

S5_T = 8
S5_ROWS = NTOK // S5_T
S5_CTX_ROWS = CTX // S5_T
S5_BLK = S5_T * S5_CH
S5_STEP_G = 8
S5_STEP_PAIRS = S5_STEP_G // 2
S5_SET_CH = S5_STEP_G * S5_CH
S5_STEP_W = S5_STEP_G * S5_BLK
S5_FLAT = S5_G * S5_BLK

ML_CHUNK = 256
ML_NCHUNK = NTOK // ML_CHUNK


def _cparams(sem, vmem=VMEM_LIMIT):
    return pltpu.CompilerParams(dimension_semantics=sem, vmem_limit_bytes=vmem)


def _dot(a, b):
    return jnp.dot(a.astype(BF16), b.astype(BF16), preferred_element_type=F32)


def _split3(x):
    a = x.astype(BF16)
    r = x - a.astype(F32)
    b = r.astype(BF16)
    c = (r - b.astype(F32)).astype(BF16)
    return a, b, c


def _mod_body(cc_ref, w_ref, b_ref, o_ref):
    s = jax.nn.silu(cc_ref[...])
    o_ref[0] = _dot(s, w_ref[0]) + b_ref[0]


def _modulation(cc, w_mod, b_mod):
    depth, _, n = w_mod.shape
    tn = 1024
    return pl.pallas_call(
        _mod_body,
        grid=(depth, n // tn),
        in_specs=[
            pl.BlockSpec((SUBLANES, D_MODEL), lambda l, j: (0, 0)),
            pl.BlockSpec((1, D_MODEL, tn), lambda l, j: (l, 0, j)),
            pl.BlockSpec((1, 1, tn), lambda l, j: (l, 0, j)),
        ],
        out_specs=pl.BlockSpec((1, SUBLANES, tn), lambda l, j: (l, 0, j)),
        out_shape=jax.ShapeDtypeStruct((depth, SUBLANES, n), F32),
        compiler_params=_cparams(("arbitrary", "arbitrary")),
        name="adaln_modulation",
    )(cc, w_mod, b_mod.reshape(depth, 1, n))


def _mod_row(m_ref, is_ctx):
    return jnp.where(is_ctx, m_ref[0, 1:2, :], m_ref[0, 0:1, :])


def _stream_specs(xs, tm):
    if isinstance(xs, tuple):
        assert tm == CTX
        return [pl.BlockSpec((tm, D_MODEL), lambda i: (0, 0)),
                pl.BlockSpec((tm, D_MODEL), lambda i: (jnp.maximum(i - 1, 0), 0))], list(xs)
    return [pl.BlockSpec((tm, D_MODEL), lambda i: (i, 0))], [xs]


def _stream_tile(x_refs):
    if len(x_refs) == 1:
        return x_refs[0][...]
    return jnp.where(pl.program_id(0) == 0, x_refs[0][...], x_refs[1][...])


def _norm_body(*refs, modulate, with_gates, n_stream=1):
    it = iter(refs)
    x_refs = [next(it) for _ in range(n_stream)]
    g_ref = next(it)
    sh_ref = sc_ref = wg_ref = bg_ref = gate_ref = None
    if modulate:
        sh_ref, sc_ref = next(it), next(it)
    if with_gates:
        wg_ref, bg_ref = next(it), next(it)
    h_ref = next(it)
    if with_gates:
        gate_ref = next(it)

    xf = _stream_tile(x_refs)
    ms = jnp.mean(xf * xf, axis=-1, keepdims=True)
    if modulate:
        is_ctx = pl.program_id(0) == 0
        gain = g_ref[...] * (1.0 + _mod_row(sc_ref, is_ctx))
        h = xf * lax.rsqrt(ms + EPS) * gain + _mod_row(sh_ref, is_ctx)
    else:
        h = xf * lax.rsqrt(ms + EPS) * g_ref[...]
    h_ref[...] = h.astype(h_ref.dtype)
    if with_gates:
        hi = h.astype(BF16)
        lo = (h - hi.astype(F32)).astype(BF16)
        whi, wlo = wg_ref[0], wg_ref[1]
        nt = lambda a, b: lax.dot_general(a, b, (((1,), (1,)), ((), ())), preferred_element_type=F32)
        gate_ref[...] = nt(hi, whi) + nt(hi, wlo) + nt(lo, whi) + bg_ref[...]


def _norm_mod(xs, g, mod_all, layer, phase, w_gate=None, b_gate=None):
    with_gates = w_gate is not None
    n_tiles = NTOK // ROW_TILE
    x_specs, x_args = _stream_specs(xs, ROW_TILE)
    in_specs = x_specs + [
        pl.BlockSpec((1, D_MODEL), lambda i: (0, 0)),
        pl.BlockSpec((1, SUBLANES, D_MODEL), lambda i: (layer, 0, 3 * phase)),
        pl.BlockSpec((1, SUBLANES, D_MODEL), lambda i: (layer, 0, 3 * phase + 1)),
    ]
    args = x_args + [g.reshape(1, D_MODEL), mod_all, mod_all]
    out_specs = [pl.BlockSpec((ROW_TILE, D_MODEL), lambda i: (i, 0))]
    out_shape = [jax.ShapeDtypeStruct((NTOK, D_MODEL), BF16)]
    if with_gates:
        in_specs += [pl.BlockSpec((2, LANES, D_MODEL), lambda i: (0, 0, 0)),
                     pl.BlockSpec((1, LANES), lambda i: (0, 0))]
        args += [w_gate, b_gate]
        out_specs.append(pl.BlockSpec((ROW_TILE, LANES), lambda i: (i, 0)))
        out_shape.append(jax.ShapeDtypeStruct((NTOK, LANES), F32))
    return pl.pallas_call(
        functools.partial(_norm_body, modulate=True, with_gates=with_gates, n_stream=len(x_args)),
        grid=(n_tiles,),
        in_specs=in_specs,
        out_specs=out_specs,
        out_shape=out_shape,
        compiler_params=_cparams(("arbitrary",)),
        name="rmsnorm_modulate",
    )(*args)


def _final_norm(xs, g):
    skip = CTX // ROW_TILE
    return pl.pallas_call(
        functools.partial(_norm_body, modulate=False, with_gates=False),
        grid=(SEQ // ROW_TILE,),
        in_specs=[pl.BlockSpec((ROW_TILE, D_MODEL), lambda i: (i + skip, 0)),
                  pl.BlockSpec((1, D_MODEL), lambda i: (0, 0))],
        out_specs=[pl.BlockSpec((ROW_TILE, D_MODEL), lambda i: (i, 0))],
        out_shape=[jax.ShapeDtypeStruct((SEQ, D_MODEL), F32)],
        compiler_params=_cparams(("arbitrary",)),
        name="final_rmsnorm",
    )(xs, g.reshape(1, D_MODEL))[0]


def _in_proj_body(*refs, n_w):
    a_ref, wt_refs, b_refs = refs[0], refs[1:1 + n_w], refs[1 + n_w:1 + 2 * n_w]
    o_ref, wbf_ref = refs[1 + 2 * n_w:]
    tn = MM_TILE_N

    @pl.when(pl.program_id(1) == 0)
    def _():
        for k, wt_ref in enumerate(wt_refs):
            wbf_ref[:, k * tn:(k + 1) * tn] = jnp.transpose(wt_ref[...]).astype(BF16)

    bias = jnp.concatenate([b_ref[...] for b_ref in b_refs], axis=1)
    acc = jnp.dot(a_ref[...], wbf_ref[...], preferred_element_type=F32) + bias
    o_ref[...] = acc.astype(o_ref.dtype)


def _in_proj(h, w_in_t, b_in, layer, first_tile, n_steps, n_w, dtype):
    tm, tn = MM_TILE_M, MM_TILE_N
    in_w = w_in_t.shape[1]
    tile = lambda j, k: first_tile + n_w * j + k
    once = dict(pipeline_mode=pl.Buffered(1)) if n_steps == 1 else {}
    w_specs = [pl.BlockSpec((None, tn, D_MODEL), functools.partial(lambda j, i, k: (layer, tile(j, k), 0), k=k),
                            **once)
               for k in range(n_w)]
    b_specs = [pl.BlockSpec((None, 1, tn), functools.partial(lambda j, i, k: (layer, 0, tile(j, k)), k=k))
               for k in range(n_w)]
    b3 = b_in.reshape(DEPTH, 1, in_w)
    return pl.pallas_call(
        functools.partial(_in_proj_body, n_w=n_w),
        grid=(n_steps, NTOK // tm),
        in_specs=[pl.BlockSpec((tm, D_MODEL), lambda j, i: (i, 0))] + w_specs + b_specs,
        out_specs=pl.BlockSpec((tm, n_w * tn), lambda j, i: (i, j)),
        out_shape=jax.ShapeDtypeStruct((NTOK, n_steps * n_w * tn), dtype),
        scratch_shapes=[pltpu.VMEM((D_MODEL, n_w * tn), BF16)],
        compiler_params=_cparams(("arbitrary", "arbitrary")),
        name="in_proj",
    )(h, *([w_in_t] * n_w), *([b3] * n_w))


def _in_proj_norm_body(x_ref, g_ref, sh_ref, sc_ref, wg_ref, bg_ref, *rest, n_w):
    wt_refs, b_refs = rest[:n_w], rest[n_w:2 * n_w]
    o_ref, h_ref, gate_ref, wbf_ref = rest[2 * n_w:]
    tn = MM_TILE_N
    n_main = n_w * tn
    i = pl.program_id(0)

    @pl.when(i == 0)
    def _():
        for k, wt_ref in enumerate(wt_refs):
            wbf_ref[:, k * tn:(k + 1) * tn] = jnp.transpose(wt_ref[...]).astype(BF16)
        wbf_ref[:, n_main:n_main + LANES] = wg_ref[0]
        wbf_ref[:, n_main + LANES:] = wg_ref[1]

    xf = x_ref[...]
    tm = xf.shape[0]
    is_ctx = i * tm + lax.broadcasted_iota(jnp.int32, (tm, 1), 0) < CTX
    pick = lambda ref: jnp.where(is_ctx, ref[0, 1:2, :], ref[0, 0:1, :])
    ms = jnp.mean(xf * xf, axis=-1, keepdims=True)
    h = xf * lax.rsqrt(ms + EPS) * (g_ref[...] * (1.0 + pick(sc_ref))) + pick(sh_ref)
    hi = h.astype(BF16)
    h_ref[...] = hi
    lo = (h - hi.astype(F32)).astype(BF16)
    acc = jnp.dot(hi, wbf_ref[...], preferred_element_type=F32)
    gate_ref[...] = (acc[:, n_main:n_main + LANES] + acc[:, n_main + LANES:]
                     + jnp.dot(lo, wg_ref[0], preferred_element_type=F32) + bg_ref[...])
    bias = jnp.concatenate([b_ref[...] for b_ref in b_refs], axis=1)
    o_ref[...] = acc[:, :n_main] + bias


def _in_proj_norm(xs, g, mod_all, layer, w_gate_t, b_gate, w_in_t, b_in, n_w):
    tm, tn = IN_NORM_TILE_M, MM_TILE_N
    in_w = w_in_t.shape[1]
    once = dict(pipeline_mode=pl.Buffered(1))
    w_specs = [pl.BlockSpec((None, tn, D_MODEL), functools.partial(lambda i, k: (layer, k, 0), k=k), **once)
               for k in range(n_w)]
    b_specs = [pl.BlockSpec((None, 1, tn), functools.partial(lambda i, k: (layer, 0, k), k=k))
               for k in range(n_w)]
    b3 = b_in.reshape(DEPTH, 1, in_w)
    mod = lambda k: pl.BlockSpec((1, SUBLANES, D_MODEL), lambda i: (layer, 0, k))
    return pl.pallas_call(
        functools.partial(_in_proj_norm_body, n_w=n_w),
        grid=(NTOK // tm,),
        in_specs=[
            pl.BlockSpec((tm, D_MODEL), lambda i: (i, 0)),
            pl.BlockSpec((1, D_MODEL), lambda i: (0, 0)),
            mod(0), mod(1),
            pl.BlockSpec((2, D_MODEL, LANES), lambda i: (0, 0, 0)),
            pl.BlockSpec((1, LANES), lambda i: (0, 0)),
        ] + w_specs + b_specs,
        out_specs=[pl.BlockSpec((tm, n_w * tn), lambda i: (i, 0)),
                   pl.BlockSpec((tm, D_MODEL), lambda i: (i, 0)),
                   pl.BlockSpec((tm, LANES), lambda i: (i, 0))],
        out_shape=[jax.ShapeDtypeStruct((NTOK, n_w * tn), F32),
                   jax.ShapeDtypeStruct((NTOK, D_MODEL), BF16),
                   jax.ShapeDtypeStruct((NTOK, LANES), F32)],
        scratch_shapes=[pltpu.VMEM((D_MODEL, n_w * tn + 2 * LANES), BF16)],
        compiler_params=_cparams(("arbitrary",)),
        name="norm_in_proj",
    )(xs, g.reshape(1, D_MODEL), mod_all, mod_all, w_gate_t, b_gate, *([w_in_t] * n_w), *([b3] * n_w))


def _row_gate(g_ref, i, tm, tn):
    rows = i * tm + lax.broadcasted_iota(jnp.int32, (tm, tn), 0)
    return jnp.where(rows < CTX, g_ref[0, 1:2, :], g_ref[0, 0:1, :])


def _out_proj_body(y_ref, wglu_ref, bglu_ref, cv_ref, hf_ref, hb_ref, o_ref, mlg_ref, w_ref,
                   gate_ref, sh_ref, sc_ref, ng_ref, *rest):
    x_refs, (xo_ref, h_ref, wbf_ref, wglu_bf_ref) = rest[:-4], rest[-4:]
    i = pl.program_id(0)

    @pl.when(i == 0)
    def _():
        wbf_ref[...] = w_ref[...].astype(BF16)
        wglu_bf_ref[...] = wglu_ref[...].astype(BF16)

    g = jax.nn.gelu(y_ref[...])
    glu = jnp.dot(g.astype(BF16), wglu_bf_ref[...], preferred_element_type=F32) + bglu_ref[...]
    s5 = (g * jax.nn.sigmoid(glu)).astype(BF16)
    acc = jnp.dot(s5, wbf_ref[0:S5_W, :], preferred_element_type=F32)
    acc += jnp.dot(cv_ref[...], wbf_ref[S5_W:S5_W + CONV_W, :], preferred_element_type=F32)
    for head in range(ML_H):
        cols = slice(head * ML_DH, (head + 1) * ML_DH)
        hh = hf_ref[:, cols].astype(F32) + hb_ref[:, cols].astype(F32)
        hc = hh - jnp.mean(hh, axis=-1, keepdims=True)
        var = jnp.mean(hc * hc, axis=-1, keepdims=True)
        ml = jax.nn.sigmoid(o_ref[:, cols].astype(F32)) * (hc * lax.rsqrt(var + LN_EPS) * mlg_ref[:, cols])
        row0 = S5_W + CONV_W + head * ML_DH
        acc += jnp.dot(ml.astype(BF16), wbf_ref[row0:row0 + ML_DH, :], preferred_element_type=F32)
    is_ctx = i == 0
    xn = _stream_tile(x_refs) + _mod_row(gate_ref, is_ctx) * acc
    xo_ref[...] = xn
    ms = jnp.mean(xn * xn, axis=-1, keepdims=True)
    gain = ng_ref[...] * (1.0 + _mod_row(sc_ref, is_ctx))
    h_ref[...] = (xn * lax.rsqrt(ms + EPS) * gain + _mod_row(sh_ref, is_ctx)).astype(BF16)


def _out_proj(xs, y_s5, w_glu, b_glu, cvo, h_f, h_b, qkvo, ml_norm_g, w_out, mod_all, norm_g, layer):
    tm = ROW_TILE
    mod = lambda k: pl.BlockSpec((1, SUBLANES, D_MODEL), lambda i: (layer, 0, k))
    rows = lambda width, col=0: pl.BlockSpec((tm, width), lambda i: (i, col))
    const = lambda shape: pl.BlockSpec(shape, lambda i: (0,) * len(shape))
    x_specs, x_args = _stream_specs(xs, tm)
    return pl.pallas_call(
        _out_proj_body,
        grid=(NTOK // tm,),
        in_specs=[
            rows(S5_W), const((S5_W, S5_W)), const((1, S5_W)),
            rows(CONV_W),
            rows(ML_W), rows(ML_W), rows(ML_W, 3), const((1, ML_W)),
            pl.BlockSpec((None, D_MODEL, D_MODEL), lambda i: (layer, 0, 0), pipeline_mode=pl.Buffered(1)),
            mod(2), mod(3), mod(4),
            const((1, D_MODEL)),
        ] + x_specs,
        out_specs=[rows(D_MODEL), rows(D_MODEL)],
        out_shape=[jax.ShapeDtypeStruct((NTOK, D_MODEL), F32), jax.ShapeDtypeStruct((NTOK, D_MODEL), BF16)],
        scratch_shapes=[pltpu.VMEM((D_MODEL, D_MODEL), BF16), pltpu.VMEM((S5_W, S5_W), BF16)],
        compiler_params=_cparams(("arbitrary",)),
        name="out_proj_residual",
    )(y_s5, w_glu, b_glu.reshape(1, S5_W), cvo, h_f, h_b, qkvo, ml_norm_g.reshape(1, ML_W), w_out,
      mod_all, mod_all, mod_all, norm_g.reshape(1, D_MODEL), *x_args)


def _ffn_in_body(a_ref, wg_ref, wu_ref, o_ref, wgbf_ref, wubf_ref):
    @pl.when(pl.program_id(1) == 0)
    def _():
        wgbf_ref[...] = wg_ref[...].astype(BF16)
        wubf_ref[...] = wu_ref[...].astype(BF16)

    a = a_ref[...]
    g = jnp.dot(a, wgbf_ref[...], preferred_element_type=F32)
    u = jnp.dot(a, wubf_ref[...], preferred_element_type=F32)
    o_ref[...] = (jax.nn.silu(g) * u).astype(BF16)


def _ffn_in(h, w_ffn_in, layer):
    tm, tn = MM_TILE_M, MM_TILE_N
    nj = D_FF // tn
    return pl.pallas_call(
        _ffn_in_body,
        grid=(nj, NTOK // tm),
        in_specs=[
            pl.BlockSpec((tm, D_MODEL), lambda j, i: (i, 0)),
            pl.BlockSpec((None, D_MODEL, tn), lambda j, i: (layer, 0, j)),
            pl.BlockSpec((None, D_MODEL, tn), lambda j, i: (layer, 0, nj + j)),
        ],
        out_specs=pl.BlockSpec((tm, tn), lambda j, i: (i, j)),
        out_shape=jax.ShapeDtypeStruct((NTOK, D_FF), BF16),
        scratch_shapes=[pltpu.VMEM((D_MODEL, tn), BF16), pltpu.VMEM((D_MODEL, tn), BF16)],
        compiler_params=_cparams(("arbitrary", "arbitrary")),
        name="ffn_in_swiglu",
    )(h, w_ffn_in, w_ffn_in)


def _ffn_out_body(a_ref, w_ref, g_ref, x_ref, o_ref, wbf_ref):
    i = pl.program_id(1)

    @pl.when(i == 0)
    def _():
        wbf_ref[...] = w_ref[...].astype(BF16)

    acc = jnp.dot(a_ref[...], wbf_ref[...], preferred_element_type=F32)
    tm, tn = o_ref.shape
    o_ref[...] = x_ref[...] + _row_gate(g_ref, i, tm, tn) * acc


def _ffn_out(xs, hid, w_ffn_out, mod_all, layer):
    tm, tn = FFN_OUT_TILE_M, MM_TILE_N
    return pl.pallas_call(
        _ffn_out_body,
        grid=(D_MODEL // tn, NTOK // tm),
        in_specs=[
            pl.BlockSpec((tm, D_FF), lambda j, i: (i, 0)),
            pl.BlockSpec((None, D_FF, tn), lambda j, i: (layer, 0, j)),
            pl.BlockSpec((1, SUBLANES, tn), lambda j, i: (layer, 0, 5 * (D_MODEL // tn) + j)),
            pl.BlockSpec((tm, tn), lambda j, i: (i, j)),
        ],
        out_specs=pl.BlockSpec((tm, tn), lambda j, i: (i, j)),
        out_shape=jax.ShapeDtypeStruct((NTOK, D_MODEL), F32),
        scratch_shapes=[pltpu.VMEM((D_FF, tn), BF16)],
        compiler_params=_cparams(("arbitrary", "arbitrary")),
        name="ffn_out_residual",
    )(hid, w_ffn_out, mod_all, xs)


def _s5_selectors():
    tau = np.arange(S5_BLK)[:, None] // S5_CH
    ch_r = np.arange(S5_BLK)[:, None] % S5_CH
    t = np.arange(S5_BLK)[None, :] // S5_CH
    ch_c = np.arange(S5_BLK)[None, :] % S5_CH
    sel = np.zeros((2, S5_T, S5_BLK, S5_BLK), np.float32)
    for s in range(S5_T):
        sel[0, s] = (tau == t - s) & (ch_r == ch_c)
        sel[1, s] = (tau == s - t) & (ch_r == ch_c)
    rp = np.arange(S5_BLK)[:, None]
    col = np.arange(2 * S5_BLK)[None, :]
    pair_map = np.stack([(col == (rp // S5_P) * S5_BLK + gi * S5_P + rp % S5_P) for gi in range(2)])
    src = np.arange(S5_STEP_W)
    dst = ((src // S5_CH) % S5_STEP_G) * S5_BLK + (src // S5_SET_CH) * S5_CH + src % S5_CH
    perm = dst[:, None] == np.arange(S5_STEP_W)[None, :]
    return (jnp.asarray(sel), jnp.asarray(pair_map, BF16), jnp.asarray(perm, BF16),
            jnp.asarray(perm.T, BF16))


def _s5_operators(lam_re, lam_im, log_step, b_re, b_im, c_re, c_im, sel):
    lam = lax.complex(lam_re.astype(F32), lam_im.astype(F32))
    lam_bar = jnp.exp(lam * jnp.exp(log_step.astype(F32)))
    b_bar = ((lam_bar - 1.0) / lam)[..., None] * lax.complex(b_re.astype(F32), b_im.astype(F32))
    c_mat = lax.complex(c_re.astype(F32), c_im.astype(F32))
    depth = lam.shape[0]

    def powers(base, count):
        out = [jnp.ones_like(base)]
        for _ in range(count - 1):
            out.append(out[-1] * base)
        return out

    pw = powers(lam_bar, S5_T + 1)
    pa = powers(pw[S5_T], SUBLANES + 1)
    t_up = list(range(S5_T))

    def table(seq, fwd_idx, bwd_idx):
        return jnp.stack([jnp.stack([seq[f][:, 0], seq[b][:, 1]], axis=1)
                          for f, b in zip(fwd_idx, bwd_idx)])

    blocks = (depth, 2, S5_G, S5_BLK, S5_BLK)

    def token_rows(w, imag_sign):
        w = jnp.moveaxis(w, 0, 3)
        return jnp.concatenate([jnp.real(w), imag_sign * jnp.imag(w)], axis=-1).reshape(blocks)

    pb = table(pw, [S5_T - 1 - t for t in t_up], t_up)
    wb = token_rows(pb[:, :, :, :, None, :] * jnp.swapaxes(b_bar, -1, -2)[None], 1.0)
    pc = table(pw, [t + 1 for t in t_up], [S5_T - t for t in t_up])
    wct = token_rows(pc[:, :, :, :, None, :] * c_mat[None], -1.0)

    lag_c = token_rows(jnp.stack(pw[:S5_T])[:, :, :, :, None, :] * c_mat[None], -1.0)
    b_t = jnp.swapaxes(b_bar, -1, -2)
    b_ri = jnp.concatenate([jnp.real(b_t), jnp.imag(b_t)], axis=-1)
    kern = jnp.einsum('ldgar,ldgxr->ldgax', b_ri, lag_c, precision=lax.Precision.HIGHEST)
    kern = kern.reshape(depth, 2, S5_G * S5_CH, S5_BLK)
    kc = jnp.einsum('ldxk,dskn->ldsxn', kern, sel, precision=lax.Precision.HIGHEST)
    kc = kc.reshape(depth, 2, S5_T, S5_G, S5_CH, S5_BLK)
    kc = jnp.transpose(kc, (0, 1, 3, 2, 4, 5)).reshape(blocks)

    def lanes(v):
        parts = jnp.stack([jnp.real(v), jnp.imag(v)], axis=4)
        parts = parts.reshape(v.shape[0], depth, 2, S5_G // 2, 2, 2, S5_P)
        parts = jnp.swapaxes(parts, 4, 5).reshape(v.shape[0], depth, 2, S5_FLAT)
        return jnp.transpose(parts, (1, 2, 0, 3))

    zero_p = jnp.zeros_like(pa[0])
    amul = lanes(jnp.stack([pa[1], pa[2], pa[4], pa[8]] + [zero_p] * 4))
    apow = lanes(table(pa, t_up, t_up[::-1]))
    return kc.astype(BF16), wb.astype(BF16), wct.astype(BF16), amul, apow


def _s5_row_scan(e_ref, amul_ref, apow_ref, block_lo, block_hi, reverse, carry):
    rows = lax.broadcasted_iota(jnp.int32, (SUBLANES, S5_BLK), 0)

    def shifted(x, k):
        if reverse:
            return jnp.where(rows < SUBLANES - k, pltpu.roll(x, SUBLANES - k, 0), 0.0)
        return jnp.where(rows >= k, pltpu.roll(x, k, 0), 0.0)

    def body(step, carry):
        blk = (block_hi - 1 - step) if reverse else (block_lo + step)
        r0 = pl.multiple_of(blk * SUBLANES, SUBLANES)
        last = 0 if reverse else SUBLANES - 1
        carry_out = []
        for q in range(S5_STEP_PAIRS):
            re_c = slice(2 * q * S5_BLK, (2 * q + 1) * S5_BLK)
            im_c = slice((2 * q + 1) * S5_BLK, (2 * q + 2) * S5_BLK)
            s_re = e_ref[pl.ds(r0, SUBLANES), re_c]
            s_im = e_ref[pl.ds(r0, SUBLANES), im_c]
            for idx, k in enumerate((1, 2, 4)):
                a_re, a_im = amul_ref[idx:idx + 1, re_c], amul_ref[idx:idx + 1, im_c]
                t_re, t_im = shifted(s_re, k), shifted(s_im, k)
                s_re, s_im = s_re + t_re * a_re - t_im * a_im, s_im + t_re * a_im + t_im * a_re
            c_re, c_im = carry[:, re_c], carry[:, im_c]
            p_re, p_im = apow_ref[:, re_c], apow_ref[:, im_c]
            e_ref[pl.ds(r0, SUBLANES), re_c] = shifted(s_re, 1) + p_re * c_re - p_im * c_im
            e_ref[pl.ds(r0, SUBLANES), im_c] = shifted(s_im, 1) + p_re * c_im + p_im * c_re
            a_re, a_im = amul_ref[3:4, re_c], amul_ref[3:4, im_c]
            l_re = jnp.broadcast_to(s_re[last:last + 1, :], (SUBLANES, S5_BLK))
            l_im = jnp.broadcast_to(s_im[last:last + 1, :], (SUBLANES, S5_BLK))
            carry_out += [l_re + a_re * c_re - a_im * c_im, l_im + a_re * c_im + a_im * c_re]
        return jnp.concatenate(carry_out, axis=1)

    return lax.fori_loop(0, block_hi - block_lo, body, carry)


def _s5_pair_operators(kc_ref, wb_ref, wct_ref, map_ref, q):
    g0, g1 = 2 * q, 2 * q + 1
    zero = jnp.zeros((S5_BLK, S5_BLK), BF16)
    kc = jnp.concatenate([jnp.concatenate([kc_ref[g0], zero], axis=1),
                          jnp.concatenate([zero, kc_ref[g1]], axis=1)], axis=0)
    spread = lambda ref: jnp.concatenate(
        [jnp.dot(ref[g0], map_ref[0], preferred_element_type=F32),
         jnp.dot(ref[g1], map_ref[1], preferred_element_type=F32)], axis=0).astype(BF16)
    return kc, spread(wb_ref), spread(wct_ref)


def _s5_body(u_ref, p_ref, pt_ref, map_ref, kc_ref, wb_ref, wct_ref, amul_ref, apow_ref, d_ref, y_ref,
             up_ref, e_ref, acc_ref):
    direction = pl.program_id(1)
    token_rows = lambda t: pl.ds(t, S5_ROWS, stride=S5_T)
    pairs = [slice(2 * q * S5_BLK, 2 * (q + 1) * S5_BLK) for q in range(S5_STEP_PAIRS)]

    @pl.when(direction == 0)
    def _():
        u_nat = jnp.concatenate([u_ref[token_rows(t), :] for t in range(S5_T)], axis=1)
        up_ref[...] = jnp.dot(u_nat.astype(BF16), p_ref[...], preferred_element_type=F32).astype(BF16)
        acc_ref[...] = jnp.zeros_like(acc_ref)

    ops = [_s5_pair_operators(kc_ref, wb_ref, wct_ref, map_ref, q) for q in range(S5_STEP_PAIRS)]
    for (_, wb, _), cols in zip(ops, pairs):
        e_ref[:, cols] = jnp.dot(up_ref[:, cols], wb, preferred_element_type=F32)

    zero = jnp.zeros((SUBLANES, S5_STEP_W), F32)
    n_blocks = S5_ROWS // SUBLANES
    ctx_blocks = S5_CTX_ROWS // SUBLANES
    amul, apow = amul_ref, apow_ref

    @pl.when(direction == 0)
    def _():
        _s5_row_scan(e_ref, amul, apow, 0, n_blocks, False, zero)

    @pl.when(direction == 1)
    def _():
        carry = _s5_row_scan(e_ref, amul, apow, 0, ctx_blocks, True, zero)
        _s5_row_scan(e_ref, amul, apow, ctx_blocks, n_blocks, True, carry)

    for (kc, _, wct), cols in zip(ops, pairs):
        y = jnp.dot(up_ref[:, cols], kc, preferred_element_type=F32)
        y += lax.dot_general(e_ref[:, cols].astype(BF16), wct, (((1,), (1,)), ((), ())),
                             preferred_element_type=F32)
        acc_ref[:, cols] += y

    @pl.when(direction == 1)
    def _():
        acc = acc_ref[...]
        hi = acc.astype(BF16)
        lo = (acc - hi.astype(F32)).astype(BF16)
        y_nat = (jnp.dot(hi, pt_ref[...], preferred_element_type=F32)
                 + jnp.dot(lo, pt_ref[...], preferred_element_type=F32))
        for t in range(S5_T):
            y_ref[token_rows(t), :] = (y_nat[:, t * S5_SET_CH:(t + 1) * S5_SET_CH]
                                       + d_ref[...] * u_ref[token_rows(t), :])


def _s5_mixer(z, ops, consts, layer, d_skip):
    kc, wb, wct, amul, apow = ops
    _, pair_map, perm, perm_t = consts
    n_steps = S5_G // S5_STEP_G
    wspec = pl.BlockSpec((None, None, S5_STEP_G, S5_BLK, S5_BLK), lambda j, d: (layer, d, j, 0, 0))
    cspec = pl.BlockSpec((None, None, SUBLANES, S5_STEP_W), lambda j, d: (layer, d, 0, j))
    pspec = pl.BlockSpec((S5_STEP_W, S5_STEP_W), lambda j, d: (0, 0))
    return pl.pallas_call(
        _s5_body,
        grid=(n_steps, 2),
        in_specs=[
            pl.BlockSpec((NTOK, S5_SET_CH), lambda j, d: (0, j)),
            pspec, pspec,
            pl.BlockSpec((2, S5_BLK, 2 * S5_BLK), lambda j, d: (0, 0, 0)),
            wspec, wspec, wspec, cspec, cspec,
            pl.BlockSpec((1, S5_SET_CH), lambda j, d: (0, j)),
        ],
        out_specs=pl.BlockSpec((NTOK, S5_SET_CH), lambda j, d: (0, j)),
        out_shape=jax.ShapeDtypeStruct((NTOK, S5_W), F32),
        scratch_shapes=[pltpu.VMEM((S5_ROWS, S5_STEP_W), BF16), pltpu.VMEM((S5_ROWS, S5_STEP_W), F32),
                        pltpu.VMEM((S5_ROWS, S5_STEP_W), F32)],
        compiler_params=_cparams(("arbitrary", "arbitrary")),
        name="s5_scan",
    )(z, perm, perm_t, pair_map, kc, wb, wct, amul, apow, d_skip.astype(F32).reshape(1, S5_W))


CONV_PAD = 16
CONV_BLK = 64


def _conv_body(a_ref, b_ref, w_ref, db_ref, lg_ref, lb_ref, o_ref, pad_ref, sh_ref):
    g = a_ref[...] * jax.nn.sigmoid(b_ref[...])

    def run(seq_len):
        n_seq = ROW_TILE // seq_len
        pitch = seq_len + 2 * CONV_PAD
        zeros = jnp.zeros((CONV_PAD, CONV_W), F32)
        for s in range(n_seq):
            pad_ref[s * pitch:s * pitch + CONV_PAD, :] = zeros
            pad_ref[s * pitch + CONV_PAD:s * pitch + CONV_PAD + seq_len, :] = g[s * seq_len:(s + 1) * seq_len]
            pad_ref[s * pitch + CONV_PAD + seq_len:(s + 1) * pitch, :] = zeros
        used = n_seq * pitch
        for r in range(1, SUBLANES):
            sh_ref[r, 0:used - SUBLANES, :] = pad_ref[r:r + used - SUBLANES, :]
        for blk in range(ROW_TILE // CONV_BLK):
            row0 = blk * CONV_BLK
            s, q = divmod(row0, seq_len)
            base = s * pitch + CONV_PAD + q - CONV_K // 2
            acc = jnp.zeros((CONV_BLK, CONV_W), F32)
            for k in range(CONV_K):
                r = (base + k) % SUBLANES
                row = base + k - r
                tap = pad_ref[row:row + CONV_BLK, :] if r == 0 else sh_ref[r, row:row + CONV_BLK, :]
                acc = acc + w_ref[k:k + 1, :] * tap
            y = acc + db_ref[...]
            yc = y - jnp.mean(y, axis=-1, keepdims=True)
            var = jnp.mean(yc * yc, axis=-1, keepdims=True)
            y = yc * lax.rsqrt(var + LN_EPS) * lg_ref[...] + lb_ref[...]
            o_ref[row0:row0 + CONV_BLK, :] = jax.nn.silu(y).astype(BF16)

    is_ctx = pl.program_id(0) == 0

    @pl.when(is_ctx)
    def _():
        run(CTX)

    @pl.when(jnp.logical_not(is_ctx))
    def _():
        run(GRID_W)


def _conv_module(z, dw_w, dw_b, ln_g, ln_b):
    w = jnp.concatenate([dw_w, jnp.zeros((1, CONV_W), dw_w.dtype)], axis=0)
    vec = lambda v: v.reshape(1, CONV_W)
    pad_rows = max(CTX + 2 * CONV_PAD, (ROW_TILE // GRID_W) * (GRID_W + 2 * CONV_PAD))
    return pl.pallas_call(
        _conv_body,
        grid=(NTOK // ROW_TILE,),
        in_specs=[
            pl.BlockSpec((ROW_TILE, CONV_W), lambda i: (i, OFF_CONV // CONV_W)),
            pl.BlockSpec((ROW_TILE, CONV_W), lambda i: (i, OFF_CONV // CONV_W + 1)),
            pl.BlockSpec((CONV_K + 1, CONV_W), lambda i: (0, 0)),
            pl.BlockSpec((1, CONV_W), lambda i: (0, 0)),
            pl.BlockSpec((1, CONV_W), lambda i: (0, 0)),
            pl.BlockSpec((1, CONV_W), lambda i: (0, 0)),
        ],
        out_specs=pl.BlockSpec((ROW_TILE, CONV_W), lambda i: (i, 0)),
        out_shape=jax.ShapeDtypeStruct((NTOK, CONV_W), BF16),
        scratch_shapes=[pltpu.VMEM((pad_rows, CONV_W), F32), pltpu.VMEM((SUBLANES, pad_rows, CONV_W), F32)],
        compiler_params=_cparams(("arbitrary",)),
        name="conv_module",
    )(z, z, w, vec(dw_b), vec(ln_g), vec(ln_b))


def _ml_chunk_index(ci, reverse):
    if not reverse:
        return ci
    return jnp.where(ci == 0, 0, ML_NCHUNK - ci)


def _ml_chain_stages(qb, kb, vx, li_rep, li_row, b_rep, b_row, cx_ref, m_ref, mask, reverse, store_h):
    c = ML_CHUNK
    scale = ML_DH ** 0.5
    wide = lambda a: jnp.concatenate([a] * (ML_DH // LANES), axis=1)
    v = {}

    def scores():
        v['m'] = m_ref[...]
        v['d_log'] = jnp.where(mask, wide(b_rep) + (li_row - b_row), -jnp.inf)
        v['inter'] = b_rep + v['m']
        v['m_row'] = jnp.maximum(v['inter'], jnp.max(v['d_log'], axis=-1, keepdims=True))
        v['qk'] = lax.dot_general(qb, kb, (((1,), (1,)), ((), ())), preferred_element_type=F32)

    def numerator():
        s = v['qk'] * jnp.exp(v['d_log'] - wide(v['m_row']))
        w_inter = jnp.exp(v['inter'] - v['m_row']) * scale
        v['cx'] = cx_ref[...]
        lhs = jnp.concatenate([s.astype(BF16), qb * wide(w_inter.astype(BF16))], axis=1)
        rhs = jnp.concatenate([vx, v['cx'].astype(BF16)], axis=0)
        v['num'] = jnp.dot(lhs, rhs, preferred_element_type=F32)

    def output():
        num = v['num']
        den = num[:, ML_DH:]
        recip = 1.0 / jnp.maximum(jnp.abs(den), jnp.exp(-v['m_row']) * scale)
        store_h(num[:, :ML_DH] * wide(recip))

    def state():
        m = v['m']
        b_tot = b_row[:, 0:1] if reverse else b_row[:, c - 1:c]
        g = b_tot - b_rep + li_rep
        m_new = jnp.maximum(b_tot + m, jnp.max(g, axis=0, keepdims=True))
        kw = kb * wide((jnp.exp(g - m_new) * (1.0 / scale)).astype(BF16))
        decay = jnp.exp(b_tot + m - m_new)[:, 0:1]
        cx_ref[...] = decay * v['cx'] + lax.dot_general(kw, vx, (((0,), (0,)), ((), ())),
                                                       preferred_element_type=F32)
        m_ref[...] = m_new

    return [scores, numerator, output, state]


def _mlstm_body(qkvf_ref, gf_ref, gtf_ref, qkvb_ref, gb_ref, gtb_ref, hf_ref, hb_ref, cx_ref, m_ref):
    @pl.when(pl.program_id(0) == 0)
    def _():
        cx_ref[...] = jnp.zeros_like(cx_ref)
        m_ref[...] = jnp.zeros_like(m_ref)

    c = ML_CHUNK
    r_idx = lax.broadcasted_iota(jnp.int32, (c, c), 0)
    c_idx = lax.broadcasted_iota(jnp.int32, (c, c), 1)
    ones_col = jnp.ones((c, LANES), BF16)
    pick_r = lax.broadcasted_iota(jnp.int32, (2 * LANES, 2 * LANES), 0)
    pick_c = lax.broadcasted_iota(jnp.int32, (2 * LANES, 2 * LANES), 1)

    def replicate(parts, col_a, col_b):
        row = jnp.where(pick_c < LANES, col_a, LANES + col_b)
        pick = jnp.where(pick_r == row, 1.0, 0.0).astype(BF16)
        both = sum(jnp.dot(p, pick, preferred_element_type=F32) for p in parts)
        return both[:, :LANES], both[:, LANES:]

    chains = []
    for d, (qkv_ref, g_ref, gt_ref, h_ref) in enumerate(
            ((qkvf_ref, gf_ref, gtf_ref, hf_ref), (qkvb_ref, gb_ref, gtb_ref, hb_ref))):
        reverse = d == 1
        gates = g_ref[...]
        gates_t = gt_ref[...]
        incl = (c_idx >= r_idx) if reverse else (c_idx <= r_idx)
        tri = jnp.where(incl, 1.0, 0.0).astype(BF16)
        incl_t = (r_idx >= c_idx) if reverse else (r_idx <= c_idx)
        tri_t = jnp.where(incl_t, 1.0, 0.0).astype(BF16)
        lf = jax.nn.log_sigmoid(gates)
        lf_t = jax.nn.log_sigmoid(gates_t)
        b_all = sum(jnp.dot(tri, p, preferred_element_type=F32) for p in _split3(lf))
        b_all_t = sum(jnp.dot(p, tri_t, preferred_element_type=F32) for p in _split3(lf_t))
        gate_b_parts = _split3(jnp.concatenate([gates, b_all], axis=1))
        for head in range(ML_H):
            i_col = 2 * ML_H * d + head
            f_col = i_col + ML_H
            state = d * ML_H + head
            col = lambda part: slice((part * ML_H + head) * ML_DH, (part * ML_H + head + 1) * ML_DH)
            vx = jnp.concatenate([qkv_ref[:, col(2)], ones_col], axis=1)

            def store_h(h, h_ref=h_ref, head=head):
                h_ref[:, head * ML_DH:(head + 1) * ML_DH] = h.astype(h_ref.dtype)

            li_rep, b_rep = replicate(gate_b_parts, i_col, f_col)
            chains.append(_ml_chain_stages(
                qkv_ref[:, col(0)], qkv_ref[:, col(1)], vx,
                li_rep, gates_t[i_col:i_col + 1, :],
                b_rep, b_all_t[f_col:f_col + 1, :],
                cx_ref.at[state], m_ref.at[state], incl, reverse, store_h))
    for stage in zip(*chains):
        for run in stage:
            run()


def _mlstm_mixer(qkvo, gates):
    gates_t = jnp.transpose(gates[:, :2 * SUBLANES])
    c = ML_CHUNK

    def specs(reverse):
        row = lambda ci: _ml_chunk_index(ci, reverse)
        return [
            pl.BlockSpec((c, 3 * ML_W), lambda ci: (row(ci), 0)),
            pl.BlockSpec((c, LANES), lambda ci: (row(ci), 0)),
            pl.BlockSpec((2 * SUBLANES, c), lambda ci: (0, row(ci))),
        ]

    out_spec = lambda reverse: pl.BlockSpec((c, ML_W), lambda ci: (_ml_chunk_index(ci, reverse), 0))
    n_state = 2 * ML_H
    return pl.pallas_call(
        _mlstm_body,
        grid=(ML_NCHUNK,),
        in_specs=specs(False) + specs(True),
        out_specs=[out_spec(False), out_spec(True)],
        out_shape=[jax.ShapeDtypeStruct((NTOK, ML_W), BF16)] * 2,
        scratch_shapes=[pltpu.VMEM((n_state, ML_DH, ML_DH + LANES), F32),
                        pltpu.VMEM((n_state, 1, LANES), F32)],
        compiler_params=_cparams(("arbitrary",)),
        name="mlstm_chunks",
    )(qkvo, gates, gates_t, qkvo, gates, gates_t)


def kernel(x, c, ctx, c_ctx, w_mod, b_mod, norm1_g, w_in, b_in, s5_lam_re, s5_lam_im, s5_log_step,
           s5_b_re, s5_b_im, s5_c_re, s5_c_im, s5_d, s5_w_glu, s5_b_glu, conv_dw_w, conv_dw_b,
           conv_ln_g, conv_ln_b, ml_norm_g, w_out, norm2_g, w_ffn_in, w_ffn_out, norm_f_g):
    assert x.shape == (1, SEQ, D_MODEL) and ctx.shape == (1, CTX, D_MODEL)
    xs = (ctx[0].astype(F32), x[0].astype(F32))
    cc = jnp.zeros((SUBLANES, D_MODEL), F32).at[0].set(c[0]).at[1].set(c_ctx)
    mod_all = _modulation(cc, w_mod, b_mod)
    s5_consts = _s5_selectors()
    s5_ops = _s5_operators(s5_lam_re, s5_lam_im, s5_log_step, s5_b_re, s5_b_im, s5_c_re, s5_c_im,
                           s5_consts[0])
    w_in_t = jnp.swapaxes(w_in, 1, 2)
    n_gate = w_in.shape[2] - OFF_G
    w_gate = jnp.pad(w_in_t[:, OFF_G:, :], ((0, 0), (0, LANES - n_gate), (0, 0)))
    w_gate_hi = w_gate.astype(BF16)
    w_gate = jnp.stack([w_gate_hi, (w_gate - w_gate_hi.astype(F32)).astype(BF16)], axis=1)
    w_gate_t = jnp.swapaxes(w_gate, 2, 3)
    b_gate = jnp.pad(b_in[:, OFF_G:], ((0, 0), (0, LANES - n_gate))).reshape(DEPTH, 1, LANES)
    tn = MM_TILE_N

    for l in range(DEPTH):
        if l == 0:
            h, gates = _norm_mod(xs, norm1_g[0], mod_all, 0, 0, w_gate[0], b_gate[0])
            z = _in_proj(h, w_in_t, b_in, l, 0, 1, OFF_Q // tn, F32)
        else:
            z, h, gates = _in_proj_norm(xs, norm1_g[l], mod_all, l, w_gate_t[l], b_gate[l], w_in_t, b_in,
                                        OFF_Q // tn)
        qkvo = _in_proj(h, w_in_t, b_in, l, OFF_Q // tn, (OFF_G - OFF_Q) // (2 * tn), 2, BF16)
        y_s5 = _s5_mixer(z, s5_ops, s5_consts, l, s5_d[l])
        cvo = _conv_module(z, conv_dw_w[l], conv_dw_b[l], conv_ln_g[l], conv_ln_b[l])
        h_f, h_b = _mlstm_mixer(qkvo, gates)
        xs, h2 = _out_proj(xs, y_s5, s5_w_glu[l], s5_b_glu[l], cvo, h_f, h_b, qkvo, ml_norm_g[l], w_out,
                           mod_all, norm2_g[l], l)
        hid = _ffn_in(h2, w_ffn_in, l)
        xs = _ffn_out(xs, hid, w_ffn_out, mod_all, l)
    return _final_norm(xs, norm_f_g)[None]
```

```python
import functools

import numpy as np
import jax
import jax.numpy as jnp
from jax import lax
from jax.experimental import pallas as pl
from jax.experimental.pallas import tpu as pltpu

F32 = jnp.float32
BF16 = jnp.bfloat16

D_MODEL = 2048
SEQ = 8192
CTX = 256
NTOK = SEQ + CTX
DEPTH = 4
GRID_W = 64

S5_W = 512
S5_CH = 16
S5_G = 32
S5_P = 64
CONV_W = 512
CONV_K = 31
ML_W = 1024
ML_H = 4
ML_DH = 256
D_FF = 5632
EPS = 1e-6
LN_EPS = 1e-5

OFF_CONV = 512
OFF_Q = 1536
OFF_K = 2560
OFF_V = 3584
OFF_O = 4608
OFF_G = 5632

LANES = 128
SUBLANES = 8
VMEM_LIMIT = 56 * 1024 * 1024

ROW_TILE = 256
MM_TILE_M = 1056
MM_TILE_N = 512
IN_NORM_TILE_M = 528
FFN_OUT_TILE_M = 704

S5_T = 8
S5_ROWS = NTOK // S5_T
S5_CTX_ROWS = CTX // S5_T
S5_BLK = S5_T * S5_CH
S5_STEP_G = 8
S5_STEP_PAIRS = S5_STEP_G // 2
S5_SET_CH = S5_STEP_G * S5_CH
S5_STEP_W = S5_STEP_G * S5_BLK
S5_FLAT = S5_G * S5_BLK

ML_CHUNK = 256
ML_NCHUNK = NTOK // ML_CHUNK


def _cparams(sem, vmem=VMEM_LIMIT):
    return pltpu.CompilerParams(dimension_semantics=sem, vmem_limit_bytes=vmem)


def _dot(a, b):
    return jnp.dot(a.astype(BF16), b.astype(BF16), preferred_element_type=F32)


def _split3(x):
    a = x.astype(BF16)
    r = x - a.astype(F32)
    b = r.astype(BF16)
    c = (r - b.astype(F32)).astype(BF16)
    return a, b, c


def _mod_body(cc_ref, w_ref, b_ref, o_ref):
    s = jax.nn.silu(cc_ref[...])
    o_ref[0] = _dot(s, w_ref[0]) + b_ref[0]


def _modulation(cc, w_mod, b_mod):
    depth, _, n = w_mod.shape
    tn = 1024
    return pl.pallas_call(
        _mod_body,
        grid=(depth, n // tn),
        in_specs=[
            pl.BlockSpec((SUBLANES, D_MODEL), lambda l, j: (0, 0)),
            pl.BlockSpec((1, D_MODEL, tn), lambda l, j: (l, 0, j)),
            pl.BlockSpec((1, 1, tn), lambda l, j: (l, 0, j)),
        ],
        out_specs=pl.BlockSpec((1, SUBLANES, tn), lambda l, j: (l, 0, j)),
        out_shape=jax.ShapeDtypeStruct((depth, SUBLANES, n), F32),
        compiler_params=_cparams(("arbitrary", "arbitrary")),
        name="adaln_modulation",
    )(cc, w_mod, b_mod.reshape(depth, 1, n))


def _mod_row(m_ref, is_ctx):
    return jnp.where(is_ctx, m_ref[0, 1:2, :], m_ref[0, 0:1, :])


def _stream_specs(xs, tm):
    if isinstance(xs, tuple):
        assert tm == CTX
        return [pl.BlockSpec((tm, D_MODEL), lambda i: (0, 0)),
                pl.BlockSpec((tm, D_MODEL), lambda i: (jnp.maximum(i - 1, 0), 0))], list(xs)
    return [pl.BlockSpec((tm, D_MODEL), lambda i: (i, 0))], [xs]


def _stream_tile(x_refs):
    if len(x_refs) == 1:
        return x_refs[0][...]
    return jnp.where(pl.program_id(0) == 0, x_refs[0][...], x_refs[1][...])


def _norm_body(*refs, modulate, with_gates, n_stream=1):
    it = iter(refs)
    x_refs = [next(it) for _ in range(n_stream)]
    g_ref = next(it)
    sh_ref = sc_ref = wg_ref = bg_ref = gate_ref = None
    if modulate:
        sh_ref, sc_ref = next(it), next(it)
    if with_gates:
        wg_ref, bg_ref = next(it), next(it)
    h_ref = next(it)
    if with_gates:
        gate_ref = next(it)

    xf = _stream_tile(x_refs)
    ms = jnp.mean(xf * xf, axis=-1, keepdims=True)
    if modulate:
        is_ctx = pl.program_id(0) == 0
        gain = g_ref[...] * (1.0 + _mod_row(sc_ref, is_ctx))
        h = xf * lax.rsqrt(ms + EPS) * gain + _mod_row(sh_ref, is_ctx)
    else:
        h = xf * lax.rsqrt(ms + EPS) * g_ref[...]
    h_ref[...] = h.astype(h_ref.dtype)
    if with_gates:
        hi = h.astype(BF16)
        lo = (h - hi.astype(F32)).astype(BF16)
        whi, wlo = wg_ref[0], wg_ref[1]
        nt = lambda a, b: lax.dot_general(a, b, (((1,), (1,)), ((), ())), preferred_element_type=F32)
        gate_ref[...] = nt(hi, whi) + nt(hi, wlo) + nt(lo, whi) + bg_ref[...]


def _norm_mod(xs, g, mod_all, layer, phase, w_gate=None, b_gate=None):
    with_gates = w_gate is not None
    n_tiles = NTOK // ROW_TILE
    x_specs, x_args = _stream_specs(xs, ROW_TILE)
    in_specs = x_specs + [
        pl.BlockSpec((1, D_MODEL), lambda i: (0, 0)),
        pl.BlockSpec((1, SUBLANES, D_MODEL), lambda i: (layer, 0, 3 * phase)),
        pl.BlockSpec((1, SUBLANES, D_MODEL), lambda i: (layer, 0, 3 * phase + 1)),
    ]
    args = x_args + [g.reshape(1, D_MODEL), mod_all, mod_all]
    out_specs = [pl.BlockSpec((ROW_TILE, D_MODEL), lambda i: (i, 0))]
    out_shape = [jax.ShapeDtypeStruct((NTOK, D_MODEL), BF16)]
    if with_gates:
        in_specs += [pl.BlockSpec((2, LANES, D_MODEL), lambda i: (0, 0, 0)),
                     pl.BlockSpec((1, LANES), lambda i: (0, 0))]
        args += [w_gate, b_gate]
        out_specs.append(pl.BlockSpec((ROW_TILE, LANES), lambda i: (i, 0)))
        out_shape.append(jax.ShapeDtypeStruct((NTOK, LANES), F32))
    return pl.pallas_call(
        functools.partial(_norm_body, modulate=True, with_gates=with_gates, n_stream=len(x_args)),
        grid=(n_tiles,),
        in_specs=in_specs,
        out_specs=out_specs,
        out_shape=out_shape,
        compiler_params=_cparams(("arbitrary",)),
        name="rmsnorm_modulate",
    )(*args)


def _final_norm(xs, g):
    skip = CTX // ROW_TILE
    return pl.pallas_call(
        functools.partial(_norm_body, modulate=False, with_gates=False),
        grid=(SEQ // ROW_TILE,),
        in_specs=[pl.BlockSpec((ROW_TILE, D_MODEL), lambda i: (i + skip, 0)),
                  pl.BlockSpec((1, D_MODEL), lambda i: (0, 0))],
        out_specs=[pl.BlockSpec((ROW_TILE, D_MODEL), lambda i: (i, 0))],
        out_shape=[jax.ShapeDtypeStruct((SEQ, D_MODEL), F32)],
        compiler_params=_cparams(("arbitrary",)),
        name="final_rmsnorm",
    )(xs, g.reshape(1, D_MODEL))[0]


def _in_proj_body(*refs, n_w):
    a_ref, wt_refs, b_refs = refs[0], refs[1:1 + n_w], refs[1 + n_w:1 + 2 * n_w]
    o_ref, wbf_ref = refs[1 + 2 * n_w:]
    tn = MM_TILE_N

    @pl.when(pl.program_id(1) == 0)
    def _():
        for k, wt_ref in enumerate(wt_refs):
            wbf_ref[:, k * tn:(k + 1) * tn] = jnp.transpose(wt_ref[...]).astype(BF16)

    bias = jnp.concatenate([b_ref[...] for b_ref in b_refs], axis=1)
    acc = jnp.dot(a_ref[...], wbf_ref[...], preferred_element_type=F32) + bias
    o_ref[...] = acc.astype(o_ref.dtype)


def _in_proj(h, w_in_t, b_in, layer, first_tile, n_steps, n_w, dtype):
    tm, tn = MM_TILE_M, MM_TILE_N
    in_w = w_in_t.shape[1]
    tile = lambda j, k: first_tile + n_w * j + k
    once = dict(pipeline_mode=pl.Buffered(1)) if n_steps == 1 else {}
    w_specs = [pl.BlockSpec((None, tn, D_MODEL), functools.partial(lambda j, i, k: (layer, tile(j, k), 0), k=k),
                            **once)
               for k in range(n_w)]
    b_specs = [pl.BlockSpec((None, 1, tn), functools.partial(lambda j, i, k: (layer, 0, tile(j, k)), k=k))
               for k in range(n_w)]
    b3 = b_in.reshape(DEPTH, 1, in_w)
    return pl.pallas_call(
        functools.partial(_in_proj_body, n_w=n_w),
        grid=(n_steps, NTOK // tm),
        in_specs=[pl.BlockSpec((tm, D_MODEL), lambda j, i: (i, 0))] + w_specs + b_specs,
        out_specs=pl.BlockSpec((tm, n_w * tn), lambda j, i: (i, j)),
        out_shape=jax.ShapeDtypeStruct((NTOK, n_steps * n_w * tn), dtype),
        scratch_shapes=[pltpu.VMEM((D_MODEL, n_w * tn), BF16)],
        compiler_params=_cparams(("arbitrary", "arbitrary")),
        name="in_proj",
    )(h, *([w_in_t] * n_w), *([b3] * n_w))


def _in_proj_norm_body(x_ref, g_ref, sh_ref, sc_ref, wg_ref, bg_ref, *rest, n_w):
    wt_refs, b_refs = rest[:n_w], rest[n_w:2 * n_w]
    o_ref, h_ref, gate_ref, wbf_ref = rest[2 * n_w:]
    tn = MM_TILE_N
    n_main = n_w * tn
    i = pl.program_id(0)

    @pl.when(i == 0)
    def _():
        for k, wt_ref in enumerate(wt_refs):
            wbf_ref[:, k * tn:(k + 1) * tn] = jnp.transpose(wt_ref[...]).astype(BF16)
        wbf_ref[:, n_main:n_main + LANES] = wg_ref[0]
        wbf_ref[:, n_main + LANES:] = wg_ref[1]

    xf = x_ref[...]
    tm = xf.shape[0]
    is_ctx = i * tm + lax.broadcasted_iota(jnp.int32, (tm, 1), 0) < CTX
    pick = lambda ref: jnp.where(is_ctx, ref[0, 1:2, :], ref[0, 0:1, :])
    ms = jnp.mean(xf * xf, axis=-1, keepdims=True)
    h = xf * lax.rsqrt(ms + EPS) * (g_ref[...] * (1.0 + pick(sc_ref))) + pick(sh_ref)
    hi = h.astype(BF16)
    h_ref[...] = hi
    lo = (h - hi.astype(F32)).astype(BF16)
    acc = jnp.dot(hi, wbf_ref[...], preferred_element_type=F32)
    gate_ref[...] = (acc[:, n_main:n_main + LANES] + acc[:, n_main + LANES:]
                     + jnp.dot(lo, wg_ref[0], preferred_element_type=F32) + bg_ref[...])
    bias = jnp.concatenate([b_ref[...] for b_ref in b_refs], axis=1)
    o_ref[...] = acc[:, :n_main] + bias


def _in_proj_norm(xs, g, mod_all, layer, w_gate_t, b_gate, w_in_t, b_in, n_w):
    tm, tn = IN_NORM_TILE_M, MM_TILE_N
    in_w = w_in_t.shape[1]
    once = dict(pipeline_mode=pl.Buffered(1))
    w_specs = [pl.BlockSpec((None, tn, D_MODEL), functools.partial(lambda i, k: (layer, k, 0), k=k), **once)
               for k in range(n_w)]
    b_specs = [pl.BlockSpec((None, 1, tn), functools.partial(lambda i, k: (layer, 0, k), k=k))
               for k in range(n_w)]
    b3 = b_in.reshape(DEPTH, 1, in_w)
    mod = lambda k: pl.BlockSpec((1, SUBLANES, D_MODEL), lambda i: (layer, 0, k))
    return pl.pallas_call(
        functools.partial(_in_proj_norm_body, n_w=n_w),
        grid=(NTOK // tm,),
        in_specs=[
            pl.BlockSpec((tm, D_MODEL), lambda i: (i, 0)),
            pl.BlockSpec((1, D_MODEL), lambda i: (0, 0)),
            mod(0), mod(1),
            pl.BlockSpec((2, D_MODEL, LANES), lambda i: (0, 0, 0)),
            pl.BlockSpec((1, LANES), lambda i: (0, 0)),
        ] + w_specs + b_specs,
        out_specs=[pl.BlockSpec((tm, n_w * tn), lambda i: (i, 0)),
                   pl.BlockSpec((tm, D_MODEL), lambda i: (i, 0)),
                   pl.BlockSpec((tm, LANES), lambda i: (i, 0))],
        out_shape=[jax.ShapeDtypeStruct((NTOK, n_w * tn), F32),
                   jax.ShapeDtypeStruct((NTOK, D_MODEL), BF16),
                   jax.ShapeDtypeStruct((NTOK, LANES), F32)],
        scratch_shapes=[pltpu.VMEM((D_MODEL, n_w * tn + 2 * LANES), BF16)],
        compiler_params=_cparams(("arbitrary",)),
        name="norm_in_proj",
    )(xs, g.reshape(1, D_MODEL), mod_all, mod_all, w_gate_t, b_gate, *([w_in_t] * n_w), *([b3] * n_w))


def _row_gate(g_ref, i, tm, tn):
    rows = i * tm + lax.broadcasted_iota(jnp.int32, (tm, tn), 0)
    return jnp.where(rows < CTX, g_ref[0, 1:2, :], g_ref[0, 0:1, :])


def _out_proj_body(y_ref, wglu_ref, bglu_ref, cvc_ref, cvx_ref, hf_ref, hb_ref, o_ref, mlg_ref, w_ref,
                   gate_ref, sh_ref, sc_ref, ng_ref, *rest):
    x_refs, (xo_ref, h_ref, wbf_ref, wglu_bf_ref) = rest[:-4], rest[-4:]
    i = pl.program_id(0)

    @pl.when(i == 0)
    def _():
        wbf_ref[...] = w_ref[...].astype(BF16)
        wglu_bf_ref[...] = wglu_ref[...].astype(BF16)

    g = jax.nn.gelu(y_ref[...])
    glu = jnp.dot(g.astype(BF16), wglu_bf_ref[...], preferred_element_type=F32) + bglu_ref[...]
    s5 = (g * jax.nn.sigmoid(glu)).astype(BF16)
    acc = jnp.dot(s5, wbf_ref[0:S5_W, :], preferred_element_type=F32)
    acc += jnp.dot(_stream_tile([cvc_ref, cvx_ref]), wbf_ref[S5_W:S5_W + CONV_W, :],
                   preferred_element_type=F32)
    for head in range(ML_H):
        cols = slice(head * ML_DH, (head + 1) * ML_DH)
        hh = hf_ref[:, cols].astype(F32) + hb_ref[:, cols].astype(F32)
        hc = hh - jnp.mean(hh, axis=-1, keepdims=True)
        var = jnp.mean(hc * hc, axis=-1, keepdims=True)
        ml = jax.nn.sigmoid(o_ref[:, cols].astype(F32)) * (hc * lax.rsqrt(var + LN_EPS) * mlg_ref[:, cols])
        row0 = S5_W + CONV_W + head * ML_DH
        acc += jnp.dot(ml.astype(BF16), wbf_ref[row0:row0 + ML_DH, :], preferred_element_type=F32)
    is_ctx = i == 0
    xn = _stream_tile(x_refs) + _mod_row(gate_ref, is_ctx) * acc
    xo_ref[...] = xn
    ms = jnp.mean(xn * xn, axis=-1, keepdims=True)
    gain = ng_ref[...] * (1.0 + _mod_row(sc_ref, is_ctx))
    h_ref[...] = (xn * lax.rsqrt(ms + EPS) * gain + _mod_row(sh_ref, is_ctx)).astype(BF16)


def _out_proj(xs, y_s5, w_glu, b_glu, cvo, h_f, h_b, qkvo, ml_norm_g, w_out, mod_all, norm_g, layer):
    tm = ROW_TILE
    mod = lambda k: pl.BlockSpec((1, SUBLANES, D_MODEL), lambda i: (layer, 0, k))
    rows = lambda width, col=0: pl.BlockSpec((tm, width), lambda i: (i, col))
    const = lambda shape: pl.BlockSpec(shape, lambda i: (0,) * len(shape))
    x_specs, x_args = _stream_specs(xs, tm)
    cv_specs = [pl.BlockSpec((tm, CONV_W), lambda i: (0, 0)),
                pl.BlockSpec((tm, CONV_W), lambda i: (jnp.maximum(i - 1, 0), 0))]
    return pl.pallas_call(
        _out_proj_body,
        grid=(NTOK // tm,),
        in_specs=[
            rows(S5_W), const((S5_W, S5_W)), const((1, S5_W)),
        ] + cv_specs + [
            rows(ML_W), rows(ML_W), rows(ML_W, 3), const((1, ML_W)),
            pl.BlockSpec((None, D_MODEL, D_MODEL), lambda i: (layer, 0, 0), pipeline_mode=pl.Buffered(1)),
            mod(2), mod(3), mod(4),
            const((1, D_MODEL)),
        ] + x_specs,
        out_specs=[rows(D_MODEL), rows(D_MODEL)],
        out_shape=[jax.ShapeDtypeStruct((NTOK, D_MODEL), F32), jax.ShapeDtypeStruct((NTOK, D_MODEL), BF16)],
        scratch_shapes=[pltpu.VMEM((D_MODEL, D_MODEL), BF16), pltpu.VMEM((S5_W, S5_W), BF16)],
        compiler_params=_cparams(("arbitrary",)),
        name="out_proj_residual",
    )(y_s5, w_glu, b_glu.reshape(1, S5_W), *cvo, h_f, h_b, qkvo, ml_norm_g.reshape(1, ML_W), w_out,
      mod_all, mod_all, mod_all, norm_g.reshape(1, D_MODEL), *x_args)


def _ffn_in_body(a_ref, wg_ref, wu_ref, o_ref, wgbf_ref, wubf_ref):
    @pl.when(pl.program_id(1) == 0)
    def _():
        wgbf_ref[...] = wg_ref[...].astype(BF16)
        wubf_ref[...] = wu_ref[...].astype(BF16)

    a = a_ref[...]
    g = jnp.dot(a, wgbf_ref[...], preferred_element_type=F32)
    u = jnp.dot(a, wubf_ref[...], preferred_element_type=F32)
    o_ref[...] = (jax.nn.silu(g) * u).astype(BF16)


def _ffn_in(h, w_ffn_in, layer):
    tm, tn = MM_TILE_M, MM_TILE_N
    nj = D_FF // tn
    return pl.pallas_call(
        _ffn_in_body,
        grid=(nj, NTOK // tm),
        in_specs=[
            pl.BlockSpec((tm, D_MODEL), lambda j, i: (i, 0)),
            pl.BlockSpec((None, D_MODEL, tn), lambda j, i: (layer, 0, j)),
            pl.BlockSpec((None, D_MODEL, tn), lambda j, i: (layer, 0, nj + j)),
        ],
        out_specs=pl.BlockSpec((tm, tn), lambda j, i: (i, j)),
        out_shape=jax.ShapeDtypeStruct((NTOK, D_FF), BF16),
        scratch_shapes=[pltpu.VMEM((D_MODEL, tn), BF16), pltpu.VMEM((D_MODEL, tn), BF16)],
        compiler_params=_cparams(("arbitrary", "arbitrary")),
        name="ffn_in_swiglu",
    )(h, w_ffn_in, w_ffn_in)


def _ffn_out_body(a_ref, w_ref, g_ref, x_ref, o_ref, wbf_ref):
    i = pl.program_id(1)

    @pl.when(i == 0)
    def _():
        wbf_ref[...] = w_ref[...].astype(BF16)

    acc = jnp.dot(a_ref[...], wbf_ref[...], preferred_element_type=F32)
    tm, tn = o_ref.shape
    o_ref[...] = x_ref[...] + _row_gate(g_ref, i, tm, tn) * acc


def _ffn_out(xs, hid, w_ffn_out, mod_all, layer):
    tm, tn = FFN_OUT_TILE_M, MM_TILE_N
    return pl.pallas_call(
        _ffn_out_body,
        grid=(D_MODEL // tn, NTOK // tm),
        in_specs=[
            pl.BlockSpec((tm, D_FF), lambda j, i: (i, 0)),
            pl.BlockSpec((None, D_FF, tn), lambda j, i: (layer, 0, j)),
            pl.BlockSpec((1, SUBLANES, tn), lambda j, i: (layer, 0, 5 * (D_MODEL // tn) + j)),
            pl.BlockSpec((tm, tn), lambda j, i: (i, j)),
        ],
        out_specs=pl.BlockSpec((tm, tn), lambda j, i: (i, j)),
        out_shape=jax.ShapeDtypeStruct((NTOK, D_MODEL), F32),
        scratch_shapes=[pltpu.VMEM((D_FF, tn), BF16)],
        compiler_params=_cparams(("arbitrary", "arbitrary")),
        name="ffn_out_residual",
    )(hid, w_ffn_out, mod_all, xs)


def _s5_selectors():
    tau = np.arange(S5_BLK)[:, None] // S5_CH
    ch_r = np.arange(S5_BLK)[:, None] % S5_CH
    t = np.arange(S5_BLK)[None, :] // S5_CH
    ch_c = np.arange(S5_BLK)[None, :] % S5_CH
    sel = np.zeros((2, S5_T, S5_BLK, S5_BLK), np.float32)
    for s in range(S5_T):
        sel[0, s] = (tau == t - s) & (ch_r == ch_c)
        sel[1, s] = (tau == s - t) & (ch_r == ch_c)
    rp = np.arange(S5_BLK)[:, None]
    col = np.arange(2 * S5_BLK)[None, :]
    pair_map = np.stack([(col == (rp // S5_P) * S5_BLK + gi * S5_P + rp % S5_P) for gi in range(2)])
    src = np.arange(S5_STEP_W)
    dst = ((src // S5_CH) % S5_STEP_G) * S5_BLK + (src // S5_SET_CH) * S5_CH + src % S5_CH
    perm = dst[:, None] == np.arange(S5_STEP_W)[None, :]
    return (jnp.asarray(sel), jnp.asarray(pair_map, BF16), jnp.asarray(perm, BF16),
            jnp.asarray(perm.T, BF16))


def _s5_operators(lam_re, lam_im, log_step, b_re, b_im, c_re, c_im, sel):
    lam = lax.complex(lam_re.astype(F32), lam_im.astype(F32))
    lam_bar = jnp.exp(lam * jnp.exp(log_step.astype(F32)))
    b_bar = ((lam_bar - 1.0) / lam)[..., None] * lax.complex(b_re.astype(F32), b_im.astype(F32))
    c_mat = lax.complex(c_re.astype(F32), c_im.astype(F32))
    depth = lam.shape[0]

    def powers(base, count):
        out = [jnp.ones_like(base)]
        for _ in range(count - 1):
            out.append(out[-1] * base)
        return out

    pw = powers(lam_bar, S5_T + 1)
    pa = powers(pw[S5_T], SUBLANES + 1)
    t_up = list(range(S5_T))

    def table(seq, fwd_idx, bwd_idx):
        return jnp.stack([jnp.stack([seq[f][:, 0], seq[b][:, 1]], axis=1)
                          for f, b in zip(fwd_idx, bwd_idx)])

    blocks = (depth, 2, S5_G, S5_BLK, S5_BLK)

    def token_rows(w, imag_sign):
        w = jnp.moveaxis(w, 0, 3)
        return jnp.concatenate([jnp.real(w), imag_sign * jnp.imag(w)], axis=-1).reshape(blocks)

    pb = table(pw, [S5_T - 1 - t for t in t_up], t_up)
    wb = token_rows(pb[:, :, :, :, None, :] * jnp.swapaxes(b_bar, -1, -2)[None], 1.0)
    pc = table(pw, [t + 1 for t in t_up], [S5_T - t for t in t_up])
    wct = token_rows(pc[:, :, :, :, None, :] * c_mat[None], -1.0)

    lag_c = token_rows(jnp.stack(pw[:S5_T])[:, :, :, :, None, :] * c_mat[None], -1.0)
    b_t = jnp.swapaxes(b_bar, -1, -2)
    b_ri = jnp.concatenate([jnp.real(b_t), jnp.imag(b_t)], axis=-1)
    kern = jnp.einsum('ldgar,ldgxr->ldgax', b_ri, lag_c, precision=lax.Precision.HIGHEST)
    kern = kern.reshape(depth, 2, S5_G * S5_CH, S5_BLK)
    kc = jnp.einsum('ldxk,dskn->ldsxn', kern, sel, precision=lax.Precision.HIGHEST)
    kc = kc.reshape(depth, 2, S5_T, S5_G, S5_CH, S5_BLK)
    kc = jnp.transpose(kc, (0, 1, 3, 2, 4, 5)).reshape(blocks)

    def lanes(v):
        parts = jnp.stack([jnp.real(v), jnp.imag(v)], axis=4)
        parts = parts.reshape(v.shape[0], depth, 2, S5_G // 2, 2, 2, S5_P)
        parts = jnp.swapaxes(parts, 4, 5).reshape(v.shape[0], depth, 2, S5_FLAT)
        return jnp.transpose(parts, (1, 2, 0, 3))

    zero_p = jnp.zeros_like(pa[0])
    amul = lanes(jnp.stack([pa[1], pa[2], pa[4], pa[8]] + [zero_p] * 4))
    apow = lanes(table(pa, t_up, t_up[::-1]))
    return kc.astype(BF16), wb.astype(BF16), wct.astype(BF16), amul, apow


def _s5_row_scan(e_ref, amul_ref, apow_ref, block_lo, block_hi, reverse, carry):
    rows = lax.broadcasted_iota(jnp.int32, (SUBLANES, S5_BLK), 0)

    def shifted(x, k):
        if reverse:
            return jnp.where(rows < SUBLANES - k, pltpu.roll(x, SUBLANES - k, 0), 0.0)
        return jnp.where(rows >= k, pltpu.roll(x, k, 0), 0.0)

    def body(step, carry):
        blk = (block_hi - 1 - step) if reverse else (block_lo + step)
        r0 = pl.multiple_of(blk * SUBLANES, SUBLANES)
        last = 0 if reverse else SUBLANES - 1
        carry_out = []
        for q in range(S5_STEP_PAIRS):
            re_c = slice(2 * q * S5_BLK, (2 * q + 1) * S5_BLK)
            im_c = slice((2 * q + 1) * S5_BLK, (2 * q + 2) * S5_BLK)
            s_re = e_ref[pl.ds(r0, SUBLANES), re_c]
            s_im = e_ref[pl.ds(r0, SUBLANES), im_c]
            for idx, k in enumerate((1, 2, 4)):
                a_re, a_im = amul_ref[idx:idx + 1, re_c], amul_ref[idx:idx + 1, im_c]
                t_re, t_im = shifted(s_re, k), shifted(s_im, k)
                s_re, s_im = s_re + t_re * a_re - t_im * a_im, s_im + t_re * a_im + t_im * a_re
            c_re, c_im = carry[:, re_c], carry[:, im_c]
            p_re, p_im = apow_ref[:, re_c], apow_ref[:, im_c]
            e_ref[pl.ds(r0, SUBLANES), re_c] = shifted(s_re, 1) + p_re * c_re - p_im * c_im
            e_ref[pl.ds(r0, SUBLANES), im_c] = shifted(s_im, 1) + p_re * c_im + p_im * c_re
            a_re, a_im = amul_ref[3:4, re_c], amul_ref[3:4, im_c]
            l_re = jnp.broadcast_to(s_re[last:last + 1, :], (SUBLANES, S5_BLK))
            l_im = jnp.broadcast_to(s_im[last:last + 1, :], (SUBLANES, S5_BLK))
            carry_out += [l_re + a_re * c_re - a_im * c_im, l_im + a_re * c_im + a_im * c_re]
        return jnp.concatenate(carry_out, axis=1)

    return lax.fori_loop(0, block_hi - block_lo, body, carry)


def _s5_pair_operators(kc_ref, wb_ref, wct_ref, map_ref, q):
    g0, g1 = 2 * q, 2 * q + 1
    zero = jnp.zeros((S5_BLK, S5_BLK), BF16)
    kc = jnp.concatenate([jnp.concatenate([kc_ref[g0], zero], axis=1),
                          jnp.concatenate([zero, kc_ref[g1]], axis=1)], axis=0)
    spread = lambda ref: jnp.concatenate(
        [jnp.dot(ref[g0], map_ref[0], preferred_element_type=F32),
         jnp.dot(ref[g1], map_ref[1], preferred_element_type=F32)], axis=0).astype(BF16)
    return kc, spread(wb_ref), spread(wct_ref)


def _s5_body(u_ref, p_ref, pt_ref, map_ref, kc_ref, wb_ref, wct_ref, amul_ref, apow_ref, d_ref, y_ref,
             up_ref, e_ref, acc_ref):
    direction = pl.program_id(1)
    token_rows = lambda t: pl.ds(t, S5_ROWS, stride=S5_T)
    pairs = [slice(2 * q * S5_BLK, 2 * (q + 1) * S5_BLK) for q in range(S5_STEP_PAIRS)]

    @pl.when(direction == 0)
    def _():
        u_nat = jnp.concatenate([u_ref[token_rows(t), :] for t in range(S5_T)], axis=1)
        up_ref[...] = jnp.dot(u_nat.astype(BF16), p_ref[...], preferred_element_type=F32).astype(BF16)
        acc_ref[...] = jnp.zeros_like(acc_ref)

    ops = [_s5_pair_operators(kc_ref, wb_ref, wct_ref, map_ref, q) for q in range(S5_STEP_PAIRS)]
    for (_, wb, _), cols in zip(ops, pairs):
        e_ref[:, cols] = jnp.dot(up_ref[:, cols], wb, preferred_element_type=F32)

    zero = jnp.zeros((SUBLANES, S5_STEP_W), F32)
    n_blocks = S5_ROWS // SUBLANES
    ctx_blocks = S5_CTX_ROWS // SUBLANES
    amul, apow = amul_ref, apow_ref

    @pl.when(direction == 0)
    def _():
        _s5_row_scan(e_ref, amul, apow, 0, n_blocks, False, zero)

    @pl.when(direction == 1)
    def _():
        carry = _s5_row_scan(e_ref, amul, apow, 0, ctx_blocks, True, zero)
        _s5_row_scan(e_ref, amul, apow, ctx_blocks, n_blocks, True, carry)

    for (kc, _, wct), cols in zip(ops, pairs):
        y = jnp.dot(up_ref[:, cols], kc, preferred_element_type=F32)
        y += lax.dot_general(e_ref[:, cols].astype(BF16), wct, (((1,), (1,)), ((), ())),
                             preferred_element_type=F32)
        acc_ref[:, cols] += y

    @pl.when(direction == 1)
    def _():
        acc = acc_ref[...]
        hi = acc.astype(BF16)
        lo = (acc - hi.astype(F32)).astype(BF16)
        y_nat = (jnp.dot(hi, pt_ref[...], preferred_element_type=F32)
                 + jnp.dot(lo, pt_ref[...], preferred_element_type=F32))
        for t in range(S5_T):
            y_ref[token_rows(t), :] = (y_nat[:, t * S5_SET_CH:(t + 1) * S5_SET_CH]
                                       + d_ref[...] * u_ref[token_rows(t), :])


def _s5_mixer(z, ops, consts, layer, d_skip):
    kc, wb, wct, amul, apow = ops
    _, pair_map, perm, perm_t = consts
    n_steps = S5_G // S5_STEP_G
    wspec = pl.BlockSpec((None, None, S5_STEP_G, S5_BLK, S5_BLK), lambda j, d: (layer, d, j, 0, 0))
    cspec = pl.BlockSpec((None, None, SUBLANES, S5_STEP_W), lambda j, d: (layer, d, 0, j))
    pspec = pl.BlockSpec((S5_STEP_W, S5_STEP_W), lambda j, d: (0, 0))
    return pl.pallas_call(
        _s5_body,
        grid=(n_steps, 2),
        in_specs=[
            pl.BlockSpec((NTOK, S5_SET_CH), lambda j, d: (0, j)),
            pspec, pspec,
            pl.BlockSpec((2, S5_BLK, 2 * S5_BLK), lambda j, d: (0, 0, 0)),
            wspec, wspec, wspec, cspec, cspec,
            pl.BlockSpec((1, S5_SET_CH), lambda j, d: (0, j)),
        ],
        out_specs=pl.BlockSpec((NTOK, S5_SET_CH), lambda j, d: (0, j)),
        out_shape=jax.ShapeDtypeStruct((NTOK, S5_W), F32),
        scratch_shapes=[pltpu.VMEM((S5_ROWS, S5_STEP_W), BF16), pltpu.VMEM((S5_ROWS, S5_STEP_W), F32),
                        pltpu.VMEM((S5_ROWS, S5_STEP_W), F32)],
        compiler_params=_cparams(("arbitrary", "arbitrary")),
        name="s5_scan",
    )(z, perm, perm_t, pair_map, kc, wb, wct, amul, apow, d_skip.astype(F32).reshape(1, S5_W))


CONV_PAD = 16
CONV_BLK = 64


CONV_PAD_ROWS = max(CTX + 2 * CONV_PAD, (ROW_TILE // GRID_W) * (GRID_W + 2 * CONV_PAD))


def _conv_tile(a_ref, b_ref, w_ref, db_ref, lg_ref, lb_ref, o_ref, pad_ref, sh_ref, seq_len):
    g = a_ref[...] * jax.nn.sigmoid(b_ref[...])
    n_seq = ROW_TILE // seq_len
    pitch = seq_len + 2 * CONV_PAD
    zeros = jnp.zeros((CONV_PAD, CONV_W), F32)
    for s in range(n_seq):
        pad_ref[s * pitch:s * pitch + CONV_PAD, :] = zeros
        pad_ref[s * pitch + CONV_PAD:s * pitch + CONV_PAD + seq_len, :] = g[s * seq_len:(s + 1) * seq_len]
        pad_ref[s * pitch + CONV_PAD + seq_len:(s + 1) * pitch, :] = zeros
    used = n_seq * pitch
    for r in range(1, SUBLANES):
        sh_ref[r, 0:used - SUBLANES, :] = pad_ref[r:r + used - SUBLANES, :]
    for blk in range(ROW_TILE // CONV_BLK):
        row0 = blk * CONV_BLK
        s, q = divmod(row0, seq_len)
        base = s * pitch + CONV_PAD + q - CONV_K // 2
        acc = jnp.zeros((CONV_BLK, CONV_W), F32)
        for k in range(CONV_K):
            r = (base + k) % SUBLANES
            row = base + k - r
            tap = pad_ref[row:row + CONV_BLK, :] if r == 0 else sh_ref[r, row:row + CONV_BLK, :]
            acc = acc + w_ref[k:k + 1, :] * tap
        y = acc + db_ref[...]
        yc = y - jnp.mean(y, axis=-1, keepdims=True)
        var = jnp.mean(yc * yc, axis=-1, keepdims=True)
        y = yc * lax.rsqrt(var + LN_EPS) * lg_ref[...] + lb_ref[...]
        o_ref[row0:row0 + CONV_BLK, :] = jax.nn.silu(y).astype(BF16)


def _in_proj_conv_body(h_ref, wt0_ref, wt1_ref, b0_ref, b1_ref, za_ref, zb_ref, zac_ref, zbc_ref,
                       cw_ref, cdb_ref, clg_ref, clb_ref, o_ref, cvc_ref, cvx_ref,
                       wbf_ref, pad_ref, sh_ref):
    j, i = pl.program_id(0), pl.program_id(1)
    tn = MM_TILE_N
    conv = (cw_ref, cdb_ref, clg_ref, clb_ref)

    @pl.when(i == 0)
    def _():
        wbf_ref[:, 0:tn] = jnp.transpose(wt0_ref[...]).astype(BF16)
        wbf_ref[:, tn:2 * tn] = jnp.transpose(wt1_ref[...]).astype(BF16)

    @pl.when(jnp.logical_and(j == 0, i == 0))
    def _():
        _conv_tile(zac_ref, zbc_ref, *conv, cvc_ref, pad_ref, sh_ref, CTX)

    bias = jnp.concatenate([b0_ref[...], b1_ref[...]], axis=1)
    acc = jnp.dot(h_ref[...], wbf_ref[...], preferred_element_type=F32) + bias
    o_ref[...] = acc.astype(o_ref.dtype)
    _conv_tile(za_ref, zb_ref, *conv, cvx_ref, pad_ref, sh_ref, GRID_W)


def _in_proj_conv(h, w_in_t, b_in, layer, z, dw_w, dw_b, ln_g, ln_b):
    tm, tn = MM_TILE_M, MM_TILE_N
    n_j = (OFF_G - OFF_Q) // (2 * tn)
    n_i = NTOK // tm
    assert n_j * n_i == SEQ // ROW_TILE
    first = OFF_Q // tn
    in_w = w_in_t.shape[1]
    b3 = b_in.reshape(DEPTH, 1, in_w)
    w = jnp.concatenate([dw_w, jnp.zeros((1, CONV_W), dw_w.dtype)], axis=0)
    vec = lambda v: v.reshape(1, CONV_W)
    a_col = OFF_CONV // CONV_W
    x_tile = lambda j, i: j * n_i + i + CTX // ROW_TILE
    const = lambda shape: pl.BlockSpec(shape, lambda j, i: (0,) * len(shape))
    return pl.pallas_call(
        _in_proj_conv_body,
        grid=(n_j, n_i),
        in_specs=[
            pl.BlockSpec((tm, D_MODEL), lambda j, i: (i, 0)),
            pl.BlockSpec((None, tn, D_MODEL), lambda j, i: (layer, first + 2 * j, 0)),
            pl.BlockSpec((None, tn, D_MODEL), lambda j, i: (layer, first + 2 * j + 1, 0)),
            pl.BlockSpec((None, 1, tn), lambda j, i: (layer, 0, first + 2 * j)),
            pl.BlockSpec((None, 1, tn), lambda j, i: (layer, 0, first + 2 * j + 1)),
            pl.BlockSpec((ROW_TILE, CONV_W), lambda j, i: (x_tile(j, i), a_col)),
            pl.BlockSpec((ROW_TILE, CONV_W), lambda j, i: (x_tile(j, i), a_col + 1)),
            pl.BlockSpec((ROW_TILE, CONV_W), lambda j, i: (0, a_col)),
            pl.BlockSpec((ROW_TILE, CONV_W), lambda j, i: (0, a_col + 1)),
            const((CONV_K + 1, CONV_W)), const((1, CONV_W)), const((1, CONV_W)), const((1, CONV_W)),
        ],
        out_specs=[pl.BlockSpec((tm, 2 * tn), lambda j, i: (i, j)),
                   const((CTX, CONV_W)),
                   pl.BlockSpec((ROW_TILE, CONV_W), lambda j, i: (j * n_i + i, 0))],
        out_shape=[jax.ShapeDtypeStruct((NTOK, OFF_G - OFF_Q), BF16),
                   jax.ShapeDtypeStruct((CTX, CONV_W), BF16),
                   jax.ShapeDtypeStruct((SEQ, CONV_W), BF16)],
        scratch_shapes=[pltpu.VMEM((D_MODEL, 2 * tn), BF16), pltpu.VMEM((CONV_PAD_ROWS, CONV_W), F32),
                        pltpu.VMEM((SUBLANES, CONV_PAD_ROWS, CONV_W), F32)],
        compiler_params=_cparams(("arbitrary", "arbitrary")),
        name="in_proj_conv",
    )(h, w_in_t, w_in_t, b3, b3, z, z, z, z, w, vec(dw_b), vec(ln_g), vec(ln_b))


def _ml_chunk_index(ci, reverse):
    if not reverse:
        return ci
    return jnp.where(ci == 0, 0, ML_NCHUNK - ci)


def _ml_chain_stages(qb, kb, vx, li_rep, li_row, b_rep, b_row, cx_ref, m_ref, mask, reverse, store_h):
    c = ML_CHUNK
    scale = ML_DH ** 0.5
    wide = lambda a: jnp.concatenate([a] * (ML_DH // LANES), axis=1)
    v = {}

    def scores():
        v['m'] = m_ref[...]
        v['d_log'] = jnp.where(mask, wide(b_rep) + (li_row - b_row), -jnp.inf)
        v['inter'] = b_rep + v['m']
        v['m_row'] = jnp.maximum(v['inter'], jnp.max(v['d_log'], axis=-1, keepdims=True))
        v['qk'] = lax.dot_general(qb, kb, (((1,), (1,)), ((), ())), preferred_element_type=F32)

    def numerator():
        s = v['qk'] * jnp.exp(v['d_log'] - wide(v['m_row']))
        w_inter = jnp.exp(v['inter'] - v['m_row']) * scale
        v['cx'] = cx_ref[...]
        lhs = jnp.concatenate([s.astype(BF16), qb * wide(w_inter.astype(BF16))], axis=1)
        rhs = jnp.concatenate([vx, v['cx'].astype(BF16)], axis=0)
        v['num'] = jnp.dot(lhs, rhs, preferred_element_type=F32)

    def output():
        num = v['num']
        den = num[:, ML_DH:]
        recip = 1.0 / jnp.maximum(jnp.abs(den), jnp.exp(-v['m_row']) * scale)
        store_h(num[:, :ML_DH] * wide(recip))

    def state():
        m = v['m']
        b_tot = b_row[:, 0:1] if reverse else b_row[:, c - 1:c]
        g = b_tot - b_rep + li_rep
        m_new = jnp.maximum(b_tot + m, jnp.max(g, axis=0, keepdims=True))
        kw = kb * wide((jnp.exp(g - m_new) * (1.0 / scale)).astype(BF16))
        decay = jnp.exp(b_tot + m - m_new)[:, 0:1]
        cx_ref[...] = decay * v['cx'] + lax.dot_general(kw, vx, (((0,), (0,)), ((), ())),
                                                       preferred_element_type=F32)
        m_ref[...] = m_new

    return [scores, numerator, output, state]


def _mlstm_body(qkvf_ref, gf_ref, gtf_ref, qkvb_ref, gb_ref, gtb_ref, hf_ref, hb_ref, cx_ref, m_ref):
    @pl.when(pl.program_id(0) == 0)
    def _():
        cx_ref[...] = jnp.zeros_like(cx_ref)
        m_ref[...] = jnp.zeros_like(m_ref)

    c = ML_CHUNK
    r_idx = lax.broadcasted_iota(jnp.int32, (c, c), 0)
    c_idx = lax.broadcasted_iota(jnp.int32, (c, c), 1)
    ones_col = jnp.ones((c, LANES), BF16)
    pick_r = lax.broadcasted_iota(jnp.int32, (2 * LANES, 2 * LANES), 0)
    pick_c = lax.broadcasted_iota(jnp.int32, (2 * LANES, 2 * LANES), 1)

    def replicate(parts, col_a, col_b):
        row = jnp.where(pick_c < LANES, col_a, LANES + col_b)
        pick = jnp.where(pick_r == row, 1.0, 0.0).astype(BF16)
        both = sum(jnp.dot(p, pick, preferred_element_type=F32) for p in parts)
        return both[:, :LANES], both[:, LANES:]

    chains = []
    for d, (qkv_ref, g_ref, gt_ref, h_ref) in enumerate(
            ((qkvf_ref, gf_ref, gtf_ref, hf_ref), (qkvb_ref, gb_ref, gtb_ref, hb_ref))):
        reverse = d == 1
        gates = g_ref[...]
        gates_t = gt_ref[...]
        incl = (c_idx >= r_idx) if reverse else (c_idx <= r_idx)
        tri = jnp.where(incl, 1.0, 0.0).astype(BF16)
        incl_t = (r_idx >= c_idx) if reverse else (r_idx <= c_idx)
        tri_t = jnp.where(incl_t, 1.0, 0.0).astype(BF16)
        lf = jax.nn.log_sigmoid(gates)
        lf_t = jax.nn.log_sigmoid(gates_t)
        b_all = sum(jnp.dot(tri, p, preferred_element_type=F32) for p in _split3(lf))
        b_all_t = sum(jnp.dot(p, tri_t, preferred_element_type=F32) for p in _split3(lf_t))
        gate_b_parts = _split3(jnp.concatenate([gates, b_all], axis=1))
        for head in range(ML_H):
            i_col = 2 * ML_H * d + head
            f_col = i_col + ML_H
            state = d * ML_H + head
            col = lambda part: slice((part * ML_H + head) * ML_DH, (part * ML_H + head + 1) * ML_DH)
            vx = jnp.concatenate([qkv_ref[:, col(2)], ones_col], axis=1)

            def store_h(h, h_ref=h_ref, head=head):
                h_ref[:, head * ML_DH:(head + 1) * ML_DH] = h.astype(h_ref.dtype)

            li_rep, b_rep = replicate(gate_b_parts, i_col, f_col)
            chains.append(_ml_chain_stages(
                qkv_ref[:, col(0)], qkv_ref[:, col(1)], vx,
                li_rep, gates_t[i_col:i_col + 1, :],
                b_rep, b_all_t[f_col:f_col + 1, :],
                cx_ref.at[state], m_ref.at[state], incl, reverse, store_h))
    for stage in zip(*chains):
        for run in stage:
            run()


def _mlstm_mixer(qkvo, gates):
    gates_t = jnp.transpose(gates[:, :2 * SUBLANES])
    c = ML_CHUNK

    def specs(reverse):
        row = lambda ci: _ml_chunk_index(ci, reverse)
        return [
            pl.BlockSpec((c, 3 * ML_W), lambda ci: (row(ci), 0)),
            pl.BlockSpec((c, LANES), lambda ci: (row(ci), 0)),
            pl.BlockSpec((2 * SUBLANES, c), lambda ci: (0, row(ci))),
        ]

    out_spec = lambda reverse: pl.BlockSpec((c, ML_W), lambda ci: (_ml_chunk_index(ci, reverse), 0))
    n_state = 2 * ML_H
    return pl.pallas_call(
        _mlstm_body,
        grid=(ML_NCHUNK,),
        in_specs=specs(False) + specs(True),
        out_specs=[out_spec(False), out_spec(True)],
        out_shape=[jax.ShapeDtypeStruct((NTOK, ML_W), BF16)] * 2,
        scratch_shapes=[pltpu.VMEM((n_state, ML_DH, ML_DH + LANES), F32),
                        pltpu.VMEM((n_state, 1, LANES), F32)],
        compiler_params=_cparams(("arbitrary",)),
        name="mlstm_chunks",
    )(qkvo, gates, gates_t, qkvo, gates, gates_t)


def kernel(x, c, ctx, c_ctx, w_mod, b_mod, norm1_g, w_in, b_in, s5_lam_re, s5_lam_im, s5_log_step,
           s5_b_re, s5_b_im, s5_c_re, s5_c_im, s5_d, s5_w_glu, s5_b_glu, conv_dw_w, conv_dw_b,
           conv_ln_g, conv_ln_b, ml_norm_g, w_out, norm2_g, w_ffn_in, w_ffn_out, norm_f_g):
    assert x.shape == (1, SEQ, D_MODEL) and ctx.shape == (1, CTX, D_MODEL)
    xs = (ctx[0].astype(F32), x[0].astype(F32))
    cc = jnp.zeros((SUBLANES, D_MODEL), F32).at[0].set(c[0]).at[1].set(c_ctx)
    mod_all = _modulation(cc, w_mod, b_mod)
    s5_consts = _s5_selectors()
    s5_ops = _s5_operators(s5_lam_re, s5_lam_im, s5_log_step, s5_b_re, s5_b_im, s5_c_re, s5_c_im,
                           s5_consts[0])
    w_in_t = jnp.swapaxes(w_in, 1, 2)
    n_gate = w_in.shape[2] - OFF_G
    w_gate = jnp.pad(w_in_t[:, OFF_G:, :], ((0, 0), (0, LANES - n_gate), (0, 0)))
    w_gate_hi = w_gate.astype(BF16)
    w_gate = jnp.stack([w_gate_hi, (w_gate - w_gate_hi.astype(F32)).astype(BF16)], axis=1)
    w_gate_t = jnp.swapaxes(w_gate, 2, 3)
    b_gate = jnp.pad(b_in[:, OFF_G:], ((0, 0), (0, LANES - n_gate))).reshape(DEPTH, 1, LANES)
    tn = MM_TILE_N

    for l in range(DEPTH):
        if l == 0:
            h, gates = _norm_mod(xs, norm1_g[0], mod_all, 0, 0, w_gate[0], b_gate[0])
            z = _in_proj(h, w_in_t, b_in, l, 0, 1, OFF_Q // tn, F32)
        else:
            z, h, gates = _in_proj_norm(xs, norm1_g[l], mod_all, l, w_gate_t[l], b_gate[l], w_in_t, b_in,
                                        OFF_Q // tn)
        qkvo, cv_c, cv_x = _in_proj_conv(h, w_in_t, b_in, l, z, conv_dw_w[l], conv_dw_b[l], conv_ln_g[l],
                                         conv_ln_b[l])
        y_s5 = _s5_mixer(z, s5_ops, s5_consts, l, s5_d[l])
        h_f, h_b = _mlstm_mixer(qkvo, gates)
        xs, h2 = _out_proj(xs, y_s5, s5_w_glu[l], s5_b_glu[l], (cv_c, cv_x), h_f, h_b, qkvo, ml_norm_g[l], w_out,
                           mod_all, norm2_g[l], l)
        hid = _ffn_in(h2, w_ffn_in, l)
        xs = _ffn_out(xs, hid, w_ffn_out, mod_all, l)
    return _final_norm(xs, norm_f_g)[None]
```

```python
import functools

import numpy as np
import jax
import jax.numpy as jnp
from jax import lax
from jax.experimental import pallas as pl
from jax.experimental.pallas import tpu as pltpu

F32 = jnp.float32
BF16 = jnp.bfloat16

D_MODEL = 2048
SEQ = 8192
CTX = 256
NTOK = SEQ + CTX
DEPTH = 4
GRID_W = 64

S5_W = 512
S5_CH = 16
S5_G = 32
S5_P = 64
CONV_W = 512
CONV_K = 31
ML_W = 1024
ML_H = 4
ML_DH = 256
D_FF = 5632
EPS = 1e-6
LN_EPS = 1e-5

OFF_CONV = 512
OFF_Q = 1536
OFF_O = 4608
OFF_G = 5632

LANES = 128
SUBLANES = 8
VMEM_LIMIT = 56 * 1024 * 1024

ROW_TILE = 256
MM_TILE_M = 1056
MM_TILE_N = 512
IN_NORM_TILE_M = 528
FFN_OUT_TILE_M = 704

S5_T = 8
S5_ROWS = NTOK // S5_T
S5_CTX_ROWS = CTX // S5_T
S5_BLK = S5_T * S5_CH
S5_STEP_G = 8
S5_STEP_PAIRS = S5_STEP_G // 2
S5_SET_CH = S5_STEP_G * S5_CH
S5_STEP_W = S5_STEP_G * S5_BLK
S5_FLAT = S5_G * S5_BLK

ML_CHUNK = 256
ML_NCHUNK = NTOK // ML_CHUNK


def _cparams(sem, vmem=VMEM_LIMIT):
    return pltpu.CompilerParams(dimension_semantics=sem, vmem_limit_bytes=vmem)


def _dot(a, b):
    return jnp.dot(a.astype(BF16), b.astype(BF16), preferred_element_type=F32)


def _split3(x):
    a = x.astype(BF16)
    r = x - a.astype(F32)
    b = r.astype(BF16)
    c = (r - b.astype(F32)).astype(BF16)
    return a, b, c


def _mod_body(cc_ref, w_ref, b_ref, o_ref):
    s = jax.nn.silu(cc_ref[...])
    o_ref[0] = _dot(s, w_ref[0]) + b_ref[0]


def _modulation(cc, w_mod, b_mod):
    depth, _, n = w_mod.shape
    tn = 1024
    return pl.pallas_call(
        _mod_body,
        grid=(depth, n // tn),
        in_specs=[
            pl.BlockSpec((SUBLANES, D_MODEL), lambda l, j: (0, 0)),
            pl.BlockSpec((1, D_MODEL, tn), lambda l, j: (l, 0, j)),
            pl.BlockSpec((1, 1, tn), lambda l, j: (l, 0, j)),
        ],
        out_specs=pl.BlockSpec((1, SUBLANES, tn), lambda l, j: (l, 0, j)),
        out_shape=jax.ShapeDtypeStruct((depth, SUBLANES, n), F32),
        compiler_params=_cparams(("arbitrary", "arbitrary")),
        name="adaln_modulation",
    )(cc, w_mod, b_mod.reshape(depth, 1, n))


def _mod_row(m_ref, is_ctx):
    return jnp.where(is_ctx, m_ref[0, 1:2, :], m_ref[0, 0:1, :])


def _stream_specs(xs, tm):
    if isinstance(xs, tuple):
        assert tm == CTX
        return [pl.BlockSpec((tm, D_MODEL), lambda i: (0, 0)),
                pl.BlockSpec((tm, D_MODEL), lambda i: (jnp.maximum(i - 1, 0), 0))], list(xs)
    return [pl.BlockSpec((tm, D_MODEL), lambda i: (i, 0))], [xs]


def _stream_tile(x_refs):
    if len(x_refs) == 1:
        return x_refs[0][...]
    return jnp.where(pl.program_id(0) == 0, x_refs[0][...], x_refs[1][...])


def _final_norm_body(x_ref, g_ref, o_ref):
    xf = x_ref[...]
    ms = jnp.mean(xf * xf, axis=-1, keepdims=True)
    o_ref[...] = xf * lax.rsqrt(ms + EPS) * g_ref[...]


def _final_norm(xs, g):
    skip = CTX // ROW_TILE
    return pl.pallas_call(
        _final_norm_body,
        grid=(SEQ // ROW_TILE,),
        in_specs=[pl.BlockSpec((ROW_TILE, D_MODEL), lambda i: (i + skip, 0)),
                  pl.BlockSpec((1, D_MODEL), lambda i: (0, 0))],
        out_specs=pl.BlockSpec((ROW_TILE, D_MODEL), lambda i: (i, 0)),
        out_shape=jax.ShapeDtypeStruct((SEQ, D_MODEL), F32),
        compiler_params=_cparams(("arbitrary",)),
        name="final_rmsnorm",
    )(xs, g.reshape(1, D_MODEL))


def _in_proj_norm_body(*refs, n_w, n_stream):
    x_refs, rest = refs[:n_stream], refs[n_stream:]
    (g_ref, sh_ref, sc_ref, wg_ref, bg_ref), rest = rest[:5], rest[5:]
    wt_refs, b_refs = rest[:n_w], rest[n_w:2 * n_w]
    o_ref, h_ref, gate_ref, wbf_ref = rest[2 * n_w:]
    tn = MM_TILE_N
    n_main = n_w * tn
    i = pl.program_id(0)

    @pl.when(i == 0)
    def _():
        for k, wt_ref in enumerate(wt_refs):
            wbf_ref[:, k * tn:(k + 1) * tn] = jnp.transpose(wt_ref[...]).astype(BF16)
        wbf_ref[:, n_main:n_main + LANES] = wg_ref[0]
        wbf_ref[:, n_main + LANES:] = wg_ref[1]

    xf = _stream_tile(x_refs)
    tm = xf.shape[0]
    is_ctx = i * tm + lax.broadcasted_iota(jnp.int32, (tm, 1), 0) < CTX
    pick = lambda ref: jnp.where(is_ctx, ref[0, 1:2, :], ref[0, 0:1, :])
    ms = jnp.mean(xf * xf, axis=-1, keepdims=True)
    h = xf * lax.rsqrt(ms + EPS) * (g_ref[...] * (1.0 + pick(sc_ref))) + pick(sh_ref)
    hi = h.astype(BF16)
    h_ref[...] = hi
    lo = (h - hi.astype(F32)).astype(BF16)
    acc = jnp.dot(hi, wbf_ref[...], preferred_element_type=F32)
    gate_ref[...] = (acc[:, n_main:n_main + LANES] + acc[:, n_main + LANES:]
                     + jnp.dot(lo, wg_ref[0], preferred_element_type=F32) + bg_ref[...])
    bias = jnp.concatenate([b_ref[...] for b_ref in b_refs], axis=1)
    o_ref[...] = acc[:, :n_main] + bias


def _in_proj_norm(xs, tm, g, mod_all, layer, w_gate_t, b_gate, w_in_t, b_in, n_w):
    tn = MM_TILE_N
    in_w = w_in_t.shape[1]
    once = dict(pipeline_mode=pl.Buffered(1))
    x_specs, x_args = _stream_specs(xs, tm)
    w_specs = [pl.BlockSpec((None, tn, D_MODEL), functools.partial(lambda i, k: (layer, k, 0), k=k), **once)
               for k in range(n_w)]
    b_specs = [pl.BlockSpec((None, 1, tn), functools.partial(lambda i, k: (layer, 0, k), k=k))
               for k in range(n_w)]
    b3 = b_in.reshape(DEPTH, 1, in_w)
    mod = lambda k: pl.BlockSpec((1, SUBLANES, D_MODEL), lambda i: (layer, 0, k))
    return pl.pallas_call(
        functools.partial(_in_proj_norm_body, n_w=n_w, n_stream=len(x_args)),
        grid=(NTOK // tm,),
        in_specs=x_specs + [
            pl.BlockSpec((1, D_MODEL), lambda i: (0, 0)),
            mod(0), mod(1),
            pl.BlockSpec((2, D_MODEL, LANES), lambda i: (0, 0, 0)),
            pl.BlockSpec((1, LANES), lambda i: (0, 0)),
        ] + w_specs + b_specs,
        out_specs=[pl.BlockSpec((tm, n_w * tn), lambda i: (i, 0)),
                   pl.BlockSpec((tm, D_MODEL), lambda i: (i, 0)),
                   pl.BlockSpec((tm, LANES), lambda i: (i, 0))],
        out_shape=[jax.ShapeDtypeStruct((NTOK, n_w * tn), F32),
                   jax.ShapeDtypeStruct((NTOK, D_MODEL), BF16),
                   jax.ShapeDtypeStruct((NTOK, LANES), F32)],
        scratch_shapes=[pltpu.VMEM((D_MODEL, n_w * tn + 2 * LANES), BF16)],
        compiler_params=_cparams(("arbitrary",)),
        name="norm_in_proj",
    )(*x_args, g.reshape(1, D_MODEL), mod_all, mod_all, w_gate_t, b_gate, *([w_in_t] * n_w), *([b3] * n_w))


def _row_gate(g_ref, i, tm, tn):
    rows = i * tm + lax.broadcasted_iota(jnp.int32, (tm, tn), 0)
    return jnp.where(rows < CTX, g_ref[0, 1:2, :], g_ref[0, 0:1, :])


def _out_proj_body(y_ref, wglu_ref, bglu_ref, cvc_ref, cvx_ref, hf_ref, hb_ref, o_ref, mlg_ref, w_ref,
                   gate_ref, sh_ref, sc_ref, ng_ref, *rest):
    x_refs, (xo_ref, h_ref, wbf_ref, wglu_bf_ref) = rest[:-4], rest[-4:]
    i = pl.program_id(0)

    @pl.when(i == 0)
    def _():
        wbf_ref[...] = w_ref[...].astype(BF16)
        wglu_bf_ref[...] = wglu_ref[...].astype(BF16)

    g = jax.nn.gelu(y_ref[...])
    glu = jnp.dot(g.astype(BF16), wglu_bf_ref[...], preferred_element_type=F32) + bglu_ref[...]
    s5 = (g * jax.nn.sigmoid(glu)).astype(BF16)
    acc = jnp.dot(s5, wbf_ref[0:S5_W, :], preferred_element_type=F32)
    acc += jnp.dot(_stream_tile([cvc_ref, cvx_ref]), wbf_ref[S5_W:S5_W + CONV_W, :],
                   preferred_element_type=F32)
    for head in range(ML_H):
        cols = slice(head * ML_DH, (head + 1) * ML_DH)
        hh = hf_ref[:, cols].astype(F32) + hb_ref[:, cols].astype(F32)
        hc = hh - jnp.mean(hh, axis=-1, keepdims=True)
        var = jnp.mean(hc * hc, axis=-1, keepdims=True)
        ml = jax.nn.sigmoid(o_ref[:, cols].astype(F32)) * (hc * lax.rsqrt(var + LN_EPS) * mlg_ref[:, cols])
        row0 = S5_W + CONV_W + head * ML_DH
        acc += jnp.dot(ml.astype(BF16), wbf_ref[row0:row0 + ML_DH, :], preferred_element_type=F32)
    is_ctx = i == 0
    xn = _stream_tile(x_refs) + _mod_row(gate_ref, is_ctx) * acc
    xo_ref[...] = xn
    ms = jnp.mean(xn * xn, axis=-1, keepdims=True)
    gain = ng_ref[...] * (1.0 + _mod_row(sc_ref, is_ctx))
    h_ref[...] = (xn * lax.rsqrt(ms + EPS) * gain + _mod_row(sh_ref, is_ctx)).astype(BF16)


def _out_proj(xs, y_s5, w_glu, b_glu, cvo, h_f, h_b, qkvo, ml_norm_g, w_out, mod_all, norm_g, layer):
    tm = ROW_TILE
    mod = lambda k: pl.BlockSpec((1, SUBLANES, D_MODEL), lambda i: (layer, 0, k))
    rows = lambda width, col=0: pl.BlockSpec((tm, width), lambda i: (i, col))
    const = lambda shape: pl.BlockSpec(shape, lambda i: (0,) * len(shape))
    x_specs, x_args = _stream_specs(xs, tm)
    cv_specs = [pl.BlockSpec((tm, CONV_W), lambda i: (0, 0)),
                pl.BlockSpec((tm, CONV_W), lambda i: (jnp.maximum(i - 1, 0), 0))]
    return pl.pallas_call(
        _out_proj_body,
        grid=(NTOK // tm,),
        in_specs=[
            rows(S5_W), const((S5_W, S5_W)), const((1, S5_W)),
        ] + cv_specs + [
            rows(ML_W), rows(ML_W), rows(ML_W, 3), const((1, ML_W)),
            pl.BlockSpec((None, D_MODEL, D_MODEL), lambda i: (layer, 0, 0), pipeline_mode=pl.Buffered(1)),
            mod(2), mod(3), mod(4),
            const((1, D_MODEL)),
        ] + x_specs,
        out_specs=[rows(D_MODEL), rows(D_MODEL)],
        out_shape=[jax.ShapeDtypeStruct((NTOK, D_MODEL), F32), jax.ShapeDtypeStruct((NTOK, D_MODEL), BF16)],
        scratch_shapes=[pltpu.VMEM((D_MODEL, D_MODEL), BF16), pltpu.VMEM((S5_W, S5_W), BF16)],
        compiler_params=_cparams(("arbitrary",)),
        name="out_proj_residual",
    )(y_s5, w_glu, b_glu.reshape(1, S5_W), *cvo, h_f, h_b, qkvo, ml_norm_g.reshape(1, ML_W), w_out,
      mod_all, mod_all, mod_all, norm_g.reshape(1, D_MODEL), *x_args)


def _ffn_in_body(a_ref, wg_ref, wu_ref, o_ref, wgbf_ref, wubf_ref):
    @pl.when(pl.program_id(1) == 0)
    def _():
        wgbf_ref[...] = wg_ref[...].astype(BF16)
        wubf_ref[...] = wu_ref[...].astype(BF16)

    a = a_ref[...]
    g = jnp.dot(a, wgbf_ref[...], preferred_element_type=F32)
    u = jnp.dot(a, wubf_ref[...], preferred_element_type=F32)
    o_ref[...] = (jax.nn.silu(g) * u).astype(BF16)


def _ffn_in(h, w_ffn_in, layer):
    tm, tn = MM_TILE_M, MM_TILE_N
    nj = D_FF // tn
    return pl.pallas_call(
        _ffn_in_body,
        grid=(nj, NTOK // tm),
        in_specs=[
            pl.BlockSpec((tm, D_MODEL), lambda j, i: (i, 0)),
            pl.BlockSpec((None, D_MODEL, tn), lambda j, i: (layer, 0, j)),
            pl.BlockSpec((None, D_MODEL, tn), lambda j, i: (layer, 0, nj + j)),
        ],
        out_specs=pl.BlockSpec((tm, tn), lambda j, i: (i, j)),
        out_shape=jax.ShapeDtypeStruct((NTOK, D_FF), BF16),
        scratch_shapes=[pltpu.VMEM((D_MODEL, tn), BF16), pltpu.VMEM((D_MODEL, tn), BF16)],
        compiler_params=_cparams(("arbitrary", "arbitrary")),
        name="ffn_in_swiglu",
    )(h, w_ffn_in, w_ffn_in)


def _ffn_out_body(a_ref, w_ref, g_ref, x_ref, o_ref, wbf_ref):
    i = pl.program_id(1)

    @pl.when(i == 0)
    def _():
        wbf_ref[...] = w_ref[...].astype(BF16)

    acc = jnp.dot(a_ref[...], wbf_ref[...], preferred_element_type=F32)
    tm, tn = o_ref.shape
    o_ref[...] = x_ref[...] + _row_gate(g_ref, i, tm, tn) * acc


def _ffn_out(xs, hid, w_ffn_out, mod_all, layer):
    tm, tn = FFN_OUT_TILE_M, MM_TILE_N
    return pl.pallas_call(
        _ffn_out_body,
        grid=(D_MODEL // tn, NTOK // tm),
        in_specs=[
            pl.BlockSpec((tm, D_FF), lambda j, i: (i, 0)),
            pl.BlockSpec((None, D_FF, tn), lambda j, i: (layer, 0, j)),
            pl.BlockSpec((1, SUBLANES, tn), lambda j, i: (layer, 0, 5 * (D_MODEL // tn) + j)),
            pl.BlockSpec((tm, tn), lambda j, i: (i, j)),
        ],
        out_specs=pl.BlockSpec((tm, tn), lambda j, i: (i, j)),
        out_shape=jax.ShapeDtypeStruct((NTOK, D_MODEL), F32),
        scratch_shapes=[pltpu.VMEM((D_FF, tn), BF16)],
        compiler_params=_cparams(("arbitrary", "arbitrary")),
        name="ffn_out_residual",
    )(hid, w_ffn_out, mod_all, xs)


def _s5_selectors():
    tau = np.arange(S5_BLK)[:, None] // S5_CH
    ch_r = np.arange(S5_BLK)[:, None] % S5_CH
    t = np.arange(S5_BLK)[None, :] // S5_CH
    ch_c = np.arange(S5_BLK)[None, :] % S5_CH
    sel = np.zeros((2, S5_T, S5_BLK, S5_BLK), np.float32)
    for s in range(S5_T):
        sel[0, s] = (tau == t - s) & (ch_r == ch_c)
        sel[1, s] = (tau == s - t) & (ch_r == ch_c)
    rp = np.arange(S5_BLK)[:, None]
    col = np.arange(2 * S5_BLK)[None, :]
    pair_map = np.stack([(col == (rp // S5_P) * S5_BLK + gi * S5_P + rp % S5_P) for gi in range(2)])
    src = np.arange(S5_STEP_W)
    dst = ((src // S5_CH) % S5_STEP_G) * S5_BLK + (src // S5_SET_CH) * S5_CH + src % S5_CH
    perm = dst[:, None] == np.arange(S5_STEP_W)[None, :]
    return (jnp.asarray(sel), jnp.asarray(pair_map, BF16), jnp.asarray(perm, BF16),
            jnp.asarray(perm.T, BF16))


def _s5_operators(lam_re, lam_im, log_step, b_re, b_im, c_re, c_im, sel):
    lam = lax.complex(lam_re.astype(F32), lam_im.astype(F32))
    lam_bar = jnp.exp(lam * jnp.exp(log_step.astype(F32)))
    b_bar = ((lam_bar - 1.0) / lam)[..., None] * lax.complex(b_re.astype(F32), b_im.astype(F32))
    c_mat = lax.complex(c_re.astype(F32), c_im.astype(F32))
    depth = lam.shape[0]

    def powers(base, count):
        out = [jnp.ones_like(base)]
        for _ in range(count - 1):
            out.append(out[-1] * base)
        return out

    pw = powers(lam_bar, S5_T + 1)
    pa = powers(pw[S5_T], SUBLANES + 1)
    t_up = list(range(S5_T))

    def table(seq, fwd_idx, bwd_idx):
        return jnp.stack([jnp.stack([seq[f][:, 0], seq[b][:, 1]], axis=1)
                          for f, b in zip(fwd_idx, bwd_idx)])

    blocks = (depth, 2, S5_G, S5_BLK, S5_BLK)

    def token_rows(w, imag_sign):
        w = jnp.moveaxis(w, 0, 3)
        return jnp.concatenate([jnp.real(w), imag_sign * jnp.imag(w)], axis=-1).reshape(blocks)

    pb = table(pw, [S5_T - 1 - t for t in t_up], t_up)
    wb = token_rows(pb[:, :, :, :, None, :] * jnp.swapaxes(b_bar, -1, -2)[None], 1.0)
    pc = table(pw, [t + 1 for t in t_up], [S5_T - t for t in t_up])
    wct = token_rows(pc[:, :, :, :, None, :] * c_mat[None], -1.0)

    lag_c = token_rows(jnp.stack(pw[:S5_T])[:, :, :, :, None, :] * c_mat[None], -1.0)
    b_t = jnp.swapaxes(b_bar, -1, -2)
    b_ri = jnp.concatenate([jnp.real(b_t), jnp.imag(b_t)], axis=-1)
    kern = jnp.einsum('ldgar,ldgxr->ldgax', b_ri, lag_c, precision=lax.Precision.HIGHEST)
    kern = kern.reshape(depth, 2, S5_G * S5_CH, S5_BLK)
    kc = jnp.einsum('ldxk,dskn->ldsxn', kern, sel, precision=lax.Precision.HIGHEST)
    kc = kc.reshape(depth, 2, S5_T, S5_G, S5_CH, S5_BLK)
    kc = jnp.transpose(kc, (0, 1, 3, 2, 4, 5)).reshape(blocks)

    def lanes(v):
        parts = jnp.stack([jnp.real(v), jnp.imag(v)], axis=4)
        parts = parts.reshape(v.shape[0], depth, 2, S5_G // 2, 2, 2, S5_P)
        parts = jnp.swapaxes(parts, 4, 5).reshape(v.shape[0], depth, 2, S5_FLAT)
        return jnp.transpose(parts, (1, 2, 0, 3))

    zero_p = jnp.zeros_like(pa[0])
    amul = lanes(jnp.stack([pa[1], pa[2], pa[4], pa[8]] + [zero_p] * 4))
    apow = lanes(table(pa, t_up, t_up[::-1]))
    return kc.astype(BF16), wb.astype(BF16), wct.astype(BF16), amul, apow


def _s5_row_scan(e_ref, amul_ref, apow_ref, block_lo, block_hi, reverse, carry):
    rows = lax.broadcasted_iota(jnp.int32, (SUBLANES, S5_BLK), 0)

    def shifted(x, k):
        if reverse:
            return jnp.where(rows < SUBLANES - k, pltpu.roll(x, SUBLANES - k, 0), 0.0)
        return jnp.where(rows >= k, pltpu.roll(x, k, 0), 0.0)

    def body(step, carry):
        blk = (block_hi - 1 - step) if reverse else (block_lo + step)
        r0 = pl.multiple_of(blk * SUBLANES, SUBLANES)
        last = 0 if reverse else SUBLANES - 1
        carry_out = []
        for q in range(S5_STEP_PAIRS):
            re_c = slice(2 * q * S5_BLK, (2 * q + 1) * S5_BLK)
            im_c = slice((2 * q + 1) * S5_BLK, (2 * q + 2) * S5_BLK)
            s_re = e_ref[pl.ds(r0, SUBLANES), re_c]
            s_im = e_ref[pl.ds(r0, SUBLANES), im_c]
            for idx, k in enumerate((1, 2, 4)):
                a_re, a_im = amul_ref[idx:idx + 1, re_c], amul_ref[idx:idx + 1, im_c]
                t_re, t_im = shifted(s_re, k), shifted(s_im, k)
                s_re, s_im = s_re + t_re * a_re - t_im * a_im, s_im + t_re * a_im + t_im * a_re
            c_re, c_im = carry[:, re_c], carry[:, im_c]
            p_re, p_im = apow_ref[:, re_c], apow_ref[:, im_c]
            e_ref[pl.ds(r0, SUBLANES), re_c] = shifted(s_re, 1) + p_re * c_re - p_im * c_im
            e_ref[pl.ds(r0, SUBLANES), im_c] = shifted(s_im, 1) + p_re * c_im + p_im * c_re
            a_re, a_im = amul_ref[3:4, re_c], amul_ref[3:4, im_c]
            l_re = jnp.broadcast_to(s_re[last:last + 1, :], (SUBLANES, S5_BLK))
            l_im = jnp.broadcast_to(s_im[last:last + 1, :], (SUBLANES, S5_BLK))
            carry_out += [l_re + a_re * c_re - a_im * c_im, l_im + a_re * c_im + a_im * c_re]
        return jnp.concatenate(carry_out, axis=1)

    return lax.fori_loop(0, block_hi - block_lo, body, carry)


def _s5_pair_operators(kc_ref, wb_ref, wct_ref, map_ref, q):
    g0, g1 = 2 * q, 2 * q + 1
    zero = jnp.zeros((S5_BLK, S5_BLK), BF16)
    kc = jnp.concatenate([jnp.concatenate([kc_ref[g0], zero], axis=1),
                          jnp.concatenate([zero, kc_ref[g1]], axis=1)], axis=0)
    spread = lambda ref: jnp.concatenate(
        [jnp.dot(ref[g0], map_ref[0], preferred_element_type=F32),
         jnp.dot(ref[g1], map_ref[1], preferred_element_type=F32)], axis=0).astype(BF16)
    return kc, spread(wb_ref), spread(wct_ref)


def _s5_body(u_ref, p_ref, pt_ref, map_ref, kc_ref, wb_ref, wct_ref, amul_ref, apow_ref, d_ref, y_ref,
             up_ref, e_ref, acc_ref):
    direction = pl.program_id(1)
    token_rows = lambda t: pl.ds(t, S5_ROWS, stride=S5_T)
    pairs = [slice(2 * q * S5_BLK, 2 * (q + 1) * S5_BLK) for q in range(S5_STEP_PAIRS)]

    @pl.when(direction == 0)
    def _():
        u_nat = jnp.concatenate([u_ref[token_rows(t), :] for t in range(S5_T)], axis=1)
        up_ref[...] = jnp.dot(u_nat.astype(BF16), p_ref[...], preferred_element_type=F32).astype(BF16)
        acc_ref[...] = jnp.zeros_like(acc_ref)

    ops = [_s5_pair_operators(kc_ref, wb_ref, wct_ref, map_ref, q) for q in range(S5_STEP_PAIRS)]
    for (_, wb, _), cols in zip(ops, pairs):
        e_ref[:, cols] = jnp.dot(up_ref[:, cols], wb, preferred_element_type=F32)

    zero = jnp.zeros((SUBLANES, S5_STEP_W), F32)
    n_blocks = S5_ROWS // SUBLANES
    ctx_blocks = S5_CTX_ROWS // SUBLANES
    amul, apow = amul_ref, apow_ref

    @pl.when(direction == 0)
    def _():
        _s5_row_scan(e_ref, amul, apow, 0, n_blocks, False, zero)

    @pl.when(direction == 1)
    def _():
        carry = _s5_row_scan(e_ref, amul, apow, 0, ctx_blocks, True, zero)
        _s5_row_scan(e_ref, amul, apow, ctx_blocks, n_blocks, True, carry)

    for (kc, _, wct), cols in zip(ops, pairs):
        y = jnp.dot(up_ref[:, cols], kc, preferred_element_type=F32)
        y += lax.dot_general(e_ref[:, cols].astype(BF16), wct, (((1,), (1,)), ((), ())),
                             preferred_element_type=F32)
        acc_ref[:, cols] += y

    @pl.when(direction == 1)
    def _():
        acc = acc_ref[...]
        hi = acc.astype(BF16)
        lo = (acc - hi.astype(F32)).astype(BF16)
        y_nat = (jnp.dot(hi, pt_ref[...], preferred_element_type=F32)
                 + jnp.dot(lo, pt_ref[...], preferred_element_type=F32))
        for t in range(S5_T):
            y_ref[token_rows(t), :] = (y_nat[:, t * S5_SET_CH:(t + 1) * S5_SET_CH]
                                       + d_ref[...] * u_ref[token_rows(t), :])


def _s5_mixer(z, ops, consts, layer, d_skip):
    kc, wb, wct, amul, apow = ops
    _, pair_map, perm, perm_t = consts
    n_steps = S5_G // S5_STEP_G
    wspec = pl.BlockSpec((None, None, S5_STEP_G, S5_BLK, S5_BLK), lambda j, d: (layer, d, j, 0, 0))
    cspec = pl.BlockSpec((None, None, SUBLANES, S5_STEP_W), lambda j, d: (layer, d, 0, j))
    pspec = pl.BlockSpec((S5_STEP_W, S5_STEP_W), lambda j, d: (0, 0))
    return pl.pallas_call(
        _s5_body,
        grid=(n_steps, 2),
        in_specs=[
            pl.BlockSpec((NTOK, S5_SET_CH), lambda j, d: (0, j)),
            pspec, pspec,
            pl.BlockSpec((2, S5_BLK, 2 * S5_BLK), lambda j, d: (0, 0, 0)),
            wspec, wspec, wspec, cspec, cspec,
            pl.BlockSpec((1, S5_SET_CH), lambda j, d: (0, j)),
        ],
        out_specs=pl.BlockSpec((NTOK, S5_SET_CH), lambda j, d: (0, j)),
        out_shape=jax.ShapeDtypeStruct((NTOK, S5_W), F32),
        scratch_shapes=[pltpu.VMEM((S5_ROWS, S5_STEP_W), BF16), pltpu.VMEM((S5_ROWS, S5_STEP_W), F32),
                        pltpu.VMEM((S5_ROWS, S5_STEP_W), F32)],
        compiler_params=_cparams(("arbitrary", "arbitrary")),
        name="s5_scan",
    )(z, perm, perm_t, pair_map, kc, wb, wct, amul, apow, d_skip.astype(F32).reshape(1, S5_W))


CONV_PAD = 16
CONV_BLK = 64


CONV_PAD_ROWS = max(CTX + 2 * CONV_PAD, (ROW_TILE // GRID_W) * (GRID_W + 2 * CONV_PAD))


def _conv_tile(a_ref, b_ref, w_ref, db_ref, lg_ref, lb_ref, o_ref, pad_ref, sh_ref, seq_len):
    g = a_ref[...] * jax.nn.sigmoid(b_ref[...])
    n_seq = ROW_TILE // seq_len
    pitch = seq_len + 2 * CONV_PAD
    zeros = jnp.zeros((CONV_PAD, CONV_W), F32)
    for s in range(n_seq):
        pad_ref[s * pitch:s * pitch + CONV_PAD, :] = zeros
        pad_ref[s * pitch + CONV_PAD:s * pitch + CONV_PAD + seq_len, :] = g[s * seq_len:(s + 1) * seq_len]
        pad_ref[s * pitch + CONV_PAD + seq_len:(s + 1) * pitch, :] = zeros
    used = n_seq * pitch
    for r in range(1, SUBLANES):
        sh_ref[r, 0:used - SUBLANES, :] = pad_ref[r:r + used - SUBLANES, :]
    for blk in range(ROW_TILE // CONV_BLK):
        row0 = blk * CONV_BLK
        s, q = divmod(row0, seq_len)
        base = s * pitch + CONV_PAD + q - CONV_K // 2
        acc = jnp.zeros((CONV_BLK, CONV_W), F32)
        for k in range(CONV_K):
            r = (base + k) % SUBLANES
            row = base + k - r
            tap = pad_ref[row:row + CONV_BLK, :] if r == 0 else sh_ref[r, row:row + CONV_BLK, :]
            acc = acc + w_ref[k:k + 1, :] * tap
        y = acc + db_ref[...]
        yc = y - jnp.mean(y, axis=-1, keepdims=True)
        var = jnp.mean(yc * yc, axis=-1, keepdims=True)
        y = yc * lax.rsqrt(var + LN_EPS) * lg_ref[...] + lb_ref[...]
        o_ref[row0:row0 + CONV_BLK, :] = jax.nn.silu(y).astype(BF16)


def _in_proj_conv_body(h_ref, wt0_ref, wt1_ref, b0_ref, b1_ref, za_ref, zb_ref, zac_ref, zbc_ref,
                       cw_ref, cdb_ref, clg_ref, clb_ref, o_ref, cvc_ref, cvx_ref,
                       wbf_ref, pad_ref, sh_ref):
    j, i = pl.program_id(0), pl.program_id(1)
    tn = MM_TILE_N
    conv = (cw_ref, cdb_ref, clg_ref, clb_ref)

    @pl.when(i == 0)
    def _():
        wbf_ref[:, 0:tn] = jnp.transpose(wt0_ref[...]).astype(BF16)
        wbf_ref[:, tn:2 * tn] = jnp.transpose(wt1_ref[...]).astype(BF16)

    @pl.when(jnp.logical_and(j == 0, i == 0))
    def _():
        _conv_tile(zac_ref, zbc_ref, *conv, cvc_ref, pad_ref, sh_ref, CTX)

    bias = jnp.concatenate([b0_ref[...], b1_ref[...]], axis=1)
    acc = jnp.dot(h_ref[...], wbf_ref[...], preferred_element_type=F32) + bias
    o_ref[...] = acc.astype(o_ref.dtype)
    _conv_tile(za_ref, zb_ref, *conv, cvx_ref, pad_ref, sh_ref, GRID_W)


def _in_proj_conv(h, w_in_t, b_in, layer, z, dw_w, dw_b, ln_g, ln_b):
    tm, tn = MM_TILE_M, MM_TILE_N
    n_j = (OFF_G - OFF_Q) // (2 * tn)
    n_i = NTOK // tm
    assert n_j * n_i == SEQ // ROW_TILE
    first = OFF_Q // tn
    in_w = w_in_t.shape[1]
    b3 = b_in.reshape(DEPTH, 1, in_w)
    w = jnp.concatenate([dw_w, jnp.zeros((1, CONV_W), dw_w.dtype)], axis=0)
    vec = lambda v: v.reshape(1, CONV_W)
    a_col = OFF_CONV // CONV_W
    x_tile = lambda j, i: j * n_i + i + CTX // ROW_TILE
    const = lambda shape: pl.BlockSpec(shape, lambda j, i: (0,) * len(shape))
    return pl.pallas_call(
        _in_proj_conv_body,
        grid=(n_j, n_i),
        in_specs=[
            pl.BlockSpec((tm, D_MODEL), lambda j, i: (i, 0)),
            pl.BlockSpec((None, tn, D_MODEL), lambda j, i: (layer, first + 2 * j, 0)),
            pl.BlockSpec((None, tn, D_MODEL), lambda j, i: (layer, first + 2 * j + 1, 0)),
            pl.BlockSpec((None, 1, tn), lambda j, i: (layer, 0, first + 2 * j)),
            pl.BlockSpec((None, 1, tn), lambda j, i: (layer, 0, first + 2 * j + 1)),
            pl.BlockSpec((ROW_TILE, CONV_W), lambda j, i: (x_tile(j, i), a_col)),
            pl.BlockSpec((ROW_TILE, CONV_W), lambda j, i: (x_tile(j, i), a_col + 1)),
            pl.BlockSpec((ROW_TILE, CONV_W), lambda j, i: (0, a_col)),
            pl.BlockSpec((ROW_TILE, CONV_W), lambda j, i: (0, a_col + 1)),
            const((CONV_K + 1, CONV_W)), const((1, CONV_W)), const((1, CONV_W)), const((1, CONV_W)),
        ],
        out_specs=[pl.BlockSpec((tm, 2 * tn), lambda j, i: (i, j)),
                   const((CTX, CONV_W)),
                   pl.BlockSpec((ROW_TILE, CONV_W), lambda j, i: (j * n_i + i, 0))],
        out_shape=[jax.ShapeDtypeStruct((NTOK, OFF_G - OFF_Q), BF16),
                   jax.ShapeDtypeStruct((CTX, CONV_W), BF16),
                   jax.ShapeDtypeStruct((SEQ, CONV_W), BF16)],
        scratch_shapes=[pltpu.VMEM((D_MODEL, 2 * tn), BF16), pltpu.VMEM((CONV_PAD_ROWS, CONV_W), F32),
                        pltpu.VMEM((SUBLANES, CONV_PAD_ROWS, CONV_W), F32)],
        compiler_params=_cparams(("arbitrary", "arbitrary")),
        name="in_proj_conv",
    )(h, w_in_t, w_in_t, b3, b3, z, z, z, z, w, vec(dw_b), vec(ln_g), vec(ln_b))


def _ml_chunk_index(ci, reverse):
    if not reverse:
        return ci
    return jnp.where(ci == 0, 0, ML_NCHUNK - ci)


def _ml_chain_stages(qb, kb, vx, li_rep, li_row, b_rep, b_row, cx_ref, m_ref, mask, reverse, store_h):
    c = ML_CHUNK
    scale = ML_DH ** 0.5
    wide = lambda a: jnp.concatenate([a] * (ML_DH // LANES), axis=1)
    v = {}

    def scores():
        v['m'] = m_ref[...]
        v['d_log'] = jnp.where(mask, wide(b_rep) + (li_row - b_row), -jnp.inf)
        v['inter'] = b_rep + v['m']
        v['m_row'] = jnp.maximum(v['inter'], jnp.max(v['d_log'], axis=-1, keepdims=True))
        v['qk'] = lax.dot_general(qb, kb, (((1,), (1,)), ((), ())), preferred_element_type=F32)

    def numerator():
        s = v['qk'] * jnp.exp(v['d_log'] - wide(v['m_row']))
        w_inter = jnp.exp(v['inter'] - v['m_row']) * scale
        v['cx'] = cx_ref[...]
        lhs = jnp.concatenate([s.astype(BF16), qb * wide(w_inter.astype(BF16))], axis=1)
        rhs = jnp.concatenate([vx, v['cx'].astype(BF16)], axis=0)
        v['num'] = jnp.dot(lhs, rhs, preferred_element_type=F32)

    def output():
        num = v['num']
        den = num[:, ML_DH:]
        recip = 1.0 / jnp.maximum(jnp.abs(den), jnp.exp(-v['m_row']) * scale)
        store_h(num[:, :ML_DH] * wide(recip))

    def state():
        m = v['m']
        b_tot = b_row[:, 0:1] if reverse else b_row[:, c - 1:c]
        g = b_tot - b_rep + li_rep
        m_new = jnp.maximum(b_tot + m, jnp.max(g, axis=0, keepdims=True))
        kw = kb * wide((jnp.exp(g - m_new) * (1.0 / scale)).astype(BF16))
        decay = jnp.exp(b_tot + m - m_new)[:, 0:1]
        cx_ref[...] = decay * v['cx'] + lax.dot_general(kw, vx, (((0,), (0,)), ((), ())),
                                                       preferred_element_type=F32)
        m_ref[...] = m_new

    return [scores, numerator, output, state]


def _mlstm_body(qkvf_ref, gf_ref, gtf_ref, qkvb_ref, gb_ref, gtb_ref, hf_ref, hb_ref, cx_ref, m_ref):
    @pl.when(pl.program_id(0) == 0)
    def _():
        cx_ref[...] = jnp.zeros_like(cx_ref)
        m_ref[...] = jnp.zeros_like(m_ref)

    c = ML_CHUNK
    r_idx = lax.broadcasted_iota(jnp.int32, (c, c), 0)
    c_idx = lax.broadcasted_iota(jnp.int32, (c, c), 1)
    ones_col = jnp.ones((c, LANES), BF16)
    pick_r = lax.broadcasted_iota(jnp.int32, (2 * LANES, 2 * LANES), 0)
    pick_c = lax.broadcasted_iota(jnp.int32, (2 * LANES, 2 * LANES), 1)

    def replicate(parts, col_a, col_b):
        row = jnp.where(pick_c < LANES, col_a, LANES + col_b)
        pick = jnp.where(pick_r == row, 1.0, 0.0).astype(BF16)
        both = sum(jnp.dot(p, pick, preferred_element_type=F32) for p in parts)
        return both[:, :LANES], both[:, LANES:]

    chains = []
    for d, (qkv_ref, g_ref, gt_ref, h_ref) in enumerate(
            ((qkvf_ref, gf_ref, gtf_ref, hf_ref), (qkvb_ref, gb_ref, gtb_ref, hb_ref))):
        reverse = d == 1
        gates = g_ref[...]
        gates_t = gt_ref[...]
        incl = (c_idx >= r_idx) if reverse else (c_idx <= r_idx)
        tri = jnp.where(incl, 1.0, 0.0).astype(BF16)
        incl_t = (r_idx >= c_idx) if reverse else (r_idx <= c_idx)
        tri_t = jnp.where(incl_t, 1.0, 0.0).astype(BF16)
        lf = jax.nn.log_sigmoid(gates)
        lf_t = jax.nn.log_sigmoid(gates_t)
        b_all = sum(jnp.dot(tri, p, preferred_element_type=F32) for p in _split3(lf))
        b_all_t = sum(jnp.dot(p, tri_t, preferred_element_type=F32) for p in _split3(lf_t))
        gate_b_parts = _split3(jnp.concatenate([gates, b_all], axis=1))
        for head in range(ML_H):
            i_col = 2 * ML_H * d + head
            f_col = i_col + ML_H
            state = d * ML_H + head
            col = lambda part: slice((part * ML_H + head) * ML_DH, (part * ML_H + head + 1) * ML_DH)
            vx = jnp.concatenate([qkv_ref[:, col(2)], ones_col], axis=1)

            def store_h(h, h_ref=h_ref, head=head):
                h_ref[:, head * ML_DH:(head + 1) * ML_DH] = h.astype(h_ref.dtype)

            li_rep, b_rep = replicate(gate_b_parts, i_col, f_col)
            chains.append(_ml_chain_stages(
                qkv_ref[:, col(0)], qkv_ref[:, col(1)], vx,
                li_rep, gates_t[i_col:i_col + 1, :],
                b_rep, b_all_t[f_col:f_col + 1, :],
                cx_ref.at[state], m_ref.at[state], incl, reverse, store_h))
    for stage in zip(*chains):
        for run in stage:
            run()


def _mlstm_mixer(qkvo, gates):
    gates_t = jnp.transpose(gates[:, :2 * SUBLANES])
    c = ML_CHUNK

    def specs(reverse):
        row = lambda ci: _ml_chunk_index(ci, reverse)
        return [
            pl.BlockSpec((c, 3 * ML_W), lambda ci: (row(ci), 0)),
            pl.BlockSpec((c, LANES), lambda ci: (row(ci), 0)),
            pl.BlockSpec((2 * SUBLANES, c), lambda ci: (0, row(ci))),
        ]

    out_spec = lambda reverse: pl.BlockSpec((c, ML_W), lambda ci: (_ml_chunk_index(ci, reverse), 0))
    n_state = 2 * ML_H
    return pl.pallas_call(
        _mlstm_body,
        grid=(ML_NCHUNK,),
        in_specs=specs(False) + specs(True),
        out_specs=[out_spec(False), out_spec(True)],
        out_shape=[jax.ShapeDtypeStruct((NTOK, ML_W), BF16)] * 2,
        scratch_shapes=[pltpu.VMEM((n_state, ML_DH, ML_DH + LANES), F32),
                        pltpu.VMEM((n_state, 1, LANES), F32)],
        compiler_params=_cparams(("arbitrary",)),
        name="mlstm_chunks",
    )(qkvo, gates, gates_t, qkvo, gates, gates_t)


def kernel(x, c, ctx, c_ctx, w_mod, b_mod, norm1_g, w_in, b_in, s5_lam_re, s5_lam_im, s5_log_step,
           s5_b_re, s5_b_im, s5_c_re, s5_c_im, s5_d, s5_w_glu, s5_b_glu, conv_dw_w, conv_dw_b,
           conv_ln_g, conv_ln_b, ml_norm_g, w_out, norm2_g, w_ffn_in, w_ffn_out, norm_f_g):
    assert x.shape == (1, SEQ, D_MODEL) and ctx.shape == (1, CTX, D_MODEL)
    xs = (ctx[0].astype(F32), x[0].astype(F32))
    cc = jnp.zeros((SUBLANES, D_MODEL), F32).at[0].set(c[0]).at[1].set(c_ctx)
    mod_all = _modulation(cc, w_mod, b_mod)
    s5_consts = _s5_selectors()
    s5_ops = _s5_operators(s5_lam_re, s5_lam_im, s5_log_step, s5_b_re, s5_b_im, s5_c_re, s5_c_im,
                           s5_consts[0])
    w_in_t = jnp.swapaxes(w_in, 1, 2)
    n_gate = w_in.shape[2] - OFF_G
    w_gate = w_in[:, :, OFF_G:]
    w_gate_hi = w_gate.astype(BF16)
    w_gate = jnp.stack([w_gate_hi, (w_gate - w_gate_hi.astype(F32)).astype(BF16)], axis=1)
    w_gate_t = jnp.pad(w_gate, ((0, 0), (0, 0), (0, 0), (0, LANES - n_gate)))
    b_gate = jnp.pad(b_in[:, OFF_G:], ((0, 0), (0, LANES - n_gate))).reshape(DEPTH, 1, LANES)
    tn = MM_TILE_N

    for l in range(DEPTH):
        tm = CTX if l == 0 else IN_NORM_TILE_M
        z, h, gates = _in_proj_norm(xs, tm, norm1_g[l], mod_all, l, w_gate_t[l], b_gate[l], w_in_t, b_in,
                                    OFF_Q // tn)
        qkvo, cv_c, cv_x = _in_proj_conv(h, w_in_t, b_in, l, z, conv_dw_w[l], conv_dw_b[l], conv_ln_g[l],
                                         conv_ln_b[l])
        y_s5 = _s5_mixer(z, s5_ops, s5_consts, l, s5_d[l])
        h_f, h_b = _mlstm_mixer(qkvo, gates)
        xs, h2 = _out_proj(xs, y_s5, s5_w_glu[l], s5_b_glu[l], (cv_c, cv_x), h_f, h_b, qkvo, ml_norm_g[l], w_out,
                           mod_all, norm2_g[l], l)
        hid = _ffn_in(h2, w_ffn_in, l)
        xs = _ffn_out(xs, hid, w_ffn_out, mod_all, l)
    return _final_norm(xs, norm_f_g)[None]
```

```python
import functools

import numpy as np
import jax
import jax.numpy as jnp
from jax import lax
from jax.experimental import pallas as pl
from jax.experimental.pallas import tpu as pltpu

F32 = jnp.float32
BF16 = jnp.bfloat16

D_MODEL = 2048
SEQ = 8192
CTX = 256
NTOK = SEQ + CTX
DEPTH = 4
GRID_W = 64

S5_W = 512
S5_CH = 16
S5_G = 32
S5_P = 64
CONV_W = 512
CONV_K = 31
ML_W = 1024
ML_H = 4
ML_DH = 256
D_FF = 5632
EPS = 1e-6
LN_EPS = 1e-5

OFF_CONV = 512
OFF_Q = 1536
OFF_O = 4608
OFF_G = 5632

LANES = 128
SUBLANES = 8
VMEM_LIMIT = 56 * 1024 * 1024

ROW_TILE = 256
MM_TILE_M = 1056
MM_TILE_N = 512
IN_NORM_TILE_M = 528
FFN_OUT_TILE_M = 704

S5_T = 8
S5_ROWS = NTOK // S5_T
S5_CTX_ROWS = CTX // S5_T
S5_BLK = S5_T * S5_CH
S5_STEP_G = 8
S5_STEP_PAIRS = S5_STEP_G // 2
S5_SET_CH = S5_STEP_G * S5_CH
S5_STEP_W = S5_STEP_G * S5_BLK
S5_FLAT = S5_G * S5_BLK

ML_CHUNK = 256
ML_NCHUNK = NTOK // ML_CHUNK


def _cparams(sem, vmem=VMEM_LIMIT):
    return pltpu.CompilerParams(dimension_semantics=sem, vmem_limit_bytes=vmem)


def _dot(a, b):
    return jnp.dot(a.astype(BF16), b.astype(BF16), preferred_element_type=F32)


def _split3(x):
    a = x.astype(BF16)
    r = x - a.astype(F32)
    b = r.astype(BF16)
    c = (r - b.astype(F32)).astype(BF16)
    return a, b, c


def _mod_body(cc_ref, w_ref, b_ref, o_ref):
    s = jax.nn.silu(cc_ref[...])
    o_ref[0] = _dot(s, w_ref[0]) + b_ref[0]


def _modulation(cc, w_mod, b_mod):
    depth, _, n = w_mod.shape
    tn = 1024
    return pl.pallas_call(
        _mod_body,
        grid=(depth, n // tn),
        in_specs=[
            pl.BlockSpec((SUBLANES, D_MODEL), lambda l, j: (0, 0)),
            pl.BlockSpec((1, D_MODEL, tn), lambda l, j: (l, 0, j)),
            pl.BlockSpec((1, 1, tn), lambda l, j: (l, 0, j)),
        ],
        out_specs=pl.BlockSpec((1, SUBLANES, tn), lambda l, j: (l, 0, j)),
        out_shape=jax.ShapeDtypeStruct((depth, SUBLANES, n), F32),
        compiler_params=_cparams(("arbitrary", "arbitrary")),
        name="adaln_modulation",
    )(cc, w_mod, b_mod.reshape(depth, 1, n))


def _mod_row(m_ref, is_ctx):
    return jnp.where(is_ctx, m_ref[0, 1:2, :], m_ref[0, 0:1, :])


def _stream_specs(xs, tm):
    if isinstance(xs, tuple):
        assert tm == CTX
        return [pl.BlockSpec((tm, D_MODEL), lambda i: (0, 0)),
                pl.BlockSpec((tm, D_MODEL), lambda i: (jnp.maximum(i - 1, 0), 0))], list(xs)
    return [pl.BlockSpec((tm, D_MODEL), lambda i: (i, 0))], [xs]


def _stream_tile(x_refs):
    if len(x_refs) == 1:
        return x_refs[0][...]
    return jnp.where(pl.program_id(0) == 0, x_refs[0][...], x_refs[1][...])


def _final_norm_body(x_ref, g_ref, o_ref):
    xf = x_ref[...]
    ms = jnp.mean(xf * xf, axis=-1, keepdims=True)
    o_ref[...] = xf * lax.rsqrt(ms + EPS) * g_ref[...]


def _final_norm(xs, g):
    skip = CTX // ROW_TILE
    return pl.pallas_call(
        _final_norm_body,
        grid=(SEQ // ROW_TILE,),
        in_specs=[pl.BlockSpec((ROW_TILE, D_MODEL), lambda i: (i + skip, 0)),
                  pl.BlockSpec((1, D_MODEL), lambda i: (0, 0))],
        out_specs=pl.BlockSpec((ROW_TILE, D_MODEL), lambda i: (i, 0)),
        out_shape=jax.ShapeDtypeStruct((SEQ, D_MODEL), F32),
        compiler_params=_cparams(("arbitrary",)),
        name="final_rmsnorm",
    )(xs, g.reshape(1, D_MODEL))


def _in_proj_norm_body(*refs, n_w, n_stream):
    x_refs, rest = refs[:n_stream], refs[n_stream:]
    (g_ref, sh_ref, sc_ref, wg_ref, bg_ref), rest = rest[:5], rest[5:]
    wt_refs, b_refs = rest[:n_w], rest[n_w:2 * n_w]
    o_ref, h_ref, gate_ref, wbf_ref = rest[2 * n_w:]
    tn = MM_TILE_N
    n_main = n_w * tn
    i = pl.program_id(0)

    @pl.when(i == 0)
    def _():
        for k, wt_ref in enumerate(wt_refs):
            wbf_ref[:, k * tn:(k + 1) * tn] = jnp.transpose(wt_ref[...]).astype(BF16)
        wbf_ref[:, n_main:n_main + LANES] = wg_ref[0]
        wbf_ref[:, n_main + LANES:] = wg_ref[1]

    xf = _stream_tile(x_refs)
    tm = xf.shape[0]
    is_ctx = i * tm + lax.broadcasted_iota(jnp.int32, (tm, 1), 0) < CTX
    pick = lambda ref: jnp.where(is_ctx, ref[0, 1:2, :], ref[0, 0:1, :])
    ms = jnp.mean(xf * xf, axis=-1, keepdims=True)
    h = xf * lax.rsqrt(ms + EPS) * (g_ref[...] * (1.0 + pick(sc_ref))) + pick(sh_ref)
    hi = h.astype(BF16)
    h_ref[...] = hi
    lo = (h - hi.astype(F32)).astype(BF16)
    acc = jnp.dot(hi, wbf_ref[...], preferred_element_type=F32)
    gate_ref[...] = (acc[:, n_main:n_main + LANES] + acc[:, n_main + LANES:]
                     + jnp.dot(lo, wg_ref[0], preferred_element_type=F32) + bg_ref[...])
    bias = jnp.concatenate([b_ref[...] for b_ref in b_refs], axis=1)
    o_ref[...] = acc[:, :n_main] + bias


def _in_proj_norm(xs, tm, g, mod_all, layer, w_gate_t, b_gate, w_in_t, b_in, n_w):
    tn = MM_TILE_N
    in_w = w_in_t.shape[1]
    once = dict(pipeline_mode=pl.Buffered(1))
    x_specs, x_args = _stream_specs(xs, tm)
    w_specs = [pl.BlockSpec((None, tn, D_MODEL), functools.partial(lambda i, k: (layer, k, 0), k=k), **once)
               for k in range(n_w)]
    b_specs = [pl.BlockSpec((None, 1, tn), functools.partial(lambda i, k: (layer, 0, k), k=k))
               for k in range(n_w)]
    b3 = b_in.reshape(DEPTH, 1, in_w)
    mod = lambda k: pl.BlockSpec((1, SUBLANES, D_MODEL), lambda i: (layer, 0, k))
    return pl.pallas_call(
        functools.partial(_in_proj_norm_body, n_w=n_w, n_stream=len(x_args)),
        grid=(NTOK // tm,),
        in_specs=x_specs + [
            pl.BlockSpec((1, D_MODEL), lambda i: (0, 0)),
            mod(0), mod(1),
            pl.BlockSpec((2, D_MODEL, LANES), lambda i: (0, 0, 0)),
            pl.BlockSpec((1, LANES), lambda i: (0, 0)),
        ] + w_specs + b_specs,
        out_specs=[pl.BlockSpec((tm, n_w * tn), lambda i: (i, 0)),
                   pl.BlockSpec((tm, D_MODEL), lambda i: (i, 0)),
                   pl.BlockSpec((tm, LANES), lambda i: (i, 0))],
        out_shape=[jax.ShapeDtypeStruct((NTOK, n_w * tn), F32),
                   jax.ShapeDtypeStruct((NTOK, D_MODEL), BF16),
                   jax.ShapeDtypeStruct((NTOK, LANES), F32)],
        scratch_shapes=[pltpu.VMEM((D_MODEL, n_w * tn + 2 * LANES), BF16)],
        compiler_params=_cparams(("arbitrary",)),
        name="norm_in_proj",
    )(*x_args, g.reshape(1, D_MODEL), mod_all, mod_all, w_gate_t, b_gate, *([w_in_t] * n_w), *([b3] * n_w))


def _row_gate(g_ref, i, tm, tn):
    rows = i * tm + lax.broadcasted_iota(jnp.int32, (tm, tn), 0)
    return jnp.where(rows < CTX, g_ref[0, 1:2, :], g_ref[0, 0:1, :])


def _out_proj_body(y_ref, wglu_ref, bglu_ref, cvc_ref, cvx_ref, hf_ref, hb_ref, o_ref, mlg_ref, w_ref,
                   gate_ref, sh_ref, sc_ref, ng_ref, *rest):
    x_refs, (xo_ref, h_ref, wbf_ref, wglu_bf_ref) = rest[:-4], rest[-4:]
    i = pl.program_id(0)

    @pl.when(i == 0)
    def _():
        wbf_ref[...] = w_ref[...].astype(BF16)
        wglu_bf_ref[...] = wglu_ref[...].astype(BF16)

    g = jax.nn.gelu(y_ref[...])
    glu = jnp.dot(g.astype(BF16), wglu_bf_ref[...], preferred_element_type=F32) + bglu_ref[...]
    s5 = (g * jax.nn.sigmoid(glu)).astype(BF16)
    acc = jnp.dot(s5, wbf_ref[0:S5_W, :], preferred_element_type=F32)
    acc += jnp.dot(_stream_tile([cvc_ref, cvx_ref]), wbf_ref[S5_W:S5_W + CONV_W, :],
                   preferred_element_type=F32)
    for head in range(ML_H):
        cols = slice(head * ML_DH, (head + 1) * ML_DH)
        hh = hf_ref[:, cols].astype(F32) + hb_ref[:, cols].astype(F32)
        hc = hh - jnp.mean(hh, axis=-1, keepdims=True)
        var = jnp.mean(hc * hc, axis=-1, keepdims=True)
        ml = jax.nn.sigmoid(o_ref[:, cols].astype(F32)) * (hc * lax.rsqrt(var + LN_EPS) * mlg_ref[:, cols])
        row0 = S5_W + CONV_W + head * ML_DH
        acc += jnp.dot(ml.astype(BF16), wbf_ref[row0:row0 + ML_DH, :], preferred_element_type=F32)
    is_ctx = i == 0
    xn = _stream_tile(x_refs) + _mod_row(gate_ref, is_ctx) * acc
    xo_ref[...] = xn
    ms = jnp.mean(xn * xn, axis=-1, keepdims=True)
    gain = ng_ref[...] * (1.0 + _mod_row(sc_ref, is_ctx))
    h_ref[...] = (xn * lax.rsqrt(ms + EPS) * gain + _mod_row(sh_ref, is_ctx)).astype(BF16)


def _out_proj(xs, y_s5, w_glu, b_glu, cvo, h_f, h_b, qkvo, ml_norm_g, w_out, mod_all, norm_g, layer):
    tm = ROW_TILE
    mod = lambda k: pl.BlockSpec((1, SUBLANES, D_MODEL), lambda i: (layer, 0, k))
    rows = lambda width, col=0: pl.BlockSpec((tm, width), lambda i: (i, col))
    const = lambda shape: pl.BlockSpec(shape, lambda i: (0,) * len(shape))
    x_specs, x_args = _stream_specs(xs, tm)
    cv_specs = [pl.BlockSpec((tm, CONV_W), lambda i: (0, 0)),
                pl.BlockSpec((tm, CONV_W), lambda i: (jnp.maximum(i - 1, 0), 0))]
    return pl.pallas_call(
        _out_proj_body,
        grid=(NTOK // tm,),
        in_specs=[
            rows(S5_W), const((S5_W, S5_W)), const((1, S5_W)),
        ] + cv_specs + [
            rows(ML_W), rows(ML_W), rows(ML_W, 3), const((1, ML_W)),
            pl.BlockSpec((None, D_MODEL, D_MODEL), lambda i: (layer, 0, 0), pipeline_mode=pl.Buffered(1)),
            mod(2), mod(3), mod(4),
            const((1, D_MODEL)),
        ] + x_specs,
        out_specs=[rows(D_MODEL), rows(D_MODEL)],
        out_shape=[jax.ShapeDtypeStruct((NTOK, D_MODEL), F32), jax.ShapeDtypeStruct((NTOK, D_MODEL), BF16)],
        scratch_shapes=[pltpu.VMEM((D_MODEL, D_MODEL), BF16), pltpu.VMEM((S5_W, S5_W), BF16)],
        compiler_params=_cparams(("arbitrary",)),
        name="out_proj_residual",
    )(y_s5, w_glu, b_glu.reshape(1, S5_W), *cvo, h_f, h_b, qkvo, ml_norm_g.reshape(1, ML_W), w_out,
      mod_all, mod_all, mod_all, norm_g.reshape(1, D_MODEL), *x_args)


def _ffn_in_body(a_ref, wg_ref, wu_ref, o_ref, wgbf_ref, wubf_ref):
    @pl.when(pl.program_id(1) == 0)
    def _():
        wgbf_ref[...] = wg_ref[...].astype(BF16)
        wubf_ref[...] = wu_ref[...].astype(BF16)

    a = a_ref[...]
    g = jnp.dot(a, wgbf_ref[...], preferred_element_type=F32)
    u = jnp.dot(a, wubf_ref[...], preferred_element_type=F32)
    o_ref[...] = (jax.nn.silu(g) * u).astype(BF16)


def _ffn_in(h, w_ffn_in, layer):
    tm, tn = MM_TILE_M, MM_TILE_N
    nj = D_FF // tn
    return pl.pallas_call(
        _ffn_in_body,
        grid=(nj, NTOK // tm),
        in_specs=[
            pl.BlockSpec((tm, D_MODEL), lambda j, i: (i, 0)),
            pl.BlockSpec((None, D_MODEL, tn), lambda j, i: (layer, 0, j)),
            pl.BlockSpec((None, D_MODEL, tn), lambda j, i: (layer, 0, nj + j)),
        ],
        out_specs=pl.BlockSpec((tm, tn), lambda j, i: (i, j)),
        out_shape=jax.ShapeDtypeStruct((NTOK, D_FF), BF16),
        scratch_shapes=[pltpu.VMEM((D_MODEL, tn), BF16), pltpu.VMEM((D_MODEL, tn), BF16)],
        compiler_params=_cparams(("arbitrary", "arbitrary")),
        name="ffn_in_swiglu",
    )(h, w_ffn_in, w_ffn_in)


def _ffn_out_body(a_ref, w_ref, g_ref, x_ref, o_ref, wbf_ref):
    i = pl.program_id(1)

    @pl.when(i == 0)
    def _():
        wbf_ref[...] = w_ref[...].astype(BF16)

    acc = jnp.dot(a_ref[...], wbf_ref[...], preferred_element_type=F32)
    tm, tn = o_ref.shape
    o_ref[...] = x_ref[...] + _row_gate(g_ref, i, tm, tn) * acc


def _ffn_out(xs, hid, w_ffn_out, mod_all, layer):
    tm, tn = FFN_OUT_TILE_M, MM_TILE_N
    return pl.pallas_call(
        _ffn_out_body,
        grid=(D_MODEL // tn, NTOK // tm),
        in_specs=[
            pl.BlockSpec((tm, D_FF), lambda j, i: (i, 0)),
            pl.BlockSpec((None, D_FF, tn), lambda j, i: (layer, 0, j)),
            pl.BlockSpec((1, SUBLANES, tn), lambda j, i: (layer, 0, 5 * (D_MODEL // tn) + j)),
            pl.BlockSpec((tm, tn), lambda j, i: (i, j)),
        ],
        out_specs=pl.BlockSpec((tm, tn), lambda j, i: (i, j)),
        out_shape=jax.ShapeDtypeStruct((NTOK, D_MODEL), F32),
        scratch_shapes=[pltpu.VMEM((D_FF, tn), BF16)],
        compiler_params=_cparams(("arbitrary", "arbitrary")),
        name="ffn_out_residual",
    )(hid, w_ffn_out, mod_all, xs)


def _s5_selectors():
    tau = np.arange(S5_BLK)[:, None] // S5_CH
    ch_r = np.arange(S5_BLK)[:, None] % S5_CH
    t = np.arange(S5_BLK)[None, :] // S5_CH
    ch_c = np.arange(S5_BLK)[None, :] % S5_CH
    sel = np.zeros((2, S5_T, S5_BLK, S5_BLK), np.float32)
    for s in range(S5_T):
        sel[0, s] = (tau == t - s) & (ch_r == ch_c)
        sel[1, s] = (tau == s - t) & (ch_r == ch_c)
    rp = np.arange(S5_BLK)[:, None]
    col = np.arange(2 * S5_BLK)[None, :]
    pair_map = np.stack([(col == (rp // S5_P) * S5_BLK + gi * S5_P + rp % S5_P) for gi in range(2)])
    src = np.arange(S5_STEP_W)
    dst = ((src // S5_CH) % S5_STEP_G) * S5_BLK + (src // S5_SET_CH) * S5_CH + src % S5_CH
    perm = dst[:, None] == np.arange(S5_STEP_W)[None, :]
    return (jnp.asarray(sel), jnp.asarray(pair_map, BF16), jnp.asarray(perm, BF16),
            jnp.asarray(perm.T, BF16))


def _s5_operators(lam_re, lam_im, log_step, b_re, b_im, c_re, c_im, sel):
    lam = lax.complex(lam_re.astype(F32), lam_im.astype(F32))
    lam_bar = jnp.exp(lam * jnp.exp(log_step.astype(F32)))
    b_bar = ((lam_bar - 1.0) / lam)[..., None] * lax.complex(b_re.astype(F32), b_im.astype(F32))
    c_mat = lax.complex(c_re.astype(F32), c_im.astype(F32))
    depth = lam.shape[0]

    def powers(base, count):
        out = [jnp.ones_like(base)]
        for _ in range(count - 1):
            out.append(out[-1] * base)
        return out

    pw = powers(lam_bar, S5_T + 1)
    pa = powers(pw[S5_T], SUBLANES + 1)
    t_up = list(range(S5_T))

    def table(seq, fwd_idx, bwd_idx):
        return jnp.stack([jnp.stack([seq[f][:, 0], seq[b][:, 1]], axis=1)
                          for f, b in zip(fwd_idx, bwd_idx)])

    blocks = (depth, 2, S5_G, S5_BLK, S5_BLK)

    def token_rows(w, imag_sign):
        w = jnp.moveaxis(w, 0, 3)
        return jnp.concatenate([jnp.real(w), imag_sign * jnp.imag(w)], axis=-1).reshape(blocks)

    pb = table(pw, [S5_T - 1 - t for t in t_up], t_up)
    wb = token_rows(pb[:, :, :, :, None, :] * jnp.swapaxes(b_bar, -1, -2)[None], 1.0)
    pc = table(pw, [t + 1 for t in t_up], [S5_T - t for t in t_up])
    wct = token_rows(pc[:, :, :, :, None, :] * c_mat[None], -1.0)

    lag_c = token_rows(jnp.stack(pw[:S5_T])[:, :, :, :, None, :] * c_mat[None], -1.0)
    b_t = jnp.swapaxes(b_bar, -1, -2)
    b_ri = jnp.concatenate([jnp.real(b_t), jnp.imag(b_t)], axis=-1)
    kern = jnp.einsum('ldgar,ldgxr->ldgax', b_ri, lag_c, precision=lax.Precision.HIGHEST)
    kern = kern.reshape(depth, 2, S5_G * S5_CH, S5_BLK)
    kc = jnp.einsum('ldxk,dskn->ldsxn', kern, sel, precision=lax.Precision.HIGHEST)
    kc = kc.reshape(depth, 2, S5_T, S5_G, S5_CH, S5_BLK)
    kc = jnp.transpose(kc, (0, 1, 3, 2, 4, 5)).reshape(blocks)

    def lanes(v):
        parts = jnp.stack([jnp.real(v), jnp.imag(v)], axis=4)
        parts = parts.reshape(v.shape[0], depth, 2, S5_G // 2, 2, 2, S5_P)
        parts = jnp.swapaxes(parts, 4, 5).reshape(v.shape[0], depth, 2, S5_FLAT)
        return jnp.transpose(parts, (1, 2, 0, 3))

    zero_p = jnp.zeros_like(pa[0])
    amul = lanes(jnp.stack([pa[1], pa[2], pa[4], pa[8]] + [zero_p] * 4))
    apow = lanes(table(pa, t_up, t_up[::-1]))
    return kc.astype(BF16), wb.astype(BF16), wct.astype(BF16), amul, apow


def _s5_row_scan(e_ref, amul_ref, apow_ref, block_lo, block_hi, reverse, carry):
    rows = lax.broadcasted_iota(jnp.int32, (SUBLANES, S5_BLK), 0)

    def shifted(x, k):
        if reverse:
            return jnp.where(rows < SUBLANES - k, pltpu.roll(x, SUBLANES - k, 0), 0.0)
        return jnp.where(rows >= k, pltpu.roll(x, k, 0), 0.0)

    def body(step, carry):
        blk = (block_hi - 1 - step) if reverse else (block_lo + step)
        r0 = pl.multiple_of(blk * SUBLANES, SUBLANES)
        last = 0 if reverse else SUBLANES - 1
        carry_out = []
        for q in range(S5_STEP_PAIRS):
            re_c = slice(2 * q * S5_BLK, (2 * q + 1) * S5_BLK)
            im_c = slice((2 * q + 1) * S5_BLK, (2 * q + 2) * S5_BLK)
            s_re = e_ref[pl.ds(r0, SUBLANES), re_c]
            s_im = e_ref[pl.ds(r0, SUBLANES), im_c]
            for idx, k in enumerate((1, 2, 4)):
                a_re, a_im = amul_ref[idx:idx + 1, re_c], amul_ref[idx:idx + 1, im_c]
                t_re, t_im = shifted(s_re, k), shifted(s_im, k)
                s_re, s_im = s_re + t_re * a_re - t_im * a_im, s_im + t_re * a_im + t_im * a_re
            c_re, c_im = carry[:, re_c], carry[:, im_c]
            p_re, p_im = apow_ref[:, re_c], apow_ref[:, im_c]
            e_ref[pl.ds(r0, SUBLANES), re_c] = shifted(s_re, 1) + p_re * c_re - p_im * c_im
            e_ref[pl.ds(r0, SUBLANES), im_c] = shifted(s_im, 1) + p_re * c_im + p_im * c_re
            a_re, a_im = amul_ref[3:4, re_c], amul_ref[3:4, im_c]
            l_re = jnp.broadcast_to(s_re[last:last + 1, :], (SUBLANES, S5_BLK))
            l_im = jnp.broadcast_to(s_im[last:last + 1, :], (SUBLANES, S5_BLK))
            carry_out += [l_re + a_re * c_re - a_im * c_im, l_im + a_re * c_im + a_im * c_re]
        return jnp.concatenate(carry_out, axis=1)

    return lax.fori_loop(0, block_hi - block_lo, body, carry)


def _s5_pair_operators(kc_ref, wb_ref, wct_ref, map_ref, q):
    g0, g1 = 2 * q, 2 * q + 1
    zero = jnp.zeros((S5_BLK, S5_BLK), BF16)
    kc = jnp.concatenate([jnp.concatenate([kc_ref[g0], zero], axis=1),
                          jnp.concatenate([zero, kc_ref[g1]], axis=1)], axis=0)
    spread = lambda ref: jnp.concatenate(
        [jnp.dot(ref[g0], map_ref[0], preferred_element_type=F32),
         jnp.dot(ref[g1], map_ref[1], preferred_element_type=F32)], axis=0).astype(BF16)
    return kc, spread(wb_ref), spread(wct_ref)


def _s5_body(u_ref, p_ref, pt_ref, map_ref, kc_ref, wb_ref, wct_ref, amul_ref, apow_ref, d_ref, y_ref,
             up_ref, e_ref, acc_ref):
    direction = pl.program_id(1)
    token_rows = lambda t: pl.ds(t, S5_ROWS, stride=S5_T)
    pairs = [slice(2 * q * S5_BLK, 2 * (q + 1) * S5_BLK) for q in range(S5_STEP_PAIRS)]

    @pl.when(direction == 0)
    def _():
        u_nat = jnp.concatenate([u_ref[token_rows(t), :] for t in range(S5_T)], axis=1)
        up_ref[...] = jnp.dot(u_nat.astype(BF16), p_ref[...], preferred_element_type=F32).astype(BF16)
        acc_ref[...] = jnp.zeros_like(acc_ref)

    ops = [_s5_pair_operators(kc_ref, wb_ref, wct_ref, map_ref, q) for q in range(S5_STEP_PAIRS)]
    for (_, wb, _), cols in zip(ops, pairs):
        e_ref[:, cols] = jnp.dot(up_ref[:, cols], wb, preferred_element_type=F32)

    zero = jnp.zeros((SUBLANES, S5_STEP_W), F32)
    n_blocks = S5_ROWS // SUBLANES
    ctx_blocks = S5_CTX_ROWS // SUBLANES
    amul, apow = amul_ref, apow_ref

    @pl.when(direction == 0)
    def _():
        _s5_row_scan(e_ref, amul, apow, 0, n_blocks, False, zero)

    @pl.when(direction == 1)
    def _():
        carry = _s5_row_scan(e_ref, amul, apow, 0, ctx_blocks, True, zero)
        _s5_row_scan(e_ref, amul, apow, ctx_blocks, n_blocks, True, carry)

    for (kc, _, wct), cols in zip(ops, pairs):
        y = jnp.dot(up_ref[:, cols], kc, preferred_element_type=F32)
        y += lax.dot_general(e_ref[:, cols].astype(BF16), wct, (((1,), (1,)), ((), ())),
                             preferred_element_type=F32)
        acc_ref[:, cols] += y

    @pl.when(direction == 1)
    def _():
        acc = acc_ref[...]
        hi = acc.astype(BF16)
        lo = (acc - hi.astype(F32)).astype(BF16)
        y_nat = (jnp.dot(hi, pt_ref[...], preferred_element_type=F32)
                 + jnp.dot(lo, pt_ref[...], preferred_element_type=F32))
        for t in range(S5_T):
            y_ref[token_rows(t), :] = (y_nat[:, t * S5_SET_CH:(t + 1) * S5_SET_CH]
                                       + d_ref[...] * u_ref[token_rows(t), :])


def _s5_mixer(z, ops, consts, layer, d_skip):
    kc, wb, wct, amul, apow = ops
    _, pair_map, perm, perm_t = consts
    n_steps = S5_G // S5_STEP_G
    wspec = pl.BlockSpec((None, None, S5_STEP_G, S5_BLK, S5_BLK), lambda j, d: (layer, d, j, 0, 0))
    cspec = pl.BlockSpec((None, None, SUBLANES, S5_STEP_W), lambda j, d: (layer, d, 0, j))
    pspec = pl.BlockSpec((S5_STEP_W, S5_STEP_W), lambda j, d: (0, 0))
    return pl.pallas_call(
        _s5_body,
        grid=(n_steps, 2),
        in_specs=[
            pl.BlockSpec((NTOK, S5_SET_CH), lambda j, d: (0, j)),
            pspec, pspec,
            pl.BlockSpec((2, S5_BLK, 2 * S5_BLK), lambda j, d: (0, 0, 0)),
            wspec, wspec, wspec, cspec, cspec,
            pl.BlockSpec((1, S5_SET_CH), lambda j, d: (0, j)),
        ],
        out_specs=pl.BlockSpec((NTOK, S5_SET_CH), lambda j, d: (0, j)),
        out_shape=jax.ShapeDtypeStruct((NTOK, S5_W), F32),
        scratch_shapes=[pltpu.VMEM((S5_ROWS, S5_STEP_W), BF16), pltpu.VMEM((S5_ROWS, S5_STEP_W), F32),
                        pltpu.VMEM((S5_ROWS, S5_STEP_W), F32)],
        compiler_params=_cparams(("arbitrary", "arbitrary")),
        name="s5_scan",
    )(z, perm, perm_t, pair_map, kc, wb, wct, amul, apow, d_skip.astype(F32).reshape(1, S5_W))


CONV_PAD = 16
CONV_BLK = 64


CONV_PAD_ROWS = max(CTX + 2 * CONV_PAD, (ROW_TILE // GRID_W) * (GRID_W + 2 * CONV_PAD))


def _conv_tile(a_ref, b_ref, w_ref, db_ref, lg_ref, lb_ref, o_ref, pad_ref, sh_ref, seq_len):
    g = a_ref[...] * jax.nn.sigmoid(b_ref[...])
    n_seq = ROW_TILE // seq_len
    pitch = seq_len + 2 * CONV_PAD
    zeros = jnp.zeros((CONV_PAD, CONV_W), F32)
    for s in range(n_seq):
        pad_ref[s * pitch:s * pitch + CONV_PAD, :] = zeros
        pad_ref[s * pitch + CONV_PAD:s * pitch + CONV_PAD + seq_len, :] = g[s * seq_len:(s + 1) * seq_len]
        pad_ref[s * pitch + CONV_PAD + seq_len:(s + 1) * pitch, :] = zeros
    used = n_seq * pitch
    for r in range(1, SUBLANES):
        sh_ref[r, 0:used - SUBLANES, :] = pad_ref[r:r + used - SUBLANES, :]
    for blk in range(ROW_TILE // CONV_BLK):
        row0 = blk * CONV_BLK
        s, q = divmod(row0, seq_len)
        base = s * pitch + CONV_PAD + q - CONV_K // 2
        acc = jnp.zeros((CONV_BLK, CONV_W), F32)
        for k in range(CONV_K):
            r = (base + k) % SUBLANES
            row = base + k - r
            tap = pad_ref[row:row + CONV_BLK, :] if r == 0 else sh_ref[r, row:row + CONV_BLK, :]
            acc = acc + w_ref[k:k + 1, :] * tap
        y = acc + db_ref[...]
        yc = y - jnp.mean(y, axis=-1, keepdims=True)
        var = jnp.mean(yc * yc, axis=-1, keepdims=True)
        y = yc * lax.rsqrt(var + LN_EPS) * lg_ref[...] + lb_ref[...]
        o_ref[row0:row0 + CONV_BLK, :] = jax.nn.silu(y).astype(BF16)


def _in_proj_conv_body(h_ref, wt0_ref, wt1_ref, b0_ref, b1_ref, za_ref, zb_ref, zac_ref, zbc_ref,
                       cw_ref, cdb_ref, clg_ref, clb_ref, o_ref, cvc_ref, cvx_ref,
                       wbf_ref, pad_ref, sh_ref):
    j, i = pl.program_id(0), pl.program_id(1)
    tn = MM_TILE_N
    conv = (cw_ref, cdb_ref, clg_ref, clb_ref)

    @pl.when(i == 0)
    def _():
        wbf_ref[:, 0:tn] = jnp.transpose(wt0_ref[...]).astype(BF16)
        wbf_ref[:, tn:2 * tn] = jnp.transpose(wt1_ref[...]).astype(BF16)

    @pl.when(jnp.logical_and(j == 0, i == 0))
    def _():
        _conv_tile(zac_ref, zbc_ref, *conv, cvc_ref, pad_ref, sh_ref, CTX)

    bias = jnp.concatenate([b0_ref[...], b1_ref[...]], axis=1)
    acc = jnp.dot(h_ref[...], wbf_ref[...], preferred_element_type=F32) + bias
    o_ref[...] = acc.astype(o_ref.dtype)
    _conv_tile(za_ref, zb_ref, *conv, cvx_ref, pad_ref, sh_ref, GRID_W)


def _in_proj_conv(h, w_in_t, b_in, layer, z, dw_w, dw_b, ln_g, ln_b):
    tm, tn = MM_TILE_M, MM_TILE_N
    n_j = (OFF_G - OFF_Q) // (2 * tn)
    n_i = NTOK // tm
    assert n_j * n_i == SEQ // ROW_TILE
    first = OFF_Q // tn
    in_w = w_in_t.shape[1]
    b3 = b_in.reshape(DEPTH, 1, in_w)
    w = jnp.concatenate([dw_w, jnp.zeros((1, CONV_W), dw_w.dtype)], axis=0)
    vec = lambda v: v.reshape(1, CONV_W)
    a_col = OFF_CONV // CONV_W
    x_tile = lambda j, i: j * n_i + i + CTX // ROW_TILE
    const = lambda shape: pl.BlockSpec(shape, lambda j, i: (0,) * len(shape))
    return pl.pallas_call(
        _in_proj_conv_body,
        grid=(n_j, n_i),
        in_specs=[
            pl.BlockSpec((tm, D_MODEL), lambda j, i: (i, 0)),
            pl.BlockSpec((None, tn, D_MODEL), lambda j, i: (layer, first + 2 * j, 0)),
            pl.BlockSpec((None, tn, D_MODEL), lambda j, i: (layer, first + 2 * j + 1, 0)),
            pl.BlockSpec((None, 1, tn), lambda j, i: (layer, 0, first + 2 * j)),
            pl.BlockSpec((None, 1, tn), lambda j, i: (layer, 0, first + 2 * j + 1)),
            pl.BlockSpec((ROW_TILE, CONV_W), lambda j, i: (x_tile(j, i), a_col)),
            pl.BlockSpec((ROW_TILE, CONV_W), lambda j, i: (x_tile(j, i), a_col + 1)),
            pl.BlockSpec((ROW_TILE, CONV_W), lambda j, i: (0, a_col)),
            pl.BlockSpec((ROW_TILE, CONV_W), lambda j, i: (0, a_col + 1)),
            const((CONV_K + 1, CONV_W)), const((1, CONV_W)), const((1, CONV_W)), const((1, CONV_W)),
        ],
        out_specs=[pl.BlockSpec((tm, 2 * tn), lambda j, i: (i, j)),
                   const((CTX, CONV_W)),
                   pl.BlockSpec((ROW_TILE, CONV_W), lambda j, i: (j * n_i + i, 0))],
        out_shape=[jax.ShapeDtypeStruct((NTOK, OFF_G - OFF_Q), BF16),
                   jax.ShapeDtypeStruct((CTX, CONV_W), BF16),
                   jax.ShapeDtypeStruct((SEQ, CONV_W), BF16)],
        scratch_shapes=[pltpu.VMEM((D_MODEL, 2 * tn), BF16), pltpu.VMEM((CONV_PAD_ROWS, CONV_W), F32),
                        pltpu.VMEM((SUBLANES, CONV_PAD_ROWS, CONV_W), F32)],
        compiler_params=_cparams(("arbitrary", "arbitrary")),
        name="in_proj_conv",
    )(h, w_in_t, w_in_t, b3, b3, z, z, z, z, w, vec(dw_b), vec(ln_g), vec(ln_b))


def _ml_chunk_index(ci, reverse):
    if not reverse:
        return ci
    return jnp.where(ci == 0, 0, ML_NCHUNK - ci)


def _ml_chain_stages(qb, kb, vx, li_rep, li_row, b_rep, b_row, cx_ref, m_ref, mask, reverse, store_h):
    c = ML_CHUNK
    scale = ML_DH ** 0.5
    wide = lambda a: jnp.concatenate([a] * (ML_DH // LANES), axis=1)
    v = {}

    def scores():
        v['m'] = m_ref[...]
        v['d_log'] = jnp.where(mask, wide(b_rep) + (li_row - b_row), -jnp.inf)
        v['inter'] = b_rep + v['m']
        v['m_row'] = jnp.maximum(v['inter'], jnp.max(v['d_log'], axis=-1, keepdims=True))
        v['qk'] = lax.dot_general(qb, kb, (((1,), (1,)), ((), ())), preferred_element_type=F32)

    def numerator():
        s = v['qk'] * jnp.exp(v['d_log'] - wide(v['m_row']))
        w_inter = jnp.exp(v['inter'] - v['m_row']) * scale
        v['cx'] = cx_ref[...]
        lhs = jnp.concatenate([s.astype(BF16), qb * wide(w_inter.astype(BF16))], axis=1)
        rhs = jnp.concatenate([vx, v['cx'].astype(BF16)], axis=0)
        v['num'] = jnp.dot(lhs, rhs, preferred_element_type=F32)

    def output():
        num = v['num']
        den = num[:, ML_DH:]
        recip = 1.0 / jnp.maximum(jnp.abs(den), jnp.exp(-v['m_row']) * scale)
        store_h(num[:, :ML_DH] * wide(recip))

    def state():
        m = v['m']
        b_tot = b_row[:, 0:1] if reverse else b_row[:, c - 1:c]
        g = b_tot - b_rep + li_rep
        m_new = jnp.maximum(b_tot + m, jnp.max(g, axis=0, keepdims=True))
        kw = kb * wide((jnp.exp(g - m_new) * (1.0 / scale)).astype(BF16))
        decay = jnp.exp(b_tot + m - m_new)[:, 0:1]
        cx_ref[...] = decay * v['cx'] + lax.dot_general(kw, vx, (((0,), (0,)), ((), ())),
                                                       preferred_element_type=F32)
        m_ref[...] = m_new

    return [scores, numerator, output, state]


def _mlstm_body(qkvf_ref, gtf_ref, qkvb_ref, gtb_ref, hf_ref, hb_ref, cx_ref, m_ref):
    @pl.when(pl.program_id(0) == 0)
    def _():
        cx_ref[...] = jnp.zeros_like(cx_ref)
        m_ref[...] = jnp.zeros_like(m_ref)

    c = ML_CHUNK
    r_idx = lax.broadcasted_iota(jnp.int32, (c, c), 0)
    c_idx = lax.broadcasted_iota(jnp.int32, (c, c), 1)
    ones_col = jnp.ones((c, LANES), BF16)
    n_piece = 3
    spread_r = lax.broadcasted_iota(jnp.int32, (SUBLANES, 2 * LANES), 0)
    spread_c = lax.broadcasted_iota(jnp.int32, (SUBLANES, 2 * LANES), 1)
    in_a = jnp.logical_and(spread_r < n_piece, spread_c < LANES)
    in_b = jnp.logical_and(jnp.logical_and(spread_r >= n_piece, spread_r < 2 * n_piece), spread_c >= LANES)
    spread = jnp.where(jnp.logical_or(in_a, in_b), 1.0, 0.0).astype(BF16)

    def replicate(a_rows, b_rows):
        pad = jnp.zeros((SUBLANES - 2 * n_piece, c), F32)
        rows = jnp.concatenate(list(a_rows) + list(b_rows) + [pad], axis=0).astype(BF16)
        both = lax.dot_general(rows, spread, (((0,), (0,)), ((), ())), preferred_element_type=F32)
        return both[:, :LANES], both[:, LANES:]

    chains = []
    for d, (qkv_ref, gt_ref, h_ref) in enumerate(((qkvf_ref, gtf_ref, hf_ref), (qkvb_ref, gtb_ref, hb_ref))):
        reverse = d == 1
        gates_t = gt_ref[...]
        incl = (c_idx >= r_idx) if reverse else (c_idx <= r_idx)
        incl_t = (r_idx >= c_idx) if reverse else (r_idx <= c_idx)
        tri_t = jnp.where(incl_t, 1.0, 0.0).astype(BF16)
        lf_t = jax.nn.log_sigmoid(gates_t)
        b_all_t = sum(jnp.dot(p, tri_t, preferred_element_type=F32) for p in _split3(lf_t))
        gate_rows = [p.astype(F32) for p in _split3(gates_t)]
        b_rows = [p.astype(F32) for p in _split3(b_all_t)]
        for head in range(ML_H):
            i_col = 2 * ML_H * d + head
            f_col = i_col + ML_H
            state = d * ML_H + head
            col = lambda part: slice((part * ML_H + head) * ML_DH, (part * ML_H + head + 1) * ML_DH)
            vx = jnp.concatenate([qkv_ref[:, col(2)], ones_col], axis=1)

            def store_h(h, h_ref=h_ref, head=head):
                h_ref[:, head * ML_DH:(head + 1) * ML_DH] = h.astype(h_ref.dtype)

            li_rep, b_rep = replicate([p[i_col:i_col + 1, :] for p in gate_rows],
                                      [p[f_col:f_col + 1, :] for p in b_rows])
            chains.append(_ml_chain_stages(
                qkv_ref[:, col(0)], qkv_ref[:, col(1)], vx,
                li_rep, gates_t[i_col:i_col + 1, :],
                b_rep, b_all_t[f_col:f_col + 1, :],
                cx_ref.at[state], m_ref.at[state], incl, reverse, store_h))
    for stage in zip(*chains):
        for run in stage:
            run()


def _mlstm_mixer(qkvo, gates):
    gates_t = jnp.transpose(gates[:, :2 * SUBLANES])
    c = ML_CHUNK

    def specs(reverse):
        row = lambda ci: _ml_chunk_index(ci, reverse)
        return [
            pl.BlockSpec((c, 3 * ML_W), lambda ci: (row(ci), 0)),
            pl.BlockSpec((2 * SUBLANES, c), lambda ci: (0, row(ci))),
        ]

    out_spec = lambda reverse: pl.BlockSpec((c, ML_W), lambda ci: (_ml_chunk_index(ci, reverse), 0))
    n_state = 2 * ML_H
    return pl.pallas_call(
        _mlstm_body,
        grid=(ML_NCHUNK,),
        in_specs=specs(False) + specs(True),
        out_specs=[out_spec(False), out_spec(True)],
        out_shape=[jax.ShapeDtypeStruct((NTOK, ML_W), BF16)] * 2,
        scratch_shapes=[pltpu.VMEM((n_state, ML_DH, ML_DH + LANES), F32),
                        pltpu.VMEM((n_state, 1, LANES), F32)],
        compiler_params=_cparams(("arbitrary",)),
        name="mlstm_chunks",
    )(qkvo, gates_t, qkvo, gates_t)


def kernel(x, c, ctx, c_ctx, w_mod, b_mod, norm1_g, w_in, b_in, s5_lam_re, s5_lam_im, s5_log_step,
           s5_b_re, s5_b_im, s5_c_re, s5_c_im, s5_d, s5_w_glu, s5_b_glu, conv_dw_w, conv_dw_b,
           conv_ln_g, conv_ln_b, ml_norm_g, w_out, norm2_g, w_ffn_in, w_ffn_out, norm_f_g):
    assert x.shape == (1, SEQ, D_MODEL) and ctx.shape == (1, CTX, D_MODEL)
    xs = (ctx[0].astype(F32), x[0].astype(F32))
    cc = jnp.zeros((SUBLANES, D_MODEL), F32).at[0].set(c[0]).at[1].set(c_ctx)
    mod_all = _modulation(cc, w_mod, b_mod)
    s5_consts = _s5_selectors()
    s5_ops = _s5_operators(s5_lam_re, s5_lam_im, s5_log_step, s5_b_re, s5_b_im, s5_c_re, s5_c_im,
                           s5_consts[0])
    w_in_t = jnp.swapaxes(w_in, 1, 2)
    n_gate = w_in.shape[2] - OFF_G
    w_gate = w_in[:, :, OFF_G:]
    w_gate_hi = w_gate.astype(BF16)
    w_gate = jnp.stack([w_gate_hi, (w_gate - w_gate_hi.astype(F32)).astype(BF16)], axis=1)
    w_gate_t = jnp.pad(w_gate, ((0, 0), (0, 0), (0, 0), (0, LANES - n_gate)))
    b_gate = jnp.pad(b_in[:, OFF_G:], ((0, 0), (0, LANES - n_gate))).reshape(DEPTH, 1, LANES)
    tn = MM_TILE_N

    for l in range(DEPTH):
        tm = CTX if l == 0 else IN_NORM_TILE_M
        z, h, gates = _in_proj_norm(xs, tm, norm1_g[l], mod_all, l, w_gate_t[l], b_gate[l], w_in_t, b_in,
                                    OFF_Q // tn)
        qkvo, cv_c, cv_x = _in_proj_conv(h, w_in_t, b_in, l, z, conv_dw_w[l], conv_dw_b[l], conv_ln_g[l],
                                         conv_ln_b[l])
        y_s5 = _s5_mixer(z, s5_ops, s5_consts, l, s5_d[l])
        h_f, h_b = _mlstm_mixer(qkvo, gates)
        xs, h2 = _out_proj(xs, y_s5, s5_w_glu[l], s5_b_glu[l], (cv_c, cv_x), h_f, h_b, qkvo, ml_norm_g[l], w_out,
                           mod_all, norm2_g[l], l)
        hid = _ffn_in(h2, w_ffn_in, l)
        xs = _ffn_out(xs, hid, w_ffn_out, mod_all, l)
    return _final_norm(xs, norm_f_g)[None]
```

```python
import functools

import numpy as np
import jax
import jax.numpy as jnp
from jax import lax
from jax.experimental import pallas as pl
from jax.experimental.pallas import tpu as pltpu

F32 = jnp.float32
BF16 = jnp.bfloat16

D_MODEL = 2048
SEQ = 8192
CTX = 256
NTOK = SEQ + CTX
DEPTH = 4
GRID_W = 64

S5_W = 512
S5_CH = 16
S5_G = 32
S5_P = 64
CONV_W = 512
CONV_K = 31
ML_W = 1024
ML_H = 4
ML_DH = 256
D_FF = 5632
EPS = 1e-6
LN_EPS = 1e-5

OFF_CONV = 512
OFF_Q = 1536
OFF_O = 4608
OFF_G = 5632

LANES = 128
SUBLANES = 8
VMEM_LIMIT = 56 * 1024 * 1024

ROW_TILE = 256
MM_TILE_M = 1056
MM_TILE_N = 512
IN_NORM_TILE_M = 528
FFN_OUT_TILE_M = 704

S5_T = 8
S5_ROWS = NTOK // S5_T
S5_CTX_ROWS = CTX // S5_T
S5_BLK = S5_T * S5_CH
S5_STEP_G = 8
S5_STEP_PAIRS = S5_STEP_G // 2
S5_SET_CH = S5_STEP_G * S5_CH
S5_STEP_W = S5_STEP_G * S5_BLK
S5_FLAT = S5_G * S5_BLK

ML_CHUNK = 256
ML_NCHUNK = NTOK // ML_CHUNK


def _cparams(sem, vmem=VMEM_LIMIT):
    return pltpu.CompilerParams(dimension_semantics=sem, vmem_limit_bytes=vmem)


def _dot(a, b):
    return jnp.dot(a.astype(BF16), b.astype(BF16), preferred_element_type=F32)


def _split3(x):
    a = x.astype(BF16)
    r = x - a.astype(F32)
    b = r.astype(BF16)
    c = (r - b.astype(F32)).astype(BF16)
    return a, b, c


def _mod_body(cc_ref, w_ref, b_ref, o_ref):
    s = jax.nn.silu(cc_ref[...])
    o_ref[0] = _dot(s, w_ref[0]) + b_ref[0]


def _modulation(cc, w_mod, b_mod):
    depth, _, n = w_mod.shape
    tn = 1024
    return pl.pallas_call(
        _mod_body,
        grid=(depth, n // tn),
        in_specs=[
            pl.BlockSpec((SUBLANES, D_MODEL), lambda l, j: (0, 0)),
            pl.BlockSpec((1, D_MODEL, tn), lambda l, j: (l, 0, j)),
            pl.BlockSpec((1, 1, tn), lambda l, j: (l, 0, j)),
        ],
        out_specs=pl.BlockSpec((1, SUBLANES, tn), lambda l, j: (l, 0, j)),
        out_shape=jax.ShapeDtypeStruct((depth, SUBLANES, n), F32),
        compiler_params=_cparams(("arbitrary", "arbitrary")),
        name="adaln_modulation",
    )(cc, w_mod, b_mod.reshape(depth, 1, n))


def _mod_row(m_ref, is_ctx):
    return jnp.where(is_ctx, m_ref[0, 1:2, :], m_ref[0, 0:1, :])


def _stream_specs(xs, tm):
    if isinstance(xs, tuple):
        assert tm == CTX
        return [pl.BlockSpec((tm, D_MODEL), lambda i: (0, 0)),
                pl.BlockSpec((tm, D_MODEL), lambda i: (jnp.maximum(i - 1, 0), 0))], list(xs)
    return [pl.BlockSpec((tm, D_MODEL), lambda i: (i, 0))], [xs]


def _stream_tile(x_refs):
    if len(x_refs) == 1:
        return x_refs[0][...]
    return jnp.where(pl.program_id(0) == 0, x_refs[0][...], x_refs[1][...])


def _final_norm_body(x_ref, g_ref, o_ref):
    xf = x_ref[...]
    ms = jnp.mean(xf * xf, axis=-1, keepdims=True)
    o_ref[...] = xf * lax.rsqrt(ms + EPS) * g_ref[...]


def _final_norm(xs, g):
    skip = CTX // ROW_TILE
    return pl.pallas_call(
        _final_norm_body,
        grid=(SEQ // ROW_TILE,),
        in_specs=[pl.BlockSpec((ROW_TILE, D_MODEL), lambda i: (i + skip, 0)),
                  pl.BlockSpec((1, D_MODEL), lambda i: (0, 0))],
        out_specs=pl.BlockSpec((ROW_TILE, D_MODEL), lambda i: (i, 0)),
        out_shape=jax.ShapeDtypeStruct((SEQ, D_MODEL), F32),
        compiler_params=_cparams(("arbitrary",)),
        name="final_rmsnorm",
    )(xs, g.reshape(1, D_MODEL))


def _in_proj_norm_body(*refs, n_w, n_stream):
    x_refs, rest = refs[:n_stream], refs[n_stream:]
    (g_ref, sh_ref, sc_ref, wg_ref, bg_ref), rest = rest[:5], rest[5:]
    wt_refs, b_refs = rest[:n_w], rest[n_w:2 * n_w]
    o_ref, h_ref, gate_ref, wbf_ref = rest[2 * n_w:]
    tn = MM_TILE_N
    n_main = n_w * tn
    i = pl.program_id(0)

    @pl.when(i == 0)
    def _():
        for k, wt_ref in enumerate(wt_refs):
            wbf_ref[:, k * tn:(k + 1) * tn] = jnp.transpose(wt_ref[...]).astype(BF16)
        wbf_ref[:, n_main:n_main + LANES] = wg_ref[0]
        wbf_ref[:, n_main + LANES:] = wg_ref[1]

    xf = _stream_tile(x_refs)
    tm = xf.shape[0]
    is_ctx = i * tm + lax.broadcasted_iota(jnp.int32, (tm, 1), 0) < CTX
    pick = lambda ref: jnp.where(is_ctx, ref[0, 1:2, :], ref[0, 0:1, :])
    ms = jnp.mean(xf * xf, axis=-1, keepdims=True)
    h = xf * lax.rsqrt(ms + EPS) * (g_ref[...] * (1.0 + pick(sc_ref))) + pick(sh_ref)
    hi = h.astype(BF16)
    h_ref[...] = hi
    lo = (h - hi.astype(F32)).astype(BF16)
    acc = jnp.dot(hi, wbf_ref[...], preferred_element_type=F32)
    gate_ref[...] = (acc[:, n_main:n_main + LANES] + acc[:, n_main + LANES:]
                     + jnp.dot(lo, wg_ref[0], preferred_element_type=F32) + bg_ref[...])
    bias = jnp.concatenate([b_ref[...] for b_ref in b_refs], axis=1)
    o_ref[...] = acc[:, :n_main] + bias


def _in_proj_norm(xs, tm, g, mod_all, layer, w_gate_t, b_gate, w_in_t, b_in, n_w):
    tn = MM_TILE_N
    in_w = w_in_t.shape[1]
    once = dict(pipeline_mode=pl.Buffered(1))
    x_specs, x_args = _stream_specs(xs, tm)
    w_specs = [pl.BlockSpec((None, tn, D_MODEL), functools.partial(lambda i, k: (layer, k, 0), k=k), **once)
               for k in range(n_w)]
    b_specs = [pl.BlockSpec((None, 1, tn), functools.partial(lambda i, k: (layer, 0, k), k=k))
               for k in range(n_w)]
    b3 = b_in.reshape(DEPTH, 1, in_w)
    mod = lambda k: pl.BlockSpec((1, SUBLANES, D_MODEL), lambda i: (layer, 0, k))
    return pl.pallas_call(
        functools.partial(_in_proj_norm_body, n_w=n_w, n_stream=len(x_args)),
        grid=(NTOK // tm,),
        in_specs=x_specs + [
            pl.BlockSpec((1, D_MODEL), lambda i: (0, 0)),
            mod(0), mod(1),
            pl.BlockSpec((2, D_MODEL, LANES), lambda i: (0, 0, 0)),
            pl.BlockSpec((1, LANES), lambda i: (0, 0)),
        ] + w_specs + b_specs,
        out_specs=[pl.BlockSpec((tm, n_w * tn), lambda i: (i, 0)),
                   pl.BlockSpec((tm, D_MODEL), lambda i: (i, 0)),
                   pl.BlockSpec((tm, LANES), lambda i: (i, 0))],
        out_shape=[jax.ShapeDtypeStruct((NTOK, n_w * tn), F32),
                   jax.ShapeDtypeStruct((NTOK, D_MODEL), BF16),
                   jax.ShapeDtypeStruct((NTOK, LANES), F32)],
        scratch_shapes=[pltpu.VMEM((D_MODEL, n_w * tn + 2 * LANES), BF16)],
        compiler_params=_cparams(("arbitrary",)),
        name="norm_in_proj",
    )(*x_args, g.reshape(1, D_MODEL), mod_all, mod_all, w_gate_t, b_gate, *([w_in_t] * n_w), *([b3] * n_w))


def _row_gate(g_ref, i, tm, tn):
    rows = i * tm + lax.broadcasted_iota(jnp.int32, (tm, tn), 0)
    return jnp.where(rows < CTX, g_ref[0, 1:2, :], g_ref[0, 0:1, :])


def _out_proj_body(y_ref, wglu_ref, bglu_ref, cvc_ref, cvx_ref, hf_ref, hb_ref, o_ref, mlg_ref, w_ref,
                   gate_ref, sh_ref, sc_ref, ng_ref, *rest):
    x_refs, (xo_ref, h_ref, wbf_ref, wglu_bf_ref) = rest[:-4], rest[-4:]
    i = pl.program_id(0)

    @pl.when(i == 0)
    def _():
        wbf_ref[...] = w_ref[...].astype(BF16)
        wglu_bf_ref[...] = wglu_ref[...].astype(BF16)

    g = jax.nn.gelu(y_ref[...])
    glu = jnp.dot(g.astype(BF16), wglu_bf_ref[...], preferred_element_type=F32) + bglu_ref[...]
    s5 = (g * jax.nn.sigmoid(glu)).astype(BF16)
    acc = jnp.dot(s5, wbf_ref[0:S5_W, :], preferred_element_type=F32)
    acc += jnp.dot(_stream_tile([cvc_ref, cvx_ref]), wbf_ref[S5_W:S5_W + CONV_W, :],
                   preferred_element_type=F32)
    for head in range(ML_H):
        cols = slice(head * ML_DH, (head + 1) * ML_DH)
        hh = hf_ref[:, cols].astype(F32) + hb_ref[:, cols].astype(F32)
        hc = hh - jnp.mean(hh, axis=-1, keepdims=True)
        var = jnp.mean(hc * hc, axis=-1, keepdims=True)
        ml = jax.nn.sigmoid(o_ref[:, cols].astype(F32)) * (hc * lax.rsqrt(var + LN_EPS) * mlg_ref[:, cols])
        row0 = S5_W + CONV_W + head * ML_DH
        acc += jnp.dot(ml.astype(BF16), wbf_ref[row0:row0 + ML_DH, :], preferred_element_type=F32)
    is_ctx = i == 0
    xn = _stream_tile(x_refs) + _mod_row(gate_ref, is_ctx) * acc
    xo_ref[...] = xn
    ms = jnp.mean(xn * xn, axis=-1, keepdims=True)
    gain = ng_ref[...] * (1.0 + _mod_row(sc_ref, is_ctx))
    h_ref[...] = (xn * lax.rsqrt(ms + EPS) * gain + _mod_row(sh_ref, is_ctx)).astype(BF16)


def _out_proj(xs, y_s5, w_glu, b_glu, cvo, h_f, h_b, qkvo, ml_norm_g, w_out, mod_all, norm_g, layer):
    tm = ROW_TILE
    mod = lambda k: pl.BlockSpec((1, SUBLANES, D_MODEL), lambda i: (layer, 0, k))
    rows = lambda width, col=0: pl.BlockSpec((tm, width), lambda i: (i, col))
    const = lambda shape: pl.BlockSpec(shape, lambda i: (0,) * len(shape))
    x_specs, x_args = _stream_specs(xs, tm)
    cv_specs = [pl.BlockSpec((tm, CONV_W), lambda i: (0, 0)),
                pl.BlockSpec((tm, CONV_W), lambda i: (jnp.maximum(i - 1, 0), 0))]
    return pl.pallas_call(
        _out_proj_body,
        grid=(NTOK // tm,),
        in_specs=[
            rows(S5_W), const((S5_W, S5_W)), const((1, S5_W)),
        ] + cv_specs + [
            rows(ML_W), rows(ML_W), rows(ML_W, 3), const((1, ML_W)),
            pl.BlockSpec((None, D_MODEL, D_MODEL), lambda i: (layer, 0, 0), pipeline_mode=pl.Buffered(1)),
            mod(2), mod(3), mod(4),
            const((1, D_MODEL)),
        ] + x_specs,
        out_specs=[rows(D_MODEL), rows(D_MODEL)],
        out_shape=[jax.ShapeDtypeStruct((NTOK, D_MODEL), F32), jax.ShapeDtypeStruct((NTOK, D_MODEL), BF16)],
        scratch_shapes=[pltpu.VMEM((D_MODEL, D_MODEL), BF16), pltpu.VMEM((S5_W, S5_W), BF16)],
        compiler_params=_cparams(("arbitrary",)),
        name="out_proj_residual",
    )(y_s5, w_glu, b_glu.reshape(1, S5_W), *cvo, h_f, h_b, qkvo, ml_norm_g.reshape(1, ML_W), w_out,
      mod_all, mod_all, mod_all, norm_g.reshape(1, D_MODEL), *x_args)


def _ffn_in_body(a_ref, wg_ref, wu_ref, o_ref, wgbf_ref, wubf_ref):
    @pl.when(pl.program_id(1) == 0)
    def _():
        wgbf_ref[...] = wg_ref[...].astype(BF16)
        wubf_ref[...] = wu_ref[...].astype(BF16)

    a = a_ref[...]
    g = jnp.dot(a, wgbf_ref[...], preferred_element_type=F32)
    u = jnp.dot(a, wubf_ref[...], preferred_element_type=F32)
    o_ref[...] = (jax.nn.silu(g) * u).astype(BF16)


def _ffn_in(h, w_ffn_in, layer):
    tm, tn = MM_TILE_M, MM_TILE_N
    nj = D_FF // tn
    return pl.pallas_call(
        _ffn_in_body,
        grid=(nj, NTOK // tm),
        in_specs=[
            pl.BlockSpec((tm, D_MODEL), lambda j, i: (i, 0)),
            pl.BlockSpec((None, D_MODEL, tn), lambda j, i: (layer, 0, j)),
            pl.BlockSpec((None, D_MODEL, tn), lambda j, i: (layer, 0, nj + j)),
        ],
        out_specs=pl.BlockSpec((tm, tn), lambda j, i: (i, j)),
        out_shape=jax.ShapeDtypeStruct((NTOK, D_FF), BF16),
        scratch_shapes=[pltpu.VMEM((D_MODEL, tn), BF16), pltpu.VMEM((D_MODEL, tn), BF16)],
        compiler_params=_cparams(("arbitrary", "arbitrary")),
        name="ffn_in_swiglu",
    )(h, w_ffn_in, w_ffn_in)


def _ffn_out_body(a_ref, w_ref, g_ref, x_ref, o_ref, wbf_ref):
    i = pl.program_id(1)

    @pl.when(i == 0)
    def _():
        wbf_ref[...] = w_ref[...].astype(BF16)

    acc = jnp.dot(a_ref[...], wbf_ref[...], preferred_element_type=F32)
    tm, tn = o_ref.shape
    o_ref[...] = x_ref[...] + _row_gate(g_ref, i, tm, tn) * acc


def _ffn_out(xs, hid, w_ffn_out, mod_all, layer):
    tm, tn = FFN_OUT_TILE_M, MM_TILE_N
    return pl.pallas_call(
        _ffn_out_body,
        grid=(D_MODEL // tn, NTOK // tm),
        in_specs=[
            pl.BlockSpec((tm, D_FF), lambda j, i: (i, 0)),
            pl.BlockSpec((None, D_FF, tn), lambda j, i: (layer, 0, j)),
            pl.BlockSpec((1, SUBLANES, tn), lambda j, i: (layer, 0, 5 * (D_MODEL // tn) + j)),
            pl.BlockSpec((tm, tn), lambda j, i: (i, j)),
        ],
        out_specs=pl.BlockSpec((tm, tn), lambda j, i: (i, j)),
        out_shape=jax.ShapeDtypeStruct((NTOK, D_MODEL), F32),
        scratch_shapes=[pltpu.VMEM((D_FF, tn), BF16)],
        compiler_params=_cparams(("arbitrary", "arbitrary")),
        name="ffn_out_residual",
    )(hid, w_ffn_out, mod_all, xs)


def _s5_selectors():
    tau = np.arange(S5_BLK)[:, None] // S5_CH
    ch_r = np.arange(S5_BLK)[:, None] % S5_CH
    t = np.arange(S5_BLK)[None, :] // S5_CH
    ch_c = np.arange(S5_BLK)[None, :] % S5_CH
    sel = np.zeros((2, S5_T, S5_BLK, S5_BLK), np.float32)
    for s in range(S5_T):
        sel[0, s] = (tau == t - s) & (ch_r == ch_c)
        sel[1, s] = (tau == s - t) & (ch_r == ch_c)
    rp = np.arange(S5_BLK)[:, None]
    col = np.arange(2 * S5_BLK)[None, :]
    pair_map = np.stack([(col == (rp // S5_P) * S5_BLK + gi * S5_P + rp % S5_P) for gi in range(2)])
    src = np.arange(S5_STEP_W)
    dst = ((src // S5_CH) % S5_STEP_G) * S5_BLK + (src // S5_SET_CH) * S5_CH + src % S5_CH
    perm = dst[:, None] == np.arange(S5_STEP_W)[None, :]
    return (jnp.asarray(sel), jnp.asarray(pair_map, BF16), jnp.asarray(perm, BF16),
            jnp.asarray(perm.T, BF16))


def _s5_operators(lam_re, lam_im, log_step, b_re, b_im, c_re, c_im, sel):
    lam = lax.complex(lam_re.astype(F32), lam_im.astype(F32))
    lam_bar = jnp.exp(lam * jnp.exp(log_step.astype(F32)))
    b_bar = ((lam_bar - 1.0) / lam)[..., None] * lax.complex(b_re.astype(F32), b_im.astype(F32))
    c_mat = lax.complex(c_re.astype(F32), c_im.astype(F32))
    depth = lam.shape[0]

    def powers(base, count):
        out = [jnp.ones_like(base)]
        for _ in range(count - 1):
            out.append(out[-1] * base)
        return out

    pw = powers(lam_bar, S5_T + 1)
    pa = powers(pw[S5_T], SUBLANES + 1)
    t_up = list(range(S5_T))

    def table(seq, fwd_idx, bwd_idx):
        return jnp.stack([jnp.stack([seq[f][:, 0], seq[b][:, 1]], axis=1)
                          for f, b in zip(fwd_idx, bwd_idx)])

    blocks = (depth, 2, S5_G, S5_BLK, S5_BLK)

    def token_rows(w, imag_sign):
        w = jnp.moveaxis(w, 0, 3)
        return jnp.concatenate([jnp.real(w), imag_sign * jnp.imag(w)], axis=-1).reshape(blocks)

    pb = table(pw, [S5_T - 1 - t for t in t_up], t_up)
    wb = token_rows(pb[:, :, :, :, None, :] * jnp.swapaxes(b_bar, -1, -2)[None], 1.0)
    pc = table(pw, [t + 1 for t in t_up], [S5_T - t for t in t_up])
    wct = token_rows(pc[:, :, :, :, None, :] * c_mat[None], -1.0)

    lag_c = token_rows(jnp.stack(pw[:S5_T])[:, :, :, :, None, :] * c_mat[None], -1.0)
    b_t = jnp.swapaxes(b_bar, -1, -2)
    b_ri = jnp.concatenate([jnp.real(b_t), jnp.imag(b_t)], axis=-1)
    kern = jnp.einsum('ldgar,ldgxr->ldgax', b_ri, lag_c, precision=lax.Precision.HIGHEST)
    kern = kern.reshape(depth, 2, S5_G * S5_CH, S5_BLK)
    kc = jnp.einsum('ldxk,dskn->ldsxn', kern, sel, precision=lax.Precision.HIGHEST)
    kc = kc.reshape(depth, 2, S5_T, S5_G, S5_CH, S5_BLK)
    kc = jnp.transpose(kc, (0, 1, 3, 2, 4, 5)).reshape(blocks)

    def lanes(v):
        parts = jnp.stack([jnp.real(v), jnp.imag(v)], axis=4)
        parts = parts.reshape(v.shape[0], depth, 2, S5_G // 2, 2, 2, S5_P)
        parts = jnp.swapaxes(parts, 4, 5).reshape(v.shape[0], depth, 2, S5_FLAT)
        return jnp.transpose(parts, (1, 2, 0, 3))

    zero_p = jnp.zeros_like(pa[0])
    amul = lanes(jnp.stack([pa[1], pa[2], pa[4], pa[8]] + [zero_p] * 4))
    apow = lanes(table(pa, t_up, t_up[::-1]))
    return kc.astype(BF16), wb.astype(BF16), wct.astype(BF16), amul, apow


def _s5_row_scan(e_ref, amul_ref, apow_ref, block_lo, block_hi, reverse, carry):
    rows = lax.broadcasted_iota(jnp.int32, (SUBLANES, S5_BLK), 0)

    def shifted(x, k):
        if reverse:
            return jnp.where(rows < SUBLANES - k, pltpu.roll(x, SUBLANES - k, 0), 0.0)
        return jnp.where(rows >= k, pltpu.roll(x, k, 0), 0.0)

    def body(step, carry):
        blk = (block_hi - 1 - step) if reverse else (block_lo + step)
        r0 = pl.multiple_of(blk * SUBLANES, SUBLANES)
        last = 0 if reverse else SUBLANES - 1
        carry_out = []
        for q in range(S5_STEP_PAIRS):
            re_c = slice(2 * q * S5_BLK, (2 * q + 1) * S5_BLK)
            im_c = slice((2 * q + 1) * S5_BLK, (2 * q + 2) * S5_BLK)
            s_re = e_ref[pl.ds(r0, SUBLANES), re_c]
            s_im = e_ref[pl.ds(r0, SUBLANES), im_c]
            for idx, k in enumerate((1, 2, 4)):
                a_re, a_im = amul_ref[idx:idx + 1, re_c], amul_ref[idx:idx + 1, im_c]
                t_re, t_im = shifted(s_re, k), shifted(s_im, k)
                s_re, s_im = s_re + t_re * a_re - t_im * a_im, s_im + t_re * a_im + t_im * a_re
            c_re, c_im = carry[:, re_c], carry[:, im_c]
            p_re, p_im = apow_ref[:, re_c], apow_ref[:, im_c]
            e_ref[pl.ds(r0, SUBLANES), re_c] = shifted(s_re, 1) + p_re * c_re - p_im * c_im
            e_ref[pl.ds(r0, SUBLANES), im_c] = shifted(s_im, 1) + p_re * c_im + p_im * c_re
            a_re, a_im = amul_ref[3:4, re_c], amul_ref[3:4, im_c]
            l_re = jnp.broadcast_to(s_re[last:last + 1, :], (SUBLANES, S5_BLK))
            l_im = jnp.broadcast_to(s_im[last:last + 1, :], (SUBLANES, S5_BLK))
            carry_out += [l_re + a_re * c_re - a_im * c_im, l_im + a_re * c_im + a_im * c_re]
        return jnp.concatenate(carry_out, axis=1)

    return lax.fori_loop(0, block_hi - block_lo, body, carry)


def _s5_pair_operators(kc_ref, wb_ref, wct_ref, map_ref, q):
    g0, g1 = 2 * q, 2 * q + 1
    zero = jnp.zeros((S5_BLK, S5_BLK), BF16)
    kc = jnp.concatenate([jnp.concatenate([kc_ref[g0], zero], axis=1),
                          jnp.concatenate([zero, kc_ref[g1]], axis=1)], axis=0)
    spread = lambda ref: jnp.concatenate(
        [jnp.dot(ref[g0], map_ref[0], preferred_element_type=F32),
         jnp.dot(ref[g1], map_ref[1], preferred_element_type=F32)], axis=0).astype(BF16)
    return kc, spread(wb_ref), spread(wct_ref)


def _s5_body(u_ref, p_ref, pt_ref, map_ref, kc_ref, wb_ref, wct_ref, amul_ref, apow_ref, d_ref, y_ref,
             up_ref, e_ref, acc_ref):
    direction = pl.program_id(1)
    token_rows = lambda t: pl.ds(t, S5_ROWS, stride=S5_T)
    pairs = [slice(2 * q * S5_BLK, 2 * (q + 1) * S5_BLK) for q in range(S5_STEP_PAIRS)]

    @pl.when(direction == 0)
    def _():
        u_nat = jnp.concatenate([u_ref[token_rows(t), :] for t in range(S5_T)], axis=1)
        up_ref[...] = jnp.dot(u_nat.astype(BF16), p_ref[...], preferred_element_type=F32).astype(BF16)
        acc_ref[...] = jnp.zeros_like(acc_ref)

    ops = [_s5_pair_operators(kc_ref, wb_ref, wct_ref, map_ref, q) for q in range(S5_STEP_PAIRS)]
    for (_, wb, _), cols in zip(ops, pairs):
        e_ref[:, cols] = jnp.dot(up_ref[:, cols], wb, preferred_element_type=F32)

    zero = jnp.zeros((SUBLANES, S5_STEP_W), F32)
    n_blocks = S5_ROWS // SUBLANES
    ctx_blocks = S5_CTX_ROWS // SUBLANES
    amul, apow = amul_ref, apow_ref

    @pl.when(direction == 0)
    def _():
        _s5_row_scan(e_ref, amul, apow, 0, n_blocks, False, zero)

    @pl.when(direction == 1)
    def _():
        carry = _s5_row_scan(e_ref, amul, apow, 0, ctx_blocks, True, zero)
        _s5_row_scan(e_ref, amul, apow, ctx_blocks, n_blocks, True, carry)

    for (kc, _, wct), cols in zip(ops, pairs):
        y = jnp.dot(up_ref[:, cols], kc, preferred_element_type=F32)
        y += lax.dot_general(e_ref[:, cols].astype(BF16), wct, (((1,), (1,)), ((), ())),
                             preferred_element_type=F32)
        acc_ref[:, cols] += y

    @pl.when(direction == 1)
    def _():
        acc = acc_ref[...]
        hi = acc.astype(BF16)
        lo = (acc - hi.astype(F32)).astype(BF16)
        y_nat = (jnp.dot(hi, pt_ref[...], preferred_element_type=F32)
                 + jnp.dot(lo, pt_ref[...], preferred_element_type=F32))
        for t in range(S5_T):
            y_ref[token_rows(t), :] = (y_nat[:, t * S5_SET_CH:(t + 1) * S5_SET_CH]
                                       + d_ref[...] * u_ref[token_rows(t), :])


def _s5_mixer(z, ops, consts, layer, d_skip):
    kc, wb, wct, amul, apow = ops
    _, pair_map, perm, perm_t = consts
    n_steps = S5_G // S5_STEP_G
    wspec = pl.BlockSpec((None, None, S5_STEP_G, S5_BLK, S5_BLK), lambda j, d: (layer, d, j, 0, 0))
    cspec = pl.BlockSpec((None, None, SUBLANES, S5_STEP_W), lambda j, d: (layer, d, 0, j))
    pspec = pl.BlockSpec((S5_STEP_W, S5_STEP_W), lambda j, d: (0, 0))
    return pl.pallas_call(
        _s5_body,
        grid=(n_steps, 2),
        in_specs=[
            pl.BlockSpec((NTOK, S5_SET_CH), lambda j, d: (0, j)),
            pspec, pspec,
            pl.BlockSpec((2, S5_BLK, 2 * S5_BLK), lambda j, d: (0, 0, 0)),
            wspec, wspec, wspec, cspec, cspec,
            pl.BlockSpec((1, S5_SET_CH), lambda j, d: (0, j)),
        ],
        out_specs=pl.BlockSpec((NTOK, S5_SET_CH), lambda j, d: (0, j)),
        out_shape=jax.ShapeDtypeStruct((NTOK, S5_W), F32),
        scratch_shapes=[pltpu.VMEM((S5_ROWS, S5_STEP_W), BF16), pltpu.VMEM((S5_ROWS, S5_STEP_W), F32),
                        pltpu.VMEM((S5_ROWS, S5_STEP_W), F32)],
        compiler_params=_cparams(("arbitrary", "arbitrary")),
        name="s5_scan",
    )(z, perm, perm_t, pair_map, kc, wb, wct, amul, apow, d_skip.astype(F32).reshape(1, S5_W))


CONV_PAD = 16
CONV_BLK = 64


CONV_PAD_ROWS = max(CTX + 2 * CONV_PAD, (ROW_TILE // GRID_W) * (GRID_W + 2 * CONV_PAD))


def _conv_tile(a_ref, b_ref, w_ref, db_ref, lg_ref, lb_ref, o_ref, pad_ref, sh_ref, seq_len):
    g = a_ref[...] * jax.nn.sigmoid(b_ref[...])
    n_seq = ROW_TILE // seq_len
    pitch = seq_len + 2 * CONV_PAD
    zeros = jnp.zeros((CONV_PAD, CONV_W), F32)
    for s in range(n_seq):
        pad_ref[s * pitch:s * pitch + CONV_PAD, :] = zeros
        pad_ref[s * pitch + CONV_PAD:s * pitch + CONV_PAD + seq_len, :] = g[s * seq_len:(s + 1) * seq_len]
        pad_ref[s * pitch + CONV_PAD + seq_len:(s + 1) * pitch, :] = zeros
    used = n_seq * pitch
    for r in range(1, SUBLANES):
        sh_ref[r, 0:used - SUBLANES, :] = pad_ref[r:r + used - SUBLANES, :]
    for blk in range(ROW_TILE // CONV_BLK):
        row0 = blk * CONV_BLK
        s, q = divmod(row0, seq_len)
        base = s * pitch + CONV_PAD + q - CONV_K // 2
        acc = jnp.zeros((CONV_BLK, CONV_W), F32)
        for k in range(CONV_K):
            r = (base + k) % SUBLANES
            row = base + k - r
            tap = pad_ref[row:row + CONV_BLK, :] if r == 0 else sh_ref[r, row:row + CONV_BLK, :]
            acc = acc + w_ref[k:k + 1, :] * tap
        y = acc + db_ref[...]
        yc = y - jnp.mean(y, axis=-1, keepdims=True)
        var = jnp.mean(yc * yc, axis=-1, keepdims=True)
        y = yc * lax.rsqrt(var + LN_EPS) * lg_ref[...] + lb_ref[...]
        o_ref[row0:row0 + CONV_BLK, :] = jax.nn.silu(y).astype(BF16)


def _in_proj_conv_body(h_ref, wt0_ref, wt1_ref, b0_ref, b1_ref, za_ref, zb_ref, zac_ref, zbc_ref,
                       cw_ref, cdb_ref, clg_ref, clb_ref, o_ref, cvc_ref, cvx_ref,
                       wbf_ref, pad_ref, sh_ref):
    j, i = pl.program_id(0), pl.program_id(1)
    tn = MM_TILE_N
    conv = (cw_ref, cdb_ref, clg_ref, clb_ref)

    @pl.when(i == 0)
    def _():
        wbf_ref[:, 0:tn] = jnp.transpose(wt0_ref[...]).astype(BF16)
        wbf_ref[:, tn:2 * tn] = jnp.transpose(wt1_ref[...]).astype(BF16)

    @pl.when(jnp.logical_and(j == 0, i == 0))
    def _():
        _conv_tile(zac_ref, zbc_ref, *conv, cvc_ref, pad_ref, sh_ref, CTX)

    bias = jnp.concatenate([b0_ref[...], b1_ref[...]], axis=1)
    acc = jnp.dot(h_ref[...], wbf_ref[...], preferred_element_type=F32) + bias
    o_ref[...] = acc.astype(o_ref.dtype)
    _conv_tile(za_ref, zb_ref, *conv, cvx_ref, pad_ref, sh_ref, GRID_W)


def _in_proj_conv(h, w_in_t, b_in, layer, z, dw_w, dw_b, ln_g, ln_b):
    tm, tn = MM_TILE_M, MM_TILE_N
    n_j = (OFF_G - OFF_Q) // (2 * tn)
    n_i = NTOK // tm
    assert n_j * n_i == SEQ // ROW_TILE
    first = OFF_Q // tn
    in_w = w_in_t.shape[1]
    b3 = b_in.reshape(DEPTH, 1, in_w)
    w = jnp.concatenate([dw_w, jnp.zeros((1, CONV_W), dw_w.dtype)], axis=0)
    vec = lambda v: v.reshape(1, CONV_W)
    a_col = OFF_CONV // CONV_W
    x_tile = lambda j, i: j * n_i + i + CTX // ROW_TILE
    const = lambda shape: pl.BlockSpec(shape, lambda j, i: (0,) * len(shape))
    return pl.pallas_call(
        _in_proj_conv_body,
        grid=(n_j, n_i),
        in_specs=[
            pl.BlockSpec((tm, D_MODEL), lambda j, i: (i, 0)),
            pl.BlockSpec((None, tn, D_MODEL), lambda j, i: (layer, first + 2 * j, 0)),
            pl.BlockSpec((None, tn, D_MODEL), lambda j, i: (layer, first + 2 * j + 1, 0)),
            pl.BlockSpec((None, 1, tn), lambda j, i: (layer, 0, first + 2 * j)),
            pl.BlockSpec((None, 1, tn), lambda j, i: (layer, 0, first + 2 * j + 1)),
            pl.BlockSpec((ROW_TILE, CONV_W), lambda j, i: (x_tile(j, i), a_col)),
            pl.BlockSpec((ROW_TILE, CONV_W), lambda j, i: (x_tile(j, i), a_col + 1)),
            pl.BlockSpec((ROW_TILE, CONV_W), lambda j, i: (0, a_col)),
            pl.BlockSpec((ROW_TILE, CONV_W), lambda j, i: (0, a_col + 1)),
            const((CONV_K + 1, CONV_W)), const((1, CONV_W)), const((1, CONV_W)), const((1, CONV_W)),
        ],
        out_specs=[pl.BlockSpec((tm, 2 * tn), lambda j, i: (i, j)),
                   const((CTX, CONV_W)),
                   pl.BlockSpec((ROW_TILE, CONV_W), lambda j, i: (j * n_i + i, 0))],
        out_shape=[jax.ShapeDtypeStruct((NTOK, OFF_G - OFF_Q), BF16),
                   jax.ShapeDtypeStruct((CTX, CONV_W), BF16),
                   jax.ShapeDtypeStruct((SEQ, CONV_W), BF16)],
        scratch_shapes=[pltpu.VMEM((D_MODEL, 2 * tn), BF16), pltpu.VMEM((CONV_PAD_ROWS, CONV_W), F32),
                        pltpu.VMEM((SUBLANES, CONV_PAD_ROWS, CONV_W), F32)],
        compiler_params=_cparams(("arbitrary", "arbitrary")),
        name="in_proj_conv",
    )(h, w_in_t, w_in_t, b3, b3, z, z, z, z, w, vec(dw_b), vec(ln_g), vec(ln_b))


def _ml_chunk_index(ci, reverse):
    if not reverse:
        return ci
    return jnp.where(ci == 0, 0, ML_NCHUNK - ci)


def _ml_chain_stages(qb, kb, vx, li_rep, li_row, b_rep, b_row, cx_ref, m_ref, mask, reverse, store_h):
    c = ML_CHUNK
    scale = ML_DH ** 0.5
    wide = lambda a: jnp.concatenate([a] * (ML_DH // LANES), axis=1)
    v = {}

    def scores():
        v['m'] = m_ref[...]
        v['d_log'] = jnp.where(mask, wide(b_rep) + (li_row - b_row), -jnp.inf)
        v['inter'] = b_rep + v['m']
        v['m_row'] = jnp.maximum(v['inter'], jnp.max(v['d_log'], axis=-1, keepdims=True))
        v['qk'] = lax.dot_general(qb, kb, (((1,), (1,)), ((), ())), preferred_element_type=F32)

    def numerator():
        s = v['qk'] * jnp.exp(v['d_log'] - wide(v['m_row']))
        w_inter = jnp.exp(v['inter'] - v['m_row']) * scale
        v['cx'] = cx_ref[...]
        lhs = jnp.concatenate([s.astype(BF16), qb * wide(w_inter.astype(BF16))], axis=1)
        rhs = jnp.concatenate([vx, v['cx'].astype(BF16)], axis=0)
        v['num'] = jnp.dot(lhs, rhs, preferred_element_type=F32)

    def output():
        num = v['num']
        den = num[:, ML_DH:]
        recip = 1.0 / jnp.maximum(jnp.abs(den), jnp.exp(-v['m_row']) * scale)
        store_h(num[:, :ML_DH] * wide(recip))

    def state():
        m = v['m']
        b_tot = b_row[:, 0:1] if reverse else b_row[:, c - 1:c]
        g = b_tot - b_rep + li_rep
        m_new = jnp.maximum(b_tot + m, jnp.max(g, axis=0, keepdims=True))
        kw = kb * wide((jnp.exp(g - m_new) * (1.0 / scale)).astype(BF16))
        decay = jnp.exp(b_tot + m - m_new)[:, 0:1]
        cx_ref[...] = decay * v['cx'] + lax.dot_general(kw, vx, (((0,), (0,)), ((), ())),
                                                       preferred_element_type=F32)
        m_ref[...] = m_new

    return [scores, numerator, output, state]


def _mlstm_body(qkvf_ref, gtf_ref, qkvb_ref, gtb_ref, hf_ref, hb_ref, cx_ref, m_ref):
    @pl.when(pl.program_id(0) == 0)
    def _():
        cx_ref[...] = jnp.zeros_like(cx_ref)
        m_ref[...] = jnp.zeros_like(m_ref)

    c = ML_CHUNK
    r_idx = lax.broadcasted_iota(jnp.int32, (c, c), 0)
    c_idx = lax.broadcasted_iota(jnp.int32, (c, c), 1)
    ones_col = jnp.ones((c, LANES), BF16)
    n_piece = 3
    spread_r = lax.broadcasted_iota(jnp.int32, (SUBLANES, 2 * LANES), 0)
    spread_c = lax.broadcasted_iota(jnp.int32, (SUBLANES, 2 * LANES), 1)
    in_a = jnp.logical_and(spread_r < n_piece, spread_c < LANES)
    in_b = jnp.logical_and(jnp.logical_and(spread_r >= n_piece, spread_r < 2 * n_piece), spread_c >= LANES)
    spread = jnp.where(jnp.logical_or(in_a, in_b), 1.0, 0.0).astype(BF16)

    def replicate(a_rows, b_rows):
        pad = jnp.zeros((SUBLANES - 2 * n_piece, c), F32)
        rows = jnp.concatenate(list(a_rows) + list(b_rows) + [pad], axis=0).astype(BF16)
        both = lax.dot_general(rows, spread, (((0,), (0,)), ((), ())), preferred_element_type=F32)
        return both[:, :LANES], both[:, LANES:]

    chains = []
    for d, (qkv_ref, gt_ref, h_ref) in enumerate(((qkvf_ref, gtf_ref, hf_ref), (qkvb_ref, gtb_ref, hb_ref))):
        reverse = d == 1
        gates_t = gt_ref[...]
        incl = (c_idx >= r_idx) if reverse else (c_idx <= r_idx)
        incl_t = (r_idx >= c_idx) if reverse else (r_idx <= c_idx)
        tri_t = jnp.where(incl_t, 1.0, 0.0).astype(BF16)
        lf_t = jax.nn.log_sigmoid(gates_t)
        b_all_t = sum(jnp.dot(p, tri_t, preferred_element_type=F32) for p in _split3(lf_t))
        gate_rows = [p.astype(F32) for p in _split3(gates_t)]
        b_rows = [p.astype(F32) for p in _split3(b_all_t)]
        for head in range(ML_H):
            i_col = 2 * ML_H * d + head
            f_col = i_col + ML_H
            state = d * ML_H + head
            col = lambda part: slice((part * ML_H + head) * ML_DH, (part * ML_H + head + 1) * ML_DH)
            vx = jnp.concatenate([qkv_ref[:, col(2)], ones_col], axis=1)

            def store_h(h, h_ref=h_ref, head=head):
                h_ref[:, head * ML_DH:(head + 1) * ML_DH] = h.astype(h_ref.dtype)

            li_rep, b_rep = replicate([p[i_col:i_col + 1, :] for p in gate_rows],
                                      [p[f_col:f_col + 1, :] for p in b_rows])
            chains.append(_ml_chain_stages(
                qkv_ref[:, col(0)], qkv_ref[:, col(1)], vx,
                li_rep, gates_t[i_col:i_col + 1, :],
                b_rep, b_all_t[f_col:f_col + 1, :],
                cx_ref.at[state], m_ref.at[state], incl, reverse, store_h))
    n_stage = len(chains[0])
    for t in range(len(chains) + n_stage - 1):
        for s in range(n_stage):
            if 0 <= t - s < len(chains):
                chains[t - s][s]()


def _mlstm_mixer(qkvo, gates):
    gates_t = jnp.transpose(gates[:, :2 * SUBLANES])
    c = ML_CHUNK

    def specs(reverse):
        row = lambda ci: _ml_chunk_index(ci, reverse)
        return [
            pl.BlockSpec((c, 3 * ML_W), lambda ci: (row(ci), 0)),
            pl.BlockSpec((2 * SUBLANES, c), lambda ci: (0, row(ci))),
        ]

    out_spec = lambda reverse: pl.BlockSpec((c, ML_W), lambda ci: (_ml_chunk_index(ci, reverse), 0))
    n_state = 2 * ML_H
    return pl.pallas_call(
        _mlstm_body,
        grid=(ML_NCHUNK,),
        in_specs=specs(False) + specs(True),
        out_specs=[out_spec(False), out_spec(True)],
        out_shape=[jax.ShapeDtypeStruct((NTOK, ML_W), BF16)] * 2,
        scratch_shapes=[pltpu.VMEM((n_state, ML_DH, ML_DH + LANES), F32),
                        pltpu.VMEM((n_state, 1, LANES), F32)],
        compiler_params=_cparams(("arbitrary",)),
        name="mlstm_chunks",
    )(qkvo, gates_t, qkvo, gates_t)


def kernel(x, c, ctx, c_ctx, w_mod, b_mod, norm1_g, w_in, b_in, s5_lam_re, s5_lam_im, s5_log_step,
           s5_b_re, s5_b_im, s5_c_re, s5_c_im, s5_d, s5_w_glu, s5_b_glu, conv_dw_w, conv_dw_b,
           conv_ln_g, conv_ln_b, ml_norm_g, w_out, norm2_g, w_ffn_in, w_ffn_out, norm_f_g):
    assert x.shape == (1, SEQ, D_MODEL) and ctx.shape == (1, CTX, D_MODEL)
    xs = (ctx[0].astype(F32), x[0].astype(F32))
    cc = jnp.zeros((SUBLANES, D_MODEL), F32).at[0].set(c[0]).at[1].set(c_ctx)
    mod_all = _modulation(cc, w_mod, b_mod)
    s5_consts = _s5_selectors()
    s5_ops = _s5_operators(s5_lam_re, s5_lam_im, s5_log_step, s5_b_re, s5_b_im, s5_c_re, s5_c_im,
                           s5_consts[0])
    w_in_t = jnp.swapaxes(w_in, 1, 2)
    n_gate = w_in.shape[2] - OFF_G
    w_gate = w_in[:, :, OFF_G:]
    w_gate_hi = w_gate.astype(BF16)
    w_gate = jnp.stack([w_gate_hi, (w_gate - w_gate_hi.astype(F32)).astype(BF16)], axis=1)
    w_gate_t = jnp.pad(w_gate, ((0, 0), (0, 0), (0, 0), (0, LANES - n_gate)))
    b_gate = jnp.pad(b_in[:, OFF_G:], ((0, 0), (0, LANES - n_gate))).reshape(DEPTH, 1, LANES)
    tn = MM_TILE_N

    for l in range(DEPTH):
        tm = CTX if l == 0 else IN_NORM_TILE_M
        z, h, gates = _in_proj_norm(xs, tm, norm1_g[l], mod_all, l, w_gate_t[l], b_gate[l], w_in_t, b_in,
                                    OFF_Q // tn)
        qkvo, cv_c, cv_x = _in_proj_conv(h, w_in_t, b_in, l, z, conv_dw_w[l], conv_dw_b[l], conv_ln_g[l],
                                         conv_ln_b[l])
        y_s5 = _s5_mixer(z, s5_ops, s5_consts, l, s5_d[l])
        h_f, h_b = _mlstm_mixer(qkvo, gates)
        xs, h2 = _out_proj(xs, y_s5, s5_w_glu[l], s5_b_glu[l], (cv_c, cv_x), h_f, h_b, qkvo, ml_norm_g[l], w_out,
                           mod_all, norm2_g[l], l)
        hid = _ffn_in(h2, w_ffn_in, l)
        xs = _ffn_out(xs, hid, w_ffn_out, mod_all, l)
    return _final_norm(xs, norm_f_g)[None]
```

```python
import functools

import numpy as np
import jax
import jax.numpy as jnp
from jax import lax
from jax.experimental import pallas as pl
from jax.experimental.pallas import tpu as pltpu

F32 = jnp.float32
BF16 = jnp.bfloat16

D_MODEL = 2048
SEQ = 8192
CTX = 256
NTOK = SEQ + CTX
DEPTH = 4
GRID_W = 64

S5_W = 512
S5_CH = 16
S5_G = 32
S5_P = 64
CONV_W = 512
CONV_K = 31
ML_W = 1024
ML_H = 4
ML_DH = 256
D_FF = 5632
EPS = 1e-6
LN_EPS = 1e-5

OFF_CONV = 512
OFF_Q = 1536
OFF_O = 4608
OFF_G = 5632

LANES = 128
SUBLANES = 8
VMEM_LIMIT = 56 * 1024 * 1024

ROW_TILE = 256
MM_TILE_M = 1056
MM_TILE_N = 512
IN_NORM_TILE_M = 528
FFN_OUT_TILE_M = 704

S5_T = 8
S5_ROWS = NTOK // S5_T
S5_CTX_ROWS = CTX // S5_T
S5_BLK = S5_T * S5_CH
S5_STEP_G = 8
S5_STEP_PAIRS = S5_STEP_G // 2
S5_SET_CH = S5_STEP_G * S5_CH
S5_STEP_W = S5_STEP_G * S5_BLK
S5_FLAT = S5_G * S5_BLK

ML_CHUNK = 256
ML_NCHUNK = NTOK // ML_CHUNK


def _cparams(sem, vmem=VMEM_LIMIT):
    return pltpu.CompilerParams(dimension_semantics=sem, vmem_limit_bytes=vmem)


def _dot(a, b):
    return jnp.dot(a.astype(BF16), b.astype(BF16), preferred_element_type=F32)


def _split3(x):
    a = x.astype(BF16)
    r = x - a.astype(F32)
    b = r.astype(BF16)
    c = (r - b.astype(F32)).astype(BF16)
    return a, b, c


def _mod_body(cc_ref, w_ref, b_ref, o_ref):
    s = jax.nn.silu(cc_ref[...])
    o_ref[0] = _dot(s, w_ref[0]) + b_ref[0]


def _modulation(cc, w_mod, b_mod):
    depth, _, n = w_mod.shape
    tn = 1024
    return pl.pallas_call(
        _mod_body,
        grid=(depth, n // tn),
        in_specs=[
            pl.BlockSpec((SUBLANES, D_MODEL), lambda l, j: (0, 0)),
            pl.BlockSpec((1, D_MODEL, tn), lambda l, j: (l, 0, j)),
            pl.BlockSpec((1, 1, tn), lambda l, j: (l, 0, j)),
        ],
        out_specs=pl.BlockSpec((1, SUBLANES, tn), lambda l, j: (l, 0, j)),
        out_shape=jax.ShapeDtypeStruct((depth, SUBLANES, n), F32),
        compiler_params=_cparams(("arbitrary", "arbitrary")),
        name="adaln_modulation",
    )(cc, w_mod, b_mod.reshape(depth, 1, n))


def _mod_row(m_ref, is_ctx):
    return jnp.where(is_ctx, m_ref[0, 1:2, :], m_ref[0, 0:1, :])


def _stream_specs(xs, tm):
    if isinstance(xs, tuple):
        assert tm == CTX
        return [pl.BlockSpec((tm, D_MODEL), lambda i: (0, 0)),
                pl.BlockSpec((tm, D_MODEL), lambda i: (jnp.maximum(i - 1, 0), 0))], list(xs)
    return [pl.BlockSpec((tm, D_MODEL), lambda i: (i, 0))], [xs]


def _stream_tile(x_refs):
    if len(x_refs) == 1:
        return x_refs[0][...]
    return jnp.where(pl.program_id(0) == 0, x_refs[0][...], x_refs[1][...])


def _final_norm_body(x_ref, g_ref, o_ref):
    xf = x_ref[...]
    ms = jnp.mean(xf * xf, axis=-1, keepdims=True)
    o_ref[...] = xf * lax.rsqrt(ms + EPS) * g_ref[...]


def _final_norm(xs, g):
    skip = CTX // ROW_TILE
    return pl.pallas_call(
        _final_norm_body,
        grid=(SEQ // ROW_TILE,),
        in_specs=[pl.BlockSpec((ROW_TILE, D_MODEL), lambda i: (i + skip, 0)),
                  pl.BlockSpec((1, D_MODEL), lambda i: (0, 0))],
        out_specs=pl.BlockSpec((ROW_TILE, D_MODEL), lambda i: (i, 0)),
        out_shape=jax.ShapeDtypeStruct((SEQ, D_MODEL), F32),
        compiler_params=_cparams(("arbitrary",)),
        name="final_rmsnorm",
    )(xs, g.reshape(1, D_MODEL))


def _in_proj_norm_body(*refs, n_w, n_stream):
    x_refs, rest = refs[:n_stream], refs[n_stream:]
    (g_ref, sh_ref, sc_ref, wg_ref, bg_ref), rest = rest[:5], rest[5:]
    wt_refs, b_refs = rest[:n_w], rest[n_w:2 * n_w]
    o_ref, h_ref, gate_ref, wbf_ref = rest[2 * n_w:]
    tn = MM_TILE_N
    n_main = n_w * tn
    i = pl.program_id(0)

    @pl.when(i == 0)
    def _():
        for k, wt_ref in enumerate(wt_refs):
            wbf_ref[:, k * tn:(k + 1) * tn] = jnp.transpose(wt_ref[...]).astype(BF16)
        wbf_ref[:, n_main:n_main + LANES] = wg_ref[0]
        wbf_ref[:, n_main + LANES:] = wg_ref[1]

    xf = _stream_tile(x_refs)
    tm = xf.shape[0]
    is_ctx = i * tm + lax.broadcasted_iota(jnp.int32, (tm, 1), 0) < CTX
    pick = lambda ref: jnp.where(is_ctx, ref[0, 1:2, :], ref[0, 0:1, :])
    ms = jnp.mean(xf * xf, axis=-1, keepdims=True)
    h = xf * lax.rsqrt(ms + EPS) * (g_ref[...] * (1.0 + pick(sc_ref))) + pick(sh_ref)
    hi = h.astype(BF16)
    h_ref[...] = hi
    lo = (h - hi.astype(F32)).astype(BF16)
    acc = jnp.dot(hi, wbf_ref[...], preferred_element_type=F32)
    gate_ref[...] = (acc[:, n_main:n_main + LANES] + acc[:, n_main + LANES:]
                     + jnp.dot(lo, wg_ref[0], preferred_element_type=F32) + bg_ref[...])
    bias = jnp.concatenate([b_ref[...] for b_ref in b_refs], axis=1)
    o_ref[...] = acc[:, :n_main] + bias


def _in_proj_norm(xs, tm, g, mod_all, layer, w_gate_t, b_gate, w_in_t, b_in, n_w):
    tn = MM_TILE_N
    in_w = w_in_t.shape[1]
    once = dict(pipeline_mode=pl.Buffered(1))
    x_specs, x_args = _stream_specs(xs, tm)
    w_specs = [pl.BlockSpec((None, tn, D_MODEL), functools.partial(lambda i, k: (layer, k, 0), k=k), **once)
               for k in range(n_w)]
    b_specs = [pl.BlockSpec((None, 1, tn), functools.partial(lambda i, k: (layer, 0, k), k=k))
               for k in range(n_w)]
    b3 = b_in.reshape(DEPTH, 1, in_w)
    mod = lambda k: pl.BlockSpec((1, SUBLANES, D_MODEL), lambda i: (layer, 0, k))
    return pl.pallas_call(
        functools.partial(_in_proj_norm_body, n_w=n_w, n_stream=len(x_args)),
        grid=(NTOK // tm,),
        in_specs=x_specs + [
            pl.BlockSpec((1, D_MODEL), lambda i: (0, 0)),
            mod(0), mod(1),
            pl.BlockSpec((2, D_MODEL, LANES), lambda i: (0, 0, 0)),
            pl.BlockSpec((1, LANES), lambda i: (0, 0)),
        ] + w_specs + b_specs,
        out_specs=[pl.BlockSpec((tm, n_w * tn), lambda i: (i, 0)),
                   pl.BlockSpec((tm, D_MODEL), lambda i: (i, 0)),
                   pl.BlockSpec((tm, LANES), lambda i: (i, 0))],
        out_shape=[jax.ShapeDtypeStruct((NTOK, n_w * tn), F32),
                   jax.ShapeDtypeStruct((NTOK, D_MODEL), BF16),
                   jax.ShapeDtypeStruct((NTOK, LANES), F32)],
        scratch_shapes=[pltpu.VMEM((D_MODEL, n_w * tn + 2 * LANES), BF16)],
        compiler_params=_cparams(("arbitrary",)),
        name="norm_in_proj",
    )(*x_args, g.reshape(1, D_MODEL), mod_all, mod_all, w_gate_t, b_gate, *([w_in_t] * n_w), *([b3] * n_w))


def _row_gate(g_ref, i, tm, tn):
    rows = i * tm + lax.broadcasted_iota(jnp.int32, (tm, tn), 0)
    return jnp.where(rows < CTX, g_ref[0, 1:2, :], g_ref[0, 0:1, :])


def _out_proj_body(y_ref, wglu_ref, bglu_ref, cvc_ref, cvx_ref, hf_ref, hb_ref, o_ref, mlg_ref, w_ref,
                   gate_ref, sh_ref, sc_ref, ng_ref, *rest):
    x_refs, (xo_ref, h_ref, wbf_ref, wglu_bf_ref) = rest[:-4], rest[-4:]
    i = pl.program_id(0)

    @pl.when(i == 0)
    def _():
        wbf_ref[...] = w_ref[...].astype(BF16)
        wglu_bf_ref[...] = wglu_ref[...].astype(BF16)

    g = jax.nn.gelu(y_ref[...])
    glu = jnp.dot(g.astype(BF16), wglu_bf_ref[...], preferred_element_type=F32) + bglu_ref[...]
    s5 = (g * jax.nn.sigmoid(glu)).astype(BF16)
    acc = jnp.dot(s5, wbf_ref[0:S5_W, :], preferred_element_type=F32)
    acc += jnp.dot(_stream_tile([cvc_ref, cvx_ref]), wbf_ref[S5_W:S5_W + CONV_W, :],
                   preferred_element_type=F32)
    for head in range(ML_H):
        cols = slice(head * ML_DH, (head + 1) * ML_DH)
        hh = hf_ref[:, cols].astype(F32) + hb_ref[:, cols].astype(F32)
        hc = hh - jnp.mean(hh, axis=-1, keepdims=True)
        var = jnp.mean(hc * hc, axis=-1, keepdims=True)
        ml = jax.nn.sigmoid(o_ref[:, cols].astype(F32)) * (hc * lax.rsqrt(var + LN_EPS) * mlg_ref[:, cols])
        row0 = S5_W + CONV_W + head * ML_DH
        acc += jnp.dot(ml.astype(BF16), wbf_ref[row0:row0 + ML_DH, :], preferred_element_type=F32)
    is_ctx = i == 0
    xn = _stream_tile(x_refs) + _mod_row(gate_ref, is_ctx) * acc
    xo_ref[...] = xn
    ms = jnp.mean(xn * xn, axis=-1, keepdims=True)
    gain = ng_ref[...] * (1.0 + _mod_row(sc_ref, is_ctx))
    h_ref[...] = (xn * lax.rsqrt(ms + EPS) * gain + _mod_row(sh_ref, is_ctx)).astype(BF16)


def _out_proj(xs, y_s5, w_glu, b_glu, cvo, h_f, h_b, qkvo, ml_norm_g, w_out, mod_all, norm_g, layer):
    tm = ROW_TILE
    mod = lambda k: pl.BlockSpec((1, SUBLANES, D_MODEL), lambda i: (layer, 0, k))
    rows = lambda width, col=0: pl.BlockSpec((tm, width), lambda i: (i, col))
    const = lambda shape: pl.BlockSpec(shape, lambda i: (0,) * len(shape))
    x_specs, x_args = _stream_specs(xs, tm)
    cv_specs = [pl.BlockSpec((tm, CONV_W), lambda i: (0, 0)),
                pl.BlockSpec((tm, CONV_W), lambda i: (jnp.maximum(i - 1, 0), 0))]
    return pl.pallas_call(
        _out_proj_body,
        grid=(NTOK // tm,),
        in_specs=[
            rows(S5_W), const((S5_W, S5_W)), const((1, S5_W)),
        ] + cv_specs + [
            rows(ML_W), rows(ML_W), rows(ML_W, 3), const((1, ML_W)),
            pl.BlockSpec((None, D_MODEL, D_MODEL), lambda i: (layer, 0, 0), pipeline_mode=pl.Buffered(1)),
            mod(2), mod(3), mod(4),
            const((1, D_MODEL)),
        ] + x_specs,
        out_specs=[rows(D_MODEL), rows(D_MODEL)],
        out_shape=[jax.ShapeDtypeStruct((NTOK, D_MODEL), F32), jax.ShapeDtypeStruct((NTOK, D_MODEL), BF16)],
        scratch_shapes=[pltpu.VMEM((D_MODEL, D_MODEL), BF16), pltpu.VMEM((S5_W, S5_W), BF16)],
        compiler_params=_cparams(("arbitrary",)),
        name="out_proj_residual",
    )(y_s5, w_glu, b_glu.reshape(1, S5_W), *cvo, h_f, h_b, qkvo, ml_norm_g.reshape(1, ML_W), w_out,
      mod_all, mod_all, mod_all, norm_g.reshape(1, D_MODEL), *x_args)


def _ffn_in_body(a_ref, wg_ref, wu_ref, o_ref, wgbf_ref, wubf_ref):
    @pl.when(pl.program_id(1) == 0)
    def _():
        wgbf_ref[...] = wg_ref[...].astype(BF16)
        wubf_ref[...] = wu_ref[...].astype(BF16)

    a = a_ref[...]
    g = jnp.dot(a, wgbf_ref[...], preferred_element_type=F32)
    u = jnp.dot(a, wubf_ref[...], preferred_element_type=F32)
    o_ref[...] = (jax.nn.silu(g) * u).astype(BF16)


def _ffn_in(h, w_ffn_in, layer):
    tm, tn = MM_TILE_M, MM_TILE_N
    nj = D_FF // tn
    return pl.pallas_call(
        _ffn_in_body,
        grid=(nj, NTOK // tm),
        in_specs=[
            pl.BlockSpec((tm, D_MODEL), lambda j, i: (i, 0)),
            pl.BlockSpec((None, D_MODEL, tn), lambda j, i: (layer, 0, j)),
            pl.BlockSpec((None, D_MODEL, tn), lambda j, i: (layer, 0, nj + j)),
        ],
        out_specs=pl.BlockSpec((tm, tn), lambda j, i: (i, j)),
        out_shape=jax.ShapeDtypeStruct((NTOK, D_FF), BF16),
        scratch_shapes=[pltpu.VMEM((D_MODEL, tn), BF16), pltpu.VMEM((D_MODEL, tn), BF16)],
        compiler_params=_cparams(("arbitrary", "arbitrary")),
        name="ffn_in_swiglu",
    )(h, w_ffn_in, w_ffn_in)


def _ffn_out_body(a_ref, w_ref, g_ref, x_ref, o_ref, wbf_ref):
    i = pl.program_id(1)

    @pl.when(i == 0)
    def _():
        wbf_ref[...] = w_ref[...].astype(BF16)

    acc = jnp.dot(a_ref[...], wbf_ref[...], preferred_element_type=F32)
    tm, tn = o_ref.shape
    o_ref[...] = x_ref[...] + _row_gate(g_ref, i, tm, tn) * acc


def _ffn_out(xs, hid, w_ffn_out, mod_all, layer):
    tm, tn = FFN_OUT_TILE_M, MM_TILE_N
    return pl.pallas_call(
        _ffn_out_body,
        grid=(D_MODEL // tn, NTOK // tm),
        in_specs=[
            pl.BlockSpec((tm, D_FF), lambda j, i: (i, 0)),
            pl.BlockSpec((None, D_FF, tn), lambda j, i: (layer, 0, j)),
            pl.BlockSpec((1, SUBLANES, tn), lambda j, i: (layer, 0, 5 * (D_MODEL // tn) + j)),
            pl.BlockSpec((tm, tn), lambda j, i: (i, j)),
        ],
        out_specs=pl.BlockSpec((tm, tn), lambda j, i: (i, j)),
        out_shape=jax.ShapeDtypeStruct((NTOK, D_MODEL), F32),
        scratch_shapes=[pltpu.VMEM((D_FF, tn), BF16)],
        compiler_params=_cparams(("arbitrary", "arbitrary")),
        name="ffn_out_residual",
    )(hid, w_ffn_out, mod_all, xs)


def _s5_selectors():
    tau = np.arange(S5_BLK)[:, None] // S5_CH
    ch_r = np.arange(S5_BLK)[:, None] % S5_CH
    t = np.arange(S5_BLK)[None, :] // S5_CH
    ch_c = np.arange(S5_BLK)[None, :] % S5_CH
    sel = np.zeros((2, S5_T, S5_BLK, S5_BLK), np.float32)
    for s in range(S5_T):
        sel[0, s] = (tau == t - s) & (ch_r == ch_c)
        sel[1, s] = (tau == s - t) & (ch_r == ch_c)
    rp = np.arange(S5_BLK)[:, None]
    col = np.arange(2 * S5_BLK)[None, :]
    pair_map = np.stack([(col == (rp // S5_P) * S5_BLK + gi * S5_P + rp % S5_P) for gi in range(2)])
    src = np.arange(S5_STEP_W)
    dst = ((src // S5_CH) % S5_STEP_G) * S5_BLK + (src // S5_SET_CH) * S5_CH + src % S5_CH
    perm = dst[:, None] == np.arange(S5_STEP_W)[None, :]
    return (jnp.asarray(sel), jnp.asarray(pair_map, BF16), jnp.asarray(perm, BF16),
            jnp.asarray(perm.T, BF16))


def _s5_operators(lam_re, lam_im, log_step, b_re, b_im, c_re, c_im, sel):
    lam = lax.complex(lam_re.astype(F32), lam_im.astype(F32))
    lam_bar = jnp.exp(lam * jnp.exp(log_step.astype(F32)))
    b_bar = ((lam_bar - 1.0) / lam)[..., None] * lax.complex(b_re.astype(F32), b_im.astype(F32))
    c_mat = lax.complex(c_re.astype(F32), c_im.astype(F32))
    depth = lam.shape[0]

    def powers(base, count):
        out = [jnp.ones_like(base)]
        for _ in range(count - 1):
            out.append(out[-1] * base)
        return out

    pw = powers(lam_bar, S5_T + 1)
    pa = powers(pw[S5_T], SUBLANES + 1)
    t_up = list(range(S5_T))

    def table(seq, fwd_idx, bwd_idx):
        return jnp.stack([jnp.stack([seq[f][:, 0], seq[b][:, 1]], axis=1)
                          for f, b in zip(fwd_idx, bwd_idx)])

    blocks = (depth, 2, S5_G, S5_BLK, S5_BLK)

    def token_rows(w, imag_sign):
        w = jnp.moveaxis(w, 0, 3)
        return jnp.concatenate([jnp.real(w), imag_sign * jnp.imag(w)], axis=-1).reshape(blocks)

    pb = table(pw, [S5_T - 1 - t for t in t_up], t_up)
    wb = token_rows(pb[:, :, :, :, None, :] * jnp.swapaxes(b_bar, -1, -2)[None], 1.0)
    pc = table(pw, [t + 1 for t in t_up], [S5_T - t for t in t_up])
    wct = token_rows(pc[:, :, :, :, None, :] * c_mat[None], -1.0)

    lag_c = token_rows(jnp.stack(pw[:S5_T])[:, :, :, :, None, :] * c_mat[None], -1.0)
    b_t = jnp.swapaxes(b_bar, -1, -2)
    b_ri = jnp.concatenate([jnp.real(b_t), jnp.imag(b_t)], axis=-1)
    kern = jnp.einsum('ldgar,ldgxr->ldgax', b_ri, lag_c, precision=lax.Precision.HIGHEST)
    kern = kern.reshape(depth, 2, S5_G * S5_CH, S5_BLK)
    kc = jnp.einsum('ldxk,dskn->ldsxn', kern, sel, precision=lax.Precision.HIGHEST)
    kc = kc.reshape(depth, 2, S5_T, S5_G, S5_CH, S5_BLK)
    kc = jnp.transpose(kc, (0, 1, 3, 2, 4, 5)).reshape(blocks)

    def lanes(v):
        parts = jnp.stack([jnp.real(v), jnp.imag(v)], axis=4)
        parts = parts.reshape(v.shape[0], depth, 2, S5_G // 2, 2, 2, S5_P)
        parts = jnp.swapaxes(parts, 4, 5).reshape(v.shape[0], depth, 2, S5_FLAT)
        return jnp.transpose(parts, (1, 2, 0, 3))

    zero_p = jnp.zeros_like(pa[0])
    amul = lanes(jnp.stack([pa[1], pa[2], pa[4], pa[8]] + [zero_p] * 4))
    apow = lanes(table(pa, t_up, t_up[::-1]))
    return kc.astype(BF16), wb.astype(BF16), wct.astype(BF16), amul, apow


def _s5_row_scan(e_ref, amul_ref, apow_ref, block_lo, block_hi, reverse, carry):
    rows = lax.broadcasted_iota(jnp.int32, (SUBLANES, S5_BLK), 0)

    def shifted(x, k):
        if reverse:
            return jnp.where(rows < SUBLANES - k, pltpu.roll(x, SUBLANES - k, 0), 0.0)
        return jnp.where(rows >= k, pltpu.roll(x, k, 0), 0.0)

    def body(step, carry):
        blk = (block_hi - 1 - step) if reverse else (block_lo + step)
        r0 = pl.multiple_of(blk * SUBLANES, SUBLANES)
        last = 0 if reverse else SUBLANES - 1
        carry_out = []
        for q in range(S5_STEP_PAIRS):
            re_c = slice(2 * q * S5_BLK, (2 * q + 1) * S5_BLK)
            im_c = slice((2 * q + 1) * S5_BLK, (2 * q + 2) * S5_BLK)
            s_re = e_ref[pl.ds(r0, SUBLANES), re_c]
            s_im = e_ref[pl.ds(r0, SUBLANES), im_c]
            for idx, k in enumerate((1, 2, 4)):
                a_re, a_im = amul_ref[idx:idx + 1, re_c], amul_ref[idx:idx + 1, im_c]
                t_re, t_im = shifted(s_re, k), shifted(s_im, k)
                s_re, s_im = s_re + t_re * a_re - t_im * a_im, s_im + t_re * a_im + t_im * a_re
            c_re, c_im = carry[:, re_c], carry[:, im_c]
            p_re, p_im = apow_ref[:, re_c], apow_ref[:, im_c]
            e_ref[pl.ds(r0, SUBLANES), re_c] = shifted(s_re, 1) + p_re * c_re - p_im * c_im
            e_ref[pl.ds(r0, SUBLANES), im_c] = shifted(s_im, 1) + p_re * c_im + p_im * c_re
            a_re, a_im = amul_ref[3:4, re_c], amul_ref[3:4, im_c]
            l_re = jnp.broadcast_to(s_re[last:last + 1, :], (SUBLANES, S5_BLK))
            l_im = jnp.broadcast_to(s_im[last:last + 1, :], (SUBLANES, S5_BLK))
            carry_out += [l_re + a_re * c_re - a_im * c_im, l_im + a_re * c_im + a_im * c_re]
        return jnp.concatenate(carry_out, axis=1)

    return lax.fori_loop(0, block_hi - block_lo, body, carry)


def _s5_pair_operators(kc_ref, wb_ref, wct_ref, map_ref, q):
    g0, g1 = 2 * q, 2 * q + 1
    zero = jnp.zeros((S5_BLK, S5_BLK), BF16)
    kc = jnp.concatenate([jnp.concatenate([kc_ref[g0], zero], axis=1),
                          jnp.concatenate([zero, kc_ref[g1]], axis=1)], axis=0)
    spread = lambda ref: jnp.concatenate(
        [jnp.dot(ref[g0], map_ref[0], preferred_element_type=F32),
         jnp.dot(ref[g1], map_ref[1], preferred_element_type=F32)], axis=0).astype(BF16)
    return kc, spread(wb_ref), spread(wct_ref)


def _s5_body(u_ref, p_ref, pt_ref, map_ref, kc_ref, wb_ref, wct_ref, amul_ref, apow_ref, d_ref, y_ref,
             up_ref, e_ref, acc_ref):
    direction = pl.program_id(1)
    token_rows = lambda t: pl.ds(t, S5_ROWS, stride=S5_T)
    pairs = [slice(2 * q * S5_BLK, 2 * (q + 1) * S5_BLK) for q in range(S5_STEP_PAIRS)]

    @pl.when(direction == 0)
    def _():
        u_nat = jnp.concatenate([u_ref[token_rows(t), :] for t in range(S5_T)], axis=1)
        up_ref[...] = jnp.dot(u_nat.astype(BF16), p_ref[...], preferred_element_type=F32).astype(BF16)
        acc_ref[...] = jnp.zeros_like(acc_ref)

    ops = [_s5_pair_operators(kc_ref, wb_ref, wct_ref, map_ref, q) for q in range(S5_STEP_PAIRS)]
    for (_, wb, _), cols in zip(ops, pairs):
        e_ref[:, cols] = jnp.dot(up_ref[:, cols], wb, preferred_element_type=F32)

    zero = jnp.zeros((SUBLANES, S5_STEP_W), F32)
    n_blocks = S5_ROWS // SUBLANES
    ctx_blocks = S5_CTX_ROWS // SUBLANES
    amul, apow = amul_ref, apow_ref

    @pl.when(direction == 0)
    def _():
        _s5_row_scan(e_ref, amul, apow, 0, n_blocks, False, zero)

    @pl.when(direction == 1)
    def _():
        carry = _s5_row_scan(e_ref, amul, apow, 0, ctx_blocks, True, zero)
        _s5_row_scan(e_ref, amul, apow, ctx_blocks, n_blocks, True, carry)

    for (kc, _, wct), cols in zip(ops, pairs):
        y = jnp.dot(up_ref[:, cols], kc, preferred_element_type=F32)
        y += lax.dot_general(e_ref[:, cols].astype(BF16), wct, (((1,), (1,)), ((), ())),
                             preferred_element_type=F32)
        acc_ref[:, cols] += y

    @pl.when(direction == 1)
    def _():
        acc = acc_ref[...]
        hi = acc.astype(BF16)
        lo = (acc - hi.astype(F32)).astype(BF16)
        y_nat = (jnp.dot(hi, pt_ref[...], preferred_element_type=F32)
                 + jnp.dot(lo, pt_ref[...], preferred_element_type=F32))
        for t in range(S5_T):
            y_ref[token_rows(t), :] = (y_nat[:, t * S5_SET_CH:(t + 1) * S5_SET_CH]
                                       + d_ref[...] * u_ref[token_rows(t), :])


def _s5_mixer(z, ops, consts, layer, d_skip):
    kc, wb, wct, amul, apow = ops
    _, pair_map, perm, perm_t = consts
    n_steps = S5_G // S5_STEP_G
    wspec = pl.BlockSpec((None, None, S5_STEP_G, S5_BLK, S5_BLK), lambda j, d: (layer, d, j, 0, 0))
    cspec = pl.BlockSpec((None, None, SUBLANES, S5_STEP_W), lambda j, d: (layer, d, 0, j))
    pspec = pl.BlockSpec((S5_STEP_W, S5_STEP_W), lambda j, d: (0, 0))
    return pl.pallas_call(
        _s5_body,
        grid=(n_steps, 2),
        in_specs=[
            pl.BlockSpec((NTOK, S5_SET_CH), lambda j, d: (0, j)),
            pspec, pspec,
            pl.BlockSpec((2, S5_BLK, 2 * S5_BLK), lambda j, d: (0, 0, 0)),
            wspec, wspec, wspec, cspec, cspec,
            pl.BlockSpec((1, S5_SET_CH), lambda j, d: (0, j)),
        ],
        out_specs=pl.BlockSpec((NTOK, S5_SET_CH), lambda j, d: (0, j)),
        out_shape=jax.ShapeDtypeStruct((NTOK, S5_W), F32),
        scratch_shapes=[pltpu.VMEM((S5_ROWS, S5_STEP_W), BF16), pltpu.VMEM((S5_ROWS, S5_STEP_W), F32),
                        pltpu.VMEM((S5_ROWS, S5_STEP_W), F32)],
        compiler_params=_cparams(("arbitrary", "arbitrary")),
        name="s5_scan",
    )(z, perm, perm_t, pair_map, kc, wb, wct, amul, apow, d_skip.astype(F32).reshape(1, S5_W))


CONV_PAD = 16
CONV_BLK = 64


CONV_PAD_ROWS = max(CTX + 2 * CONV_PAD, (ROW_TILE // GRID_W) * (GRID_W + 2 * CONV_PAD))


def _conv_tile(a_ref, b_ref, w_ref, db_ref, lg_ref, lb_ref, o_ref, pad_ref, sh_ref, seq_len):
    g = a_ref[...] * jax.nn.sigmoid(b_ref[...])
    n_seq = ROW_TILE // seq_len
    pitch = seq_len + 2 * CONV_PAD
    zeros = jnp.zeros((CONV_PAD, CONV_W), F32)
    for s in range(n_seq):
        pad_ref[s * pitch:s * pitch + CONV_PAD, :] = zeros
        pad_ref[s * pitch + CONV_PAD:s * pitch + CONV_PAD + seq_len, :] = g[s * seq_len:(s + 1) * seq_len]
        pad_ref[s * pitch + CONV_PAD + seq_len:(s + 1) * pitch, :] = zeros
    used = n_seq * pitch
    for r in range(1, SUBLANES):
        sh_ref[r, 0:used - SUBLANES, :] = pad_ref[r:r + used - SUBLANES, :]
    for blk in range(ROW_TILE // CONV_BLK):
        row0 = blk * CONV_BLK
        s, q = divmod(row0, seq_len)
        base = s * pitch + CONV_PAD + q - CONV_K // 2
        tiles = []
        for lt in range(CONV_W // LANES):
            lanes = slice(lt * LANES, (lt + 1) * LANES)
            acc = jnp.zeros((CONV_BLK, LANES), F32)
            for k in range(CONV_K):
                r = (base + k) % SUBLANES
                row = base + k - r
                tap = pad_ref[row:row + CONV_BLK, lanes] if r == 0 else sh_ref[r, row:row + CONV_BLK, lanes]
                acc = acc + w_ref[k:k + 1, lanes] * tap
            tiles.append(acc)
        y = jnp.concatenate(tiles, axis=1) + db_ref[...]
        yc = y - jnp.mean(y, axis=-1, keepdims=True)
        var = jnp.mean(yc * yc, axis=-1, keepdims=True)
        y = yc * lax.rsqrt(var + LN_EPS) * lg_ref[...] + lb_ref[...]
        o_ref[row0:row0 + CONV_BLK, :] = jax.nn.silu(y).astype(BF16)


def _in_proj_conv_body(h_ref, wt0_ref, wt1_ref, b0_ref, b1_ref, za_ref, zb_ref, zac_ref, zbc_ref,
                       cw_ref, cdb_ref, clg_ref, clb_ref, o_ref, cvc_ref, cvx_ref,
                       wbf_ref, pad_ref, sh_ref):
    j, i = pl.program_id(0), pl.program_id(1)
    tn = MM_TILE_N
    conv = (cw_ref, cdb_ref, clg_ref, clb_ref)

    @pl.when(i == 0)
    def _():
        wbf_ref[:, 0:tn] = jnp.transpose(wt0_ref[...]).astype(BF16)
        wbf_ref[:, tn:2 * tn] = jnp.transpose(wt1_ref[...]).astype(BF16)

    @pl.when(jnp.logical_and(j == 0, i == 0))
    def _():
        _conv_tile(zac_ref, zbc_ref, *conv, cvc_ref, pad_ref, sh_ref, CTX)

    bias = jnp.concatenate([b0_ref[...], b1_ref[...]], axis=1)
    acc = jnp.dot(h_ref[...], wbf_ref[...], preferred_element_type=F32) + bias
    o_ref[...] = acc.astype(o_ref.dtype)
    _conv_tile(za_ref, zb_ref, *conv, cvx_ref, pad_ref, sh_ref, GRID_W)


def _in_proj_conv(h, w_in_t, b_in, layer, z, dw_w, dw_b, ln_g, ln_b):
    tm, tn = MM_TILE_M, MM_TILE_N
    n_j = (OFF_G - OFF_Q) // (2 * tn)
    n_i = NTOK // tm
    assert n_j * n_i == SEQ // ROW_TILE
    first = OFF_Q // tn
    in_w = w_in_t.shape[1]
    b3 = b_in.reshape(DEPTH, 1, in_w)
    w = jnp.concatenate([dw_w, jnp.zeros((1, CONV_W), dw_w.dtype)], axis=0)
    vec = lambda v: v.reshape(1, CONV_W)
    a_col = OFF_CONV // CONV_W
    x_tile = lambda j, i: j * n_i + i + CTX // ROW_TILE
    const = lambda shape: pl.BlockSpec(shape, lambda j, i: (0,) * len(shape))
    return pl.pallas_call(
        _in_proj_conv_body,
        grid=(n_j, n_i),
        in_specs=[
            pl.BlockSpec((tm, D_MODEL), lambda j, i: (i, 0)),
            pl.BlockSpec((None, tn, D_MODEL), lambda j, i: (layer, first + 2 * j, 0)),
            pl.BlockSpec((None, tn, D_MODEL), lambda j, i: (layer, first + 2 * j + 1, 0)),
            pl.BlockSpec((None, 1, tn), lambda j, i: (layer, 0, first + 2 * j)),
            pl.BlockSpec((None, 1, tn), lambda j, i: (layer, 0, first + 2 * j + 1)),
            pl.BlockSpec((ROW_TILE, CONV_W), lambda j, i: (x_tile(j, i), a_col)),
            pl.BlockSpec((ROW_TILE, CONV_W), lambda j, i: (x_tile(j, i), a_col + 1)),
            pl.BlockSpec((ROW_TILE, CONV_W), lambda j, i: (0, a_col)),
            pl.BlockSpec((ROW_TILE, CONV_W), lambda j, i: (0, a_col + 1)),
            const((CONV_K + 1, CONV_W)), const((1, CONV_W)), const((1, CONV_W)), const((1, CONV_W)),
        ],
        out_specs=[pl.BlockSpec((tm, 2 * tn), lambda j, i: (i, j)),
                   const((CTX, CONV_W)),
                   pl.BlockSpec((ROW_TILE, CONV_W), lambda j, i: (j * n_i + i, 0))],
        out_shape=[jax.ShapeDtypeStruct((NTOK, OFF_G - OFF_Q), BF16),
                   jax.ShapeDtypeStruct((CTX, CONV_W), BF16),
                   jax.ShapeDtypeStruct((SEQ, CONV_W), BF16)],
        scratch_shapes=[pltpu.VMEM((D_MODEL, 2 * tn), BF16), pltpu.VMEM((CONV_PAD_ROWS, CONV_W), F32),
                        pltpu.VMEM((SUBLANES, CONV_PAD_ROWS, CONV_W), F32)],
        compiler_params=_cparams(("arbitrary", "arbitrary")),
        name="in_proj_conv",
    )(h, w_in_t, w_in_t, b3, b3, z, z, z, z, w, vec(dw_b), vec(ln_g), vec(ln_b))


def _ml_chunk_index(ci, reverse):
    if not reverse:
        return ci
    return jnp.where(ci == 0, 0, ML_NCHUNK - ci)


def _ml_chain_stages(qb, kb, vx, li_rep, li_row, b_rep, b_row, cx_ref, m_ref, mask, reverse, store_h):
    c = ML_CHUNK
    scale = ML_DH ** 0.5
    wide = lambda a: jnp.concatenate([a] * (ML_DH // LANES), axis=1)
    v = {}

    def scores():
        v['m'] = m_ref[...]
        v['d_log'] = jnp.where(mask, wide(b_rep) + (li_row - b_row), -jnp.inf)
        v['inter'] = b_rep + v['m']
        v['m_row'] = jnp.maximum(v['inter'], jnp.max(v['d_log'], axis=-1, keepdims=True))
        v['qk'] = lax.dot_general(qb, kb, (((1,), (1,)), ((), ())), preferred_element_type=F32)

    def numerator():
        s = v['qk'] * jnp.exp(v['d_log'] - wide(v['m_row']))
        w_inter = jnp.exp(v['inter'] - v['m_row']) * scale
        v['cx'] = cx_ref[...]
        lhs = jnp.concatenate([s.astype(BF16), qb * wide(w_inter.astype(BF16))], axis=1)
        rhs = jnp.concatenate([vx, v['cx'].astype(BF16)], axis=0)
        v['num'] = jnp.dot(lhs, rhs, preferred_element_type=F32)

    def output():
        num = v['num']
        den = num[:, ML_DH:]
        recip = 1.0 / jnp.maximum(jnp.abs(den), jnp.exp(-v['m_row']) * scale)
        store_h(num[:, :ML_DH] * wide(recip))

    def state():
        m = v['m']
        b_tot = b_row[:, 0:1] if reverse else b_row[:, c - 1:c]
        g = b_tot - b_rep + li_rep
        m_new = jnp.maximum(b_tot + m, jnp.max(g, axis=0, keepdims=True))
        kw = kb * wide((jnp.exp(g - m_new) * (1.0 / scale)).astype(BF16))
        decay = jnp.exp(b_tot + m - m_new)[:, 0:1]
        cx_ref[...] = decay * v['cx'] + lax.dot_general(kw, vx, (((0,), (0,)), ((), ())),
                                                       preferred_element_type=F32)
        m_ref[...] = m_new

    return [scores, numerator, output, state]


def _mlstm_body(qkvf_ref, gtf_ref, qkvb_ref, gtb_ref, hf_ref, hb_ref, cx_ref, m_ref):
    @pl.when(pl.program_id(0) == 0)
    def _():
        cx_ref[...] = jnp.zeros_like(cx_ref)
        m_ref[...] = jnp.zeros_like(m_ref)

    c = ML_CHUNK
    r_idx = lax.broadcasted_iota(jnp.int32, (c, c), 0)
    c_idx = lax.broadcasted_iota(jnp.int32, (c, c), 1)
    ones_col = jnp.ones((c, LANES), BF16)
    n_piece = 3
    spread_r = lax.broadcasted_iota(jnp.int32, (SUBLANES, 2 * LANES), 0)
    spread_c = lax.broadcasted_iota(jnp.int32, (SUBLANES, 2 * LANES), 1)
    in_a = jnp.logical_and(spread_r < n_piece, spread_c < LANES)
    in_b = jnp.logical_and(jnp.logical_and(spread_r >= n_piece, spread_r < 2 * n_piece), spread_c >= LANES)
    spread = jnp.where(jnp.logical_or(in_a, in_b), 1.0, 0.0).astype(BF16)

    def replicate(a_rows, b_rows):
        pad = jnp.zeros((SUBLANES - 2 * n_piece, c), F32)
        rows = jnp.concatenate(list(a_rows) + list(b_rows) + [pad], axis=0).astype(BF16)
        both = lax.dot_general(rows, spread, (((0,), (0,)), ((), ())), preferred_element_type=F32)
        return both[:, :LANES], both[:, LANES:]

    chains = []
    for d, (qkv_ref, gt_ref, h_ref) in enumerate(((qkvf_ref, gtf_ref, hf_ref), (qkvb_ref, gtb_ref, hb_ref))):
        reverse = d == 1
        gates_t = gt_ref[...]
        incl = (c_idx >= r_idx) if reverse else (c_idx <= r_idx)
        incl_t = (r_idx >= c_idx) if reverse else (r_idx <= c_idx)
        tri_t = jnp.where(incl_t, 1.0, 0.0).astype(BF16)
        lf_t = jax.nn.log_sigmoid(gates_t)
        b_all_t = sum(jnp.dot(p, tri_t, preferred_element_type=F32) for p in _split3(lf_t))
        gate_rows = [p.astype(F32) for p in _split3(gates_t)]
        b_rows = [p.astype(F32) for p in _split3(b_all_t)]
        for head in range(ML_H):
            i_col = 2 * ML_H * d + head
            f_col = i_col + ML_H
            state = d * ML_H + head
            col = lambda part: slice((part * ML_H + head) * ML_DH, (part * ML_H + head + 1) * ML_DH)
            vx = jnp.concatenate([qkv_ref[:, col(2)], ones_col], axis=1)

            def store_h(h, h_ref=h_ref, head=head):
                h_ref[:, head * ML_DH:(head + 1) * ML_DH] = h.astype(h_ref.dtype)

            li_rep, b_rep = replicate([p[i_col:i_col + 1, :] for p in gate_rows],
                                      [p[f_col:f_col + 1, :] for p in b_rows])
            chains.append(_ml_chain_stages(
                qkv_ref[:, col(0)], qkv_ref[:, col(1)], vx,
                li_rep, gates_t[i_col:i_col + 1, :],
                b_rep, b_all_t[f_col:f_col + 1, :],
                cx_ref.at[state], m_ref.at[state], incl, reverse, store_h))
    for stage in zip(*chains):
        for run in stage:
            run()


def _mlstm_mixer(qkvo, gates):
    gates_t = jnp.transpose(gates[:, :2 * SUBLANES])
    c = ML_CHUNK

    def specs(reverse):
        row = lambda ci: _ml_chunk_index(ci, reverse)
        return [
            pl.BlockSpec((c, 3 * ML_W), lambda ci: (row(ci), 0)),
            pl.BlockSpec((2 * SUBLANES, c), lambda ci: (0, row(ci))),
        ]

    out_spec = lambda reverse: pl.BlockSpec((c, ML_W), lambda ci: (_ml_chunk_index(ci, reverse), 0))
    n_state = 2 * ML_H
    return pl.pallas_call(
        _mlstm_body,
        grid=(ML_NCHUNK,),
        in_specs=specs(False) + specs(True),
        out_specs=[out_spec(False), out_spec(True)],
        out_shape=[jax.ShapeDtypeStruct((NTOK, ML_W), BF16)] * 2,
        scratch_shapes=[pltpu.VMEM((n_state, ML_DH, ML_DH + LANES), F32),
                        pltpu.VMEM((n_state, 1, LANES), F32)],
        compiler_params=_cparams(("arbitrary",)),
        name="mlstm_chunks",
    )(qkvo, gates_t, qkvo, gates_t)


def kernel(x, c, ctx, c_ctx, w_mod, b_mod, norm1_g, w_in, b_in, s5_lam_re, s5_lam_im, s5_log_step,
           s5_b_re, s5_b_im, s5_c_re, s5_c_im, s5_d, s5_w_glu, s5_b_glu, conv_dw_w, conv_dw_b,
           conv_ln_g, conv_ln_b, ml_norm_g, w_out, norm2_g, w_ffn_in, w_ffn_out, norm_f_g):
    assert x.shape == (1, SEQ, D_MODEL) and ctx.shape == (1, CTX, D_MODEL)
    xs = (ctx[0].astype(F32), x[0].astype(F32))
    cc = jnp.zeros((SUBLANES, D_MODEL), F32).at[0].set(c[0]).at[1].set(c_ctx)
    mod_all = _modulation(cc, w_mod, b_mod)
    s5_consts = _s5_selectors()
    s5_ops = _s5_operators(s5_lam_re, s5_lam_im, s5_log_step, s5_b_re, s5_b_im, s5_c_re, s5_c_im,
                           s5_consts[0])
    w_in_t = jnp.swapaxes(w_in, 1, 2)
    n_gate = w_in.shape[2] - OFF_G
    w_gate = w_in[:, :, OFF_G:]
    w_gate_hi = w_gate.astype(BF16)
    w_gate = jnp.stack([w_gate_hi, (w_gate - w_gate_hi.astype(F32)).astype(BF16)], axis=1)
    w_gate_t = jnp.pad(w_gate, ((0, 0), (0, 0), (0, 0), (0, LANES - n_gate)))
    b_gate = jnp.pad(b_in[:, OFF_G:], ((0, 0), (0, LANES - n_gate))).reshape(DEPTH, 1, LANES)
    tn = MM_TILE_N

    for l in range(DEPTH):
        tm = CTX if l == 0 else IN_NORM_TILE_M
        z, h, gates = _in_proj_norm(xs, tm, norm1_g[l], mod_all, l, w_gate_t[l], b_gate[l], w_in_t, b_in,
                                    OFF_Q // tn)
        qkvo, cv_c, cv_x = _in_proj_conv(h, w_in_t, b_in, l, z, conv_dw_w[l], conv_dw_b[l], conv_ln_g[l],
                                         conv_ln_b[l])
        y_s5 = _s5_mixer(z, s5_ops, s5_consts, l, s5_d[l])
        h_f, h_b = _mlstm_mixer(qkvo, gates)
        xs, h2 = _out_proj(xs, y_s5, s5_w_glu[l], s5_b_glu[l], (cv_c, cv_x), h_f, h_b, qkvo, ml_norm_g[l], w_out,
                           mod_all, norm2_g[l], l)
        hid = _ffn_in(h2, w_ffn_in, l)
        xs = _ffn_out(xs, hid, w_ffn_out, mod_all, l)
    return _final_norm(xs, norm_f_g)[None]
```

```python
import functools

import numpy as np
import jax
import jax.numpy as jnp
from jax import lax
from jax.experimental import pallas as pl
from jax.experimental.pallas import tpu as pltpu

F32 = jnp.float32
BF16 = jnp.bfloat16

D_MODEL = 2048
SEQ = 8192
CTX = 256
NTOK = SEQ + CTX
DEPTH = 4
GRID_W = 64

S5_W = 512
S5_CH = 16
S5_G = 32
S5_P = 64
CONV_W = 512
CONV_K = 31
ML_W = 1024
ML_H = 4
ML_DH = 256
D_FF = 5632
EPS = 1e-6
LN_EPS = 1e-5

OFF_CONV = 512
OFF_Q = 1536
OFF_O = 4608
OFF_G = 5632

LANES = 128
SUBLANES = 8
VMEM_LIMIT = 56 * 1024 * 1024

ROW_TILE = 256
MM_TILE_M = 1056
MM_TILE_N = 512
IN_NORM_TILE_M = 528
FFN_OUT_TILE_M = 704

S5_T = 8
S5_ROWS = NTOK // S5_T
S5_CTX_ROWS = CTX // S5_T
S5_BLK = S5_T * S5_CH
S5_STEP_G = 8
S5_STEP_PAIRS = S5_STEP_G // 2
S5_SET_CH = S5_STEP_G * S5_CH
S5_STEP_W = S5_STEP_G * S5_BLK
S5_FLAT = S5_G * S5_BLK

ML_CHUNK = 256
ML_NCHUNK = NTOK // ML_CHUNK


def _cparams(sem, vmem=VMEM_LIMIT):
    return pltpu.CompilerParams(dimension_semantics=sem, vmem_limit_bytes=vmem)


def _dot(a, b):
    return jnp.dot(a.astype(BF16), b.astype(BF16), preferred_element_type=F32)


def _split3(x):
    a = x.astype(BF16)
    r = x - a.astype(F32)
    b = r.astype(BF16)
    c = (r - b.astype(F32)).astype(BF16)
    return a, b, c


def _mod_body(cc_ref, w_ref, b_ref, o_ref):
    s = jax.nn.silu(cc_ref[...])
    o_ref[0] = _dot(s, w_ref[0]) + b_ref[0]


def _modulation(cc, w_mod, b_mod):
    depth, _, n = w_mod.shape
    tn = 1024
    return pl.pallas_call(
        _mod_body,
        grid=(depth, n // tn),
        in_specs=[
            pl.BlockSpec((SUBLANES, D_MODEL), lambda l, j: (0, 0)),
            pl.BlockSpec((1, D_MODEL, tn), lambda l, j: (l, 0, j)),
            pl.BlockSpec((1, 1, tn), lambda l, j: (l, 0, j)),
        ],
        out_specs=pl.BlockSpec((1, SUBLANES, tn), lambda l, j: (l, 0, j)),
        out_shape=jax.ShapeDtypeStruct((depth, SUBLANES, n), F32),
        compiler_params=_cparams(("arbitrary", "arbitrary")),
        name="adaln_modulation",
    )(cc, w_mod, b_mod.reshape(depth, 1, n))


def _mod_row(m_ref, is_ctx):
    return jnp.where(is_ctx, m_ref[0, 1:2, :], m_ref[0, 0:1, :])


def _stream_specs(xs, tm):
    if isinstance(xs, tuple):
        assert tm == CTX
        return [pl.BlockSpec((tm, D_MODEL), lambda i: (0, 0)),
                pl.BlockSpec((tm, D_MODEL), lambda i: (jnp.maximum(i - 1, 0), 0))], list(xs)
    return [pl.BlockSpec((tm, D_MODEL), lambda i: (i, 0))], [xs]


def _stream_tile(x_refs):
    if len(x_refs) == 1:
        return x_refs[0][...]
    return jnp.where(pl.program_id(0) == 0, x_refs[0][...], x_refs[1][...])


def _final_norm_body(x_ref, g_ref, o_ref):
    xf = x_ref[...]
    ms = jnp.mean(xf * xf, axis=-1, keepdims=True)
    o_ref[...] = xf * lax.rsqrt(ms + EPS) * g_ref[...]


def _final_norm(xs, g):
    skip = CTX // ROW_TILE
    return pl.pallas_call(
        _final_norm_body,
        grid=(SEQ // ROW_TILE,),
        in_specs=[pl.BlockSpec((ROW_TILE, D_MODEL), lambda i: (i + skip, 0)),
                  pl.BlockSpec((1, D_MODEL), lambda i: (0, 0))],
        out_specs=pl.BlockSpec((ROW_TILE, D_MODEL), lambda i: (i, 0)),
        out_shape=jax.ShapeDtypeStruct((SEQ, D_MODEL), F32),
        compiler_params=_cparams(("arbitrary",)),
        name="final_rmsnorm",
    )(xs, g.reshape(1, D_MODEL))


def _in_proj_norm_body(*refs, n_w, n_stream):
    x_refs, rest = refs[:n_stream], refs[n_stream:]
    (g_ref, sh_ref, sc_ref, wg_ref, bg_ref), rest = rest[:5], rest[5:]
    wt_refs, b_refs = rest[:n_w], rest[n_w:2 * n_w]
    o_ref, h_ref, gate_ref, wbf_ref = rest[2 * n_w:]
    tn = MM_TILE_N
    n_main = n_w * tn
    i = pl.program_id(0)

    @pl.when(i == 0)
    def _():
        for k, wt_ref in enumerate(wt_refs):
            wbf_ref[:, k * tn:(k + 1) * tn] = jnp.transpose(wt_ref[...]).astype(BF16)
        wbf_ref[:, n_main:n_main + LANES] = wg_ref[0]
        wbf_ref[:, n_main + LANES:] = wg_ref[1]

    xf = _stream_tile(x_refs)
    tm = xf.shape[0]
    is_ctx = i * tm + lax.broadcasted_iota(jnp.int32, (tm, 1), 0) < CTX
    pick = lambda ref: jnp.where(is_ctx, ref[0, 1:2, :], ref[0, 0:1, :])
    ms = jnp.mean(xf * xf, axis=-1, keepdims=True)
    h = xf * lax.rsqrt(ms + EPS) * (g_ref[...] * (1.0 + pick(sc_ref))) + pick(sh_ref)
    hi = h.astype(BF16)
    h_ref[...] = hi
    lo = (h - hi.astype(F32)).astype(BF16)
    acc = jnp.dot(hi, wbf_ref[...], preferred_element_type=F32)
    gate_ref[...] = (acc[:, n_main:n_main + LANES] + acc[:, n_main + LANES:]
                     + jnp.dot(lo, wg_ref[0], preferred_element_type=F32) + bg_ref[...])
    bias = jnp.concatenate([b_ref[...] for b_ref in b_refs], axis=1)
    o_ref[...] = acc[:, :n_main] + bias


def _in_proj_norm(xs, tm, g, mod_all, layer, w_gate_t, b_gate, w_in_t, b_in, n_w):
    tn = MM_TILE_N
    in_w = w_in_t.shape[1]
    once = dict(pipeline_mode=pl.Buffered(1))
    x_specs, x_args = _stream_specs(xs, tm)
    w_specs = [pl.BlockSpec((None, tn, D_MODEL), functools.partial(lambda i, k: (layer, k, 0), k=k), **once)
               for k in range(n_w)]
    b_specs = [pl.BlockSpec((None, 1, tn), functools.partial(lambda i, k: (layer, 0, k), k=k))
               for k in range(n_w)]
    b3 = b_in.reshape(DEPTH, 1, in_w)
    mod = lambda k: pl.BlockSpec((1, SUBLANES, D_MODEL), lambda i: (layer, 0, k))
    return pl.pallas_call(
        functools.partial(_in_proj_norm_body, n_w=n_w, n_stream=len(x_args)),
        grid=(NTOK // tm,),
        in_specs=x_specs + [
            pl.BlockSpec((1, D_MODEL), lambda i: (0, 0)),
            mod(0), mod(1),
            pl.BlockSpec((2, D_MODEL, LANES), lambda i: (0, 0, 0)),
            pl.BlockSpec((1, LANES), lambda i: (0, 0)),
        ] + w_specs + b_specs,
        out_specs=[pl.BlockSpec((tm, n_w * tn), lambda i: (i, 0)),
                   pl.BlockSpec((tm, D_MODEL), lambda i: (i, 0)),
                   pl.BlockSpec((tm, LANES), lambda i: (i, 0))],
        out_shape=[jax.ShapeDtypeStruct((NTOK, n_w * tn), F32),
                   jax.ShapeDtypeStruct((NTOK, D_MODEL), BF16),
                   jax.ShapeDtypeStruct((NTOK, LANES), F32)],
        scratch_shapes=[pltpu.VMEM((D_MODEL, n_w * tn + 2 * LANES), BF16)],
        compiler_params=_cparams(("arbitrary",)),
        name="norm_in_proj",
    )(*x_args, g.reshape(1, D_MODEL), mod_all, mod_all, w_gate_t, b_gate, *([w_in_t] * n_w), *([b3] * n_w))


def _row_gate(g_ref, i, tm, tn):
    rows = i * tm + lax.broadcasted_iota(jnp.int32, (tm, tn), 0)
    return jnp.where(rows < CTX, g_ref[0, 1:2, :], g_ref[0, 0:1, :])


def _out_proj_body(y_ref, wglu_ref, bglu_ref, cvc_ref, cvx_ref, hf_ref, hb_ref, o_ref, mlg_ref, w_ref,
                   gate_ref, sh_ref, sc_ref, ng_ref, *rest):
    x_refs, (xo_ref, h_ref, wbf_ref, wglu_bf_ref) = rest[:-4], rest[-4:]
    i = pl.program_id(0)

    @pl.when(i == 0)
    def _():
        wbf_ref[...] = w_ref[...].astype(BF16)
        wglu_bf_ref[...] = wglu_ref[...].astype(BF16)

    g = jax.nn.gelu(y_ref[...])
    glu = jnp.dot(g.astype(BF16), wglu_bf_ref[...], preferred_element_type=F32) + bglu_ref[...]
    s5 = (g * jax.nn.sigmoid(glu)).astype(BF16)
    acc = jnp.dot(s5, wbf_ref[0:S5_W, :], preferred_element_type=F32)
    acc += jnp.dot(_stream_tile([cvc_ref, cvx_ref]), wbf_ref[S5_W:S5_W + CONV_W, :],
                   preferred_element_type=F32)
    for head in range(ML_H):
        cols = slice(head * ML_DH, (head + 1) * ML_DH)
        hh = hf_ref[:, cols].astype(F32) + hb_ref[:, cols].astype(F32)
        hc = hh - jnp.mean(hh, axis=-1, keepdims=True)
        var = jnp.mean(hc * hc, axis=-1, keepdims=True)
        ml = jax.nn.sigmoid(o_ref[:, cols].astype(F32)) * (hc * lax.rsqrt(var + LN_EPS) * mlg_ref[:, cols])
        row0 = S5_W + CONV_W + head * ML_DH
        acc += jnp.dot(ml.astype(BF16), wbf_ref[row0:row0 + ML_DH, :], preferred_element_type=F32)
    is_ctx = i == 0
    xn = _stream_tile(x_refs) + _mod_row(gate_ref, is_ctx) * acc
    xo_ref[...] = xn
    ms = jnp.mean(xn * xn, axis=-1, keepdims=True)
    gain = ng_ref[...] * (1.0 + _mod_row(sc_ref, is_ctx))
    h_ref[...] = (xn * lax.rsqrt(ms + EPS) * gain + _mod_row(sh_ref, is_ctx)).astype(BF16)


def _out_proj(xs, y_s5, w_glu, b_glu, cvo, h_f, h_b, qkvo, ml_norm_g, w_out, mod_all, norm_g, layer):
    tm = ROW_TILE
    mod = lambda k: pl.BlockSpec((1, SUBLANES, D_MODEL), lambda i: (layer, 0, k))
    rows = lambda width, col=0: pl.BlockSpec((tm, width), lambda i: (i, col))
    const = lambda shape: pl.BlockSpec(shape, lambda i: (0,) * len(shape))
    x_specs, x_args = _stream_specs(xs, tm)
    cv_specs = [pl.BlockSpec((tm, CONV_W), lambda i: (0, 0)),
                pl.BlockSpec((tm, CONV_W), lambda i: (jnp.maximum(i - 1, 0), 0))]
    return pl.pallas_call(
        _out_proj_body,
        grid=(NTOK // tm,),
        in_specs=[
            rows(S5_W), const((S5_W, S5_W)), const((1, S5_W)),
        ] + cv_specs + [
            rows(ML_W), rows(ML_W), rows(ML_W, 3), const((1, ML_W)),
            pl.BlockSpec((None, D_MODEL, D_MODEL), lambda i: (layer, 0, 0), pipeline_mode=pl.Buffered(1)),
            mod(2), mod(3), mod(4),
            const((1, D_MODEL)),
        ] + x_specs,
        out_specs=[rows(D_MODEL), rows(D_MODEL)],
        out_shape=[jax.ShapeDtypeStruct((NTOK, D_MODEL), F32), jax.ShapeDtypeStruct((NTOK, D_MODEL), BF16)],
        scratch_shapes=[pltpu.VMEM((D_MODEL, D_MODEL), BF16), pltpu.VMEM((S5_W, S5_W), BF16)],
        compiler_params=_cparams(("arbitrary",)),
        name="out_proj_residual",
    )(y_s5, w_glu, b_glu.reshape(1, S5_W), *cvo, h_f, h_b, qkvo, ml_norm_g.reshape(1, ML_W), w_out,
      mod_all, mod_all, mod_all, norm_g.reshape(1, D_MODEL), *x_args)


def _ffn_in_body(a_ref, wg_ref, wu_ref, o_ref, wgbf_ref, wubf_ref):
    @pl.when(pl.program_id(1) == 0)
    def _():
        wgbf_ref[...] = wg_ref[...].astype(BF16)
        wubf_ref[...] = wu_ref[...].astype(BF16)

    a = a_ref[...]
    g = jnp.dot(a, wgbf_ref[...], preferred_element_type=F32)
    u = jnp.dot(a, wubf_ref[...], preferred_element_type=F32)
    o_ref[...] = (jax.nn.silu(g) * u).astype(BF16)


def _ffn_in(h, w_ffn_in, layer):
    tm, tn = MM_TILE_M, MM_TILE_N
    nj = D_FF // tn
    return pl.pallas_call(
        _ffn_in_body,
        grid=(nj, NTOK // tm),
        in_specs=[
            pl.BlockSpec((tm, D_MODEL), lambda j, i: (i, 0)),
            pl.BlockSpec((None, D_MODEL, tn), lambda j, i: (layer, 0, j)),
            pl.BlockSpec((None, D_MODEL, tn), lambda j, i: (layer, 0, nj + j)),
        ],
        out_specs=pl.BlockSpec((tm, tn), lambda j, i: (i, j)),
        out_shape=jax.ShapeDtypeStruct((NTOK, D_FF), BF16),
        scratch_shapes=[pltpu.VMEM((D_MODEL, tn), BF16), pltpu.VMEM((D_MODEL, tn), BF16)],
        compiler_params=_cparams(("arbitrary", "arbitrary")),
        name="ffn_in_swiglu",
    )(h, w_ffn_in, w_ffn_in)


def _ffn_out_body(a_ref, w_ref, g_ref, x_ref, o_ref, wbf_ref):
    i = pl.program_id(1)

    @pl.when(i == 0)
    def _():
        wbf_ref[...] = w_ref[...].astype(BF16)

    acc = jnp.dot(a_ref[...], wbf_ref[...], preferred_element_type=F32)
    tm, tn = o_ref.shape
    o_ref[...] = x_ref[...] + _row_gate(g_ref, i, tm, tn) * acc


def _ffn_out(xs, hid, w_ffn_out, mod_all, layer):
    tm, tn = FFN_OUT_TILE_M, MM_TILE_N
    return pl.pallas_call(
        _ffn_out_body,
        grid=(D_MODEL // tn, NTOK // tm),
        in_specs=[
            pl.BlockSpec((tm, D_FF), lambda j, i: (i, 0)),
            pl.BlockSpec((None, D_FF, tn), lambda j, i: (layer, 0, j)),
            pl.BlockSpec((1, SUBLANES, tn), lambda j, i: (layer, 0, 5 * (D_MODEL // tn) + j)),
            pl.BlockSpec((tm, tn), lambda j, i: (i, j)),
        ],
        out_specs=pl.BlockSpec((tm, tn), lambda j, i: (i, j)),
        out_shape=jax.ShapeDtypeStruct((NTOK, D_MODEL), F32),
        scratch_shapes=[pltpu.VMEM((D_FF, tn), BF16)],
        compiler_params=_cparams(("arbitrary", "arbitrary")),
        name="ffn_out_residual",
    )(hid, w_ffn_out, mod_all, xs)


def _s5_selectors():
    tau = np.arange(S5_BLK)[:, None] // S5_CH
    ch_r = np.arange(S5_BLK)[:, None] % S5_CH
    t = np.arange(S5_BLK)[None, :] // S5_CH
    ch_c = np.arange(S5_BLK)[None, :] % S5_CH
    sel = np.zeros((2, S5_T, S5_BLK, S5_BLK), np.float32)
    for s in range(S5_T):
        sel[0, s] = (tau == t - s) & (ch_r == ch_c)
        sel[1, s] = (tau == s - t) & (ch_r == ch_c)
    rp = np.arange(S5_BLK)[:, None]
    col = np.arange(2 * S5_BLK)[None, :]
    pair_map = np.stack([(col == (rp // S5_P) * S5_BLK + gi * S5_P + rp % S5_P) for gi in range(2)])
    src = np.arange(S5_STEP_W)
    dst = ((src // S5_CH) % S5_STEP_G) * S5_BLK + (src // S5_SET_CH) * S5_CH + src % S5_CH
    perm = dst[:, None] == np.arange(S5_STEP_W)[None, :]
    return (jnp.asarray(sel), jnp.asarray(pair_map, BF16), jnp.asarray(perm, BF16),
            jnp.asarray(perm.T, BF16))


def _s5_operators(lam_re, lam_im, log_step, b_re, b_im, c_re, c_im, sel):
    lam = lax.complex(lam_re.astype(F32), lam_im.astype(F32))
    lam_bar = jnp.exp(lam * jnp.exp(log_step.astype(F32)))
    b_bar = ((lam_bar - 1.0) / lam)[..., None] * lax.complex(b_re.astype(F32), b_im.astype(F32))
    c_mat = lax.complex(c_re.astype(F32), c_im.astype(F32))
    depth = lam.shape[0]

    def powers(base, count):
        out = [jnp.ones_like(base)]
        for _ in range(count - 1):
            out.append(out[-1] * base)
        return out

    pw = powers(lam_bar, S5_T + 1)
    pa = powers(pw[S5_T], SUBLANES + 1)
    t_up = list(range(S5_T))

    def table(seq, fwd_idx, bwd_idx):
        return jnp.stack([jnp.stack([seq[f][:, 0], seq[b][:, 1]], axis=1)
                          for f, b in zip(fwd_idx, bwd_idx)])

    blocks = (depth, 2, S5_G, S5_BLK, S5_BLK)

    def token_rows(w, imag_sign):
        w = jnp.moveaxis(w, 0, 3)
        return jnp.concatenate([jnp.real(w), imag_sign * jnp.imag(w)], axis=-1).reshape(blocks)

    pb = table(pw, [S5_T - 1 - t for t in t_up], t_up)
    wb = token_rows(pb[:, :, :, :, None, :] * jnp.swapaxes(b_bar, -1, -2)[None], 1.0)
    pc = table(pw, [t + 1 for t in t_up], [S5_T - t for t in t_up])
    wct = token_rows(pc[:, :, :, :, None, :] * c_mat[None], -1.0)

    lag_c = token_rows(jnp.stack(pw[:S5_T])[:, :, :, :, None, :] * c_mat[None], -1.0)
    b_t = jnp.swapaxes(b_bar, -1, -2)
    b_ri = jnp.concatenate([jnp.real(b_t), jnp.imag(b_t)], axis=-1)
    kern = jnp.einsum('ldgar,ldgxr->ldgax', b_ri, lag_c, precision=lax.Precision.HIGHEST)
    kern = kern.reshape(depth, 2, S5_G * S5_CH, S5_BLK)
    kc = jnp.einsum('ldxk,dskn->ldsxn', kern, sel, precision=lax.Precision.HIGHEST)
    kc = kc.reshape(depth, 2, S5_T, S5_G, S5_CH, S5_BLK)
    kc = jnp.transpose(kc, (0, 1, 3, 2, 4, 5)).reshape(blocks)

    def lanes(v):
        parts = jnp.stack([jnp.real(v), jnp.imag(v)], axis=4)
        parts = parts.reshape(v.shape[0], depth, 2, S5_G // 2, 2, 2, S5_P)
        parts = jnp.swapaxes(parts, 4, 5).reshape(v.shape[0], depth, 2, S5_FLAT)
        return jnp.transpose(parts, (1, 2, 0, 3))

    zero_p = jnp.zeros_like(pa[0])
    amul = lanes(jnp.stack([pa[1], pa[2], pa[4], pa[8]] + [zero_p] * 4))
    apow = lanes(table(pa, t_up, t_up[::-1]))
    return kc.astype(BF16), wb.astype(BF16), wct.astype(BF16), amul, apow


def _s5_row_scan(e_ref, amul_ref, apow_ref, block_lo, block_hi, reverse, carry):
    rows = lax.broadcasted_iota(jnp.int32, (SUBLANES, S5_BLK), 0)

    def shifted(x, k):
        if reverse:
            return jnp.where(rows < SUBLANES - k, pltpu.roll(x, SUBLANES - k, 0), 0.0)
        return jnp.where(rows >= k, pltpu.roll(x, k, 0), 0.0)

    def body(step, carry):
        blk = (block_hi - 1 - step) if reverse else (block_lo + step)
        r0 = pl.multiple_of(blk * SUBLANES, SUBLANES)
        last = 0 if reverse else SUBLANES - 1
        carry_out = []
        for q in range(S5_STEP_PAIRS):
            re_c = slice(2 * q * S5_BLK, (2 * q + 1) * S5_BLK)
            im_c = slice((2 * q + 1) * S5_BLK, (2 * q + 2) * S5_BLK)
            s_re = e_ref[pl.ds(r0, SUBLANES), re_c]
            s_im = e_ref[pl.ds(r0, SUBLANES), im_c]
            for idx, k in enumerate((1, 2, 4)):
                a_re, a_im = amul_ref[idx:idx + 1, re_c], amul_ref[idx:idx + 1, im_c]
                t_re, t_im = shifted(s_re, k), shifted(s_im, k)
                s_re, s_im = s_re + t_re * a_re - t_im * a_im, s_im + t_re * a_im + t_im * a_re
            c_re, c_im = carry[:, re_c], carry[:, im_c]
            p_re, p_im = apow_ref[:, re_c], apow_ref[:, im_c]
            e_ref[pl.ds(r0, SUBLANES), re_c] = shifted(s_re, 1) + p_re * c_re - p_im * c_im
            e_ref[pl.ds(r0, SUBLANES), im_c] = shifted(s_im, 1) + p_re * c_im + p_im * c_re
            a_re, a_im = amul_ref[3:4, re_c], amul_ref[3:4, im_c]
            l_re = jnp.broadcast_to(s_re[last:last + 1, :], (SUBLANES, S5_BLK))
            l_im = jnp.broadcast_to(s_im[last:last + 1, :], (SUBLANES, S5_BLK))
            carry_out += [l_re + a_re * c_re - a_im * c_im, l_im + a_re * c_im + a_im * c_re]
        return jnp.concatenate(carry_out, axis=1)

    return lax.fori_loop(0, block_hi - block_lo, body, carry)


def _s5_pair_operators(kc_ref, wb_ref, wct_ref, map_ref, q):
    g0, g1 = 2 * q, 2 * q + 1
    zero = jnp.zeros((S5_BLK, S5_BLK), BF16)
    kc = jnp.concatenate([jnp.concatenate([kc_ref[g0], zero], axis=1),
                          jnp.concatenate([zero, kc_ref[g1]], axis=1)], axis=0)
    spread = lambda ref: jnp.concatenate(
        [jnp.dot(ref[g0], map_ref[0], preferred_element_type=F32),
         jnp.dot(ref[g1], map_ref[1], preferred_element_type=F32)], axis=0).astype(BF16)
    return kc, spread(wb_ref), spread(wct_ref)


def _s5_body(u_ref, p_ref, pt_ref, map_ref, kc_ref, wb_ref, wct_ref, amul_ref, apow_ref, d_ref, y_ref,
             up_ref, e_ref, acc_ref):
    direction = pl.program_id(1)
    token_rows = lambda t: pl.ds(t, S5_ROWS, stride=S5_T)
    pairs = [slice(2 * q * S5_BLK, 2 * (q + 1) * S5_BLK) for q in range(S5_STEP_PAIRS)]

    @pl.when(direction == 0)
    def _():
        u_nat = jnp.concatenate([u_ref[token_rows(t), :] for t in range(S5_T)], axis=1)
        up_ref[...] = jnp.dot(u_nat.astype(BF16), p_ref[...], preferred_element_type=F32).astype(BF16)
        acc_ref[...] = jnp.zeros_like(acc_ref)

    ops = [_s5_pair_operators(kc_ref, wb_ref, wct_ref, map_ref, q) for q in range(S5_STEP_PAIRS)]
    for (_, wb, _), cols in zip(ops, pairs):
        e_ref[:, cols] = jnp.dot(up_ref[:, cols], wb, preferred_element_type=F32)

    zero = jnp.zeros((SUBLANES, S5_STEP_W), F32)
    n_blocks = S5_ROWS // SUBLANES
    ctx_blocks = S5_CTX_ROWS // SUBLANES
    amul, apow = amul_ref, apow_ref

    @pl.when(direction == 0)
    def _():
        _s5_row_scan(e_ref, amul, apow, 0, n_blocks, False, zero)

    @pl.when(direction == 1)
    def _():
        carry = _s5_row_scan(e_ref, amul, apow, 0, ctx_blocks, True, zero)
        _s5_row_scan(e_ref, amul, apow, ctx_blocks, n_blocks, True, carry)

    for (kc, _, wct), cols in zip(ops, pairs):
        y = jnp.dot(up_ref[:, cols], kc, preferred_element_type=F32)
        y += lax.dot_general(e_ref[:, cols].astype(BF16), wct, (((1,), (1,)), ((), ())),
                             preferred_element_type=F32)
        acc_ref[:, cols] += y

    @pl.when(direction == 1)
    def _():
        acc = acc_ref[...]
        hi = acc.astype(BF16)
        lo = (acc - hi.astype(F32)).astype(BF16)
        y_nat = (jnp.dot(hi, pt_ref[...], preferred_element_type=F32)
                 + jnp.dot(lo, pt_ref[...], preferred_element_type=F32))
        for t in range(S5_T):
            y_ref[token_rows(t), :] = (y_nat[:, t * S5_SET_CH:(t + 1) * S5_SET_CH]
                                       + d_ref[...] * u_ref[token_rows(t), :])


def _s5_mixer(z, ops, consts, layer, d_skip):
    kc, wb, wct, amul, apow = ops
    _, pair_map, perm, perm_t = consts
    n_steps = S5_G // S5_STEP_G
    wspec = pl.BlockSpec((None, None, S5_STEP_G, S5_BLK, S5_BLK), lambda j, d: (layer, d, j, 0, 0))
    cspec = pl.BlockSpec((None, None, SUBLANES, S5_STEP_W), lambda j, d: (layer, d, 0, j))
    pspec = pl.BlockSpec((S5_STEP_W, S5_STEP_W), lambda j, d: (0, 0))
    return pl.pallas_call(
        _s5_body,
        grid=(n_steps, 2),
        in_specs=[
            pl.BlockSpec((NTOK, S5_SET_CH), lambda j, d: (0, j)),
            pspec, pspec,
            pl.BlockSpec((2, S5_BLK, 2 * S5_BLK), lambda j, d: (0, 0, 0)),
            wspec, wspec, wspec, cspec, cspec,
            pl.BlockSpec((1, S5_SET_CH), lambda j, d: (0, j)),
        ],
        out_specs=pl.BlockSpec((NTOK, S5_SET_CH), lambda j, d: (0, j)),
        out_shape=jax.ShapeDtypeStruct((NTOK, S5_W), F32),
        scratch_shapes=[pltpu.VMEM((S5_ROWS, S5_STEP_W), BF16), pltpu.VMEM((S5_ROWS, S5_STEP_W), F32),
                        pltpu.VMEM((S5_ROWS, S5_STEP_W), F32)],
        compiler_params=_cparams(("arbitrary", "arbitrary")),
        name="s5_scan",
    )(z, perm, perm_t, pair_map, kc, wb, wct, amul, apow, d_skip.astype(F32).reshape(1, S5_W))


CONV_PAD = 16
CONV_BLK = 64


CONV_PAD_ROWS = max(CTX + 2 * CONV_PAD, (ROW_TILE // GRID_W) * (GRID_W + 2 * CONV_PAD))


def _conv_tile(a_ref, b_ref, w_ref, db_ref, lg_ref, lb_ref, o_ref, pad_ref, sh_ref, seq_len):
    n_seq = ROW_TILE // seq_len
    pitch = seq_len + 2 * CONV_PAD
    zeros = jnp.zeros((CONV_PAD, CONV_W), F32)
    for s in range(n_seq):
        pad_ref[s * pitch:s * pitch + CONV_PAD, :] = zeros
        for q in range(0, seq_len, CONV_BLK):
            src = slice(s * seq_len + q, s * seq_len + q + CONV_BLK)
            dst = s * pitch + CONV_PAD + q
            pad_ref[dst:dst + CONV_BLK, :] = a_ref[src, :] * jax.nn.sigmoid(b_ref[src, :])
        pad_ref[s * pitch + CONV_PAD + seq_len:(s + 1) * pitch, :] = zeros
    used = n_seq * pitch
    for r in range(1, SUBLANES):
        for q in range(0, used - SUBLANES, CONV_BLK):
            n = min(CONV_BLK, used - SUBLANES - q)
            sh_ref[r, q:q + n, :] = pad_ref[r + q:r + q + n, :]
    for blk in range(ROW_TILE // CONV_BLK):
        row0 = blk * CONV_BLK
        s, q = divmod(row0, seq_len)
        base = s * pitch + CONV_PAD + q - CONV_K // 2
        tiles = []
        for lt in range(CONV_W // LANES):
            lanes = slice(lt * LANES, (lt + 1) * LANES)
            acc = jnp.zeros((CONV_BLK, LANES), F32)
            for k in range(CONV_K):
                r = (base + k) % SUBLANES
                row = base + k - r
                tap = pad_ref[row:row + CONV_BLK, lanes] if r == 0 else sh_ref[r, row:row + CONV_BLK, lanes]
                acc = acc + w_ref[k:k + 1, lanes] * tap
            tiles.append(acc)
        y = jnp.concatenate(tiles, axis=1) + db_ref[...]
        yc = y - jnp.mean(y, axis=-1, keepdims=True)
        var = jnp.mean(yc * yc, axis=-1, keepdims=True)
        y = yc * lax.rsqrt(var + LN_EPS) * lg_ref[...] + lb_ref[...]
        o_ref[row0:row0 + CONV_BLK, :] = jax.nn.silu(y).astype(BF16)


def _in_proj_conv_body(h_ref, wt0_ref, wt1_ref, b0_ref, b1_ref, za_ref, zb_ref, zac_ref, zbc_ref,
                       cw_ref, cdb_ref, clg_ref, clb_ref, o_ref, cvc_ref, cvx_ref,
                       wbf_ref, pad_ref, sh_ref):
    j, i = pl.program_id(0), pl.program_id(1)
    tn = MM_TILE_N
    conv = (cw_ref, cdb_ref, clg_ref, clb_ref)

    @pl.when(i == 0)
    def _():
        wbf_ref[:, 0:tn] = jnp.transpose(wt0_ref[...]).astype(BF16)
        wbf_ref[:, tn:2 * tn] = jnp.transpose(wt1_ref[...]).astype(BF16)

    @pl.when(jnp.logical_and(j == 0, i == 0))
    def _():
        _conv_tile(zac_ref, zbc_ref, *conv, cvc_ref, pad_ref, sh_ref, CTX)

    bias = jnp.concatenate([b0_ref[...], b1_ref[...]], axis=1)
    acc = jnp.dot(h_ref[...], wbf_ref[...], preferred_element_type=F32) + bias
    o_ref[...] = acc.astype(o_ref.dtype)
    _conv_tile(za_ref, zb_ref, *conv, cvx_ref, pad_ref, sh_ref, GRID_W)


def _in_proj_conv(h, w_in_t, b_in, layer, z, dw_w, dw_b, ln_g, ln_b):
    tm, tn = MM_TILE_M, MM_TILE_N
    n_j = (OFF_G - OFF_Q) // (2 * tn)
    n_i = NTOK // tm
    assert n_j * n_i == SEQ // ROW_TILE
    first = OFF_Q // tn
    in_w = w_in_t.shape[1]
    b3 = b_in.reshape(DEPTH, 1, in_w)
    w = jnp.concatenate([dw_w, jnp.zeros((1, CONV_W), dw_w.dtype)], axis=0)
    vec = lambda v: v.reshape(1, CONV_W)
    a_col = OFF_CONV // CONV_W
    x_tile = lambda j, i: j * n_i + i + CTX // ROW_TILE
    const = lambda shape: pl.BlockSpec(shape, lambda j, i: (0,) * len(shape))
    return pl.pallas_call(
        _in_proj_conv_body,
        grid=(n_j, n_i),
        in_specs=[
            pl.BlockSpec((tm, D_MODEL), lambda j, i: (i, 0)),
            pl.BlockSpec((None, tn, D_MODEL), lambda j, i: (layer, first + 2 * j, 0)),
            pl.BlockSpec((None, tn, D_MODEL), lambda j, i: (layer, first + 2 * j + 1, 0)),
            pl.BlockSpec((None, 1, tn), lambda j, i: (layer, 0, first + 2 * j)),
            pl.BlockSpec((None, 1, tn), lambda j, i: (layer, 0, first + 2 * j + 1)),
            pl.BlockSpec((ROW_TILE, CONV_W), lambda j, i: (x_tile(j, i), a_col)),
            pl.BlockSpec((ROW_TILE, CONV_W), lambda j, i: (x_tile(j, i), a_col + 1)),
            pl.BlockSpec((ROW_TILE, CONV_W), lambda j, i: (0, a_col)),
            pl.BlockSpec((ROW_TILE, CONV_W), lambda j, i: (0, a_col + 1)),
            const((CONV_K + 1, CONV_W)), const((1, CONV_W)), const((1, CONV_W)), const((1, CONV_W)),
        ],
        out_specs=[pl.BlockSpec((tm, 2 * tn), lambda j, i: (i, j)),
                   const((CTX, CONV_W)),
                   pl.BlockSpec((ROW_TILE, CONV_W), lambda j, i: (j * n_i + i, 0))],
        out_shape=[jax.ShapeDtypeStruct((NTOK, OFF_G - OFF_Q), BF16),
                   jax.ShapeDtypeStruct((CTX, CONV_W), BF16),
                   jax.ShapeDtypeStruct((SEQ, CONV_W), BF16)],
        scratch_shapes=[pltpu.VMEM((D_MODEL, 2 * tn), BF16), pltpu.VMEM((CONV_PAD_ROWS, CONV_W), F32),
                        pltpu.VMEM((SUBLANES, CONV_PAD_ROWS, CONV_W), F32)],
        compiler_params=_cparams(("arbitrary", "arbitrary")),
        name="in_proj_conv",
    )(h, w_in_t, w_in_t, b3, b3, z, z, z, z, w, vec(dw_b), vec(ln_g), vec(ln_b))


def _ml_chunk_index(ci, reverse):
    if not reverse:
        return ci
    return jnp.where(ci == 0, 0, ML_NCHUNK - ci)


def _ml_chain_stages(qb, kb, vx, li_rep, li_row, b_rep, b_row, cx_ref, m_ref, mask, reverse, store_h):
    c = ML_CHUNK
    scale = ML_DH ** 0.5
    wide = lambda a: jnp.concatenate([a] * (ML_DH // LANES), axis=1)
    v = {}

    def scores():
        v['m'] = m_ref[...]
        v['d_log'] = jnp.where(mask, wide(b_rep) + (li_row - b_row), -jnp.inf)
        v['inter'] = b_rep + v['m']
        v['m_row'] = jnp.maximum(v['inter'], jnp.max(v['d_log'], axis=-1, keepdims=True))
        v['qk'] = lax.dot_general(qb, kb, (((1,), (1,)), ((), ())), preferred_element_type=F32)

    def numerator():
        s = v['qk'] * jnp.exp(v['d_log'] - wide(v['m_row']))
        w_inter = jnp.exp(v['inter'] - v['m_row']) * scale
        v['cx'] = cx_ref[...]
        lhs = jnp.concatenate([s.astype(BF16), qb * wide(w_inter.astype(BF16))], axis=1)
        rhs = jnp.concatenate([vx, v['cx'].astype(BF16)], axis=0)
        v['num'] = jnp.dot(lhs, rhs, preferred_element_type=F32)

    def output():
        num = v['num']
        den = num[:, ML_DH:]
        recip = 1.0 / jnp.maximum(jnp.abs(den), jnp.exp(-v['m_row']) * scale)
        store_h(num[:, :ML_DH] * wide(recip))

    def state():
        m = v['m']
        b_tot = b_row[:, 0:1] if reverse else b_row[:, c - 1:c]
        g = b_tot - b_rep + li_rep
        m_new = jnp.maximum(b_tot + m, jnp.max(g, axis=0, keepdims=True))
        kw = kb * wide((jnp.exp(g - m_new) * (1.0 / scale)).astype(BF16))
        decay = jnp.exp(b_tot + m - m_new)[:, 0:1]
        cx_ref[...] = decay * v['cx'] + lax.dot_general(kw, vx, (((0,), (0,)), ((), ())),
                                                       preferred_element_type=F32)
        m_ref[...] = m_new

    return [scores, numerator, output, state]


def _mlstm_body(qkvf_ref, gtf_ref, qkvb_ref, gtb_ref, hf_ref, hb_ref, cx_ref, m_ref):
    @pl.when(pl.program_id(0) == 0)
    def _():
        cx_ref[...] = jnp.zeros_like(cx_ref)
        m_ref[...] = jnp.zeros_like(m_ref)

    c = ML_CHUNK
    r_idx = lax.broadcasted_iota(jnp.int32, (c, c), 0)
    c_idx = lax.broadcasted_iota(jnp.int32, (c, c), 1)
    ones_col = jnp.ones((c, LANES), BF16)
    n_piece = 3
    spread_r = lax.broadcasted_iota(jnp.int32, (SUBLANES, 2 * LANES), 0)
    spread_c = lax.broadcasted_iota(jnp.int32, (SUBLANES, 2 * LANES), 1)
    in_a = jnp.logical_and(spread_r < n_piece, spread_c < LANES)
    in_b = jnp.logical_and(jnp.logical_and(spread_r >= n_piece, spread_r < 2 * n_piece), spread_c >= LANES)
    spread = jnp.where(jnp.logical_or(in_a, in_b), 1.0, 0.0).astype(BF16)

    def replicate(a_rows, b_rows):
        pad = jnp.zeros((SUBLANES - 2 * n_piece, c), F32)
        rows = jnp.concatenate(list(a_rows) + list(b_rows) + [pad], axis=0).astype(BF16)
        both = lax.dot_general(rows, spread, (((0,), (0,)), ((), ())), preferred_element_type=F32)
        return both[:, :LANES], both[:, LANES:]

    chains = []
    for d, (qkv_ref, gt_ref, h_ref) in enumerate(((qkvf_ref, gtf_ref, hf_ref), (qkvb_ref, gtb_ref, hb_ref))):
        reverse = d == 1
        gates_t = gt_ref[...]
        incl = (c_idx >= r_idx) if reverse else (c_idx <= r_idx)
        incl_t = (r_idx >= c_idx) if reverse else (r_idx <= c_idx)
        tri_t = jnp.where(incl_t, 1.0, 0.0).astype(BF16)
        lf_t = jax.nn.log_sigmoid(gates_t)
        b_all_t = sum(jnp.dot(p, tri_t, preferred_element_type=F32) for p in _split3(lf_t))
        gate_rows = [p.astype(F32) for p in _split3(gates_t)]
        b_rows = [p.astype(F32) for p in _split3(b_all_t)]
        for head in range(ML_H):
            i_col = 2 * ML_H * d + head
            f_col = i_col + ML_H
            state = d * ML_H + head
            col = lambda part: slice((part * ML_H + head) * ML_DH, (part * ML_H + head + 1) * ML_DH)
            vx = jnp.concatenate([qkv_ref[:, col(2)], ones_col], axis=1)

            def store_h(h, h_ref=h_ref, head=head):
                h_ref[:, head * ML_DH:(head + 1) * ML_DH] = h.astype(h_ref.dtype)

            li_rep, b_rep = replicate([p[i_col:i_col + 1, :] for p in gate_rows],
                                      [p[f_col:f_col + 1, :] for p in b_rows])
            chains.append(_ml_chain_stages(
                qkv_ref[:, col(0)], qkv_ref[:, col(1)], vx,
                li_rep, gates_t[i_col:i_col + 1, :],
                b_rep, b_all_t[f_col:f_col + 1, :],
                cx_ref.at[state], m_ref.at[state], incl, reverse, store_h))
    for stage in zip(*chains):
        for run in stage:
            run()


def _mlstm_mixer(qkvo, gates):
    gates_t = jnp.transpose(gates[:, :2 * SUBLANES])
    c = ML_CHUNK

    def specs(reverse):
        row = lambda ci: _ml_chunk_index(ci, reverse)
        return [
            pl.BlockSpec((c, 3 * ML_W), lambda ci: (row(ci), 0)),
            pl.BlockSpec((2 * SUBLANES, c), lambda ci: (0, row(ci))),
        ]

    out_spec = lambda reverse: pl.BlockSpec((c, ML_W), lambda ci: (_ml_chunk_index(ci, reverse), 0))
    n_state = 2 * ML_H
    return pl.pallas_call(
        _mlstm_body,
        grid=(ML_NCHUNK,),
        in_specs=specs(False) + specs(True),
        out_specs=[out_spec(False), out_spec(True)],
        out_shape=[jax.ShapeDtypeStruct((NTOK, ML_W), BF16)] * 2,
        scratch_shapes=[pltpu.VMEM((n_state, ML_DH, ML_DH + LANES), F32),
                        pltpu.VMEM((n_state, 1, LANES), F32)],
        compiler_params=_cparams(("arbitrary",)),
        name="mlstm_chunks",
    )(qkvo, gates_t, qkvo, gates_t)


def kernel(x, c, ctx, c_ctx, w_mod, b_mod, norm1_g, w_in, b_in, s5_lam_re, s5_lam_im, s5_log_step,
           s5_b_re, s5_b_im, s5_c_re, s5_c_im, s5_d, s5_w_glu, s5_b_glu, conv_dw_w, conv_dw_b,
           conv_ln_g, conv_ln_b, ml_norm_g, w_out, norm2_g, w_ffn_in, w_ffn_out, norm_f_g):
    assert x.shape == (1, SEQ, D_MODEL) and ctx.shape == (1, CTX, D_MODEL)
    xs = (ctx[0].astype(F32), x[0].astype(F32))
    cc = jnp.zeros((SUBLANES, D_MODEL), F32).at[0].set(c[0]).at[1].set(c_ctx)
    mod_all = _modulation(cc, w_mod, b_mod)
    s5_consts = _s5_selectors()
    s5_ops = _s5_operators(s5_lam_re, s5_lam_im, s5_log_step, s5_b_re, s5_b_im, s5_c_re, s5_c_im,
                           s5_consts[0])
    w_in_t = jnp.swapaxes(w_in, 1, 2)
    n_gate = w_in.shape[2] - OFF_G
    w_gate = w_in[:, :, OFF_G:]
    w_gate_hi = w_gate.astype(BF16)
    w_gate = jnp.stack([w_gate_hi, (w_gate - w_gate_hi.astype(F32)).astype(BF16)], axis=1)
    w_gate_t = jnp.pad(w_gate, ((0, 0), (0, 0), (0, 0), (0, LANES - n_gate)))
    b_gate = jnp.pad(b_in[:, OFF_G:], ((0, 0), (0, LANES - n_gate))).reshape(DEPTH, 1, LANES)
    tn = MM_TILE_N

    for l in range(DEPTH):
        tm = CTX if l == 0 else IN_NORM_TILE_M
        z, h, gates = _in_proj_norm(xs, tm, norm1_g[l], mod_all, l, w_gate_t[l], b_gate[l], w_in_t, b_in,
                                    OFF_Q // tn)
        qkvo, cv_c, cv_x = _in_proj_conv(h, w_in_t, b_in, l, z, conv_dw_w[l], conv_dw_b[l], conv_ln_g[l],
                                         conv_ln_b[l])
        y_s5 = _s5_mixer(z, s5_ops, s5_consts, l, s5_d[l])
        h_f, h_b = _mlstm_mixer(qkvo, gates)
        xs, h2 = _out_proj(xs, y_s5, s5_w_glu[l], s5_b_glu[l], (cv_c, cv_x), h_f, h_b, qkvo, ml_norm_g[l], w_out,
                           mod_all, norm2_g[l], l)
        hid = _ffn_in(h2, w_ffn_in, l)
        xs = _ffn_out(xs, hid, w_ffn_out, mod_all, l)
    return _final_norm(xs, norm_f_g)[None]
```

```python
import functools

import numpy as np
import jax
import jax.numpy as jnp
from jax import lax
from jax.experimental import pallas as pl
from jax.experimental.pallas import tpu as pltpu

F32 = jnp.float32
BF16 = jnp.bfloat16

D_MODEL = 2048
SEQ = 8192
CTX = 256
NTOK = SEQ + CTX
DEPTH = 4
GRID_W = 64

S5_W = 512
S5_CH = 16
S5_G = 32
S5_P = 64
CONV_W = 512
CONV_K = 31
ML_W = 1024
ML_H = 4
ML_DH = 256
D_FF = 5632
EPS = 1e-6
LN_EPS = 1e-5

OFF_CONV = 512
OFF_Q = 1536
OFF_O = 4608
OFF_G = 5632

LANES = 128
SUBLANES = 8
VMEM_LIMIT = 56 * 1024 * 1024

ROW_TILE = 256
MM_TILE_M = 1056
MM_TILE_N = 512
IN_NORM_TILE_M = 528
FFN_OUT_TILE_M = 704

S5_T = 8
S5_ROWS = NTOK // S5_T
S5_CTX_ROWS = CTX // S5_T
S5_BLK = S5_T * S5_CH
S5_STEP_G = 8
S5_STEP_PAIRS = S5_STEP_G // 2
S5_SET_CH = S5_STEP_G * S5_CH
S5_STEP_W = S5_STEP_G * S5_BLK
S5_FLAT = S5_G * S5_BLK

ML_CHUNK = 256
ML_NCHUNK = NTOK // ML_CHUNK


def _cparams(sem, vmem=VMEM_LIMIT):
    return pltpu.CompilerParams(dimension_semantics=sem, vmem_limit_bytes=vmem)


def _dot(a, b):
    return jnp.dot(a.astype(BF16), b.astype(BF16), preferred_element_type=F32)


def _split3(x):
    a = x.astype(BF16)
    r = x - a.astype(F32)
    b = r.astype(BF16)
    c = (r - b.astype(F32)).astype(BF16)
    return a, b, c


def _mod_body(cc_ref, w_ref, b_ref, o_ref):
    s = jax.nn.silu(cc_ref[...])
    o_ref[0] = _dot(s, w_ref[0]) + b_ref[0]


def _modulation(cc, w_mod, b_mod):
    depth, _, n = w_mod.shape
    tn = 1024
    return pl.pallas_call(
        _mod_body,
        grid=(depth, n // tn),
        in_specs=[
            pl.BlockSpec((SUBLANES, D_MODEL), lambda l, j: (0, 0)),
            pl.BlockSpec((1, D_MODEL, tn), lambda l, j: (l, 0, j)),
            pl.BlockSpec((1, 1, tn), lambda l, j: (l, 0, j)),
        ],
        out_specs=pl.BlockSpec((1, SUBLANES, tn), lambda l, j: (l, 0, j)),
        out_shape=jax.ShapeDtypeStruct((depth, SUBLANES, n), F32),
        compiler_params=_cparams(("arbitrary", "arbitrary")),
        name="adaln_modulation",
    )(cc, w_mod, b_mod.reshape(depth, 1, n))


def _mod_row(m_ref, is_ctx):
    return jnp.where(is_ctx, m_ref[0, 1:2, :], m_ref[0, 0:1, :])


def _stream_specs(xs, tm):
    if isinstance(xs, tuple):
        assert tm == CTX
        return [pl.BlockSpec((tm, D_MODEL), lambda i: (0, 0)),
                pl.BlockSpec((tm, D_MODEL), lambda i: (jnp.maximum(i - 1, 0), 0))], list(xs)
    return [pl.BlockSpec((tm, D_MODEL), lambda i: (i, 0))], [xs]


def _stream_tile(x_refs):
    if len(x_refs) == 1:
        return x_refs[0][...]
    return jnp.where(pl.program_id(0) == 0, x_refs[0][...], x_refs[1][...])


def _final_norm_body(x_ref, g_ref, o_ref):
    xf = x_ref[...]
    ms = jnp.mean(xf * xf, axis=-1, keepdims=True)
    o_ref[...] = xf * lax.rsqrt(ms + EPS) * g_ref[...]


def _final_norm(xs, g):
    skip = CTX // ROW_TILE
    return pl.pallas_call(
        _final_norm_body,
        grid=(SEQ // ROW_TILE,),
        in_specs=[pl.BlockSpec((ROW_TILE, D_MODEL), lambda i: (i + skip, 0)),
                  pl.BlockSpec((1, D_MODEL), lambda i: (0, 0))],
        out_specs=pl.BlockSpec((ROW_TILE, D_MODEL), lambda i: (i, 0)),
        out_shape=jax.ShapeDtypeStruct((SEQ, D_MODEL), F32),
        compiler_params=_cparams(("arbitrary",)),
        name="final_rmsnorm",
    )(xs, g.reshape(1, D_MODEL))


def _in_proj_norm_body(*refs, n_w, n_stream):
    x_refs, rest = refs[:n_stream], refs[n_stream:]
    (g_ref, sh_ref, sc_ref, wg_ref, bg_ref), rest = rest[:5], rest[5:]
    wt_refs, b_refs = rest[:n_w], rest[n_w:2 * n_w]
    o_ref, h_ref, gate_ref, wbf_ref = rest[2 * n_w:]
    tn = MM_TILE_N
    n_main = n_w * tn
    i = pl.program_id(0)

    @pl.when(i == 0)
    def _():
        for k, wt_ref in enumerate(wt_refs):
            wbf_ref[:, k * tn:(k + 1) * tn] = jnp.transpose(wt_ref[...]).astype(BF16)
        wbf_ref[:, n_main:n_main + LANES] = wg_ref[0]
        wbf_ref[:, n_main + LANES:] = wg_ref[1]

    xf = _stream_tile(x_refs)
    tm = xf.shape[0]
    is_ctx = i * tm + lax.broadcasted_iota(jnp.int32, (tm, 1), 0) < CTX
    pick = lambda ref: jnp.where(is_ctx, ref[0, 1:2, :], ref[0, 0:1, :])
    ms = jnp.mean(xf * xf, axis=-1, keepdims=True)
    h = xf * lax.rsqrt(ms + EPS) * (g_ref[...] * (1.0 + pick(sc_ref))) + pick(sh_ref)
    hi = h.astype(BF16)
    h_ref[...] = hi
    lo = (h - hi.astype(F32)).astype(BF16)
    acc = jnp.dot(hi, wbf_ref[...], preferred_element_type=F32)
    gate_ref[...] = (acc[:, n_main:n_main + LANES] + acc[:, n_main + LANES:]
                     + jnp.dot(lo, wg_ref[0], preferred_element_type=F32) + bg_ref[...])
    bias = jnp.concatenate([b_ref[...] for b_ref in b_refs], axis=1)
    o_ref[...] = acc[:, :n_main] + bias


def _in_proj_norm(xs, tm, g, mod_all, layer, w_gate_t, b_gate, w_in_t, b_in, n_w):
    tn = MM_TILE_N
    in_w = w_in_t.shape[1]
    once = dict(pipeline_mode=pl.Buffered(1))
    x_specs, x_args = _stream_specs(xs, tm)
    w_specs = [pl.BlockSpec((None, tn, D_MODEL), functools.partial(lambda i, k: (layer, k, 0), k=k), **once)
               for k in range(n_w)]
    b_specs = [pl.BlockSpec((None, 1, tn), functools.partial(lambda i, k: (layer, 0, k), k=k))
               for k in range(n_w)]
    b3 = b_in.reshape(DEPTH, 1, in_w)
    mod = lambda k: pl.BlockSpec((1, SUBLANES, D_MODEL), lambda i: (layer, 0, k))
    return pl.pallas_call(
        functools.partial(_in_proj_norm_body, n_w=n_w, n_stream=len(x_args)),
        grid=(NTOK // tm,),
        in_specs=x_specs + [
            pl.BlockSpec((1, D_MODEL), lambda i: (0, 0)),
            mod(0), mod(1),
            pl.BlockSpec((2, D_MODEL, LANES), lambda i: (0, 0, 0)),
            pl.BlockSpec((1, LANES), lambda i: (0, 0)),
        ] + w_specs + b_specs,
        out_specs=[pl.BlockSpec((tm, n_w * tn), lambda i: (i, 0)),
                   pl.BlockSpec((tm, D_MODEL), lambda i: (i, 0)),
                   pl.BlockSpec((tm, LANES), lambda i: (i, 0))],
        out_shape=[jax.ShapeDtypeStruct((NTOK, n_w * tn), F32),
                   jax.ShapeDtypeStruct((NTOK, D_MODEL), BF16),
                   jax.ShapeDtypeStruct((NTOK, LANES), F32)],
        scratch_shapes=[pltpu.VMEM((D_MODEL, n_w * tn + 2 * LANES), BF16)],
        compiler_params=_cparams(("arbitrary",)),
        name="norm_in_proj",
    )(*x_args, g.reshape(1, D_MODEL), mod_all, mod_all, w_gate_t, b_gate, *([w_in_t] * n_w), *([b3] * n_w))


def _row_gate(g_ref, i, tm, tn):
    rows = i * tm + lax.broadcasted_iota(jnp.int32, (tm, tn), 0)
    return jnp.where(rows < CTX, g_ref[0, 1:2, :], g_ref[0, 0:1, :])


def _out_proj_body(y_ref, wglu_ref, bglu_ref, cvc_ref, cvx_ref, hf_ref, hb_ref, o_ref, mlg_ref, w_ref,
                   gate_ref, sh_ref, sc_ref, ng_ref, *rest):
    x_refs, (xo_ref, h_ref, wbf_ref, wglu_bf_ref) = rest[:-4], rest[-4:]
    i = pl.program_id(0)

    @pl.when(i == 0)
    def _():
        wbf_ref[...] = w_ref[...].astype(BF16)
        wglu_bf_ref[...] = wglu_ref[...].astype(BF16)

    g = jax.nn.gelu(y_ref[...])
    glu = jnp.dot(g.astype(BF16), wglu_bf_ref[...], preferred_element_type=F32) + bglu_ref[...]
    s5 = (g * jax.nn.sigmoid(glu)).astype(BF16)
    acc = jnp.dot(s5, wbf_ref[0:S5_W, :], preferred_element_type=F32)
    acc += jnp.dot(_stream_tile([cvc_ref, cvx_ref]), wbf_ref[S5_W:S5_W + CONV_W, :],
                   preferred_element_type=F32)
    for head in range(ML_H):
        cols = slice(head * ML_DH, (head + 1) * ML_DH)
        hh = hf_ref[:, cols].astype(F32) + hb_ref[:, cols].astype(F32)
        hc = hh - jnp.mean(hh, axis=-1, keepdims=True)
        var = jnp.mean(hc * hc, axis=-1, keepdims=True)
        ml = jax.nn.sigmoid(o_ref[:, cols].astype(F32)) * (hc * lax.rsqrt(var + LN_EPS) * mlg_ref[:, cols])
        row0 = S5_W + CONV_W + head * ML_DH
        acc += jnp.dot(ml.astype(BF16), wbf_ref[row0:row0 + ML_DH, :], preferred_element_type=F32)
    is_ctx = i == 0
    xn = _stream_tile(x_refs) + _mod_row(gate_ref, is_ctx) * acc
    xo_ref[...] = xn
    ms = jnp.mean(xn * xn, axis=-1, keepdims=True)
    gain = ng_ref[...] * (1.0 + _mod_row(sc_ref, is_ctx))
    h_ref[...] = (xn * lax.rsqrt(ms + EPS) * gain + _mod_row(sh_ref, is_ctx)).astype(BF16)


def _out_proj(xs, y_s5, w_glu, b_glu, cvo, h_f, h_b, qkvo, ml_norm_g, w_out, mod_all, norm_g, layer):
    tm = ROW_TILE
    mod = lambda k: pl.BlockSpec((1, SUBLANES, D_MODEL), lambda i: (layer, 0, k))
    rows = lambda width, col=0: pl.BlockSpec((tm, width), lambda i: (i, col))
    const = lambda shape: pl.BlockSpec(shape, lambda i: (0,) * len(shape))
    x_specs, x_args = _stream_specs(xs, tm)
    cv_specs = [pl.BlockSpec((tm, CONV_W), lambda i: (0, 0)),
                pl.BlockSpec((tm, CONV_W), lambda i: (jnp.maximum(i - 1, 0), 0))]
    return pl.pallas_call(
        _out_proj_body,
        grid=(NTOK // tm,),
        in_specs=[
            rows(S5_W), const((S5_W, S5_W)), const((1, S5_W)),
        ] + cv_specs + [
            rows(ML_W), rows(ML_W), rows(ML_W, 3), const((1, ML_W)),
            pl.BlockSpec((None, D_MODEL, D_MODEL), lambda i: (layer, 0, 0), pipeline_mode=pl.Buffered(1)),
            mod(2), mod(3), mod(4),
            const((1, D_MODEL)),
        ] + x_specs,
        out_specs=[rows(D_MODEL), rows(D_MODEL)],
        out_shape=[jax.ShapeDtypeStruct((NTOK, D_MODEL), F32), jax.ShapeDtypeStruct((NTOK, D_MODEL), BF16)],
        scratch_shapes=[pltpu.VMEM((D_MODEL, D_MODEL), BF16), pltpu.VMEM((S5_W, S5_W), BF16)],
        compiler_params=_cparams(("arbitrary",)),
        name="out_proj_residual",
    )(y_s5, w_glu, b_glu.reshape(1, S5_W), *cvo, h_f, h_b, qkvo, ml_norm_g.reshape(1, ML_W), w_out,
      mod_all, mod_all, mod_all, norm_g.reshape(1, D_MODEL), *x_args)


def _ffn_in_body(a_ref, wg_ref, wu_ref, o_ref, wgbf_ref, wubf_ref):
    @pl.when(pl.program_id(1) == 0)
    def _():
        wgbf_ref[...] = wg_ref[...].astype(BF16)
        wubf_ref[...] = wu_ref[...].astype(BF16)

    a = a_ref[...]
    g = jnp.dot(a, wgbf_ref[...], preferred_element_type=F32)
    u = jnp.dot(a, wubf_ref[...], preferred_element_type=F32)
    o_ref[...] = (jax.nn.silu(g) * u).astype(BF16)


def _ffn_in(h, w_ffn_in, layer):
    tm, tn = MM_TILE_M, MM_TILE_N
    nj = D_FF // tn
    return pl.pallas_call(
        _ffn_in_body,
        grid=(nj, NTOK // tm),
        in_specs=[
            pl.BlockSpec((tm, D_MODEL), lambda j, i: (i, 0)),
            pl.BlockSpec((None, D_MODEL, tn), lambda j, i: (layer, 0, j)),
            pl.BlockSpec((None, D_MODEL, tn), lambda j, i: (layer, 0, nj + j)),
        ],
        out_specs=pl.BlockSpec((tm, tn), lambda j, i: (i, j)),
        out_shape=jax.ShapeDtypeStruct((NTOK, D_FF), BF16),
        scratch_shapes=[pltpu.VMEM((D_MODEL, tn), BF16), pltpu.VMEM((D_MODEL, tn), BF16)],
        compiler_params=_cparams(("arbitrary", "arbitrary")),
        name="ffn_in_swiglu",
    )(h, w_ffn_in, w_ffn_in)


def _ffn_out_body(a_ref, w_ref, g_ref, x_ref, o_ref, wbf_ref):
    i = pl.program_id(1)

    @pl.when(i == 0)
    def _():
        wbf_ref[...] = w_ref[...].astype(BF16)

    acc = jnp.dot(a_ref[...], wbf_ref[...], preferred_element_type=F32)
    tm, tn = o_ref.shape
    o_ref[...] = x_ref[...] + _row_gate(g_ref, i, tm, tn) * acc


def _ffn_out(xs, hid, w_ffn_out, mod_all, layer):
    tm, tn = FFN_OUT_TILE_M, MM_TILE_N
    return pl.pallas_call(
        _ffn_out_body,
        grid=(D_MODEL // tn, NTOK // tm),
        in_specs=[
            pl.BlockSpec((tm, D_FF), lambda j, i: (i, 0)),
            pl.BlockSpec((None, D_FF, tn), lambda j, i: (layer, 0, j)),
            pl.BlockSpec((1, SUBLANES, tn), lambda j, i: (layer, 0, 5 * (D_MODEL // tn) + j)),
            pl.BlockSpec((tm, tn), lambda j, i: (i, j)),
        ],
        out_specs=pl.BlockSpec((tm, tn), lambda j, i: (i, j)),
        out_shape=jax.ShapeDtypeStruct((NTOK, D_MODEL), F32),
        scratch_shapes=[pltpu.VMEM((D_FF, tn), BF16)],
        compiler_params=_cparams(("arbitrary", "arbitrary")),
        name="ffn_out_residual",
    )(hid, w_ffn_out, mod_all, xs)


def _s5_selectors():
    tau = np.arange(S5_BLK)[:, None] // S5_CH
    ch_r = np.arange(S5_BLK)[:, None] % S5_CH
    t = np.arange(S5_BLK)[None, :] // S5_CH
    ch_c = np.arange(S5_BLK)[None, :] % S5_CH
    sel = np.zeros((2, S5_T, S5_BLK, S5_BLK), np.float32)
    for s in range(S5_T):
        sel[0, s] = (tau == t - s) & (ch_r == ch_c)
        sel[1, s] = (tau == s - t) & (ch_r == ch_c)
    rp = np.arange(S5_BLK)[:, None]
    col = np.arange(2 * S5_BLK)[None, :]
    pair_map = np.stack([(col == (rp // S5_P) * S5_BLK + gi * S5_P + rp % S5_P) for gi in range(2)])
    src = np.arange(S5_STEP_W)
    dst = ((src // S5_CH) % S5_STEP_G) * S5_BLK + (src // S5_SET_CH) * S5_CH + src % S5_CH
    perm = dst[:, None] == np.arange(S5_STEP_W)[None, :]
    return (jnp.asarray(sel), jnp.asarray(pair_map, BF16), jnp.asarray(perm, BF16),
            jnp.asarray(perm.T, BF16))


def _s5_operators(lam_re, lam_im, log_step, b_re, b_im, c_re, c_im, sel):
    lam = lax.complex(lam_re.astype(F32), lam_im.astype(F32))
    lam_bar = jnp.exp(lam * jnp.exp(log_step.astype(F32)))
    b_bar = ((lam_bar - 1.0) / lam)[..., None] * lax.complex(b_re.astype(F32), b_im.astype(F32))
    c_mat = lax.complex(c_re.astype(F32), c_im.astype(F32))
    depth = lam.shape[0]

    def powers(base, count):
        out = [jnp.ones_like(base)]
        for _ in range(count - 1):
            out.append(out[-1] * base)
        return out

    pw = powers(lam_bar, S5_T + 1)
    pa = powers(pw[S5_T], SUBLANES + 1)
    t_up = list(range(S5_T))

    def table(seq, fwd_idx, bwd_idx):
        return jnp.stack([jnp.stack([seq[f][:, 0], seq[b][:, 1]], axis=1)
                          for f, b in zip(fwd_idx, bwd_idx)])

    blocks = (depth, 2, S5_G, S5_BLK, S5_BLK)

    def token_rows(w, imag_sign):
        w = jnp.moveaxis(w, 0, 3)
        return jnp.concatenate([jnp.real(w), imag_sign * jnp.imag(w)], axis=-1).reshape(blocks)

    pb = table(pw, [S5_T - 1 - t for t in t_up], t_up)
    wb = token_rows(pb[:, :, :, :, None, :] * jnp.swapaxes(b_bar, -1, -2)[None], 1.0)
    pc = table(pw, [t + 1 for t in t_up], [S5_T - t for t in t_up])
    wct = token_rows(pc[:, :, :, :, None, :] * c_mat[None], -1.0)

    lag_c = token_rows(jnp.stack(pw[:S5_T])[:, :, :, :, None, :] * c_mat[None], -1.0)
    b_t = jnp.swapaxes(b_bar, -1, -2)
    b_ri = jnp.concatenate([jnp.real(b_t), jnp.imag(b_t)], axis=-1)
    kern = jnp.einsum('ldgar,ldgxr->ldgax', b_ri, lag_c, precision=lax.Precision.HIGHEST)
    kern = kern.reshape(depth, 2, S5_G * S5_CH, S5_BLK)
    kc = jnp.einsum('ldxk,dskn->ldsxn', kern, sel, precision=lax.Precision.HIGHEST)
    kc = kc.reshape(depth, 2, S5_T, S5_G, S5_CH, S5_BLK)
    kc = jnp.transpose(kc, (0, 1, 3, 2, 4, 5)).reshape(blocks)

    def lanes(v):
        parts = jnp.stack([jnp.real(v), jnp.imag(v)], axis=4)
        parts = parts.reshape(v.shape[0], depth, 2, S5_G // 2, 2, 2, S5_P)
        parts = jnp.swapaxes(parts, 4, 5).reshape(v.shape[0], depth, 2, S5_FLAT)
        return jnp.transpose(parts, (1, 2, 0, 3))

    zero_p = jnp.zeros_like(pa[0])
    amul = lanes(jnp.stack([pa[1], pa[2], pa[4], pa[8]] + [zero_p] * 4))
    apow = lanes(table(pa, t_up, t_up[::-1]))
    return kc.astype(BF16), wb.astype(BF16), wct.astype(BF16), amul, apow


def _s5_row_scan(e_ref, amul_ref, apow_ref, block_lo, block_hi, reverse, carry):
    rows = lax.broadcasted_iota(jnp.int32, (SUBLANES, S5_BLK), 0)

    def shifted(x, k):
        if reverse:
            return jnp.where(rows < SUBLANES - k, pltpu.roll(x, SUBLANES - k, 0), 0.0)
        return jnp.where(rows >= k, pltpu.roll(x, k, 0), 0.0)

    def body(step, carry):
        blk = (block_hi - 1 - step) if reverse else (block_lo + step)
        r0 = pl.multiple_of(blk * SUBLANES, SUBLANES)
        last = 0 if reverse else SUBLANES - 1
        carry_out = []
        for q in range(S5_STEP_PAIRS):
            re_c = slice(2 * q * S5_BLK, (2 * q + 1) * S5_BLK)
            im_c = slice((2 * q + 1) * S5_BLK, (2 * q + 2) * S5_BLK)
            s_re = e_ref[pl.ds(r0, SUBLANES), re_c]
            s_im = e_ref[pl.ds(r0, SUBLANES), im_c]
            for idx, k in enumerate((1, 2, 4)):
                a_re, a_im = amul_ref[idx:idx + 1, re_c], amul_ref[idx:idx + 1, im_c]
                t_re, t_im = shifted(s_re, k), shifted(s_im, k)
                s_re, s_im = s_re + t_re * a_re - t_im * a_im, s_im + t_re * a_im + t_im * a_re
            c_re, c_im = carry[:, re_c], carry[:, im_c]
            p_re, p_im = apow_ref[:, re_c], apow_ref[:, im_c]
            e_ref[pl.ds(r0, SUBLANES), re_c] = shifted(s_re, 1) + p_re * c_re - p_im * c_im
            e_ref[pl.ds(r0, SUBLANES), im_c] = shifted(s_im, 1) + p_re * c_im + p_im * c_re
            a_re, a_im = amul_ref[3:4, re_c], amul_ref[3:4, im_c]
            l_re = jnp.broadcast_to(s_re[last:last + 1, :], (SUBLANES, S5_BLK))
            l_im = jnp.broadcast_to(s_im[last:last + 1, :], (SUBLANES, S5_BLK))
            carry_out += [l_re + a_re * c_re - a_im * c_im, l_im + a_re * c_im + a_im * c_re]
        return jnp.concatenate(carry_out, axis=1)

    return lax.fori_loop(0, block_hi - block_lo, body, carry)


def _s5_pair_operators(kc_ref, wb_ref, wct_ref, map_ref, q):
    g0, g1 = 2 * q, 2 * q + 1
    zero = jnp.zeros((S5_BLK, S5_BLK), BF16)
    kc = jnp.concatenate([jnp.concatenate([kc_ref[g0], zero], axis=1),
                          jnp.concatenate([zero, kc_ref[g1]], axis=1)], axis=0)
    spread = lambda ref: jnp.concatenate(
        [jnp.dot(ref[g0], map_ref[0], preferred_element_type=F32),
         jnp.dot(ref[g1], map_ref[1], preferred_element_type=F32)], axis=0).astype(BF16)
    return kc, spread(wb_ref), spread(wct_ref)


def _s5_body(u_ref, p_ref, pt_ref, map_ref, kc_ref, wb_ref, wct_ref, amul_ref, apow_ref, d_ref, y_ref,
             up_ref, e_ref, acc_ref):
    direction = pl.program_id(1)
    token_rows = lambda t: pl.ds(t, S5_ROWS, stride=S5_T)
    pairs = [slice(2 * q * S5_BLK, 2 * (q + 1) * S5_BLK) for q in range(S5_STEP_PAIRS)]

    @pl.when(direction == 0)
    def _():
        u_nat = jnp.concatenate([u_ref[token_rows(t), :] for t in range(S5_T)], axis=1)
        up_ref[...] = jnp.dot(u_nat.astype(BF16), p_ref[...], preferred_element_type=F32).astype(BF16)
        acc_ref[...] = jnp.zeros_like(acc_ref)

    ops = [_s5_pair_operators(kc_ref, wb_ref, wct_ref, map_ref, q) for q in range(S5_STEP_PAIRS)]
    for (_, wb, _), cols in zip(ops, pairs):
        e_ref[:, cols] = jnp.dot(up_ref[:, cols], wb, preferred_element_type=F32)

    zero = jnp.zeros((SUBLANES, S5_STEP_W), F32)
    n_blocks = S5_ROWS // SUBLANES
    ctx_blocks = S5_CTX_ROWS // SUBLANES
    amul, apow = amul_ref, apow_ref

    @pl.when(direction == 0)
    def _():
        _s5_row_scan(e_ref, amul, apow, 0, n_blocks, False, zero)

    @pl.when(direction == 1)
    def _():
        carry = _s5_row_scan(e_ref, amul, apow, 0, ctx_blocks, True, zero)
        _s5_row_scan(e_ref, amul, apow, ctx_blocks, n_blocks, True, carry)

    for (kc, _, wct), cols in zip(ops, pairs):
        y = jnp.dot(up_ref[:, cols], kc, preferred_element_type=F32)
        y += lax.dot_general(e_ref[:, cols].astype(BF16), wct, (((1,), (1,)), ((), ())),
                             preferred_element_type=F32)
        acc_ref[:, cols] += y

    @pl.when(direction == 1)
    def _():
        acc = acc_ref[...]
        hi = acc.astype(BF16)
        lo = (acc - hi.astype(F32)).astype(BF16)
        y_nat = (jnp.dot(hi, pt_ref[...], preferred_element_type=F32)
                 + jnp.dot(lo, pt_ref[...], preferred_element_type=F32))
        for t in range(S5_T):
            y_ref[token_rows(t), :] = (y_nat[:, t * S5_SET_CH:(t + 1) * S5_SET_CH]
                                       + d_ref[...] * u_ref[token_rows(t), :])


def _s5_mixer(z, ops, consts, layer, d_skip):
    kc, wb, wct, amul, apow = ops
    _, pair_map, perm, perm_t = consts
    n_steps = S5_G // S5_STEP_G
    wspec = pl.BlockSpec((None, None, S5_STEP_G, S5_BLK, S5_BLK), lambda j, d: (layer, d, j, 0, 0))
    cspec = pl.BlockSpec((None, None, SUBLANES, S5_STEP_W), lambda j, d: (layer, d, 0, j))
    pspec = pl.BlockSpec((S5_STEP_W, S5_STEP_W), lambda j, d: (0, 0))
    return pl.pallas_call(
        _s5_body,
        grid=(n_steps, 2),
        in_specs=[
            pl.BlockSpec((NTOK, S5_SET_CH), lambda j, d: (0, j)),
            pspec, pspec,
            pl.BlockSpec((2, S5_BLK, 2 * S5_BLK), lambda j, d: (0, 0, 0)),
            wspec, wspec, wspec, cspec, cspec,
            pl.BlockSpec((1, S5_SET_CH), lambda j, d: (0, j)),
        ],
        out_specs=pl.BlockSpec((NTOK, S5_SET_CH), lambda j, d: (0, j)),
        out_shape=jax.ShapeDtypeStruct((NTOK, S5_W), F32),
        scratch_shapes=[pltpu.VMEM((S5_ROWS, S5_STEP_W), BF16), pltpu.VMEM((S5_ROWS, S5_STEP_W), F32),
                        pltpu.VMEM((S5_ROWS, S5_STEP_W), F32)],
        compiler_params=_cparams(("arbitrary", "arbitrary")),
        name="s5_scan",
    )(z, perm, perm_t, pair_map, kc, wb, wct, amul, apow, d_skip.astype(F32).reshape(1, S5_W))


CONV_PAD = 16
CONV_BLK = 64


CONV_PAD_ROWS = max(CTX + 2 * CONV_PAD, (ROW_TILE // GRID_W) * (GRID_W + 2 * CONV_PAD))


def _conv_tile(a_ref, b_ref, w_ref, db_ref, lg_ref, lb_ref, o_ref, pad_ref, sh_ref, seq_len):
    g = a_ref[...] * jax.nn.sigmoid(b_ref[...])
    n_seq = ROW_TILE // seq_len
    pitch = seq_len + 2 * CONV_PAD
    zeros = jnp.zeros((CONV_PAD, CONV_W), F32)
    for s in range(n_seq):
        pad_ref[s * pitch:s * pitch + CONV_PAD, :] = zeros
        pad_ref[s * pitch + CONV_PAD:s * pitch + CONV_PAD + seq_len, :] = g[s * seq_len:(s + 1) * seq_len]
        pad_ref[s * pitch + CONV_PAD + seq_len:(s + 1) * pitch, :] = zeros
    used = n_seq * pitch
    for r in range(1, SUBLANES):
        sh_ref[r, 0:used - SUBLANES, :] = pad_ref[r:r + used - SUBLANES, :]
    for blk in range(ROW_TILE // CONV_BLK):
        row0 = blk * CONV_BLK
        s, q = divmod(row0, seq_len)
        base = s * pitch + CONV_PAD + q - CONV_K // 2
        tiles = []
        for lt in range(CONV_W // LANES):
            lanes = slice(lt * LANES, (lt + 1) * LANES)
            acc = jnp.zeros((CONV_BLK, LANES), F32)
            for k in range(CONV_K):
                r = (base + k) % SUBLANES
                row = base + k - r
                tap = pad_ref[row:row + CONV_BLK, lanes] if r == 0 else sh_ref[r, row:row + CONV_BLK, lanes]
                acc = acc + w_ref[k:k + 1, lanes] * tap
            tiles.append(acc)
        half = CONV_BLK // 2
        for h0 in (0, half):
            y = jnp.concatenate([t[h0:h0 + half] for t in tiles], axis=1) + db_ref[...]
            yc = y - jnp.mean(y, axis=-1, keepdims=True)
            var = jnp.mean(yc * yc, axis=-1, keepdims=True)
            y = yc * lax.rsqrt(var + LN_EPS) * lg_ref[...] + lb_ref[...]
            o_ref[row0 + h0:row0 + h0 + half, :] = jax.nn.silu(y).astype(BF16)


def _in_proj_conv_body(h_ref, wt0_ref, wt1_ref, b0_ref, b1_ref, za_ref, zb_ref, zac_ref, zbc_ref,
                       cw_ref, cdb_ref, clg_ref, clb_ref, o_ref, cvc_ref, cvx_ref,
                       wbf_ref, pad_ref, sh_ref):
    j, i = pl.program_id(0), pl.program_id(1)
    tn = MM_TILE_N
    conv = (cw_ref, cdb_ref, clg_ref, clb_ref)

    @pl.when(i == 0)
    def _():
        wbf_ref[:, 0:tn] = jnp.transpose(wt0_ref[...]).astype(BF16)
        wbf_ref[:, tn:2 * tn] = jnp.transpose(wt1_ref[...]).astype(BF16)

    @pl.when(jnp.logical_and(j == 0, i == 0))
    def _():
        _conv_tile(zac_ref, zbc_ref, *conv, cvc_ref, pad_ref, sh_ref, CTX)

    bias = jnp.concatenate([b0_ref[...], b1_ref[...]], axis=1)
    acc = jnp.dot(h_ref[...], wbf_ref[...], preferred_element_type=F32) + bias
    o_ref[...] = acc.astype(o_ref.dtype)
    _conv_tile(za_ref, zb_ref, *conv, cvx_ref, pad_ref, sh_ref, GRID_W)


def _in_proj_conv(h, w_in_t, b_in, layer, z, dw_w, dw_b, ln_g, ln_b):
    tm, tn = MM_TILE_M, MM_TILE_N
    n_j = (OFF_G - OFF_Q) // (2 * tn)
    n_i = NTOK // tm
    assert n_j * n_i == SEQ // ROW_TILE
    first = OFF_Q // tn
    in_w = w_in_t.shape[1]
    b3 = b_in.reshape(DEPTH, 1, in_w)
    w = jnp.concatenate([dw_w, jnp.zeros((1, CONV_W), dw_w.dtype)], axis=0)
    vec = lambda v: v.reshape(1, CONV_W)
    a_col = OFF_CONV // CONV_W
    x_tile = lambda j, i: j * n_i + i + CTX // ROW_TILE
    const = lambda shape: pl.BlockSpec(shape, lambda j, i: (0,) * len(shape))
    return pl.pallas_call(
        _in_proj_conv_body,
        grid=(n_j, n_i),
        in_specs=[
            pl.BlockSpec((tm, D_MODEL), lambda j, i: (i, 0)),
            pl.BlockSpec((None, tn, D_MODEL), lambda j, i: (layer, first + 2 * j, 0)),
            pl.BlockSpec((None, tn, D_MODEL), lambda j, i: (layer, first + 2 * j + 1, 0)),
            pl.BlockSpec((None, 1, tn), lambda j, i: (layer, 0, first + 2 * j)),
            pl.BlockSpec((None, 1, tn), lambda j, i: (layer, 0, first + 2 * j + 1)),
            pl.BlockSpec((ROW_TILE, CONV_W), lambda j, i: (x_tile(j, i), a_col)),
            pl.BlockSpec((ROW_TILE, CONV_W), lambda j, i: (x_tile(j, i), a_col + 1)),
            pl.BlockSpec((ROW_TILE, CONV_W), lambda j, i: (0, a_col)),
            pl.BlockSpec((ROW_TILE, CONV_W), lambda j, i: (0, a_col + 1)),
            const((CONV_K + 1, CONV_W)), const((1, CONV_W)), const((1, CONV_W)), const((1, CONV_W)),
        ],
        out_specs=[pl.BlockSpec((tm, 2 * tn), lambda j, i: (i, j)),
                   const((CTX, CONV_W)),
                   pl.BlockSpec((ROW_TILE, CONV_W), lambda j, i: (j * n_i + i, 0))],
        out_shape=[jax.ShapeDtypeStruct((NTOK, OFF_G - OFF_Q), BF16),
                   jax.ShapeDtypeStruct((CTX, CONV_W), BF16),
                   jax.ShapeDtypeStruct((SEQ, CONV_W), BF16)],
        scratch_shapes=[pltpu.VMEM((D_MODEL, 2 * tn), BF16), pltpu.VMEM((CONV_PAD_ROWS, CONV_W), F32),
                        pltpu.VMEM((SUBLANES, CONV_PAD_ROWS, CONV_W), F32)],
        compiler_params=_cparams(("arbitrary", "arbitrary")),
        name="in_proj_conv",
    )(h, w_in_t, w_in_t, b3, b3, z, z, z, z, w, vec(dw_b), vec(ln_g), vec(ln_b))


def _ml_chunk_index(ci, reverse):
    if not reverse:
        return ci
    return jnp.where(ci == 0, 0, ML_NCHUNK - ci)


def _ml_chain_stages(qb, kb, vx, li_rep, li_row, b_rep, b_row, cx_ref, m_ref, mask, reverse, store_h):
    c = ML_CHUNK
    scale = ML_DH ** 0.5
    wide = lambda a: jnp.concatenate([a] * (ML_DH // LANES), axis=1)
    v = {}

    def scores():
        v['m'] = m_ref[...]
        v['d_log'] = jnp.where(mask, wide(b_rep) + (li_row - b_row), -jnp.inf)
        v['inter'] = b_rep + v['m']
        v['m_row'] = jnp.maximum(v['inter'], jnp.max(v['d_log'], axis=-1, keepdims=True))
        v['qk'] = lax.dot_general(qb, kb, (((1,), (1,)), ((), ())), preferred_element_type=F32)

    def numerator():
        s = v['qk'] * jnp.exp(v['d_log'] - wide(v['m_row']))
        w_inter = jnp.exp(v['inter'] - v['m_row']) * scale
        v['cx'] = cx_ref[...]
        lhs = jnp.concatenate([s.astype(BF16), qb * wide(w_inter.astype(BF16))], axis=1)
        rhs = jnp.concatenate([vx, v['cx'].astype(BF16)], axis=0)
        v['num'] = jnp.dot(lhs, rhs, preferred_element_type=F32)

    def output():
        num = v['num']
        den = num[:, ML_DH:]
        recip = 1.0 / jnp.maximum(jnp.abs(den), jnp.exp(-v['m_row']) * scale)
        store_h(num[:, :ML_DH] * wide(recip))

    def state():
        m = v['m']
        b_tot = b_row[:, 0:1] if reverse else b_row[:, c - 1:c]
        g = b_tot - b_rep + li_rep
        m_new = jnp.maximum(b_tot + m, jnp.max(g, axis=0, keepdims=True))
        kw = kb * wide((jnp.exp(g - m_new) * (1.0 / scale)).astype(BF16))
        decay = jnp.exp(b_tot + m - m_new)[:, 0:1]
        cx_ref[...] = decay * v['cx'] + lax.dot_general(kw, vx, (((0,), (0,)), ((), ())),
                                                       preferred_element_type=F32)
        m_ref[...] = m_new

    return [scores, numerator, output, state]


def _mlstm_body(qkvf_ref, gtf_ref, qkvb_ref, gtb_ref, hf_ref, hb_ref, cx_ref, m_ref):
    @pl.when(pl.program_id(0) == 0)
    def _():
        cx_ref[...] = jnp.zeros_like(cx_ref)
        m_ref[...] = jnp.zeros_like(m_ref)

    c = ML_CHUNK
    r_idx = lax.broadcasted_iota(jnp.int32, (c, c), 0)
    c_idx = lax.broadcasted_iota(jnp.int32, (c, c), 1)
    ones_col = jnp.ones((c, LANES), BF16)
    n_piece = 3
    spread_r = lax.broadcasted_iota(jnp.int32, (SUBLANES, 2 * LANES), 0)
    spread_c = lax.broadcasted_iota(jnp.int32, (SUBLANES, 2 * LANES), 1)
    in_a = jnp.logical_and(spread_r < n_piece, spread_c < LANES)
    in_b = jnp.logical_and(jnp.logical_and(spread_r >= n_piece, spread_r < 2 * n_piece), spread_c >= LANES)
    spread = jnp.where(jnp.logical_or(in_a, in_b), 1.0, 0.0).astype(BF16)

    def replicate(a_rows, b_rows):
        pad = jnp.zeros((SUBLANES - 2 * n_piece, c), F32)
        rows = jnp.concatenate(list(a_rows) + list(b_rows) + [pad], axis=0).astype(BF16)
        both = lax.dot_general(rows, spread, (((0,), (0,)), ((), ())), preferred_element_type=F32)
        return both[:, :LANES], both[:, LANES:]

    chains = []
    for d, (qkv_ref, gt_ref, h_ref) in enumerate(((qkvf_ref, gtf_ref, hf_ref), (qkvb_ref, gtb_ref, hb_ref))):
        reverse = d == 1
        gates_t = gt_ref[...]
        incl = (c_idx >= r_idx) if reverse else (c_idx <= r_idx)
        incl_t = (r_idx >= c_idx) if reverse else (r_idx <= c_idx)
        tri_t = jnp.where(incl_t, 1.0, 0.0).astype(BF16)
        lf_t = jax.nn.log_sigmoid(gates_t)
        b_all_t = sum(jnp.dot(p, tri_t, preferred_element_type=F32) for p in _split3(lf_t))
        gate_rows = [p.astype(F32) for p in _split3(gates_t)]
        b_rows = [p.astype(F32) for p in _split3(b_all_t)]
        for head in range(ML_H):
            i_col = 2 * ML_H * d + head
            f_col = i_col + ML_H
            state = d * ML_H + head
            col = lambda part: slice((part * ML_H + head) * ML_DH, (part * ML_H + head + 1) * ML_DH)
            vx = jnp.concatenate([qkv_ref[:, col(2)], ones_col], axis=1)

            def store_h(h, h_ref=h_ref, head=head):
                h_ref[:, head * ML_DH:(head + 1) * ML_DH] = h.astype(h_ref.dtype)

            li_rep, b_rep = replicate([p[i_col:i_col + 1, :] for p in gate_rows],
                                      [p[f_col:f_col + 1, :] for p in b_rows])
            chains.append(_ml_chain_stages(
                qkv_ref[:, col(0)], qkv_ref[:, col(1)], vx,
                li_rep, gates_t[i_col:i_col + 1, :],
                b_rep, b_all_t[f_col:f_col + 1, :],
                cx_ref.at[state], m_ref.at[state], incl, reverse, store_h))
    for stage in zip(*chains):
        for run in stage:
            run()


def _mlstm_mixer(qkvo, gates):
    gates_t = jnp.transpose(gates[:, :2 * SUBLANES])
    c = ML_CHUNK

    def specs(reverse):
        row = lambda ci: _ml_chunk_index(ci, reverse)
        return [
            pl.BlockSpec((c, 3 * ML_W), lambda ci: (row(ci), 0)),
            pl.BlockSpec((2 * SUBLANES, c), lambda ci: (0, row(ci))),
        ]

    out_spec = lambda reverse: pl.BlockSpec((c, ML_W), lambda ci: (_ml_chunk_index(ci, reverse), 0))
    n_state = 2 * ML_H
    return pl.pallas_call(
        _mlstm_body,
        grid=(ML_NCHUNK,),
        in_specs=specs(False) + specs(True),
        out_specs=[out_spec(False), out_spec(True)],
        out_shape=[jax.ShapeDtypeStruct((NTOK, ML_W), BF16)] * 2,
        scratch_shapes=[pltpu.VMEM((n_state, ML_DH, ML_DH + LANES), F32),
                        pltpu.VMEM((n_state, 1, LANES), F32)],
        compiler_params=_cparams(("arbitrary",)),
        name="mlstm_chunks",
    )(qkvo, gates_t, qkvo, gates_t)


def kernel(x, c, ctx, c_ctx, w_mod, b_mod, norm1_g, w_in, b_in, s5_lam_re, s5_lam_im, s5_log_step,
           s5_b_re, s5_b_im, s5_c_re, s5_c_im, s5_d, s5_w_glu, s5_b_glu, conv_dw_w, conv_dw_b,
           conv_ln_g, conv_ln_b, ml_norm_g, w_out, norm2_g, w_ffn_in, w_ffn_out, norm_f_g):
    assert x.shape == (1, SEQ, D_MODEL) and ctx.shape == (1, CTX, D_MODEL)
    xs = (ctx[0].astype(F32), x[0].astype(F32))
    cc = jnp.zeros((SUBLANES, D_MODEL), F32).at[0].set(c[0]).at[1].set(c_ctx)
    mod_all = _modulation(cc, w_mod, b_mod)
    s5_consts = _s5_selectors()
    s5_ops = _s5_operators(s5_lam_re, s5_lam_im, s5_log_step, s5_b_re, s5_b_im, s5_c_re, s5_c_im,
                           s5_consts[0])
    w_in_t = jnp.swapaxes(w_in, 1, 2)
    n_gate = w_in.shape[2] - OFF_G
    w_gate = w_in[:, :, OFF_G:]
    w_gate_hi = w_gate.astype(BF16)
    w_gate = jnp.stack([w_gate_hi, (w_gate - w_gate_hi.astype(F32)).astype(BF16)], axis=1)
    w_gate_t = jnp.pad(w_gate, ((0, 0), (0, 0), (0, 0), (0, LANES - n_gate)))
    b_gate = jnp.pad(b_in[:, OFF_G:], ((0, 0), (0, LANES - n_gate))).reshape(DEPTH, 1, LANES)
    tn = MM_TILE_N

    for l in range(DEPTH):
        tm = CTX if l == 0 else IN_NORM_TILE_M
        z, h, gates = _in_proj_norm(xs, tm, norm1_g[l], mod_all, l, w_gate_t[l], b_gate[l], w_in_t, b_in,
                                    OFF_Q // tn)
        qkvo, cv_c, cv_x = _in_proj_conv(h, w_in_t, b_in, l, z, conv_dw_w[l], conv_dw_b[l], conv_ln_g[l],
                                         conv_ln_b[l])
        y_s5 = _s5_mixer(z, s5_ops, s5_consts, l, s5_d[l])
        h_f, h_b = _mlstm_mixer(qkvo, gates)
        xs, h2 = _out_proj(xs, y_s5, s5_w_glu[l], s5_b_glu[l], (cv_c, cv_x), h_f, h_b, qkvo, ml_norm_g[l], w_out,
                           mod_all, norm2_g[l], l)
        hid = _ffn_in(h2, w_ffn_in, l)
        xs = _ffn_out(xs, hid, w_ffn_out, mod_all, l)
    return _final_norm(xs, norm_f_g)[None]
```
